```python
import math
import jax, jax.numpy as jnp
from jax import lax
import numpy as np

D_MODEL = 1024
BATCH = 4
SEQ = 4096
DEPTH = 2
DEC_BATCH = 32
DEC_SEQ = 4
PAST_LEN = 8192
PAGE_SIZE = 128

N_A_LAYERS = DEPTH // 2
N_B_LAYERS = DEPTH - N_A_LAYERS
D_INNER = 2 * D_MODEL
SSM_HEAD_DIM = 64
N_SSM_HEADS = D_INNER // SSM_HEAD_DIM
N_SSM_GROUPS = 4
HEADS_PER_GROUP = N_SSM_HEADS // N_SSM_GROUPS
D_STATE = 128
D_CONV = 4
SSD_CHUNK = 128
GN = N_SSM_GROUPS * D_STATE
CONV_DIM = D_INNER + 2 * GN
IN_DIM = D_INNER + CONV_DIM + N_SSM_HEADS
HEAD_DIM = 64
N_HEADS = D_MODEL // HEAD_DIM
N_KV_HEADS = 4
Q_PER_KV = N_HEADS // N_KV_HEADS
KV_DIM = N_KV_HEADS * HEAD_DIM
MOBA_BLOCK = 256
MOBA_TOP_K = 3
Q_CHUNK = 32
ROPE_THETA = 10000.0
D_FF = ((8 * D_MODEL + 3 * 256 - 1) // (3 * 256)) * 256
EPS = 1e-6

kernel_name = 'yoco_ssd_moba_decode_step'


def rmsnorm(x, g):
    xf = x.astype(jnp.float32)
    y = xf * lax.rsqrt(jnp.mean(xf * xf, axis=-1, keepdims=True) + EPS)
    return (y * g.astype(jnp.float32)).astype(x.dtype)


def rope(x, pos):
    half = HEAD_DIM // 2
    inv = ROPE_THETA ** (-jnp.arange(half, dtype=jnp.float32) / half)
    ang = pos.astype(jnp.float32)[:, None] * inv[None, :]
    cos = jnp.cos(ang)[None, :, None, :]
    sin = jnp.sin(ang)[None, :, None, :]
    xf = x.astype(jnp.float32)
    x1, x2 = xf[..., :half], xf[..., half:]
    return jnp.concatenate([x1 * cos - x2 * sin, x2 * cos + x1 * sin], axis=-1).astype(x.dtype)


def swiglu_ffn(h, w_gu, w_down):
    gu = h @ w_gu
    return (jax.nn.silu(gu[..., :D_FF]) * gu[..., D_FF:]) @ w_down


def ssd_scan(x, dt, a, bmat, cmat, h0):
    b, T = x.shape[:2]
    L = min(SSD_CHUNK, T)
    nc = T // L
    G, J, P, N = N_SSM_GROUPS, HEADS_PER_GROUP, SSM_HEAD_DIM, D_STATE
    f32 = jnp.float32
    xr = x.astype(f32).reshape(b, nc, L, G, J, P)
    dtr = dt.reshape(b, nc, L, G, J)
    br = bmat.astype(f32).reshape(b, nc, L, G, N)
    cr = cmat.astype(f32).reshape(b, nc, L, G, N)
    a_cs = jnp.cumsum(dtr * a.reshape(G, J), axis=2)
    xd = xr * dtr[..., None]
    causal = jnp.tril(jnp.ones((L, L), dtype=bool))[:, :, None, None]
    seg = jnp.exp(jnp.where(causal, a_cs[:, :, :, None] - a_cs[:, :, None], -jnp.inf))
    cb = jnp.einsum('bclgn,bcsgn->bclsg', cr, br)
    y_diag = jnp.einsum('bclsgj,bcsgjp->bclgjp', cb[..., None] * seg, xd)
    decay_to_end = jnp.exp(a_cs[:, :, -1:] - a_cs)
    chunk_states = jnp.einsum('bcsgn,bcsgjp->bcgjpn', br, xd * decay_to_end[..., None])
    chunk_decay = jnp.exp(a_cs[:, :, -1])

    def step(h, inp):
        dec, st = inp
        return h * dec[..., None, None] + st, h

    h_final, h_prev = lax.scan(step, h0.astype(f32).reshape(b, G, J, P, N),
                               (jnp.moveaxis(chunk_decay, 1, 0), jnp.moveaxis(chunk_states, 1, 0)))
    h_prev = jnp.moveaxis(h_prev, 0, 1)
    y_off = jnp.einsum('bclgn,bcgjpn->bclgjp', cr, h_prev) * jnp.exp(a_cs)[..., None]
    y = (y_diag + y_off).reshape(b, T, N_SSM_HEADS, P)
    return y, h_final.reshape(b, N_SSM_HEADS, P, N)


def mamba_mixer(h, conv0, ssm0, w_in, conv_w, conv_b, dt_bias, a_log, d_skip, g_norm, w_out):
    b, T, _ = h.shape
    proj = h @ w_in
    z = proj[..., :D_INNER]
    xbc = proj[..., D_INNER:D_INNER + CONV_DIM]
    dt_raw = proj[..., D_INNER + CONV_DIM:]
    padded = jnp.concatenate([conv0.astype(xbc.dtype), xbc], axis=1)
    conv = conv_b
    for i in range(D_CONV):
        conv = conv + padded[:, i:i + T] * conv_w[i]
    new_conv = padded[:, T:]
    xbc = jax.nn.silu(conv)
    xs = xbc[..., :D_INNER].reshape(b, T, N_SSM_HEADS, SSM_HEAD_DIM)
    bm = xbc[..., D_INNER:D_INNER + GN].reshape(b, T, N_SSM_GROUPS, D_STATE)
    cm = xbc[..., D_INNER + GN:].reshape(b, T, N_SSM_GROUPS, D_STATE)
    dt = jax.nn.softplus(dt_raw.astype(jnp.float32) + dt_bias.astype(jnp.float32))
    a = -jnp.exp(a_log.astype(jnp.float32))
    y, new_ssm = ssd_scan(xs, dt, a, bm, cm, ssm0)
    y = y + d_skip.astype(jnp.float32)[:, None] * xs.astype(jnp.float32)
    y = y.reshape(b, T, D_INNER) * jax.nn.silu(z.astype(jnp.float32))
    yg = y.reshape(b, T, N_SSM_GROUPS, D_INNER // N_SSM_GROUPS)
    yg = yg * lax.rsqrt(jnp.mean(yg * yg, axis=-1, keepdims=True) + EPS)
    y = yg.reshape(b, T, D_INNER) * g_norm.astype(jnp.float32)
    return y.astype(h.dtype) @ w_out, new_conv, new_ssm.astype(h.dtype)


def moba_blocks(k, v):
    b, T = k.shape[:2]
    nb = -(-T // MOBA_BLOCK)
    pad = nb * MOBA_BLOCK - T

    def blk(t):
        t = jnp.pad(t, ((0, 0), (0, pad), (0, 0), (0, 0)))
        return t.reshape(b, nb, MOBA_BLOCK, N_KV_HEADS, HEAD_DIM).transpose(0, 3, 1, 2, 4)

    kb, vb = blk(k), blk(v)
    kmean = jnp.mean(kb.astype(jnp.float32), axis=3)
    return kb, vb, kmean


def moba_attend(q, q_pos, kb, vb, kmean):
    b, nq = q.shape[:2]
    nb = kb.shape[2]
    qg = q.reshape(b, nq, N_KV_HEADS, Q_PER_KV, HEAD_DIM)
    n_sel = min(MOBA_TOP_K, nb - 1)
    qc = min(Q_CHUNK, nq)
    nc = nq // qc
    xs = {'q': jnp.swapaxes(qg.reshape(b, nc, qc, N_KV_HEADS, Q_PER_KV, HEAD_DIM), 0, 1),
          'pos': q_pos.reshape(nc, qc)}
    if n_sel > 0:
        q_blk = q_pos // MOBA_BLOCK
        gate = jnp.einsum('bqkgd,bknd->bkgqn', qg.astype(jnp.float32), kmean)
        fully_past = jnp.arange(nb)[None, :] < q_blk[:, None]
        gate = jnp.where(fully_past, gate, -jnp.inf)
        _, idx = lax.top_k(gate, n_sel)
        xs['idx'] = jnp.moveaxis(idx.reshape(b, N_KV_HEADS, Q_PER_KV, nc, qc, n_sel), 3, 0)
    bi = jnp.arange(b)[:, None, None, None, None]
    hi = jnp.arange(N_KV_HEADS)[None, :, None, None, None]
    key_off = jnp.arange(MOBA_BLOCK)
    scale = HEAD_DIM ** -0.5

    def chunk(c):
        qcb, pos = c['q'], c['pos']
        blk = pos // MOBA_BLOCK
        kown = kb[:, :, blk]
        vown = vb[:, :, blk]
        s_own = jnp.einsum('bqkgd,bkqld->bkgql', qcb, kown).astype(jnp.float32) * scale
        own_ok = (blk[:, None] * MOBA_BLOCK + key_off[None, :]) <= pos[:, None]
        s_own = jnp.where(own_ok, s_own, -jnp.inf)
        if n_sel > 0:
            idx = c['idx']
            ksel = kb[bi, hi, idx]
            vsel = vb[bi, hi, idx]
            s_sel = jnp.einsum('bqkgd,bkgqsld->bkgqsl', qcb, ksel).astype(jnp.float32) * scale
            sel_ok = (idx < blk[:, None])[..., None]
            s_sel = jnp.where(sel_ok, s_sel, -jnp.inf).reshape(s_sel.shape[:4] + (n_sel * MOBA_BLOCK,))
            p = jax.nn.softmax(jnp.concatenate([s_sel, s_own], axis=-1), axis=-1).astype(vb.dtype)
            p_sel = p[..., :n_sel * MOBA_BLOCK].reshape(s_own.shape[:4] + (n_sel, MOBA_BLOCK))
            p_own = p[..., n_sel * MOBA_BLOCK:]
            return (jnp.einsum('bkgqsl,bkgqsld->bqkgd', p_sel, vsel)
                    + jnp.einsum('bkgql,bkqld->bqkgd', p_own, vown))
        p_own = jax.nn.softmax(s_own, axis=-1).astype(vb.dtype)
        return jnp.einsum('bkgql,bkqld->bqkgd', p_own, vown)

    out = lax.map(chunk, xs)
    return jnp.swapaxes(out, 0, 1).reshape(b, nq, N_HEADS * HEAD_DIM)


def run_group(x, conv_in, ssm_in, past_k, past_v, p):
    b, T, _ = x.shape
    start = 0 if past_k is None else past_k.shape[1]
    pos = start + jnp.arange(T, dtype=jnp.int32)
    convs, ssms = [], []
    k_new = v_new = kb = vb = kmean = None
    for l in range(DEPTH):
        if l < N_A_LAYERS:
            conv0 = jnp.zeros((b, D_CONV - 1, CONV_DIM), x.dtype) if conv_in is None else conv_in[l]
            ssm0 = (jnp.zeros((b, N_SSM_HEADS, SSM_HEAD_DIM, D_STATE), jnp.float32)
                    if ssm_in is None else ssm_in[l])
            y, c_new, s_new = mamba_mixer(rmsnorm(x, p['norm_mix'][l]), conv0, ssm0,
                                          p['w_in_ssm'][l], p['conv_w'][l], p['conv_b'][l],
                                          p['dt_bias'][l], p['a_log'][l], p['d_skip'][l],
                                          p['norm_ssm'][l], p['w_out_ssm'][l])
            convs.append(c_new)
            ssms.append(s_new)
        else:
            if l == N_A_LAYERS:
                kv = rmsnorm(x, p['norm_kv']) @ p['w_kv']
                k_new = rope(kv[..., :KV_DIM].reshape(b, T, N_KV_HEADS, HEAD_DIM), pos)
                v_new = kv[..., KV_DIM:].reshape(b, T, N_KV_HEADS, HEAD_DIM)
                k_all = k_new if past_k is None else jnp.concatenate([past_k.astype(k_new.dtype), k_new], axis=1)
                v_all = v_new if past_v is None else jnp.concatenate([past_v.astype(v_new.dtype), v_new], axis=1)
                kb, vb, kmean = moba_blocks(k_all, v_all)
            j = l - N_A_LAYERS
            q = rope((rmsnorm(x, p['norm_mix'][l]) @ p['w_q'][j]).reshape(b, T, N_HEADS, HEAD_DIM), pos)
            y = moba_attend(q, pos, kb, vb, kmean) @ p['w_o'][j]
        x = x + y
        x = x + swiglu_ffn(rmsnorm(x, p['norm_ffn'][l]), p['w_gu'][l], p['w_down'][l])
    return rmsnorm(x, p['norm_final']), jnp.stack(convs), jnp.stack(ssms), k_new, v_new


def setup_inputs(seed: int = 0) -> dict:
    key = jax.random.key(seed)
    ks = jax.random.split(key, 26)
    f32 = jnp.float32
    n_pages = PAST_LEN // PAGE_SIZE
    n_used = DEC_BATCH * n_pages
    n_pool = n_used + (n_used + 3) // 4

    def nrm(k, shape, scale):
        return jax.random.normal(k, shape, f32) * scale

    dt0 = jnp.exp(jax.random.uniform(ks[0], (N_A_LAYERS, N_SSM_HEADS), f32, math.log(1e-3), math.log(1e-1)))
    page_table = jax.random.permutation(ks[1], n_pool)[:n_used].reshape(DEC_BATCH, n_pages).astype(jnp.int32)
    return {
        'x_prompt': nrm(ks[2], (BATCH, SEQ, D_MODEL), 1.0),
        'x_sample': nrm(ks[3], (DEC_BATCH, DEC_SEQ, D_MODEL), 1.0),
        'state_conv': nrm(ks[4], (N_A_LAYERS, DEC_BATCH, D_CONV - 1, CONV_DIM), 1.0),
        'state_ssm': nrm(ks[5], (N_A_LAYERS, DEC_BATCH, N_SSM_HEADS, SSM_HEAD_DIM, D_STATE), 0.1),
        'cache_k': nrm(ks[6], (n_pool, PAGE_SIZE, N_KV_HEADS, HEAD_DIM), 1.0),
        'cache_v': nrm(ks[7], (n_pool, PAGE_SIZE, N_KV_HEADS, HEAD_DIM), 1.0),
        'page_table': page_table,
        'norm_mix': 1.0 + nrm(ks[8], (DEPTH, D_MODEL), 0.02),
        'norm_ffn': 1.0 + nrm(ks[9], (DEPTH, D_MODEL), 0.02),
        'w_in_ssm': nrm(ks[10], (N_A_LAYERS, D_MODEL, IN_DIM), D_MODEL ** -0.5),
        'conv_w': nrm(ks[11], (N_A_LAYERS, D_CONV, CONV_DIM), D_CONV ** -0.5),
        'conv_b': nrm(ks[12], (N_A_LAYERS, CONV_DIM), 0.01),
        'dt_bias': dt0 + jnp.log(-jnp.expm1(-dt0)),
        'a_log': jnp.log(jax.random.uniform(ks[13], (N_A_LAYERS, N_SSM_HEADS), f32, 1.0, 16.0)),
        'd_skip': 1.0 + nrm(ks[14], (N_A_LAYERS, N_SSM_HEADS), 0.1),
        'norm_ssm': 1.0 + nrm(ks[15], (N_A_LAYERS, D_INNER), 0.02),
        'w_out_ssm': nrm(ks[16], (N_A_LAYERS, D_INNER, D_MODEL), D_INNER ** -0.5),
        'norm_kv': 1.0 + nrm(ks[17], (D_MODEL,), 0.02),
        'w_kv': nrm(ks[18], (D_MODEL, 2 * KV_DIM), D_MODEL ** -0.5),
        'w_q': nrm(ks[19], (N_B_LAYERS, D_MODEL, N_HEADS * HEAD_DIM), D_MODEL ** -0.5),
        'w_o': nrm(ks[20], (N_B_LAYERS, N_HEADS * HEAD_DIM, D_MODEL), (N_HEADS * HEAD_DIM) ** -0.5),
        'w_gu': nrm(ks[21], (DEPTH, D_MODEL, 2 * D_FF), D_MODEL ** -0.5),
        'w_down': nrm(ks[22], (DEPTH, D_FF, D_MODEL), D_FF ** -0.5),
        'norm_final': 1.0 + nrm(ks[23], (D_MODEL,), 0.02),
    }


def reference(x_prompt, x_sample, state_conv, state_ssm, cache_k, cache_v, page_table,
              norm_mix, norm_ffn, w_in_ssm, conv_w, conv_b, dt_bias, a_log, d_skip, norm_ssm,
              w_out_ssm, norm_kv, w_kv, w_q, w_o, w_gu, w_down, norm_final):
    p = dict(norm_mix=norm_mix, norm_ffn=norm_ffn, w_in_ssm=w_in_ssm, conv_w=conv_w, conv_b=conv_b,
             dt_bias=dt_bias, a_log=a_log, d_skip=d_skip, norm_ssm=norm_ssm, w_out_ssm=w_out_ssm,
             norm_kv=norm_kv, w_kv=w_kv, w_q=w_q, w_o=w_o, w_gu=w_gu, w_down=w_down,
             norm_final=norm_final)
    y_prompt, conv_prompt, ssm_prompt, k_prompt, v_prompt = run_group(x_prompt, None, None, None, None, p)
    n_seq = page_table.shape[0]
    past_k = cache_k[page_table].reshape(n_seq, -1, N_KV_HEADS, HEAD_DIM)
    past_v = cache_v[page_table].reshape(n_seq, -1, N_KV_HEADS, HEAD_DIM)
    y_sample, conv_sample, ssm_sample, k_sample, v_sample = run_group(
        x_sample, state_conv, state_ssm, past_k, past_v, p)
    return (y_prompt, y_sample, conv_prompt, ssm_prompt, k_prompt, v_prompt,
            conv_sample, ssm_sample, k_sample, v_sample)
```

```python
import functools

import jax
import jax.numpy as jnp
from jax import lax
from jax.experimental import pallas as pl
from jax.experimental.pallas import tpu as pltpu

F32 = jnp.float32
BF16 = jnp.bfloat16

D_MODEL = 1024
D_INNER = 2048
SSM_HEAD_DIM = 64
N_SSM_HEADS = 32
N_SSM_GROUPS = 4
HEADS_PER_GROUP = 8
D_STATE = 128
D_CONV = 4
SSD_CHUNK = 128
GN = N_SSM_GROUPS * D_STATE
CONV_DIM = D_INNER + 2 * GN
HEAD_DIM = 64
N_HEADS = 16
N_KV_HEADS = 4
Q_PER_KV = 4
KV_DIM = N_KV_HEADS * HEAD_DIM
MOBA_BLOCK = 256
MOBA_TOP_K = 3
ROPE_THETA = 10000.0
D_FF = 2816
EPS = 1e-6
PAGE_SIZE = 128

LANES = 128
SUBLANES = 8
VMEM_LIMIT = 56 * 1024 * 1024

NT_DIMS = (((1,), (1,)), ((), ()))
TN_DIMS = (((0,), (0,)), ((), ()))
NEG_INF = float("-inf")


def _params(sem):
    return pltpu.CompilerParams(dimension_semantics=sem, vmem_limit_bytes=VMEM_LIMIT)


def _const_spec(shape):
    nd = len(shape)
    return pl.BlockSpec(shape, lambda *_: (0,) * nd, pipeline_mode=pl.Buffered(1))


def _silu(x):
    return x * (1.0 / (1.0 + jnp.exp(-x)))


def _inv_rms(x):
    return lax.rsqrt(jnp.mean(x * x, axis=-1, keepdims=True) + EPS)


def _split3(x):
    hi = x.astype(BF16)
    r = x - hi.astype(F32)
    mid = r.astype(BF16)
    lo = (r - mid.astype(F32)).astype(BF16)
    return hi, mid, lo


def _dot01(a01, x, dims, a_is_lhs=True):
    out = None
    for p in _split3(x):
        t = (lax.dot_general(a01, p, dims, preferred_element_type=F32) if a_is_lhs
             else lax.dot_general(p, a01, dims, preferred_element_type=F32))
        out = t if out is None else out + t
    return out


def _top3_mask(gate, valid):
    nb = gate.shape[1]
    col = lax.broadcasted_iota(jnp.int32, gate.shape, 1).astype(F32)
    g = jnp.where(valid, gate, NEG_INF)
    sel = jnp.zeros(gate.shape, F32)
    for _ in range(MOBA_TOP_K):
        m = jnp.max(g, axis=1, keepdims=True)
        idx = jnp.min(jnp.where(g == m, col, float(nb)), axis=1, keepdims=True)
        pick = col == idx
        sel = jnp.where(pick, 1.0, sel)
        g = jnp.where(pick, NEG_INF, g)
    return jnp.logical_and(sel > 0.5, valid)


def _inproj_body(x_ref, g_ref, wz_ref, wx_ref, wdt_ref, z_ref, xbc_ref, dt_ref):
    x = x_ref[...]
    h = (x * _inv_rms(x) * g_ref[...]).astype(BF16)
    z_ref[...] = jnp.dot(h, wz_ref[...], preferred_element_type=F32)
    xbc_ref[...] = jnp.dot(h, wx_ref[...], preferred_element_type=F32)
    dt_ref[...] = jnp.dot(h, wdt_ref[...], preferred_element_type=F32)


def _inproj(x, g, wz, wx, wdt, tm):
    m = x.shape[0]
    row = lambda w: pl.BlockSpec((tm, w), lambda i: (i, 0))
    return pl.pallas_call(
        _inproj_body,
        grid=(m // tm,),
        in_specs=[row(D_MODEL), _const_spec((1, D_MODEL)), _const_spec(wz.shape),
                  _const_spec(wx.shape), _const_spec(wdt.shape)],
        out_specs=[row(D_INNER), row(CONV_DIM), row(LANES)],
        out_shape=[jax.ShapeDtypeStruct((m, D_INNER), F32),
                   jax.ShapeDtypeStruct((m, CONV_DIM), F32),
                   jax.ShapeDtypeStruct((m, LANES), F32)],
        compiler_params=_params(("parallel",)),
        name="ssd_inproj",
    )(x, g, wz, wx, wdt)


def _ssd_body(xbc_ref, dtr_ref, conv0_ref, ssm0_ref, cw_ref, cb_ref, dtb_ref, alog_ref, dsk_ref,
              y_ref, st_ref, pad_ref, *, L, valid):
    c = pl.program_id(1)

    @pl.when(c == 0)
    def _():
        pad_ref[0:SUBLANES, :] = conv0_ref[0]
        st_ref[0] = ssm0_ref[0]

    pad_ref[SUBLANES:SUBLANES + L, :] = xbc_ref[0]
    conv = cb_ref[...]
    for i in range(D_CONV):
        off = SUBLANES - (D_CONV - 1) + i
        conv = conv + pad_ref[off:off + L, :] * cw_ref[i:i + 1, :]
    pad_ref[0:SUBLANES, :] = pad_ref[L:L + SUBLANES, :]
    act = _silu(conv)
    xs = act[:, :D_INNER]
    bm = act[:, D_INNER:D_INNER + GN].astype(BF16)
    cm = act[:, D_INNER + GN:].astype(BF16)

    t = dtr_ref[0] + dtb_ref[...]
    dt = jnp.maximum(t, 0.0) + jnp.log1p(jnp.exp(-jnp.abs(t)))
    if valid < L:
        rows = lax.broadcasted_iota(jnp.int32, dt.shape, 0)
        dt = jnp.where(rows < valid, dt, 0.0)
    a = -jnp.exp(alog_ref[...])
    dta = dt * a

    r_i = lax.broadcasted_iota(jnp.int32, (L, L), 0)
    c_i = lax.broadcasted_iota(jnp.int32, (L, L), 1)
    causal = r_i >= c_i
    tril = jnp.where(causal, 1.0, 0.0).astype(BF16)
    e_r = lax.broadcasted_iota(jnp.int32, (LANES, LANES), 0)
    e_c = lax.broadcasted_iota(jnp.int32, (LANES, LANES), 1)
    eye = jnp.where(e_r == e_c, 1.0, 0.0).astype(BF16)

    acs = _dot01(tril, dta, (((1,), (0,)), ((), ())))
    acs_t = _dot01(eye, acs, NT_DIMS)
    last = acs[L - 1:L, :]
    dec_end = jnp.exp(last - acs)
    e_acs = jnp.exp(acs)
    dec_chunk = jnp.exp(last)

    for g in range(N_SSM_GROUPS):
        bg = bm[:, g * D_STATE:(g + 1) * D_STATE]
        cg = cm[:, g * D_STATE:(g + 1) * D_STATE]
        cb = lax.dot_general(cg, bg, NT_DIMS, preferred_element_type=F32)
        for jp in range(HEADS_PER_GROUP // 2):
            pair = []
            for h in (g * HEADS_PER_GROUP + 2 * jp, g * HEADS_PER_GROUP + 2 * jp + 1):
                diff = acs[:, h:h + 1] - acs_t[h:h + 1, :]
                seg = jnp.exp(jnp.where(causal, diff, NEG_INF))
                w = (cb * seg).astype(BF16)
                xs_h = xs[:, h * SSM_HEAD_DIM:(h + 1) * SSM_HEAD_DIM]
                xd_h = xs_h * dt[:, h:h + 1]
                y_diag = jnp.dot(w, xd_h.astype(BF16), preferred_element_type=F32)
                st_h = st_ref[0, h]
                y_off = lax.dot_general(cg, st_h.astype(BF16), NT_DIMS,
                                        preferred_element_type=F32) * e_acs[:, h:h + 1]
                xdd = (xd_h * dec_end[:, h:h + 1]).astype(BF16)
                cs = lax.dot_general(xdd, bg, TN_DIMS, preferred_element_type=F32)
                st_ref[0, h] = st_h * dec_chunk[:, h:h + 1] + cs
                pair.append(y_diag + y_off
                            + dsk_ref[:, h * SSM_HEAD_DIM:(h + 1) * SSM_HEAD_DIM] * xs_h)
            lo = (g * HEADS_PER_GROUP + 2 * jp) * SSM_HEAD_DIM
            y_ref[0, :, lo:lo + 2 * SSM_HEAD_DIM] = jnp.concatenate(pair, axis=1)


def _ssd(xbc, dtr, conv0, ssm0, cw, cb, dtb, alog, dsk, L, valid):
    nb, t = xbc.shape[0], xbc.shape[1]
    nc = t // L
    st_spec = pl.BlockSpec((1, N_SSM_HEADS, SSM_HEAD_DIM, D_STATE), lambda b, c: (b, 0, 0, 0))
    return pl.pallas_call(
        functools.partial(_ssd_body, L=L, valid=valid),
        grid=(nb, nc),
        in_specs=[pl.BlockSpec((1, L, CONV_DIM), lambda b, c: (b, c, 0)),
                  pl.BlockSpec((1, L, LANES), lambda b, c: (b, c, 0)),
                  pl.BlockSpec((1, SUBLANES, CONV_DIM), lambda b, c: (b, 0, 0)),
                  st_spec,
                  _const_spec(cw.shape), _const_spec(cb.shape), _const_spec(dtb.shape),
                  _const_spec(alog.shape), _const_spec(dsk.shape)],
        out_specs=[pl.BlockSpec((1, L, D_INNER), lambda b, c: (b, c, 0)), st_spec],
        out_shape=[jax.ShapeDtypeStruct((nb, t, D_INNER), F32),
                   jax.ShapeDtypeStruct(ssm0.shape, F32)],
        scratch_shapes=[pltpu.VMEM((L + SUBLANES, CONV_DIM), F32)],
        compiler_params=_params(("parallel", "arbitrary")),
        name="ssd_scan",
    )(xbc, dtr, conv0, ssm0, cw, cb, dtb, alog, dsk)


def _ffn(x1, nf_ref, wg_ref, wu_ref, wd_ref):
    h = (x1 * _inv_rms(x1) * nf_ref[...]).astype(BF16)
    gate = jnp.dot(h, wg_ref[...], preferred_element_type=F32)
    up = jnp.dot(h, wu_ref[...], preferred_element_type=F32)
    act = (_silu(gate) * up).astype(BF16)
    return x1 + jnp.dot(act, wd_ref[...], preferred_element_type=F32)


def _ssm_out_ffn_body(y_ref, z_ref, x_ref, gn_ref, wo_ref, nf_ref, wg_ref, wu_ref, wd_ref, o_ref):
    y = y_ref[...] * _silu(z_ref[...])
    gw = D_INNER // N_SSM_GROUPS
    parts = []
    for g in range(N_SSM_GROUPS):
        yg = y[:, g * gw:(g + 1) * gw]
        parts.append(yg * _inv_rms(yg))
    y = (jnp.concatenate(parts, axis=1) * gn_ref[...]).astype(BF16)
    x1 = x_ref[...] + jnp.dot(y, wo_ref[...], preferred_element_type=F32)
    o_ref[...] = _ffn(x1, nf_ref, wg_ref, wu_ref, wd_ref)


def _attn_out_ffn_body(a_ref, x_ref, wo_ref, nf_ref, wg_ref, wu_ref, wd_ref, nfin_ref, o_ref):
    x1 = x_ref[...] + jnp.dot(a_ref[...], wo_ref[...], preferred_element_type=F32)
    x2 = _ffn(x1, nf_ref, wg_ref, wu_ref, wd_ref)
    o_ref[...] = x2 * _inv_rms(x2) * nfin_ref[...]


def _ssm_out_ffn(y, z, x, gn, wo, nf, wg, wu, wd, tm):
    m = x.shape[0]
    row = lambda w: pl.BlockSpec((tm, w), lambda i: (i, 0))
    return pl.pallas_call(
        _ssm_out_ffn_body,
        grid=(m // tm,),
        in_specs=[row(D_INNER), row(D_INNER), row(D_MODEL), _const_spec(gn.shape),
                  _const_spec(wo.shape), _const_spec(nf.shape), _const_spec(wg.shape),
                  _const_spec(wu.shape), _const_spec(wd.shape)],
        out_specs=row(D_MODEL),
        out_shape=jax.ShapeDtypeStruct((m, D_MODEL), F32),
        compiler_params=_params(("parallel",)),
        name="ssm_out_ffn",
    )(y, z, x, gn, wo, nf, wg, wu, wd)


def _attn_out_ffn(a, x, wo, nf, wg, wu, wd, nfin, tm):
    m = x.shape[0]
    row = lambda w: pl.BlockSpec((tm, w), lambda i: (i, 0))
    return pl.pallas_call(
        _attn_out_ffn_body,
        grid=(m // tm,),
        in_specs=[row(D_MODEL), row(D_MODEL), _const_spec(wo.shape), _const_spec(nf.shape),
                  _const_spec(wg.shape), _const_spec(wu.shape), _const_spec(wd.shape),
                  _const_spec(nfin.shape)],
        out_specs=row(D_MODEL),
        out_shape=jax.ShapeDtypeStruct((m, D_MODEL), F32),
        compiler_params=_params(("parallel",)),
        name="attn_out_ffn",
    )(a, x, wo, nf, wg, wu, wd, nfin)


def _rope(x, cos, sin_signed, first_half):
    outs = []
    for c in range(x.shape[1] // LANES):
        xc = x[:, c * LANES:(c + 1) * LANES]
        partner = jnp.where(first_half, pltpu.roll(xc, LANES - HEAD_DIM // 2, 1),
                            pltpu.roll(xc, HEAD_DIM // 2, 1))
        outs.append(xc * cos + partner * sin_signed)
    return jnp.concatenate(outs, axis=1)


def _kvq_body(x_ref, gkv_ref, gq_ref, wkv_ref, wq_ref, cos_ref, sin_ref, k_ref, v_ref, q_ref,
              *km_refs, n_means):
    x = x_ref[...]
    xn = x * _inv_rms(x)
    hk = (xn * gkv_ref[...]).astype(BF16)
    hq = (xn * gq_ref[...]).astype(BF16)
    cos = cos_ref[...]
    sin = sin_ref[...]
    lane = lax.broadcasted_iota(jnp.int32, cos.shape, 1)
    first_half = (lane % HEAD_DIM) < (HEAD_DIM // 2)
    kv = jnp.dot(hk, wkv_ref[...], preferred_element_type=F32)
    k = _rope(kv[:, :KV_DIM], cos, sin, first_half)
    k_ref[...] = k
    v_ref[...] = kv[:, KV_DIM:]
    q_ref[...] = _rope(jnp.dot(hq, wq_ref[...], preferred_element_type=F32), cos, sin, first_half)
    if n_means:
        km_ref = km_refs[0]
        for i in range(n_means):
            blk = k[i * MOBA_BLOCK:(i + 1) * MOBA_BLOCK, :]
            km_ref[i] = jnp.sum(blk, axis=0, keepdims=True) * (1.0 / MOBA_BLOCK)


def _kvq(x, gkv, gq, wkv, wq, cos, sin, tm, with_means):
    m = x.shape[0]
    n_tab = cos.shape[0] // tm
    row = lambda w: pl.BlockSpec((tm, w), lambda i: (i, 0))
    tab = pl.BlockSpec((tm, LANES), lambda i: (i % n_tab, 0))
    n_means = tm // MOBA_BLOCK if with_means else 0
    out_specs = [row(KV_DIM), row(KV_DIM), row(D_MODEL)]
    out_shape = [jax.ShapeDtypeStruct((m, KV_DIM), F32), jax.ShapeDtypeStruct((m, KV_DIM), F32),
                 jax.ShapeDtypeStruct((m, D_MODEL), F32)]
    if n_means:
        out_specs.append(pl.BlockSpec((n_means, 1, KV_DIM), lambda i: (i, 0, 0)))
        out_shape.append(jax.ShapeDtypeStruct((m // MOBA_BLOCK, 1, KV_DIM), F32))
    return pl.pallas_call(
        functools.partial(_kvq_body, n_means=n_means),
        grid=(m // tm,),
        in_specs=[row(D_MODEL), _const_spec(gkv.shape), _const_spec(gq.shape),
                  _const_spec(wkv.shape), _const_spec(wq.shape), tab, tab],
        out_specs=out_specs,
        out_shape=out_shape,
        compiler_params=_params(("parallel",)),
        name="kvq_proj",
    )(x, gkv, gq, wkv, wq, cos, sin)


def _moba_prompt_body(q_ref, k_ref, v_ref, km_ref, o_ref, m_scr, l_scr, acc_scr, pen_scr, *, nb):
    i = pl.program_id(1)
    rows = Q_PER_KV * MOBA_BLOCK
    scale = HEAD_DIM ** -0.5
    blk_col = lax.broadcasted_iota(jnp.int32, (rows, nb), 1)
    r_i = lax.broadcasted_iota(jnp.int32, (rows, MOBA_BLOCK), 0)
    c_i = lax.broadcasted_iota(jnp.int32, (rows, MOBA_BLOCK), 1)
    own_ok = c_i <= (r_i % MOBA_BLOCK)
    own_start = pl.multiple_of(i * MOBA_BLOCK, MOBA_BLOCK)

    for kvh in range(N_KV_HEADS):
        lo = kvh * HEAD_DIM
        q4 = q_ref[0, :, kvh * Q_PER_KV * HEAD_DIM:(kvh + 1) * Q_PER_KV * HEAD_DIM]
        qf = jnp.concatenate([q4[:, g * HEAD_DIM:(g + 1) * HEAD_DIM] for g in range(Q_PER_KV)],
                             axis=0)
        km = km_ref[0, :, lo:lo + HEAD_DIM]
        gate = lax.dot_general(qf, km, NT_DIMS, preferred_element_type=F32,
                               precision=lax.Precision.HIGHEST)
        sel = _top3_mask(gate, blk_col < i)
        pen_scr[...] = jnp.where(sel, 0.0, NEG_INF)
        qb = (qf * scale).astype(BF16)

        k_own = k_ref[0, pl.ds(own_start, MOBA_BLOCK), lo:lo + HEAD_DIM].astype(BF16)
        v_own = v_ref[0, pl.ds(own_start, MOBA_BLOCK), lo:lo + HEAD_DIM].astype(BF16)
        s = lax.dot_general(qb, k_own, NT_DIMS, preferred_element_type=F32)
        s = jnp.where(own_ok, s, NEG_INF)
        m0 = jnp.max(s, axis=1, keepdims=True)
        p = jnp.exp(s - m0)
        m_scr[...] = m0
        l_scr[...] = jnp.sum(p, axis=1, keepdims=True)
        acc_scr[...] = jnp.dot(p.astype(BF16), v_own, preferred_element_type=F32)

        def step(j, carry):
            start = pl.multiple_of(j * MOBA_BLOCK, MOBA_BLOCK)
            k_j = k_ref[0, pl.ds(start, MOBA_BLOCK), lo:lo + HEAD_DIM].astype(BF16)
            v_j = v_ref[0, pl.ds(start, MOBA_BLOCK), lo:lo + HEAD_DIM].astype(BF16)
            pen = jnp.max(jnp.where(blk_col == j, pen_scr[...], NEG_INF), axis=1, keepdims=True)
            sj = lax.dot_general(qb, k_j, NT_DIMS, preferred_element_type=F32) + pen
            m_old = m_scr[...]
            m_new = jnp.maximum(m_old, jnp.max(sj, axis=1, keepdims=True))
            alpha = jnp.exp(m_old - m_new)
            pj = jnp.exp(sj - m_new)
            m_scr[...] = m_new
            l_scr[...] = alpha * l_scr[...] + jnp.sum(pj, axis=1, keepdims=True)
            acc_scr[...] = alpha * acc_scr[...] + jnp.dot(pj.astype(BF16), v_j,
                                                          preferred_element_type=F32)
            return carry

        lax.fori_loop(0, i, step, 0)
        out = acc_scr[...] / l_scr[...]
        o_ref[0, :, kvh * Q_PER_KV * HEAD_DIM:(kvh + 1) * Q_PER_KV * HEAD_DIM] = jnp.concatenate(
            [out[g * MOBA_BLOCK:(g + 1) * MOBA_BLOCK, :] for g in range(Q_PER_KV)],
            axis=1).astype(o_ref.dtype)


def _moba_prompt(q, k, v, km):
    b, t = q.shape[0], q.shape[1]
    nb = t // MOBA_BLOCK
    rows = Q_PER_KV * MOBA_BLOCK
    return pl.pallas_call(
        functools.partial(_moba_prompt_body, nb=nb),
        grid=(b, nb),
        in_specs=[pl.BlockSpec((1, MOBA_BLOCK, D_MODEL), lambda bi, i: (bi, i, 0)),
                  pl.BlockSpec((1, t, KV_DIM), lambda bi, i: (bi, 0, 0)),
                  pl.BlockSpec((1, t, KV_DIM), lambda bi, i: (bi, 0, 0)),
                  pl.BlockSpec((1, nb, KV_DIM), lambda bi, i: (bi, 0, 0))],
        out_specs=pl.BlockSpec((1, MOBA_BLOCK, D_MODEL), lambda bi, i: (bi, i, 0)),
        out_shape=jax.ShapeDtypeStruct((b, t, D_MODEL), BF16),
        scratch_shapes=[pltpu.VMEM((rows, 1), F32), pltpu.VMEM((rows, 1), F32),
                        pltpu.VMEM((rows, HEAD_DIM), F32), pltpu.VMEM((rows, nb), F32)],
        compiler_params=_params(("parallel", "arbitrary")),
        name="moba_prompt",
    )(q, k, v, km)


def _page_copy(cache_hbm, buf, sem, page, slot):
    return pltpu.make_async_copy(cache_hbm.at[page], buf.at[pl.ds(slot * PAGE_SIZE, PAGE_SIZE)], sem)


def _moba_sample_body(pt_ref, q_ref, kn_ref, vn_ref, ck_hbm, cv_hbm, o_ref,
                      kbuf, vbuf, s_scr, km_scr, new_scr, sems, *, n_pages, n_new):
    b = pl.program_id(0)
    rows = q_ref.shape[1]
    nb = n_pages * PAGE_SIZE // MOBA_BLOCK
    scale = HEAD_DIM ** -0.5

    def start_page(p, carry):
        _page_copy(ck_hbm, kbuf, sems.at[0], pt_ref[b, p], p).start()
        _page_copy(cv_hbm, vbuf, sems.at[1], pt_ref[b, p], p).start()
        return carry

    lax.fori_loop(0, n_pages, start_page, 0)

    q64 = q_ref[0]
    qt = jnp.concatenate([q64] * N_KV_HEADS, axis=1)
    r_i = lax.broadcasted_iota(jnp.int32, qt.shape, 0)
    c_i = lax.broadcasted_iota(jnp.int32, qt.shape, 1)
    qpad = jnp.where(c_i // HEAD_DIM == r_i // (Q_PER_KV * n_new), qt, 0.0)
    qb = (qpad * scale).astype(BF16)

    def wait_k(p, carry):
        _page_copy(ck_hbm, kbuf, sems.at[0], 0, p).wait()
        return carry

    lax.fori_loop(0, n_pages, wait_k, 0)

    for j in range(nb):
        kj = kbuf[j * MOBA_BLOCK:(j + 1) * MOBA_BLOCK, :]
        km_scr[j:j + 1, :] = jnp.sum(kj, axis=0, keepdims=True) * (1.0 / MOBA_BLOCK)
        s_scr[:, j * MOBA_BLOCK:(j + 1) * MOBA_BLOCK] = lax.dot_general(
            qb, kj.astype(BF16), NT_DIMS, preferred_element_type=F32)

    gate = lax.dot_general(qpad, km_scr[...], NT_DIMS, preferred_element_type=F32,
                           precision=lax.Precision.HIGHEST)
    sel = _top3_mask(gate, jnp.full(gate.shape, True))
    pen = jnp.where(sel, 0.0, NEG_INF)

    new_scr[...] = jnp.zeros(new_scr.shape, F32)
    new_scr[0:n_new, :] = kn_ref[0]
    s_new = lax.dot_general(qb, new_scr[...].astype(BF16), NT_DIMS, preferred_element_type=F32)
    nr = lax.broadcasted_iota(jnp.int32, s_new.shape, 0)
    ncol = lax.broadcasted_iota(jnp.int32, s_new.shape, 1)
    s_new = jnp.where(ncol <= nr % n_new, s_new, NEG_INF)

    m = jnp.max(s_new, axis=1, keepdims=True)
    for j in range(nb):
        sj = s_scr[:, j * MOBA_BLOCK:(j + 1) * MOBA_BLOCK] + pen[:, j:j + 1]
        m = jnp.maximum(m, jnp.max(sj, axis=1, keepdims=True))

    def wait_v(p, carry):
        _page_copy(cv_hbm, vbuf, sems.at[1], 0, p).wait()
        return carry

    lax.fori_loop(0, n_pages, wait_v, 0)

    p_new = jnp.exp(s_new - m)
    l = jnp.sum(p_new, axis=1, keepdims=True)
    new_scr[0:n_new, :] = vn_ref[0]
    acc = jnp.dot(p_new.astype(BF16), new_scr[...].astype(BF16), preferred_element_type=F32)
    for j in range(nb):
        pj = jnp.exp(s_scr[:, j * MOBA_BLOCK:(j + 1) * MOBA_BLOCK] + pen[:, j:j + 1] - m)
        l = l + jnp.sum(pj, axis=1, keepdims=True)
        vj = vbuf[j * MOBA_BLOCK:(j + 1) * MOBA_BLOCK, :].astype(BF16)
        acc = acc + jnp.dot(pj.astype(BF16), vj, preferred_element_type=F32)
    acc = acc / l
    ro = lax.broadcasted_iota(jnp.int32, (rows, HEAD_DIM), 0) // (Q_PER_KV * n_new)
    out = jnp.zeros((rows, HEAD_DIM), F32)
    for kvh in range(N_KV_HEADS):
        out = jnp.where(ro == kvh, acc[:, kvh * HEAD_DIM:(kvh + 1) * HEAD_DIM], out)
    o_ref[0] = out


def _moba_sample(page_table, q_rows, k_new, v_new, cache_k, cache_v):
    nseq, n_pages = page_table.shape
    rows = q_rows.shape[1]
    n_new = k_new.shape[1]
    past = n_pages * PAGE_SIZE
    grid_spec = pltpu.PrefetchScalarGridSpec(
        num_scalar_prefetch=1,
        grid=(nseq,),
        in_specs=[pl.BlockSpec((1, rows, HEAD_DIM), lambda b, pt: (b, 0, 0)),
                  pl.BlockSpec((1, n_new, KV_DIM), lambda b, pt: (b, 0, 0)),
                  pl.BlockSpec((1, n_new, KV_DIM), lambda b, pt: (b, 0, 0)),
                  pl.BlockSpec(memory_space=pl.ANY),
                  pl.BlockSpec(memory_space=pl.ANY)],
        out_specs=pl.BlockSpec((1, rows, HEAD_DIM), lambda b, pt: (b, 0, 0)),
        scratch_shapes=[pltpu.VMEM((past, KV_DIM), F32), pltpu.VMEM((past, KV_DIM), F32),
                        pltpu.VMEM((rows, past), F32),
                        pltpu.VMEM((past // MOBA_BLOCK, KV_DIM), F32),
                        pltpu.VMEM((LANES, KV_DIM), F32),
                        pltpu.SemaphoreType.DMA((2,))],
    )
    return pl.pallas_call(
        functools.partial(_moba_sample_body, n_pages=n_pages, n_new=n_new),
        grid_spec=grid_spec,
        out_shape=jax.ShapeDtypeStruct((nseq, rows, HEAD_DIM), F32),
        compiler_params=_params(("arbitrary",)),
        name="moba_sample",
    )(page_table, q_rows, k_new, v_new, cache_k, cache_v)


def _rope_tables(pos):
    half = HEAD_DIM // 2
    inv = ROPE_THETA ** (-jnp.arange(half, dtype=F32) / half)
    ang = pos.astype(F32)[:, None] * inv[None, :]
    cos, sin = jnp.cos(ang), jnp.sin(ang)
    cos_h = jnp.concatenate([cos, cos], axis=1)
    sin_h = jnp.concatenate([-sin, sin], axis=1)
    reps = LANES // HEAD_DIM
    return jnp.tile(cos_h, (1, reps)), jnp.tile(sin_h, (1, reps))


def _prep_weights(norm_mix, norm_ffn, w_in_ssm, conv_w, conv_b, dt_bias, a_log, d_skip, norm_ssm,
                  w_out_ssm, norm_kv, w_kv, w_q, w_o, w_gu, w_down, norm_final):
    pad_h = LANES - N_SSM_HEADS
    w_in = w_in_ssm[0]
    return dict(
        g_mix0=norm_mix[0][None], g_mix1=norm_mix[1][None],
        g_ffn0=norm_ffn[0][None], g_ffn1=norm_ffn[1][None],
        wz=w_in[:, :D_INNER].astype(BF16),
        wx=w_in[:, D_INNER:D_INNER + CONV_DIM].astype(BF16),
        wdt=jnp.pad(w_in[:, D_INNER + CONV_DIM:], ((0, 0), (0, pad_h))).astype(BF16),
        cw=conv_w[0], cb=conv_b[0][None],
        dtb=jnp.pad(dt_bias[0], (0, pad_h))[None], alog=jnp.pad(a_log[0], (0, pad_h))[None],
        dsk=jnp.repeat(d_skip[0], SSM_HEAD_DIM)[None],
        gn=norm_ssm[0][None], wo_ssm=w_out_ssm[0].astype(BF16),
        g_kv=norm_kv[None], wkv=w_kv.astype(BF16), wq=w_q[0].astype(BF16),
        wo=w_o[0].astype(BF16),
        wg0=w_gu[0][:, :D_FF].astype(BF16), wu0=w_gu[0][:, D_FF:].astype(BF16),
        wd0=w_down[0].astype(BF16),
        wg1=w_gu[1][:, :D_FF].astype(BF16), wu1=w_gu[1][:, D_FF:].astype(BF16),
        wd1=w_down[1].astype(BF16),
        g_fin=norm_final[None],
    )


def _ssd_layer(x, conv_in, ssm0, w, tm, L, valid):
    b, t, _ = x.shape
    xf = x.reshape(b * t, D_MODEL)
    z, xbc, dtr = _inproj(xf, w["g_mix0"], w["wz"], w["wx"], w["wdt"], tm)
    tp = -(-t // L) * L
    xbc3 = xbc.reshape(b, t, CONV_DIM)
    dtr3 = dtr.reshape(b, t, LANES)
    if tp != t:
        xbc3 = jnp.pad(xbc3, ((0, 0), (0, tp - t), (0, 0)))
        dtr3 = jnp.pad(dtr3, ((0, 0), (0, tp - t), (0, 0)))
    conv0 = jnp.pad(conv_in, ((0, 0), (SUBLANES - (D_CONV - 1), 0), (0, 0)))
    y, ssm_new = _ssd(xbc3, dtr3, conv0, ssm0, w["cw"], w["cb"], w["dtb"], w["alog"], w["dsk"],
                      L, valid)
    y = y[:, :t].reshape(b * t, D_INNER)
    x2 = _ssm_out_ffn(y, z, xf, w["gn"], w["wo_ssm"], w["g_ffn0"], w["wg0"], w["wu0"], w["wd0"], tm)
    keep = D_CONV - 1
    conv_new = jnp.concatenate([conv_in[:, t:], xbc.reshape(b, t, CONV_DIM)[:, max(0, t - keep):]],
                               axis=1)
    return x2, conv_new, ssm_new


def kernel(x_prompt, x_sample, state_conv, state_ssm, cache_k, cache_v, page_table, norm_mix,
           norm_ffn, w_in_ssm, conv_w, conv_b, dt_bias, a_log, d_skip, norm_ssm, w_out_ssm,
           norm_kv, w_kv, w_q, w_o, w_gu, w_down, norm_final):
    w = _prep_weights(norm_mix, norm_ffn, w_in_ssm, conv_w, conv_b, dt_bias, a_log, d_skip,
                      norm_ssm, w_out_ssm, norm_kv, w_kv, w_q, w_o, w_gu, w_down, norm_final)
    bp, tp, _ = x_prompt.shape
    bs, ts, _ = x_sample.shape
    past_len = page_table.shape[1] * PAGE_SIZE

    tm_p = 256
    conv0_p = jnp.zeros((bp, D_CONV - 1, CONV_DIM), F32)
    ssm0_p = jnp.zeros((bp, N_SSM_HEADS, SSM_HEAD_DIM, D_STATE), F32)
    x2_p, conv_p, ssm_p = _ssd_layer(x_prompt, conv0_p, ssm0_p, w, tm_p, SSD_CHUNK, SSD_CHUNK)
    cos_p, sin_p = _rope_tables(jnp.arange(tp, dtype=jnp.int32))
    k_p, v_p, q_p, km_p = _kvq(x2_p, w["g_kv"], w["g_mix1"], w["wkv"], w["wq"], cos_p, sin_p,
                               tm_p, True)
    attn_p = _moba_prompt(q_p.reshape(bp, tp, D_MODEL), k_p.reshape(bp, tp, KV_DIM),
                          v_p.reshape(bp, tp, KV_DIM), km_p.reshape(bp, tp // MOBA_BLOCK, KV_DIM))
    y_p = _attn_out_ffn(attn_p.reshape(bp * tp, D_MODEL), x2_p, w["wo"], w["g_ffn1"], w["wg1"],
                        w["wu1"], w["wd1"], w["g_fin"], tm_p)

    tm_s = bs * ts
    x2_s, conv_s, ssm_s = _ssd_layer(x_sample, state_conv[0], state_ssm[0], w, tm_s, SUBLANES, ts)
    pos_s = past_len + jnp.tile(jnp.arange(ts, dtype=jnp.int32), bs)
    cos_s, sin_s = _rope_tables(pos_s)
    k_s, v_s, q_s = _kvq(x2_s, w["g_kv"], w["g_mix1"], w["wkv"], w["wq"], cos_s, sin_s, tm_s, False)
    q_rows = q_s.reshape(bs, ts, N_HEADS, HEAD_DIM).transpose(0, 2, 1, 3).reshape(
        bs, N_HEADS * ts, HEAD_DIM)
    q_rows = jnp.pad(q_rows, ((0, 0), (0, LANES - N_HEADS * ts), (0, 0)))
    attn_rows = _moba_sample(page_table, q_rows, k_s.reshape(bs, ts, KV_DIM),
                             v_s.reshape(bs, ts, KV_DIM),
                             cache_k.reshape(cache_k.shape[0], PAGE_SIZE, KV_DIM),
                             cache_v.reshape(cache_v.shape[0], PAGE_SIZE, KV_DIM))
    attn_s = attn_rows[:, :N_HEADS * ts].reshape(bs, N_HEADS, ts, HEAD_DIM).transpose(
        0, 2, 1, 3).reshape(bs * ts, D_MODEL).astype(BF16)
    y_s = _attn_out_ffn(attn_s, x2_s, w["wo"], w["g_ffn1"], w["wg1"], w["wu1"], w["wd1"],
                        w["g_fin"], tm_s)

    return (y_p.reshape(bp, tp, D_MODEL), y_s.reshape(bs, ts, D_MODEL),
            conv_p[None], ssm_p[None],
            k_p.reshape(bp, tp, N_KV_HEADS, HEAD_DIM), v_p.reshape(bp, tp, N_KV_HEADS, HEAD_DIM),
            conv_s[None], ssm_s[None],
            k_s.reshape(bs, ts, N_KV_HEADS, HEAD_DIM), v_s.reshape(bs, ts, N_KV_HEADS, HEAD_DIM))
```

```python
import functools

import jax
import jax.numpy as jnp
from jax import lax
from jax.experimental import pallas as pl
from jax.experimental.pallas import tpu as pltpu

F32 = jnp.float32
BF16 = jnp.bfloat16

D_MODEL = 1024
D_INNER = 2048
SSM_HEAD_DIM = 64
N_SSM_HEADS = 32
N_SSM_GROUPS = 4
HEADS_PER_GROUP = 8
D_STATE = 128
D_CONV = 4
SSD_CHUNK = 128
GN = N_SSM_GROUPS * D_STATE
CONV_DIM = D_INNER + 2 * GN
HEAD_DIM = 64
N_HEADS = 16
N_KV_HEADS = 4
Q_PER_KV = 4
KV_DIM = N_KV_HEADS * HEAD_DIM
MOBA_BLOCK = 256
MOBA_TOP_K = 3
ROPE_THETA = 10000.0
D_FF = 2816
EPS = 1e-6
PAGE_SIZE = 128

LANES = 128
SUBLANES = 8
VMEM_LIMIT = 56 * 1024 * 1024

NT_DIMS = (((1,), (1,)), ((), ()))
TN_DIMS = (((0,), (0,)), ((), ()))
NEG_INF = float("-inf")


def _params(sem):
    return pltpu.CompilerParams(dimension_semantics=sem, vmem_limit_bytes=VMEM_LIMIT)


def _const_spec(shape):
    nd = len(shape)
    return pl.BlockSpec(shape, lambda *_: (0,) * nd, pipeline_mode=pl.Buffered(1))


def _silu(x):
    return x * (1.0 / (1.0 + jnp.exp(-x)))


def _inv_rms(x):
    return lax.rsqrt(jnp.mean(x * x, axis=-1, keepdims=True) + EPS)


def _split3(x):
    hi = x.astype(BF16)
    r = x - hi.astype(F32)
    mid = r.astype(BF16)
    lo = (r - mid.astype(F32)).astype(BF16)
    return hi, mid, lo


def _dot01(a01, x, dims, a_is_lhs=True):
    out = None
    for p in _split3(x):
        t = (lax.dot_general(a01, p, dims, preferred_element_type=F32) if a_is_lhs
             else lax.dot_general(p, a01, dims, preferred_element_type=F32))
        out = t if out is None else out + t
    return out


def _top3_mask(gate, valid, axis=1):
    nb = gate.shape[axis]
    col = lax.broadcasted_iota(jnp.int32, gate.shape, axis).astype(F32)
    g = jnp.where(valid, gate, NEG_INF)
    sel = jnp.zeros(gate.shape, F32)
    for _ in range(MOBA_TOP_K):
        m = jnp.max(g, axis=axis, keepdims=True)
        idx = jnp.min(jnp.where(g == m, col, float(nb)), axis=axis, keepdims=True)
        pick = col == idx
        sel = jnp.where(pick, 1.0, sel)
        g = jnp.where(pick, NEG_INF, g)
    return jnp.logical_and(sel > 0.5, valid)


def _inproj_body(x_ref, g_ref, wz_ref, wx_ref, wdt_ref, z_ref, xbc_ref, dt_ref):
    x = x_ref[...]
    h = (x * _inv_rms(x) * g_ref[...]).astype(BF16)
    z_ref[...] = jnp.dot(h, wz_ref[...], preferred_element_type=F32)
    xbc_ref[...] = jnp.dot(h, wx_ref[...], preferred_element_type=F32)
    dt_ref[...] = jnp.dot(h, wdt_ref[...], preferred_element_type=F32)


def _inproj(x, g, wz, wx, wdt, tm):
    m = x.shape[0]
    row = lambda w: pl.BlockSpec((tm, w), lambda i: (i, 0))
    return pl.pallas_call(
        _inproj_body,
        grid=(m // tm,),
        in_specs=[row(D_MODEL), _const_spec((1, D_MODEL)), _const_spec(wz.shape),
                  _const_spec(wx.shape), _const_spec(wdt.shape)],
        out_specs=[row(D_INNER), row(CONV_DIM), row(LANES)],
        out_shape=[jax.ShapeDtypeStruct((m, D_INNER), F32),
                   jax.ShapeDtypeStruct((m, CONV_DIM), F32),
                   jax.ShapeDtypeStruct((m, LANES), F32)],
        compiler_params=_params(("parallel",)),
        name="ssd_inproj",
    )(x, g, wz, wx, wdt)


def _ssd_body(xbc_ref, dtr_ref, conv0_ref, ssm0_ref, cw_ref, cb_ref, dtb_ref, alog_ref, dsk_ref,
              y_ref, st_ref, pad_ref, *, L, valid):
    c = pl.program_id(1)

    @pl.when(c == 0)
    def _():
        pad_ref[0:SUBLANES, :] = conv0_ref[0]
        st_ref[0] = ssm0_ref[0]

    pad_ref[SUBLANES:SUBLANES + L, :] = xbc_ref[0]
    conv = cb_ref[...]
    for i in range(D_CONV):
        off = SUBLANES - (D_CONV - 1) + i
        conv = conv + pad_ref[off:off + L, :] * cw_ref[i:i + 1, :]
    pad_ref[0:SUBLANES, :] = pad_ref[L:L + SUBLANES, :]
    act = _silu(conv)
    xs = act[:, :D_INNER]
    bm = act[:, D_INNER:D_INNER + GN].astype(BF16)
    cm = act[:, D_INNER + GN:].astype(BF16)

    t = dtr_ref[0] + dtb_ref[...]
    dt = jnp.maximum(t, 0.0) + jnp.log1p(jnp.exp(-jnp.abs(t)))
    if valid < L:
        rows = lax.broadcasted_iota(jnp.int32, dt.shape, 0)
        dt = jnp.where(rows < valid, dt, 0.0)
    a = -jnp.exp(alog_ref[...])
    dta = dt * a

    r_i = lax.broadcasted_iota(jnp.int32, (L, L), 0)
    c_i = lax.broadcasted_iota(jnp.int32, (L, L), 1)
    causal = r_i >= c_i
    tril = jnp.where(causal, 1.0, 0.0).astype(BF16)
    e_r = lax.broadcasted_iota(jnp.int32, (LANES, LANES), 0)
    e_c = lax.broadcasted_iota(jnp.int32, (LANES, LANES), 1)
    eye = jnp.where(e_r == e_c, 1.0, 0.0).astype(BF16)

    acs = _dot01(tril, dta, (((1,), (0,)), ((), ())))
    acs_t = _dot01(eye, acs, NT_DIMS)
    last = acs[L - 1:L, :]
    dec_end = jnp.exp(last - acs)
    e_acs = jnp.exp(acs)
    dec_chunk = jnp.exp(last)

    for g in range(N_SSM_GROUPS):
        bg = bm[:, g * D_STATE:(g + 1) * D_STATE]
        cg = cm[:, g * D_STATE:(g + 1) * D_STATE]
        cb = lax.dot_general(cg, bg, NT_DIMS, preferred_element_type=F32)
        for jp in range(HEADS_PER_GROUP // 2):
            pair = []
            for h in (g * HEADS_PER_GROUP + 2 * jp, g * HEADS_PER_GROUP + 2 * jp + 1):
                diff = acs[:, h:h + 1] - acs_t[h:h + 1, :]
                seg = jnp.exp(jnp.where(causal, diff, NEG_INF))
                w = (cb * seg).astype(BF16)
                xs_h = xs[:, h * SSM_HEAD_DIM:(h + 1) * SSM_HEAD_DIM]
                xd_h = xs_h * dt[:, h:h + 1]
                y_diag = jnp.dot(w, xd_h.astype(BF16), preferred_element_type=F32)
                st_h = st_ref[0, h]
                y_off = lax.dot_general(cg, st_h.astype(BF16), NT_DIMS,
                                        preferred_element_type=F32) * e_acs[:, h:h + 1]
                xdd = (xd_h * dec_end[:, h:h + 1]).astype(BF16)
                cs = lax.dot_general(xdd, bg, TN_DIMS, preferred_element_type=F32)
                st_ref[0, h] = st_h * dec_chunk[:, h:h + 1] + cs
                pair.append(y_diag + y_off
                            + dsk_ref[:, h * SSM_HEAD_DIM:(h + 1) * SSM_HEAD_DIM] * xs_h)
            lo = (g * HEADS_PER_GROUP + 2 * jp) * SSM_HEAD_DIM
            y_ref[0, :, lo:lo + 2 * SSM_HEAD_DIM] = jnp.concatenate(pair, axis=1)


def _ssd(xbc, dtr, conv0, ssm0, cw, cb, dtb, alog, dsk, L, valid):
    nb, t = xbc.shape[0], xbc.shape[1]
    nc = t // L
    st_spec = pl.BlockSpec((1, N_SSM_HEADS, SSM_HEAD_DIM, D_STATE), lambda b, c: (b, 0, 0, 0))
    return pl.pallas_call(
        functools.partial(_ssd_body, L=L, valid=valid),
        grid=(nb, nc),
        in_specs=[pl.BlockSpec((1, L, CONV_DIM), lambda b, c: (b, c, 0)),
                  pl.BlockSpec((1, L, LANES), lambda b, c: (b, c, 0)),
                  pl.BlockSpec((1, SUBLANES, CONV_DIM), lambda b, c: (b, 0, 0)),
                  st_spec,
                  _const_spec(cw.shape), _const_spec(cb.shape), _const_spec(dtb.shape),
                  _const_spec(alog.shape), _const_spec(dsk.shape)],
        out_specs=[pl.BlockSpec((1, L, D_INNER), lambda b, c: (b, c, 0)), st_spec],
        out_shape=[jax.ShapeDtypeStruct((nb, t, D_INNER), F32),
                   jax.ShapeDtypeStruct(ssm0.shape, F32)],
        scratch_shapes=[pltpu.VMEM((L + SUBLANES, CONV_DIM), F32)],
        compiler_params=_params(("parallel", "arbitrary")),
        name="ssd_scan",
    )(xbc, dtr, conv0, ssm0, cw, cb, dtb, alog, dsk)


def _ffn(x1, nf_ref, wg_ref, wu_ref, wd_ref):
    h = (x1 * _inv_rms(x1) * nf_ref[...]).astype(BF16)
    gate = jnp.dot(h, wg_ref[...], preferred_element_type=F32)
    up = jnp.dot(h, wu_ref[...], preferred_element_type=F32)
    act = (_silu(gate) * up).astype(BF16)
    return x1 + jnp.dot(act, wd_ref[...], preferred_element_type=F32)


def _ssm_out_ffn_body(y_ref, z_ref, x_ref, gn_ref, wo_ref, nf_ref, wg_ref, wu_ref, wd_ref, o_ref):
    y = y_ref[...] * _silu(z_ref[...])
    gw = D_INNER // N_SSM_GROUPS
    parts = []
    for g in range(N_SSM_GROUPS):
        yg = y[:, g * gw:(g + 1) * gw]
        parts.append(yg * _inv_rms(yg))
    y = (jnp.concatenate(parts, axis=1) * gn_ref[...]).astype(BF16)
    x1 = x_ref[...] + jnp.dot(y, wo_ref[...], preferred_element_type=F32)
    o_ref[...] = _ffn(x1, nf_ref, wg_ref, wu_ref, wd_ref)


def _attn_out_ffn_body(a_ref, x_ref, wo_ref, nf_ref, wg_ref, wu_ref, wd_ref, nfin_ref, o_ref):
    x1 = x_ref[...] + jnp.dot(a_ref[...], wo_ref[...], preferred_element_type=F32)
    x2 = _ffn(x1, nf_ref, wg_ref, wu_ref, wd_ref)
    o_ref[...] = x2 * _inv_rms(x2) * nfin_ref[...]


def _ssm_out_ffn(y, z, x, gn, wo, nf, wg, wu, wd, tm):
    m = x.shape[0]
    row = lambda w: pl.BlockSpec((tm, w), lambda i: (i, 0))
    return pl.pallas_call(
        _ssm_out_ffn_body,
        grid=(m // tm,),
        in_specs=[row(D_INNER), row(D_INNER), row(D_MODEL), _const_spec(gn.shape),
                  _const_spec(wo.shape), _const_spec(nf.shape), _const_spec(wg.shape),
                  _const_spec(wu.shape), _const_spec(wd.shape)],
        out_specs=row(D_MODEL),
        out_shape=jax.ShapeDtypeStruct((m, D_MODEL), F32),
        compiler_params=_params(("parallel",)),
        name="ssm_out_ffn",
    )(y, z, x, gn, wo, nf, wg, wu, wd)


def _attn_out_ffn(a, x, wo, nf, wg, wu, wd, nfin, tm):
    m = x.shape[0]
    row = lambda w: pl.BlockSpec((tm, w), lambda i: (i, 0))
    return pl.pallas_call(
        _attn_out_ffn_body,
        grid=(m // tm,),
        in_specs=[row(D_MODEL), row(D_MODEL), _const_spec(wo.shape), _const_spec(nf.shape),
                  _const_spec(wg.shape), _const_spec(wu.shape), _const_spec(wd.shape),
                  _const_spec(nfin.shape)],
        out_specs=row(D_MODEL),
        out_shape=jax.ShapeDtypeStruct((m, D_MODEL), F32),
        compiler_params=_params(("parallel",)),
        name="attn_out_ffn",
    )(a, x, wo, nf, wg, wu, wd, nfin)


def _rope(x, cos, sin_signed, first_half):
    outs = []
    for c in range(x.shape[1] // LANES):
        xc = x[:, c * LANES:(c + 1) * LANES]
        partner = jnp.where(first_half, pltpu.roll(xc, LANES - HEAD_DIM // 2, 1),
                            pltpu.roll(xc, HEAD_DIM // 2, 1))
        outs.append(xc * cos + partner * sin_signed)
    return jnp.concatenate(outs, axis=1)


def _kvq_compute(x_ref, gkv_ref, gq_ref, wkv_ref, wq_ref, cos_ref, sin_ref):
    x = x_ref[...]
    xn = x * _inv_rms(x)
    hk = (xn * gkv_ref[...]).astype(BF16)
    hq = (xn * gq_ref[...]).astype(BF16)
    cos = cos_ref[...]
    sin = sin_ref[...]
    lane = lax.broadcasted_iota(jnp.int32, cos.shape, 1)
    first_half = (lane % HEAD_DIM) < (HEAD_DIM // 2)
    kv = jnp.dot(hk, wkv_ref[...], preferred_element_type=F32)
    k = _rope(kv[:, :KV_DIM], cos, sin, first_half)
    q = _rope(jnp.dot(hq, wq_ref[...], preferred_element_type=F32), cos, sin, first_half)
    return k, kv[:, KV_DIM:], q


def _kvq_body(x_ref, gkv_ref, gq_ref, wkv_ref, wq_ref, cos_ref, sin_ref, k_ref, v_ref, q_ref):
    k, v, q = _kvq_compute(x_ref, gkv_ref, gq_ref, wkv_ref, wq_ref, cos_ref, sin_ref)
    k_ref[...] = k
    v_ref[...] = v
    q_ref[...] = q


def _kvq_prompt_body(x_ref, gkv_ref, gq_ref, wkv_ref, wq_ref, cos_ref, sin_ref,
                     k_ref, v_ref, qt_ref, kh_ref, vt_ref, km_ref):
    k, v, q = _kvq_compute(x_ref, gkv_ref, gq_ref, wkv_ref, wq_ref, cos_ref, sin_ref)
    k_ref[...] = k
    v_ref[...] = v
    qt_ref[0] = q.T
    vt_ref[0, 0] = v.T.astype(BF16)
    for kvh in range(N_KV_HEADS):
        kh_ref[0, 0, kvh] = k[:, kvh * HEAD_DIM:(kvh + 1) * HEAD_DIM].astype(BF16)
    km_ref[0, 0] = jnp.sum(k, axis=0, keepdims=True) * (1.0 / MOBA_BLOCK)


def _kvq(x, gkv, gq, wkv, wq, cos, sin, tm):
    m = x.shape[0]
    row = lambda w: pl.BlockSpec((tm, w), lambda i: (i, 0))
    tab = pl.BlockSpec((tm, LANES), lambda i: (0, 0))
    return pl.pallas_call(
        _kvq_body,
        grid=(m // tm,),
        in_specs=[row(D_MODEL), _const_spec(gkv.shape), _const_spec(gq.shape),
                  _const_spec(wkv.shape), _const_spec(wq.shape), tab, tab],
        out_specs=[row(KV_DIM), row(KV_DIM), row(D_MODEL)],
        out_shape=[jax.ShapeDtypeStruct((m, KV_DIM), F32), jax.ShapeDtypeStruct((m, KV_DIM), F32),
                   jax.ShapeDtypeStruct((m, D_MODEL), F32)],
        compiler_params=_params(("parallel",)),
        name="kvq_proj",
    )(x, gkv, gq, wkv, wq, cos, sin)


def _kvq_prompt(x, b, t, gkv, gq, wkv, wq, cos, sin):
    nb = t // MOBA_BLOCK
    tm = MOBA_BLOCK
    row = lambda w: pl.BlockSpec((tm, w), lambda bi, i: (bi * nb + i, 0))
    tab = pl.BlockSpec((tm, LANES), lambda bi, i: (i, 0))
    return pl.pallas_call(
        _kvq_prompt_body,
        grid=(b, nb),
        in_specs=[row(D_MODEL), _const_spec(gkv.shape), _const_spec(gq.shape),
                  _const_spec(wkv.shape), _const_spec(wq.shape), tab, tab],
        out_specs=[row(KV_DIM), row(KV_DIM),
                   pl.BlockSpec((1, D_MODEL, tm), lambda bi, i: (bi, 0, i)),
                   pl.BlockSpec((1, 1, N_KV_HEADS, tm, HEAD_DIM), lambda bi, i: (bi, i, 0, 0, 0)),
                   pl.BlockSpec((1, 1, KV_DIM, tm), lambda bi, i: (bi, i, 0, 0)),
                   pl.BlockSpec((1, 1, 1, KV_DIM), lambda bi, i: (bi, i, 0, 0))],
        out_shape=[jax.ShapeDtypeStruct((b * t, KV_DIM), F32),
                   jax.ShapeDtypeStruct((b * t, KV_DIM), F32),
                   jax.ShapeDtypeStruct((b, D_MODEL, t), F32),
                   jax.ShapeDtypeStruct((b, nb, N_KV_HEADS, tm, HEAD_DIM), BF16),
                   jax.ShapeDtypeStruct((b, nb, KV_DIM, tm), BF16),
                   jax.ShapeDtypeStruct((b, nb, 1, KV_DIM), F32)],
        compiler_params=_params(("parallel", "parallel")),
        name="kvq_proj_prompt",
    )(x, gkv, gq, wkv, wq, cos, sin)


def _moba_prompt_body(qt_ref, kh_ref, vt_ref, km_ref, o_ref, m_scr, l_scr, acc_scr, pen_scr, *, nb):
    i = pl.program_id(1)
    rows = Q_PER_KV * MOBA_BLOCK
    scale = HEAD_DIM ** -0.5
    blk_row = lax.broadcasted_iota(jnp.int32, (nb, rows), 0)
    key_i = lax.broadcasted_iota(jnp.int32, (MOBA_BLOCK, rows), 0)
    tok_i = lax.broadcasted_iota(jnp.int32, (MOBA_BLOCK, rows), 1) % MOBA_BLOCK
    own_ok = key_i <= tok_i

    for kvh in range(N_KV_HEADS):
        lo = kvh * HEAD_DIM
        qt4 = qt_ref[0, kvh * Q_PER_KV * HEAD_DIM:(kvh + 1) * Q_PER_KV * HEAD_DIM, :]
        qt = jnp.concatenate([qt4[g * HEAD_DIM:(g + 1) * HEAD_DIM, :] for g in range(Q_PER_KV)],
                             axis=1)
        km = km_ref[0, :, lo:lo + HEAD_DIM]
        gate = jnp.dot(km, qt, preferred_element_type=F32,
                       precision=lax.Precision.HIGHEST)
        sel = _top3_mask(gate, blk_row < i, axis=0)
        pen_scr[...] = jnp.where(sel, 0.0, NEG_INF)
        qb = (qt * scale).astype(BF16)

        s = jnp.dot(kh_ref[0, i, kvh], qb, preferred_element_type=F32)
        s = jnp.where(own_ok, s, NEG_INF)
        m0 = jnp.max(s, axis=0, keepdims=True)
        p = jnp.exp(s - m0)
        m_scr[...] = m0
        l_scr[...] = jnp.sum(p, axis=0, keepdims=True)
        acc_scr[...] = jnp.dot(vt_ref[0, i, lo:lo + HEAD_DIM, :], p.astype(BF16),
                               preferred_element_type=F32)

        def step(j, carry):
            sj = jnp.dot(kh_ref[0, j, kvh], qb, preferred_element_type=F32) + pen_scr[pl.ds(j, 1), :]
            m_old = m_scr[...]
            m_new = jnp.maximum(m_old, jnp.max(sj, axis=0, keepdims=True))
            alpha = jnp.exp(m_old - m_new)
            pj = jnp.exp(sj - m_new)
            m_scr[...] = m_new
            l_scr[...] = alpha * l_scr[...] + jnp.sum(pj, axis=0, keepdims=True)
            acc_scr[...] = alpha * acc_scr[...] + jnp.dot(
                vt_ref[0, j, lo:lo + HEAD_DIM, :], pj.astype(BF16), preferred_element_type=F32)
            return carry

        lax.fori_loop(0, i, step, 0)
        out_t = acc_scr[...] / l_scr[...]
        out4 = jnp.concatenate([out_t[:, g * MOBA_BLOCK:(g + 1) * MOBA_BLOCK]
                                for g in range(Q_PER_KV)], axis=0)
        o_ref[0, :, kvh * Q_PER_KV * HEAD_DIM:(kvh + 1) * Q_PER_KV * HEAD_DIM] = out4.T.astype(
            o_ref.dtype)


def _moba_prompt(qt, kh, vt, km):
    b, t = qt.shape[0], qt.shape[2]
    nb = t // MOBA_BLOCK
    rows = Q_PER_KV * MOBA_BLOCK
    return pl.pallas_call(
        functools.partial(_moba_prompt_body, nb=nb),
        grid=(b, nb),
        in_specs=[pl.BlockSpec((1, D_MODEL, MOBA_BLOCK), lambda bi, i: (bi, 0, i)),
                  pl.BlockSpec((1, nb, N_KV_HEADS, MOBA_BLOCK, HEAD_DIM),
                               lambda bi, i: (bi, 0, 0, 0, 0)),
                  pl.BlockSpec((1, nb, KV_DIM, MOBA_BLOCK), lambda bi, i: (bi, 0, 0, 0)),
                  pl.BlockSpec((1, nb, KV_DIM), lambda bi, i: (bi, 0, 0))],
        out_specs=pl.BlockSpec((1, MOBA_BLOCK, D_MODEL), lambda bi, i: (bi, i, 0)),
        out_shape=jax.ShapeDtypeStruct((b, t, D_MODEL), BF16),
        scratch_shapes=[pltpu.VMEM((1, rows), F32), pltpu.VMEM((1, rows), F32),
                        pltpu.VMEM((HEAD_DIM, rows), F32), pltpu.VMEM((nb, rows), F32)],
        compiler_params=_params(("parallel", "arbitrary")),
        name="moba_prompt",
    )(qt, kh, vt, km)


def _page_copy(cache_hbm, buf, sem, page, slot):
    return pltpu.make_async_copy(cache_hbm.at[page], buf.at[pl.ds(slot * PAGE_SIZE, PAGE_SIZE)], sem)


def _moba_sample_body(pt_ref, q_ref, kn_ref, vn_ref, ck_hbm, cv_hbm, o_ref,
                      kbuf, vbuf, s_scr, km_scr, new_scr, sems, *, n_pages, n_new):
    b = pl.program_id(0)
    rows = q_ref.shape[1]
    nb = n_pages * PAGE_SIZE // MOBA_BLOCK
    scale = HEAD_DIM ** -0.5

    def start_page(p, carry):
        _page_copy(ck_hbm, kbuf, sems.at[0], pt_ref[b, p], p).start()
        _page_copy(cv_hbm, vbuf, sems.at[1], pt_ref[b, p], p).start()
        return carry

    lax.fori_loop(0, n_pages, start_page, 0)

    q64 = q_ref[0]
    qt = jnp.concatenate([q64] * N_KV_HEADS, axis=1)
    r_i = lax.broadcasted_iota(jnp.int32, qt.shape, 0)
    c_i = lax.broadcasted_iota(jnp.int32, qt.shape, 1)
    qpad = jnp.where(c_i // HEAD_DIM == r_i // (Q_PER_KV * n_new), qt, 0.0)
    qb = (qpad * scale).astype(BF16)

    def wait_k(p, carry):
        _page_copy(ck_hbm, kbuf, sems.at[0], 0, p).wait()
        return carry

    lax.fori_loop(0, n_pages, wait_k, 0)

    for j in range(nb):
        kj = kbuf[j * MOBA_BLOCK:(j + 1) * MOBA_BLOCK, :]
        km_scr[j:j + 1, :] = jnp.sum(kj, axis=0, keepdims=True) * (1.0 / MOBA_BLOCK)
        s_scr[:, j * MOBA_BLOCK:(j + 1) * MOBA_BLOCK] = lax.dot_general(
            qb, kj.astype(BF16), NT_DIMS, preferred_element_type=F32)

    gate = lax.dot_general(qpad, km_scr[...], NT_DIMS, preferred_element_type=F32,
                           precision=lax.Precision.HIGHEST)
    sel = _top3_mask(gate, jnp.full(gate.shape, True))
    pen = jnp.where(sel, 0.0, NEG_INF)

    new_scr[...] = jnp.zeros(new_scr.shape, F32)
    new_scr[0:n_new, :] = kn_ref[0]
    s_new = lax.dot_general(qb, new_scr[...].astype(BF16), NT_DIMS, preferred_element_type=F32)
    nr = lax.broadcasted_iota(jnp.int32, s_new.shape, 0)
    ncol = lax.broadcasted_iota(jnp.int32, s_new.shape, 1)
    s_new = jnp.where(ncol <= nr % n_new, s_new, NEG_INF)

    m = jnp.max(s_new, axis=1, keepdims=True)
    for j in range(nb):
        sj = s_scr[:, j * MOBA_BLOCK:(j + 1) * MOBA_BLOCK] + pen[:, j:j + 1]
        m = jnp.maximum(m, jnp.max(sj, axis=1, keepdims=True))

    def wait_v(p, carry):
        _page_copy(cv_hbm, vbuf, sems.at[1], 0, p).wait()
        return carry

    lax.fori_loop(0, n_pages, wait_v, 0)

    p_new = jnp.exp(s_new - m)
    l = jnp.sum(p_new, axis=1, keepdims=True)
    new_scr[0:n_new, :] = vn_ref[0]
    acc = jnp.dot(p_new.astype(BF16), new_scr[...].astype(BF16), preferred_element_type=F32)
    for j in range(nb):
        pj = jnp.exp(s_scr[:, j * MOBA_BLOCK:(j + 1) * MOBA_BLOCK] + pen[:, j:j + 1] - m)
        l = l + jnp.sum(pj, axis=1, keepdims=True)
        vj = vbuf[j * MOBA_BLOCK:(j + 1) * MOBA_BLOCK, :].astype(BF16)
        acc = acc + jnp.dot(pj.astype(BF16), vj, preferred_element_type=F32)
    acc = acc / l
    ro = lax.broadcasted_iota(jnp.int32, (rows, HEAD_DIM), 0) // (Q_PER_KV * n_new)
    out = jnp.zeros((rows, HEAD_DIM), F32)
    for kvh in range(N_KV_HEADS):
        out = jnp.where(ro == kvh, acc[:, kvh * HEAD_DIM:(kvh + 1) * HEAD_DIM], out)
    o_ref[0] = out


def _moba_sample(page_table, q_rows, k_new, v_new, cache_k, cache_v):
    nseq, n_pages = page_table.shape
    rows = q_rows.shape[1]
    n_new = k_new.shape[1]
    past = n_pages * PAGE_SIZE
    grid_spec = pltpu.PrefetchScalarGridSpec(
        num_scalar_prefetch=1,
        grid=(nseq,),
        in_specs=[pl.BlockSpec((1, rows, HEAD_DIM), lambda b, pt: (b, 0, 0)),
                  pl.BlockSpec((1, n_new, KV_DIM), lambda b, pt: (b, 0, 0)),
                  pl.BlockSpec((1, n_new, KV_DIM), lambda b, pt: (b, 0, 0)),
                  pl.BlockSpec(memory_space=pl.ANY),
                  pl.BlockSpec(memory_space=pl.ANY)],
        out_specs=pl.BlockSpec((1, rows, HEAD_DIM), lambda b, pt: (b, 0, 0)),
        scratch_shapes=[pltpu.VMEM((past, KV_DIM), F32), pltpu.VMEM((past, KV_DIM), F32),
                        pltpu.VMEM((rows, past), F32),
                        pltpu.VMEM((past // MOBA_BLOCK, KV_DIM), F32),
                        pltpu.VMEM((LANES, KV_DIM), F32),
                        pltpu.SemaphoreType.DMA((2,))],
    )
    return pl.pallas_call(
        functools.partial(_moba_sample_body, n_pages=n_pages, n_new=n_new),
        grid_spec=grid_spec,
        out_shape=jax.ShapeDtypeStruct((nseq, rows, HEAD_DIM), F32),
        compiler_params=_params(("arbitrary",)),
        name="moba_sample",
    )(page_table, q_rows, k_new, v_new, cache_k, cache_v)


def _rope_tables(pos):
    half = HEAD_DIM // 2
    inv = ROPE_THETA ** (-jnp.arange(half, dtype=F32) / half)
    ang = pos.astype(F32)[:, None] * inv[None, :]
    cos, sin = jnp.cos(ang), jnp.sin(ang)
    cos_h = jnp.concatenate([cos, cos], axis=1)
    sin_h = jnp.concatenate([-sin, sin], axis=1)
    reps = LANES // HEAD_DIM
    return jnp.tile(cos_h, (1, reps)), jnp.tile(sin_h, (1, reps))


def _prep_weights(norm_mix, norm_ffn, w_in_ssm, conv_w, conv_b, dt_bias, a_log, d_skip, norm_ssm,
                  w_out_ssm, norm_kv, w_kv, w_q, w_o, w_gu, w_down, norm_final):
    pad_h = LANES - N_SSM_HEADS
    w_in = w_in_ssm[0]
    return dict(
        g_mix0=norm_mix[0][None], g_mix1=norm_mix[1][None],
        g_ffn0=norm_ffn[0][None], g_ffn1=norm_ffn[1][None],
        wz=w_in[:, :D_INNER].astype(BF16),
        wx=w_in[:, D_INNER:D_INNER + CONV_DIM].astype(BF16),
        wdt=jnp.pad(w_in[:, D_INNER + CONV_DIM:], ((0, 0), (0, pad_h))).astype(BF16),
        cw=conv_w[0], cb=conv_b[0][None],
        dtb=jnp.pad(dt_bias[0], (0, pad_h))[None], alog=jnp.pad(a_log[0], (0, pad_h))[None],
        dsk=jnp.repeat(d_skip[0], SSM_HEAD_DIM)[None],
        gn=norm_ssm[0][None], wo_ssm=w_out_ssm[0].astype(BF16),
        g_kv=norm_kv[None], wkv=w_kv.astype(BF16), wq=w_q[0].astype(BF16),
        wo=w_o[0].astype(BF16),
        wg0=w_gu[0][:, :D_FF].astype(BF16), wu0=w_gu[0][:, D_FF:].astype(BF16),
        wd0=w_down[0].astype(BF16),
        wg1=w_gu[1][:, :D_FF].astype(BF16), wu1=w_gu[1][:, D_FF:].astype(BF16),
        wd1=w_down[1].astype(BF16),
        g_fin=norm_final[None],
    )


def _ssd_layer(x, conv_in, ssm0, w, tm, L, valid):
    b, t, _ = x.shape
    xf = x.reshape(b * t, D_MODEL)
    z, xbc, dtr = _inproj(xf, w["g_mix0"], w["wz"], w["wx"], w["wdt"], tm)
    tp = -(-t // L) * L
    xbc3 = xbc.reshape(b, t, CONV_DIM)
    dtr3 = dtr.reshape(b, t, LANES)
    if tp != t:
        xbc3 = jnp.pad(xbc3, ((0, 0), (0, tp - t), (0, 0)))
        dtr3 = jnp.pad(dtr3, ((0, 0), (0, tp - t), (0, 0)))
    conv0 = jnp.pad(conv_in, ((0, 0), (SUBLANES - (D_CONV - 1), 0), (0, 0)))
    y, ssm_new = _ssd(xbc3, dtr3, conv0, ssm0, w["cw"], w["cb"], w["dtb"], w["alog"], w["dsk"],
                      L, valid)
    y = y[:, :t].reshape(b * t, D_INNER)
    x2 = _ssm_out_ffn(y, z, xf, w["gn"], w["wo_ssm"], w["g_ffn0"], w["wg0"], w["wu0"], w["wd0"], tm)
    keep = D_CONV - 1
    conv_new = jnp.concatenate([conv_in[:, t:], xbc.reshape(b, t, CONV_DIM)[:, max(0, t - keep):]],
                               axis=1)
    return x2, conv_new, ssm_new


def kernel(x_prompt, x_sample, state_conv, state_ssm, cache_k, cache_v, page_table, norm_mix,
           norm_ffn, w_in_ssm, conv_w, conv_b, dt_bias, a_log, d_skip, norm_ssm, w_out_ssm,
           norm_kv, w_kv, w_q, w_o, w_gu, w_down, norm_final):
    w = _prep_weights(norm_mix, norm_ffn, w_in_ssm, conv_w, conv_b, dt_bias, a_log, d_skip,
                      norm_ssm, w_out_ssm, norm_kv, w_kv, w_q, w_o, w_gu, w_down, norm_final)
    bp, tp, _ = x_prompt.shape
    bs, ts, _ = x_sample.shape
    past_len = page_table.shape[1] * PAGE_SIZE

    tm_p = 256
    conv0_p = jnp.zeros((bp, D_CONV - 1, CONV_DIM), F32)
    ssm0_p = jnp.zeros((bp, N_SSM_HEADS, SSM_HEAD_DIM, D_STATE), F32)
    x2_p, conv_p, ssm_p = _ssd_layer(x_prompt, conv0_p, ssm0_p, w, tm_p, SSD_CHUNK, SSD_CHUNK)
    cos_p, sin_p = _rope_tables(jnp.arange(tp, dtype=jnp.int32))
    k_p, v_p, qt_p, kh_p, vt_p, km_p = _kvq_prompt(x2_p, bp, tp, w["g_kv"], w["g_mix1"], w["wkv"],
                                                   w["wq"], cos_p, sin_p)
    attn_p = _moba_prompt(qt_p, kh_p, vt_p, km_p.reshape(bp, tp // MOBA_BLOCK, KV_DIM))
    y_p = _attn_out_ffn(attn_p.reshape(bp * tp, D_MODEL), x2_p, w["wo"], w["g_ffn1"], w["wg1"],
                        w["wu1"], w["wd1"], w["g_fin"], tm_p)

    tm_s = bs * ts
    x2_s, conv_s, ssm_s = _ssd_layer(x_sample, state_conv[0], state_ssm[0], w, tm_s, SUBLANES, ts)
    pos_s = past_len + jnp.tile(jnp.arange(ts, dtype=jnp.int32), bs)
    cos_s, sin_s = _rope_tables(pos_s)
    k_s, v_s, q_s = _kvq(x2_s, w["g_kv"], w["g_mix1"], w["wkv"], w["wq"], cos_s, sin_s, tm_s)
    q_rows = q_s.reshape(bs, ts, N_HEADS, HEAD_DIM).transpose(0, 2, 1, 3).reshape(
        bs, N_HEADS * ts, HEAD_DIM)
    q_rows = jnp.pad(q_rows, ((0, 0), (0, LANES - N_HEADS * ts), (0, 0)))
    attn_rows = _moba_sample(page_table, q_rows, k_s.reshape(bs, ts, KV_DIM),
                             v_s.reshape(bs, ts, KV_DIM),
                             cache_k.reshape(cache_k.shape[0], PAGE_SIZE, KV_DIM),
                             cache_v.reshape(cache_v.shape[0], PAGE_SIZE, KV_DIM))
    attn_s = attn_rows[:, :N_HEADS * ts].reshape(bs, N_HEADS, ts, HEAD_DIM).transpose(
        0, 2, 1, 3).reshape(bs * ts, D_MODEL).astype(BF16)
    y_s = _attn_out_ffn(attn_s, x2_s, w["wo"], w["g_ffn1"], w["wg1"], w["wu1"], w["wd1"],
                        w["g_fin"], tm_s)

    return (y_p.reshape(bp, tp, D_MODEL), y_s.reshape(bs, ts, D_MODEL),
            conv_p[None], ssm_p[None],
            k_p.reshape(bp, tp, N_KV_HEADS, HEAD_DIM), v_p.reshape(bp, tp, N_KV_HEADS, HEAD_DIM),
            conv_s[None], ssm_s[None],
            k_s.reshape(bs, ts, N_KV_HEADS, HEAD_DIM), v_s.reshape(bs, ts, N_KV_HEADS, HEAD_DIM))
```

```python
import functools

import jax
import jax.numpy as jnp
from jax import lax
from jax.experimental import pallas as pl
from jax.experimental.pallas import tpu as pltpu

F32 = jnp.float32
BF16 = jnp.bfloat16

D_MODEL = 1024
D_INNER = 2048
SSM_HEAD_DIM = 64
N_SSM_HEADS = 32
N_SSM_GROUPS = 4
HEADS_PER_GROUP = 8
D_STATE = 128
D_CONV = 4
SSD_CHUNK = 128
GN = N_SSM_GROUPS * D_STATE
CONV_DIM = D_INNER + 2 * GN
HEAD_DIM = 64
N_HEADS = 16
N_KV_HEADS = 4
Q_PER_KV = 4
KV_DIM = N_KV_HEADS * HEAD_DIM
MOBA_BLOCK = 256
MOBA_TOP_K = 3
ROPE_THETA = 10000.0
D_FF = 2816
EPS = 1e-6
PAGE_SIZE = 128

LANES = 128
SUBLANES = 8
VMEM_LIMIT = 56 * 1024 * 1024

NT_DIMS = (((1,), (1,)), ((), ()))
TN_DIMS = (((0,), (0,)), ((), ()))
NEG_INF = float("-inf")
MASKED = -1e30
LOG2_E = 1.4426950408889634
V_AUG = HEAD_DIM + 16


def _params(sem):
    return pltpu.CompilerParams(dimension_semantics=sem, vmem_limit_bytes=VMEM_LIMIT)


def _const_spec(shape):
    nd = len(shape)
    return pl.BlockSpec(shape, lambda *_: (0,) * nd, pipeline_mode=pl.Buffered(1))


def _silu(x):
    return x * (1.0 / (1.0 + jnp.exp(-x)))


def _inv_rms(x):
    return lax.rsqrt(jnp.mean(x * x, axis=-1, keepdims=True) + EPS)


def _split3(x):
    hi = x.astype(BF16)
    r = x - hi.astype(F32)
    mid = r.astype(BF16)
    lo = (r - mid.astype(F32)).astype(BF16)
    return hi, mid, lo


def _dot01(a01, x, dims, a_is_lhs=True):
    out = None
    for p in _split3(x):
        t = (lax.dot_general(a01, p, dims, preferred_element_type=F32) if a_is_lhs
             else lax.dot_general(p, a01, dims, preferred_element_type=F32))
        out = t if out is None else out + t
    return out


def _top3_mask(gate, valid, axis=1):
    nb = gate.shape[axis]
    col = lax.broadcasted_iota(jnp.int32, gate.shape, axis).astype(F32)
    g = jnp.where(valid, gate, NEG_INF)
    sel = jnp.zeros(gate.shape, F32)
    for _ in range(MOBA_TOP_K):
        m = jnp.max(g, axis=axis, keepdims=True)
        idx = jnp.min(jnp.where(g == m, col, float(nb)), axis=axis, keepdims=True)
        pick = col == idx
        sel = jnp.where(pick, 1.0, sel)
        g = jnp.where(pick, NEG_INF, g)
    return jnp.logical_and(sel > 0.5, valid)


def _inproj_body(x_ref, g_ref, wz_ref, wx_ref, wdt_ref, z_ref, xbc_ref, dt_ref):
    x = x_ref[...]
    h = (x * _inv_rms(x) * g_ref[...]).astype(BF16)
    z_ref[...] = jnp.dot(h, wz_ref[...], preferred_element_type=F32)
    xbc_ref[...] = jnp.dot(h, wx_ref[...], preferred_element_type=F32)
    dt_ref[...] = jnp.dot(h, wdt_ref[...], preferred_element_type=F32)


def _inproj(x, g, wz, wx, wdt, tm):
    m = x.shape[0]
    row = lambda w: pl.BlockSpec((tm, w), lambda i: (i, 0))
    return pl.pallas_call(
        _inproj_body,
        grid=(m // tm,),
        in_specs=[row(D_MODEL), _const_spec((1, D_MODEL)), _const_spec(wz.shape),
                  _const_spec(wx.shape), _const_spec(wdt.shape)],
        out_specs=[row(D_INNER), row(CONV_DIM), row(LANES)],
        out_shape=[jax.ShapeDtypeStruct((m, D_INNER), F32),
                   jax.ShapeDtypeStruct((m, CONV_DIM), F32),
                   jax.ShapeDtypeStruct((m, LANES), F32)],
        compiler_params=_params(("parallel",)),
        name="ssd_inproj",
    )(x, g, wz, wx, wdt)


def _ssd_body(xbc_ref, dtr_ref, conv0_ref, ssm0_ref, cw_ref, cb_ref, dtb_ref, alog_ref, dsk_ref,
              y_ref, st_ref, pad_ref, *, L, valid):
    c = pl.program_id(1)

    @pl.when(c == 0)
    def _():
        pad_ref[0:SUBLANES, :] = conv0_ref[0]
        st_ref[0] = ssm0_ref[0]

    pad_ref[SUBLANES:SUBLANES + L, :] = xbc_ref[0]
    conv = cb_ref[...]
    for i in range(D_CONV):
        off = SUBLANES - (D_CONV - 1) + i
        conv = conv + pad_ref[off:off + L, :] * cw_ref[i:i + 1, :]
    pad_ref[0:SUBLANES, :] = pad_ref[L:L + SUBLANES, :]
    act = _silu(conv)
    xs = act[:, :D_INNER]
    bm = act[:, D_INNER:D_INNER + GN].astype(BF16)
    cm = act[:, D_INNER + GN:].astype(BF16)

    t = dtr_ref[0] + dtb_ref[...]
    dt = jnp.maximum(t, 0.0) + jnp.log1p(jnp.exp(-jnp.abs(t)))
    if valid < L:
        rows = lax.broadcasted_iota(jnp.int32, dt.shape, 0)
        dt = jnp.where(rows < valid, dt, 0.0)
    a = -jnp.exp(alog_ref[...])
    dta = dt * a

    r_i = lax.broadcasted_iota(jnp.int32, (L, L), 0)
    c_i = lax.broadcasted_iota(jnp.int32, (L, L), 1)
    causal = r_i >= c_i
    tril = jnp.where(causal, 1.0, 0.0).astype(BF16)
    e_r = lax.broadcasted_iota(jnp.int32, (LANES, LANES), 0)
    e_c = lax.broadcasted_iota(jnp.int32, (LANES, LANES), 1)
    eye = jnp.where(e_r == e_c, 1.0, 0.0).astype(BF16)

    acs = _dot01(tril, dta, (((1,), (0,)), ((), ())))
    acs_t = _dot01(eye, acs, NT_DIMS)
    last = acs[L - 1:L, :]
    dec_end = jnp.exp(last - acs)
    e_acs = jnp.exp(acs)
    dec_chunk = jnp.exp(last)

    for g in range(N_SSM_GROUPS):
        bg = bm[:, g * D_STATE:(g + 1) * D_STATE]
        cg = cm[:, g * D_STATE:(g + 1) * D_STATE]
        cb = lax.dot_general(cg, bg, NT_DIMS, preferred_element_type=F32)
        for jp in range(HEADS_PER_GROUP // 2):
            pair = []
            for h in (g * HEADS_PER_GROUP + 2 * jp, g * HEADS_PER_GROUP + 2 * jp + 1):
                diff = acs[:, h:h + 1] - acs_t[h:h + 1, :]
                seg = jnp.exp(jnp.where(causal, diff, NEG_INF))
                w = (cb * seg).astype(BF16)
                xs_h = xs[:, h * SSM_HEAD_DIM:(h + 1) * SSM_HEAD_DIM]
                xd_h = xs_h * dt[:, h:h + 1]
                y_diag = jnp.dot(w, xd_h.astype(BF16), preferred_element_type=F32)
                st_h = st_ref[0, h]
                y_off = lax.dot_general(cg, st_h.astype(BF16), NT_DIMS,
                                        preferred_element_type=F32) * e_acs[:, h:h + 1]
                xdd = (xd_h * dec_end[:, h:h + 1]).astype(BF16)
                cs = lax.dot_general(xdd, bg, TN_DIMS, preferred_element_type=F32)
                st_ref[0, h] = st_h * dec_chunk[:, h:h + 1] + cs
                pair.append(y_diag + y_off
                            + dsk_ref[:, h * SSM_HEAD_DIM:(h + 1) * SSM_HEAD_DIM] * xs_h)
            lo = (g * HEADS_PER_GROUP + 2 * jp) * SSM_HEAD_DIM
            y_ref[0, :, lo:lo + 2 * SSM_HEAD_DIM] = jnp.concatenate(pair, axis=1)


def _ssd(xbc, dtr, conv0, ssm0, cw, cb, dtb, alog, dsk, L, valid):
    nb, t = xbc.shape[0], xbc.shape[1]
    nc = t // L
    st_spec = pl.BlockSpec((1, N_SSM_HEADS, SSM_HEAD_DIM, D_STATE), lambda b, c: (b, 0, 0, 0))
    return pl.pallas_call(
        functools.partial(_ssd_body, L=L, valid=valid),
        grid=(nb, nc),
        in_specs=[pl.BlockSpec((1, L, CONV_DIM), lambda b, c: (b, c, 0)),
                  pl.BlockSpec((1, L, LANES), lambda b, c: (b, c, 0)),
                  pl.BlockSpec((1, SUBLANES, CONV_DIM), lambda b, c: (b, 0, 0)),
                  st_spec,
                  _const_spec(cw.shape), _const_spec(cb.shape), _const_spec(dtb.shape),
                  _const_spec(alog.shape), _const_spec(dsk.shape)],
        out_specs=[pl.BlockSpec((1, L, D_INNER), lambda b, c: (b, c, 0)), st_spec],
        out_shape=[jax.ShapeDtypeStruct((nb, t, D_INNER), F32),
                   jax.ShapeDtypeStruct(ssm0.shape, F32)],
        scratch_shapes=[pltpu.VMEM((L + SUBLANES, CONV_DIM), F32)],
        compiler_params=_params(("parallel", "arbitrary")),
        name="ssd_scan",
    )(xbc, dtr, conv0, ssm0, cw, cb, dtb, alog, dsk)


def _ffn(x1, nf_ref, wg_ref, wu_ref, wd_ref):
    h = (x1 * _inv_rms(x1) * nf_ref[...]).astype(BF16)
    gate = jnp.dot(h, wg_ref[...], preferred_element_type=F32)
    up = jnp.dot(h, wu_ref[...], preferred_element_type=F32)
    act = (_silu(gate) * up).astype(BF16)
    return x1 + jnp.dot(act, wd_ref[...], preferred_element_type=F32)


def _ssm_out_ffn_body(y_ref, z_ref, x_ref, gn_ref, wo_ref, nf_ref, wg_ref, wu_ref, wd_ref, o_ref):
    y = y_ref[...] * _silu(z_ref[...])
    gw = D_INNER // N_SSM_GROUPS
    parts = []
    for g in range(N_SSM_GROUPS):
        yg = y[:, g * gw:(g + 1) * gw]
        parts.append(yg * _inv_rms(yg))
    y = (jnp.concatenate(parts, axis=1) * gn_ref[...]).astype(BF16)
    x1 = x_ref[...] + jnp.dot(y, wo_ref[...], preferred_element_type=F32)
    o_ref[...] = _ffn(x1, nf_ref, wg_ref, wu_ref, wd_ref)


def _attn_out_ffn_body(a_ref, x_ref, wo_ref, nf_ref, wg_ref, wu_ref, wd_ref, nfin_ref, o_ref):
    x1 = x_ref[...] + jnp.dot(a_ref[...], wo_ref[...], preferred_element_type=F32)
    x2 = _ffn(x1, nf_ref, wg_ref, wu_ref, wd_ref)
    o_ref[...] = x2 * _inv_rms(x2) * nfin_ref[...]


def _ssm_out_ffn(y, z, x, gn, wo, nf, wg, wu, wd, tm):
    m = x.shape[0]
    row = lambda w: pl.BlockSpec((tm, w), lambda i: (i, 0))
    return pl.pallas_call(
        _ssm_out_ffn_body,
        grid=(m // tm,),
        in_specs=[row(D_INNER), row(D_INNER), row(D_MODEL), _const_spec(gn.shape),
                  _const_spec(wo.shape), _const_spec(nf.shape), _const_spec(wg.shape),
                  _const_spec(wu.shape), _const_spec(wd.shape)],
        out_specs=row(D_MODEL),
        out_shape=jax.ShapeDtypeStruct((m, D_MODEL), F32),
        compiler_params=_params(("parallel",)),
        name="ssm_out_ffn",
    )(y, z, x, gn, wo, nf, wg, wu, wd)


def _attn_out_ffn(a, x, wo, nf, wg, wu, wd, nfin, tm):
    m = x.shape[0]
    row = lambda w: pl.BlockSpec((tm, w), lambda i: (i, 0))
    return pl.pallas_call(
        _attn_out_ffn_body,
        grid=(m // tm,),
        in_specs=[row(D_MODEL), row(D_MODEL), _const_spec(wo.shape), _const_spec(nf.shape),
                  _const_spec(wg.shape), _const_spec(wu.shape), _const_spec(wd.shape),
                  _const_spec(nfin.shape)],
        out_specs=row(D_MODEL),
        out_shape=jax.ShapeDtypeStruct((m, D_MODEL), F32),
        compiler_params=_params(("parallel",)),
        name="attn_out_ffn",
    )(a, x, wo, nf, wg, wu, wd, nfin)


def _rope(x, cos, sin_signed, first_half):
    outs = []
    for c in range(x.shape[1] // LANES):
        xc = x[:, c * LANES:(c + 1) * LANES]
        partner = jnp.where(first_half, pltpu.roll(xc, LANES - HEAD_DIM // 2, 1),
                            pltpu.roll(xc, HEAD_DIM // 2, 1))
        outs.append(xc * cos + partner * sin_signed)
    return jnp.concatenate(outs, axis=1)


def _kvq_compute(x_ref, gkv_ref, gq_ref, wkv_ref, wq_ref, cos_ref, sin_ref):
    x = x_ref[...]
    xn = x * _inv_rms(x)
    hk = (xn * gkv_ref[...]).astype(BF16)
    hq = (xn * gq_ref[...]).astype(BF16)
    cos = cos_ref[...]
    sin = sin_ref[...]
    lane = lax.broadcasted_iota(jnp.int32, cos.shape, 1)
    first_half = (lane % HEAD_DIM) < (HEAD_DIM // 2)
    kv = jnp.dot(hk, wkv_ref[...], preferred_element_type=F32)
    k = _rope(kv[:, :KV_DIM], cos, sin, first_half)
    q = _rope(jnp.dot(hq, wq_ref[...], preferred_element_type=F32), cos, sin, first_half)
    return k, kv[:, KV_DIM:], q


def _kvq_body(x_ref, gkv_ref, gq_ref, wkv_ref, wq_ref, cos_ref, sin_ref, k_ref, v_ref, q_ref):
    k, v, q = _kvq_compute(x_ref, gkv_ref, gq_ref, wkv_ref, wq_ref, cos_ref, sin_ref)
    k_ref[...] = k
    v_ref[...] = v
    q_ref[...] = q


def _kvq_prompt_body(x_ref, gkv_ref, gq_ref, wkv_ref, wq_ref, cos_ref, sin_ref,
                     k_ref, v_ref, qt_ref, kh_ref, vt_ref, km_ref):
    k, v, q = _kvq_compute(x_ref, gkv_ref, gq_ref, wkv_ref, wq_ref, cos_ref, sin_ref)
    k_ref[...] = k
    v_ref[...] = v
    qt_ref[0] = q.T
    vt = v.T.astype(BF16)
    blk = pl.program_id(1)
    col = lax.broadcasted_iota(jnp.int32, (MOBA_BLOCK, LANES - HEAD_DIM), 1)
    onehot = jnp.where(col == blk, 1.0, 0.0).astype(BF16)
    row = lax.broadcasted_iota(jnp.int32, (V_AUG - HEAD_DIM, MOBA_BLOCK), 0)
    ones_row = jnp.where(row == 0, 1.0, 0.0).astype(BF16)
    for kvh in range(N_KV_HEADS):
        kh_ref[0, 0, kvh] = jnp.concatenate(
            [k[:, kvh * HEAD_DIM:(kvh + 1) * HEAD_DIM].astype(BF16), onehot], axis=1)
        vt_ref[0, 0, kvh] = jnp.concatenate(
            [vt[kvh * HEAD_DIM:(kvh + 1) * HEAD_DIM, :], ones_row], axis=0)
    km_ref[0, 0] = jnp.sum(k, axis=0, keepdims=True) * (1.0 / MOBA_BLOCK)


def _kvq(x, gkv, gq, wkv, wq, cos, sin, tm):
    m = x.shape[0]
    row = lambda w: pl.BlockSpec((tm, w), lambda i: (i, 0))
    tab = pl.BlockSpec((tm, LANES), lambda i: (0, 0))
    return pl.pallas_call(
        _kvq_body,
        grid=(m // tm,),
        in_specs=[row(D_MODEL), _const_spec(gkv.shape), _const_spec(gq.shape),
                  _const_spec(wkv.shape), _const_spec(wq.shape), tab, tab],
        out_specs=[row(KV_DIM), row(KV_DIM), row(D_MODEL)],
        out_shape=[jax.ShapeDtypeStruct((m, KV_DIM), F32), jax.ShapeDtypeStruct((m, KV_DIM), F32),
                   jax.ShapeDtypeStruct((m, D_MODEL), F32)],
        compiler_params=_params(("parallel",)),
        name="kvq_proj",
    )(x, gkv, gq, wkv, wq, cos, sin)


def _kvq_prompt(x, b, t, gkv, gq, wkv, wq, cos, sin):
    nb = t // MOBA_BLOCK
    tm = MOBA_BLOCK
    row = lambda w: pl.BlockSpec((tm, w), lambda bi, i: (bi * nb + i, 0))
    tab = pl.BlockSpec((tm, LANES), lambda bi, i: (i, 0))
    return pl.pallas_call(
        _kvq_prompt_body,
        grid=(b, nb),
        in_specs=[row(D_MODEL), _const_spec(gkv.shape), _const_spec(gq.shape),
                  _const_spec(wkv.shape), _const_spec(wq.shape), tab, tab],
        out_specs=[row(KV_DIM), row(KV_DIM),
                   pl.BlockSpec((1, D_MODEL, tm), lambda bi, i: (bi, 0, i)),
                   pl.BlockSpec((1, 1, N_KV_HEADS, tm, LANES), lambda bi, i: (bi, i, 0, 0, 0)),
                   pl.BlockSpec((1, 1, N_KV_HEADS, V_AUG, tm), lambda bi, i: (bi, i, 0, 0, 0)),
                   pl.BlockSpec((1, 1, 1, KV_DIM), lambda bi, i: (bi, i, 0, 0))],
        out_shape=[jax.ShapeDtypeStruct((b * t, KV_DIM), F32),
                   jax.ShapeDtypeStruct((b * t, KV_DIM), F32),
                   jax.ShapeDtypeStruct((b, D_MODEL, t), F32),
                   jax.ShapeDtypeStruct((b, nb, N_KV_HEADS, tm, LANES), BF16),
                   jax.ShapeDtypeStruct((b, nb, N_KV_HEADS, V_AUG, tm), BF16),
                   jax.ShapeDtypeStruct((b, nb, 1, KV_DIM), F32)],
        compiler_params=_params(("parallel", "parallel")),
        name="kvq_proj_prompt",
    )(x, gkv, gq, wkv, wq, cos, sin)


def _moba_prompt_body(qt_ref, kh_ref, vt_ref, km_ref, o_ref, s_all, acc_scr, *, nb):
    i = pl.program_id(1)
    rows = Q_PER_KV * MOBA_BLOCK
    scale = (HEAD_DIM ** -0.5) * LOG2_E
    n_pairs = (i + 1) // 2
    blk_row = lax.broadcasted_iota(jnp.int32, (nb, rows), 0)
    key_i = lax.broadcasted_iota(jnp.int32, (MOBA_BLOCK, rows), 0)
    tok_i = lax.broadcasted_iota(jnp.int32, (MOBA_BLOCK, rows), 1) % MOBA_BLOCK
    own_ok = key_i <= tok_i
    fold = lambda s: jnp.max(s.reshape(MOBA_BLOCK // SUBLANES, SUBLANES, rows), axis=0)

    for kvh in range(N_KV_HEADS):
        lo = kvh * HEAD_DIM
        qt4 = qt_ref[0, kvh * Q_PER_KV * HEAD_DIM:(kvh + 1) * Q_PER_KV * HEAD_DIM, :]
        qt = jnp.concatenate([qt4[g * HEAD_DIM:(g + 1) * HEAD_DIM, :] for g in range(Q_PER_KV)],
                             axis=1)
        km = km_ref[0, :, lo:lo + HEAD_DIM]
        gate = jnp.dot(km, qt, preferred_element_type=F32,
                       precision=lax.Precision.HIGHEST)
        sel = _top3_mask(gate, blk_row < i, axis=0)
        pen = jnp.where(sel, 0.0, MASKED).astype(BF16)
        qb = (qt * scale).astype(BF16)
        q_own = jnp.concatenate([qb, jnp.zeros((LANES - HEAD_DIM, rows), BF16)], axis=0)
        q_aug = jnp.concatenate([qb, pen, jnp.zeros((LANES - HEAD_DIM - nb, rows), BF16)], axis=0)

        s_own = jnp.dot(kh_ref[0, i, kvh], q_own, preferred_element_type=F32)
        s_own = jnp.where(own_ok, s_own, MASKED)
        s_all[nb] = s_own

        def scores(jj, m8):
            for u in range(2):
                j = 2 * jj + u
                s = jnp.dot(kh_ref[0, j, kvh], q_aug, preferred_element_type=F32)
                s_all[j] = s
                m8 = jnp.maximum(m8, fold(s))
            return m8

        m8 = lax.fori_loop(0, n_pairs, scores, fold(s_own))
        m = jnp.max(m8, axis=0, keepdims=True)

        p_own = jnp.exp2(s_all[nb] - m).astype(BF16)
        acc_scr[...] = jnp.dot(vt_ref[0, i, kvh], p_own, preferred_element_type=F32)

        def weighted(jj, carry):
            j = 2 * jj
            p = jnp.concatenate([jnp.exp2(s_all[j] - m).astype(BF16),
                                 jnp.exp2(s_all[j + 1] - m).astype(BF16)], axis=0)
            v2 = jnp.concatenate([vt_ref[0, j, kvh], vt_ref[0, j + 1, kvh]], axis=1)
            acc_scr[...] += jnp.dot(v2, p, preferred_element_type=F32)
            return carry

        lax.fori_loop(0, n_pairs, weighted, 0)
        acc = acc_scr[...]
        out_t = acc[:HEAD_DIM, :] / acc[HEAD_DIM:HEAD_DIM + 1, :]
        out4 = jnp.concatenate([out_t[:, g * MOBA_BLOCK:(g + 1) * MOBA_BLOCK]
                                for g in range(Q_PER_KV)], axis=0)
        o_ref[0, :, kvh * Q_PER_KV * HEAD_DIM:(kvh + 1) * Q_PER_KV * HEAD_DIM] = out4.T.astype(
            o_ref.dtype)


def _moba_prompt(qt, kh, vt, km):
    b, t = qt.shape[0], qt.shape[2]
    nb = t // MOBA_BLOCK
    rows = Q_PER_KV * MOBA_BLOCK
    return pl.pallas_call(
        functools.partial(_moba_prompt_body, nb=nb),
        grid=(b, nb),
        in_specs=[pl.BlockSpec((1, D_MODEL, MOBA_BLOCK), lambda bi, i: (bi, 0, i)),
                  pl.BlockSpec((1, nb, N_KV_HEADS, MOBA_BLOCK, LANES),
                               lambda bi, i: (bi, 0, 0, 0, 0)),
                  pl.BlockSpec((1, nb, N_KV_HEADS, V_AUG, MOBA_BLOCK),
                               lambda bi, i: (bi, 0, 0, 0, 0)),
                  pl.BlockSpec((1, nb, KV_DIM), lambda bi, i: (bi, 0, 0))],
        out_specs=pl.BlockSpec((1, MOBA_BLOCK, D_MODEL), lambda bi, i: (bi, i, 0)),
        out_shape=jax.ShapeDtypeStruct((b, t, D_MODEL), BF16),
        scratch_shapes=[pltpu.VMEM((nb + 1, MOBA_BLOCK, rows), F32),
                        pltpu.VMEM((V_AUG, rows), F32)],
        compiler_params=_params(("parallel", "arbitrary")),
        name="moba_prompt",
    )(qt, kh, vt, km)


def _page_copy(cache_hbm, buf, sem, page, slot):
    return pltpu.make_async_copy(cache_hbm.at[page], buf.at[pl.ds(slot * PAGE_SIZE, PAGE_SIZE)], sem)


def _moba_sample_body(pt_ref, q_ref, kn_ref, vn_ref, ck_hbm, cv_hbm, o_ref,
                      kbuf, vbuf, s_scr, km_scr, new_scr, sems, *, n_pages, n_new):
    b = pl.program_id(0)
    rows = q_ref.shape[1]
    nb = n_pages * PAGE_SIZE // MOBA_BLOCK
    scale = HEAD_DIM ** -0.5

    def start_page(p, carry):
        _page_copy(ck_hbm, kbuf, sems.at[0], pt_ref[b, p], p).start()
        _page_copy(cv_hbm, vbuf, sems.at[1], pt_ref[b, p], p).start()
        return carry

    lax.fori_loop(0, n_pages, start_page, 0)

    q64 = q_ref[0]
    qt = jnp.concatenate([q64] * N_KV_HEADS, axis=1)
    r_i = lax.broadcasted_iota(jnp.int32, qt.shape, 0)
    c_i = lax.broadcasted_iota(jnp.int32, qt.shape, 1)
    qpad = jnp.where(c_i // HEAD_DIM == r_i // (Q_PER_KV * n_new), qt, 0.0)
    qb = (qpad * scale).astype(BF16)

    def wait_k(p, carry):
        _page_copy(ck_hbm, kbuf, sems.at[0], 0, p).wait()
        return carry

    lax.fori_loop(0, n_pages, wait_k, 0)

    for j in range(nb):
        kj = kbuf[j * MOBA_BLOCK:(j + 1) * MOBA_BLOCK, :]
        km_scr[j:j + 1, :] = jnp.sum(kj, axis=0, keepdims=True) * (1.0 / MOBA_BLOCK)
        s_scr[:, j * MOBA_BLOCK:(j + 1) * MOBA_BLOCK] = lax.dot_general(
            qb, kj.astype(BF16), NT_DIMS, preferred_element_type=F32)

    gate = lax.dot_general(qpad, km_scr[...], NT_DIMS, preferred_element_type=F32,
                           precision=lax.Precision.HIGHEST)
    sel = _top3_mask(gate, jnp.full(gate.shape, True))
    pen = jnp.where(sel, 0.0, NEG_INF)

    new_scr[...] = jnp.zeros(new_scr.shape, F32)
    new_scr[0:n_new, :] = kn_ref[0]
    s_new = lax.dot_general(qb, new_scr[...].astype(BF16), NT_DIMS, preferred_element_type=F32)
    nr = lax.broadcasted_iota(jnp.int32, s_new.shape, 0)
    ncol = lax.broadcasted_iota(jnp.int32, s_new.shape, 1)
    s_new = jnp.where(ncol <= nr % n_new, s_new, NEG_INF)

    m = jnp.max(s_new, axis=1, keepdims=True)
    for j in range(nb):
        sj = s_scr[:, j * MOBA_BLOCK:(j + 1) * MOBA_BLOCK] + pen[:, j:j + 1]
        m = jnp.maximum(m, jnp.max(sj, axis=1, keepdims=True))

    def wait_v(p, carry):
        _page_copy(cv_hbm, vbuf, sems.at[1], 0, p).wait()
        return carry

    lax.fori_loop(0, n_pages, wait_v, 0)

    p_new = jnp.exp(s_new - m)
    l = jnp.sum(p_new, axis=1, keepdims=True)
    new_scr[0:n_new, :] = vn_ref[0]
    acc = jnp.dot(p_new.astype(BF16), new_scr[...].astype(BF16), preferred_element_type=F32)
    for j in range(nb):
        pj = jnp.exp(s_scr[:, j * MOBA_BLOCK:(j + 1) * MOBA_BLOCK] + pen[:, j:j + 1] - m)
        l = l + jnp.sum(pj, axis=1, keepdims=True)
        vj = vbuf[j * MOBA_BLOCK:(j + 1) * MOBA_BLOCK, :].astype(BF16)
        acc = acc + jnp.dot(pj.astype(BF16), vj, preferred_element_type=F32)
    acc = acc / l
    ro = lax.broadcasted_iota(jnp.int32, (rows, HEAD_DIM), 0) // (Q_PER_KV * n_new)
    out = jnp.zeros((rows, HEAD_DIM), F32)
    for kvh in range(N_KV_HEADS):
        out = jnp.where(ro == kvh, acc[:, kvh * HEAD_DIM:(kvh + 1) * HEAD_DIM], out)
    o_ref[0] = out


def _moba_sample(page_table, q_rows, k_new, v_new, cache_k, cache_v):
    nseq, n_pages = page_table.shape
    rows = q_rows.shape[1]
    n_new = k_new.shape[1]
    past = n_pages * PAGE_SIZE
    grid_spec = pltpu.PrefetchScalarGridSpec(
        num_scalar_prefetch=1,
        grid=(nseq,),
        in_specs=[pl.BlockSpec((1, rows, HEAD_DIM), lambda b, pt: (b, 0, 0)),
                  pl.BlockSpec((1, n_new, KV_DIM), lambda b, pt: (b, 0, 0)),
                  pl.BlockSpec((1, n_new, KV_DIM), lambda b, pt: (b, 0, 0)),
                  pl.BlockSpec(memory_space=pl.ANY),
                  pl.BlockSpec(memory_space=pl.ANY)],
        out_specs=pl.BlockSpec((1, rows, HEAD_DIM), lambda b, pt: (b, 0, 0)),
        scratch_shapes=[pltpu.VMEM((past, KV_DIM), F32), pltpu.VMEM((past, KV_DIM), F32),
                        pltpu.VMEM((rows, past), F32),
                        pltpu.VMEM((past // MOBA_BLOCK, KV_DIM), F32),
                        pltpu.VMEM((LANES, KV_DIM), F32),
                        pltpu.SemaphoreType.DMA((2,))],
    )
    return pl.pallas_call(
        functools.partial(_moba_sample_body, n_pages=n_pages, n_new=n_new),
        grid_spec=grid_spec,
        out_shape=jax.ShapeDtypeStruct((nseq, rows, HEAD_DIM), F32),
        compiler_params=_params(("arbitrary",)),
        name="moba_sample",
    )(page_table, q_rows, k_new, v_new, cache_k, cache_v)


def _rope_tables(pos):
    half = HEAD_DIM // 2
    inv = ROPE_THETA ** (-jnp.arange(half, dtype=F32) / half)
    ang = pos.astype(F32)[:, None] * inv[None, :]
    cos, sin = jnp.cos(ang), jnp.sin(ang)
    cos_h = jnp.concatenate([cos, cos], axis=1)
    sin_h = jnp.concatenate([-sin, sin], axis=1)
    reps = LANES // HEAD_DIM
    return jnp.tile(cos_h, (1, reps)), jnp.tile(sin_h, (1, reps))


def _prep_weights(norm_mix, norm_ffn, w_in_ssm, conv_w, conv_b, dt_bias, a_log, d_skip, norm_ssm,
                  w_out_ssm, norm_kv, w_kv, w_q, w_o, w_gu, w_down, norm_final):
    pad_h = LANES - N_SSM_HEADS
    w_in = w_in_ssm[0]
    return dict(
        g_mix0=norm_mix[0][None], g_mix1=norm_mix[1][None],
        g_ffn0=norm_ffn[0][None], g_ffn1=norm_ffn[1][None],
        wz=w_in[:, :D_INNER].astype(BF16),
        wx=w_in[:, D_INNER:D_INNER + CONV_DIM].astype(BF16),
        wdt=jnp.pad(w_in[:, D_INNER + CONV_DIM:], ((0, 0), (0, pad_h))).astype(BF16),
        cw=conv_w[0], cb=conv_b[0][None],
        dtb=jnp.pad(dt_bias[0], (0, pad_h))[None], alog=jnp.pad(a_log[0], (0, pad_h))[None],
        dsk=jnp.repeat(d_skip[0], SSM_HEAD_DIM)[None],
        gn=norm_ssm[0][None], wo_ssm=w_out_ssm[0].astype(BF16),
        g_kv=norm_kv[None], wkv=w_kv.astype(BF16), wq=w_q[0].astype(BF16),
        wo=w_o[0].astype(BF16),
        wg0=w_gu[0][:, :D_FF].astype(BF16), wu0=w_gu[0][:, D_FF:].astype(BF16),
        wd0=w_down[0].astype(BF16),
        wg1=w_gu[1][:, :D_FF].astype(BF16), wu1=w_gu[1][:, D_FF:].astype(BF16),
        wd1=w_down[1].astype(BF16),
        g_fin=norm_final[None],
    )


def _ssd_layer(x, conv_in, ssm0, w, tm, L, valid):
    b, t, _ = x.shape
    xf = x.reshape(b * t, D_MODEL)
    z, xbc, dtr = _inproj(xf, w["g_mix0"], w["wz"], w["wx"], w["wdt"], tm)
    tp = -(-t // L) * L
    xbc3 = xbc.reshape(b, t, CONV_DIM)
    dtr3 = dtr.reshape(b, t, LANES)
    if tp != t:
        xbc3 = jnp.pad(xbc3, ((0, 0), (0, tp - t), (0, 0)))
        dtr3 = jnp.pad(dtr3, ((0, 0), (0, tp - t), (0, 0)))
    conv0 = jnp.pad(conv_in, ((0, 0), (SUBLANES - (D_CONV - 1), 0), (0, 0)))
    y, ssm_new = _ssd(xbc3, dtr3, conv0, ssm0, w["cw"], w["cb"], w["dtb"], w["alog"], w["dsk"],
                      L, valid)
    y = y[:, :t].reshape(b * t, D_INNER)
    x2 = _ssm_out_ffn(y, z, xf, w["gn"], w["wo_ssm"], w["g_ffn0"], w["wg0"], w["wu0"], w["wd0"], tm)
    keep = D_CONV - 1
    conv_new = jnp.concatenate([conv_in[:, t:], xbc.reshape(b, t, CONV_DIM)[:, max(0, t - keep):]],
                               axis=1)
    return x2, conv_new, ssm_new


def kernel(x_prompt, x_sample, state_conv, state_ssm, cache_k, cache_v, page_table, norm_mix,
           norm_ffn, w_in_ssm, conv_w, conv_b, dt_bias, a_log, d_skip, norm_ssm, w_out_ssm,
           norm_kv, w_kv, w_q, w_o, w_gu, w_down, norm_final):
    w = _prep_weights(norm_mix, norm_ffn, w_in_ssm, conv_w, conv_b, dt_bias, a_log, d_skip,
                      norm_ssm, w_out_ssm, norm_kv, w_kv, w_q, w_o, w_gu, w_down, norm_final)
    bp, tp, _ = x_prompt.shape
    bs, ts, _ = x_sample.shape
    past_len = page_table.shape[1] * PAGE_SIZE

    tm_p = 256
    conv0_p = jnp.zeros((bp, D_CONV - 1, CONV_DIM), F32)
    ssm0_p = jnp.zeros((bp, N_SSM_HEADS, SSM_HEAD_DIM, D_STATE), F32)
    x2_p, conv_p, ssm_p = _ssd_layer(x_prompt, conv0_p, ssm0_p, w, tm_p, SSD_CHUNK, SSD_CHUNK)
    cos_p, sin_p = _rope_tables(jnp.arange(tp, dtype=jnp.int32))
    k_p, v_p, qt_p, kh_p, vt_p, km_p = _kvq_prompt(x2_p, bp, tp, w["g_kv"], w["g_mix1"], w["wkv"],
                                                   w["wq"], cos_p, sin_p)
    attn_p = _moba_prompt(qt_p, kh_p, vt_p, km_p.reshape(bp, tp // MOBA_BLOCK, KV_DIM))
    y_p = _attn_out_ffn(attn_p.reshape(bp * tp, D_MODEL), x2_p, w["wo"], w["g_ffn1"], w["wg1"],
                        w["wu1"], w["wd1"], w["g_fin"], tm_p)

    tm_s = bs * ts
    x2_s, conv_s, ssm_s = _ssd_layer(x_sample, state_conv[0], state_ssm[0], w, tm_s, SUBLANES, ts)
    pos_s = past_len + jnp.tile(jnp.arange(ts, dtype=jnp.int32), bs)
    cos_s, sin_s = _rope_tables(pos_s)
    k_s, v_s, q_s = _kvq(x2_s, w["g_kv"], w["g_mix1"], w["wkv"], w["wq"], cos_s, sin_s, tm_s)
    q_rows = q_s.reshape(bs, ts, N_HEADS, HEAD_DIM).transpose(0, 2, 1, 3).reshape(
        bs, N_HEADS * ts, HEAD_DIM)
    q_rows = jnp.pad(q_rows, ((0, 0), (0, LANES - N_HEADS * ts), (0, 0)))
    attn_rows = _moba_sample(page_table, q_rows, k_s.reshape(bs, ts, KV_DIM),
                             v_s.reshape(bs, ts, KV_DIM),
                             cache_k.reshape(cache_k.shape[0], PAGE_SIZE, KV_DIM),
                             cache_v.reshape(cache_v.shape[0], PAGE_SIZE, KV_DIM))
    attn_s = attn_rows[:, :N_HEADS * ts].reshape(bs, N_HEADS, ts, HEAD_DIM).transpose(
        0, 2, 1, 3).reshape(bs * ts, D_MODEL).astype(BF16)
    y_s = _attn_out_ffn(attn_s, x2_s, w["wo"], w["g_ffn1"], w["wg1"], w["wu1"], w["wd1"],
                        w["g_fin"], tm_s)

    return (y_p.reshape(bp, tp, D_MODEL), y_s.reshape(bs, ts, D_MODEL),
            conv_p[None], ssm_p[None],
            k_p.reshape(bp, tp, N_KV_HEADS, HEAD_DIM), v_p.reshape(bp, tp, N_KV_HEADS, HEAD_DIM),
            conv_s[None], ssm_s[None],
            k_s.reshape(bs, ts, N_KV_HEADS, HEAD_DIM), v_s.reshape(bs, ts, N_KV_HEADS, HEAD_DIM))
```

```python
import functools

import jax
import jax.numpy as jnp
from jax import lax
from jax.experimental import pallas as pl
from jax.experimental.pallas import tpu as pltpu

F32 = jnp.float32
BF16 = jnp.bfloat16

D_MODEL = 1024
D_INNER = 2048
SSM_HEAD_DIM = 64
N_SSM_HEADS = 32
N_SSM_GROUPS = 4
HEADS_PER_GROUP = 8
D_STATE = 128
D_CONV = 4
SSD_CHUNK = 128
GN = N_SSM_GROUPS * D_STATE
CONV_DIM = D_INNER + 2 * GN
HEAD_DIM = 64
N_HEADS = 16
N_KV_HEADS = 4
Q_PER_KV = 4
KV_DIM = N_KV_HEADS * HEAD_DIM
MOBA_BLOCK = 256
MOBA_TOP_K = 3
ROPE_THETA = 10000.0
D_FF = 2816
EPS = 1e-6
PAGE_SIZE = 128

LANES = 128
SUBLANES = 8
VMEM_LIMIT = 56 * 1024 * 1024

NT_DIMS = (((1,), (1,)), ((), ()))
TN_DIMS = (((0,), (0,)), ((), ()))
NEG_INF = float("-inf")
MASKED = -1e30
LOG2_E = 1.4426950408889634
V_AUG = HEAD_DIM + 16
ATTN_UNIT_HEADS = 4


def _params(sem):
    return pltpu.CompilerParams(dimension_semantics=sem, vmem_limit_bytes=VMEM_LIMIT)


def _const_spec(shape):
    nd = len(shape)
    return pl.BlockSpec(shape, lambda *_: (0,) * nd, pipeline_mode=pl.Buffered(1))


def _silu(x):
    return x * (1.0 / (1.0 + jnp.exp(-x)))


def _inv_rms(x):
    return lax.rsqrt(jnp.mean(x * x, axis=-1, keepdims=True) + EPS)


def _split3(x):
    hi = x.astype(BF16)
    r = x - hi.astype(F32)
    mid = r.astype(BF16)
    lo = (r - mid.astype(F32)).astype(BF16)
    return hi, mid, lo


def _dot01(a01, x, dims, a_is_lhs=True):
    out = None
    for p in _split3(x):
        t = (lax.dot_general(a01, p, dims, preferred_element_type=F32) if a_is_lhs
             else lax.dot_general(p, a01, dims, preferred_element_type=F32))
        out = t if out is None else out + t
    return out


def _top3_mask(gate, valid, axis=1):
    nb = gate.shape[axis]
    col = lax.broadcasted_iota(jnp.int32, gate.shape, axis).astype(F32)
    g = jnp.where(valid, gate, NEG_INF)
    sel = jnp.zeros(gate.shape, F32)
    for _ in range(MOBA_TOP_K):
        m = jnp.max(g, axis=axis, keepdims=True)
        idx = jnp.min(jnp.where(g == m, col, float(nb)), axis=axis, keepdims=True)
        pick = col == idx
        sel = jnp.where(pick, 1.0, sel)
        g = jnp.where(pick, NEG_INF, g)
    return jnp.logical_and(sel > 0.5, valid)


def _inproj_body(x_ref, g_ref, wz_ref, wx_ref, wdt_ref, z_ref, xbc_ref, dt_ref):
    x = x_ref[...]
    h = (x * _inv_rms(x) * g_ref[...]).astype(BF16)
    z_ref[...] = jnp.dot(h, wz_ref[...], preferred_element_type=F32)
    xbc_ref[...] = jnp.dot(h, wx_ref[...], preferred_element_type=F32)
    dt_ref[...] = jnp.dot(h, wdt_ref[...], preferred_element_type=F32)


def _inproj(x, g, wz, wx, wdt, tm):
    m = x.shape[0]
    row = lambda w: pl.BlockSpec((tm, w), lambda i: (i, 0))
    return pl.pallas_call(
        _inproj_body,
        grid=(m // tm,),
        in_specs=[row(D_MODEL), _const_spec((1, D_MODEL)), _const_spec(wz.shape),
                  _const_spec(wx.shape), _const_spec(wdt.shape)],
        out_specs=[row(D_INNER), row(CONV_DIM), row(LANES)],
        out_shape=[jax.ShapeDtypeStruct((m, D_INNER), F32),
                   jax.ShapeDtypeStruct((m, CONV_DIM), F32),
                   jax.ShapeDtypeStruct((m, LANES), F32)],
        compiler_params=_params(("parallel",)),
        name="ssd_inproj",
    )(x, g, wz, wx, wdt)


def _ssd_body(xbc_ref, dtr_ref, conv0_ref, ssm0_ref, cw_ref, cb_ref, dtb_ref, alog_ref, dsk_ref,
              y_ref, st_ref, pad_ref, *, L, valid):
    c = pl.program_id(1)

    @pl.when(c == 0)
    def _():
        pad_ref[0:SUBLANES, :] = conv0_ref[0]
        st_ref[0] = ssm0_ref[0]

    pad_ref[SUBLANES:SUBLANES + L, :] = xbc_ref[0]
    conv = cb_ref[...]
    for i in range(D_CONV):
        off = SUBLANES - (D_CONV - 1) + i
        conv = conv + pad_ref[off:off + L, :] * cw_ref[i:i + 1, :]
    pad_ref[0:SUBLANES, :] = pad_ref[L:L + SUBLANES, :]
    act = _silu(conv)
    xs = act[:, :D_INNER]
    bm = act[:, D_INNER:D_INNER + GN].astype(BF16)
    cm = act[:, D_INNER + GN:].astype(BF16)

    t = dtr_ref[0] + dtb_ref[...]
    dt = jnp.maximum(t, 0.0) + jnp.log1p(jnp.exp(-jnp.abs(t)))
    if valid < L:
        rows = lax.broadcasted_iota(jnp.int32, dt.shape, 0)
        dt = jnp.where(rows < valid, dt, 0.0)
    a = -jnp.exp(alog_ref[...])
    dta = dt * a

    r_i = lax.broadcasted_iota(jnp.int32, (L, L), 0)
    c_i = lax.broadcasted_iota(jnp.int32, (L, L), 1)
    causal = r_i >= c_i
    tril = jnp.where(causal, 1.0, 0.0).astype(BF16)
    e_r = lax.broadcasted_iota(jnp.int32, (LANES, LANES), 0)
    e_c = lax.broadcasted_iota(jnp.int32, (LANES, LANES), 1)
    eye = jnp.where(e_r == e_c, 1.0, 0.0).astype(BF16)

    acs = _dot01(tril, dta, (((1,), (0,)), ((), ())))
    acs_t = _dot01(eye, acs, NT_DIMS)
    last = acs[L - 1:L, :]
    dec_end = jnp.exp(last - acs)
    e_acs = jnp.exp(acs)
    dec_chunk = jnp.exp(last)

    for g in range(N_SSM_GROUPS):
        bg = bm[:, g * D_STATE:(g + 1) * D_STATE]
        cg = cm[:, g * D_STATE:(g + 1) * D_STATE]
        cb = lax.dot_general(cg, bg, NT_DIMS, preferred_element_type=F32)
        for jp in range(HEADS_PER_GROUP // 2):
            pair = []
            for h in (g * HEADS_PER_GROUP + 2 * jp, g * HEADS_PER_GROUP + 2 * jp + 1):
                diff = acs[:, h:h + 1] - acs_t[h:h + 1, :]
                seg = jnp.exp(jnp.where(causal, diff, NEG_INF))
                w = (cb * seg).astype(BF16)
                xs_h = xs[:, h * SSM_HEAD_DIM:(h + 1) * SSM_HEAD_DIM]
                xd_h = xs_h * dt[:, h:h + 1]
                y_diag = jnp.dot(w, xd_h.astype(BF16), preferred_element_type=F32)
                st_h = st_ref[0, h]
                y_off = lax.dot_general(cg, st_h.astype(BF16), NT_DIMS,
                                        preferred_element_type=F32) * e_acs[:, h:h + 1]
                xdd = (xd_h * dec_end[:, h:h + 1]).astype(BF16)
                cs = lax.dot_general(xdd, bg, TN_DIMS, preferred_element_type=F32)
                st_ref[0, h] = st_h * dec_chunk[:, h:h + 1] + cs
                pair.append(y_diag + y_off
                            + dsk_ref[:, h * SSM_HEAD_DIM:(h + 1) * SSM_HEAD_DIM] * xs_h)
            lo = (g * HEADS_PER_GROUP + 2 * jp) * SSM_HEAD_DIM
            y_ref[0, :, lo:lo + 2 * SSM_HEAD_DIM] = jnp.concatenate(pair, axis=1)


def _ssd(xbc, dtr, conv0, ssm0, cw, cb, dtb, alog, dsk, L, valid):
    nb, t = xbc.shape[0], xbc.shape[1]
    nc = t // L
    st_spec = pl.BlockSpec((1, N_SSM_HEADS, SSM_HEAD_DIM, D_STATE), lambda b, c: (b, 0, 0, 0))
    return pl.pallas_call(
        functools.partial(_ssd_body, L=L, valid=valid),
        grid=(nb, nc),
        in_specs=[pl.BlockSpec((1, L, CONV_DIM), lambda b, c: (b, c, 0)),
                  pl.BlockSpec((1, L, LANES), lambda b, c: (b, c, 0)),
                  pl.BlockSpec((1, SUBLANES, CONV_DIM), lambda b, c: (b, 0, 0)),
                  st_spec,
                  _const_spec(cw.shape), _const_spec(cb.shape), _const_spec(dtb.shape),
                  _const_spec(alog.shape), _const_spec(dsk.shape)],
        out_specs=[pl.BlockSpec((1, L, D_INNER), lambda b, c: (b, c, 0)), st_spec],
        out_shape=[jax.ShapeDtypeStruct((nb, t, D_INNER), F32),
                   jax.ShapeDtypeStruct(ssm0.shape, F32)],
        scratch_shapes=[pltpu.VMEM((L + SUBLANES, CONV_DIM), F32)],
        compiler_params=_params(("parallel", "arbitrary")),
        name="ssd_scan",
    )(xbc, dtr, conv0, ssm0, cw, cb, dtb, alog, dsk)


def _ffn(x1, nf_ref, wg_ref, wu_ref, wd_ref):
    h = (x1 * _inv_rms(x1) * nf_ref[...]).astype(BF16)
    gate = jnp.dot(h, wg_ref[...], preferred_element_type=F32)
    up = jnp.dot(h, wu_ref[...], preferred_element_type=F32)
    act = (_silu(gate) * up).astype(BF16)
    return x1 + jnp.dot(act, wd_ref[...], preferred_element_type=F32)


def _ssm_out_ffn_body(y_ref, z_ref, x_ref, gn_ref, wo_ref, nf_ref, wg_ref, wu_ref, wd_ref, o_ref):
    y = y_ref[...] * _silu(z_ref[...])
    gw = D_INNER // N_SSM_GROUPS
    parts = []
    for g in range(N_SSM_GROUPS):
        yg = y[:, g * gw:(g + 1) * gw]
        parts.append(yg * _inv_rms(yg))
    y = (jnp.concatenate(parts, axis=1) * gn_ref[...]).astype(BF16)
    x1 = x_ref[...] + jnp.dot(y, wo_ref[...], preferred_element_type=F32)
    o_ref[...] = _ffn(x1, nf_ref, wg_ref, wu_ref, wd_ref)


def _attn_out_ffn_body(a_ref, x_ref, wo_ref, nf_ref, wg_ref, wu_ref, wd_ref, nfin_ref, o_ref):
    x1 = x_ref[...] + jnp.dot(a_ref[...], wo_ref[...], preferred_element_type=F32)
    x2 = _ffn(x1, nf_ref, wg_ref, wu_ref, wd_ref)
    o_ref[...] = x2 * _inv_rms(x2) * nfin_ref[...]


def _ssm_out_ffn(y, z, x, gn, wo, nf, wg, wu, wd, tm):
    m = x.shape[0]
    row = lambda w: pl.BlockSpec((tm, w), lambda i: (i, 0))
    return pl.pallas_call(
        _ssm_out_ffn_body,
        grid=(m // tm,),
        in_specs=[row(D_INNER), row(D_INNER), row(D_MODEL), _const_spec(gn.shape),
                  _const_spec(wo.shape), _const_spec(nf.shape), _const_spec(wg.shape),
                  _const_spec(wu.shape), _const_spec(wd.shape)],
        out_specs=row(D_MODEL),
        out_shape=jax.ShapeDtypeStruct((m, D_MODEL), F32),
        compiler_params=_params(("parallel",)),
        name="ssm_out_ffn",
    )(y, z, x, gn, wo, nf, wg, wu, wd)


def _attn_out_ffn(a, x, wo, nf, wg, wu, wd, nfin, tm):
    m = x.shape[0]
    row = lambda w: pl.BlockSpec((tm, w), lambda i: (i, 0))
    return pl.pallas_call(
        _attn_out_ffn_body,
        grid=(m // tm,),
        in_specs=[row(D_MODEL), row(D_MODEL), _const_spec(wo.shape), _const_spec(nf.shape),
                  _const_spec(wg.shape), _const_spec(wu.shape), _const_spec(wd.shape),
                  _const_spec(nfin.shape)],
        out_specs=row(D_MODEL),
        out_shape=jax.ShapeDtypeStruct((m, D_MODEL), F32),
        compiler_params=_params(("parallel",)),
        name="attn_out_ffn",
    )(a, x, wo, nf, wg, wu, wd, nfin)


def _rope(x, cos, sin_signed, first_half):
    outs = []
    for c in range(x.shape[1] // LANES):
        xc = x[:, c * LANES:(c + 1) * LANES]
        partner = jnp.where(first_half, pltpu.roll(xc, LANES - HEAD_DIM // 2, 1),
                            pltpu.roll(xc, HEAD_DIM // 2, 1))
        outs.append(xc * cos + partner * sin_signed)
    return jnp.concatenate(outs, axis=1)


def _kvq_compute(x_ref, gkv_ref, gq_ref, wkv_ref, wq_ref, cos_ref, sin_ref):
    x = x_ref[...]
    xn = x * _inv_rms(x)
    hk = (xn * gkv_ref[...]).astype(BF16)
    hq = (xn * gq_ref[...]).astype(BF16)
    cos = cos_ref[...]
    sin = sin_ref[...]
    lane = lax.broadcasted_iota(jnp.int32, cos.shape, 1)
    first_half = (lane % HEAD_DIM) < (HEAD_DIM // 2)
    kv = jnp.dot(hk, wkv_ref[...], preferred_element_type=F32)
    k = _rope(kv[:, :KV_DIM], cos, sin, first_half)
    q = _rope(jnp.dot(hq, wq_ref[...], preferred_element_type=F32), cos, sin, first_half)
    return k, kv[:, KV_DIM:], q


def _kvq_body(x_ref, gkv_ref, gq_ref, wkv_ref, wq_ref, cos_ref, sin_ref, k_ref, v_ref, q_ref):
    k, v, q = _kvq_compute(x_ref, gkv_ref, gq_ref, wkv_ref, wq_ref, cos_ref, sin_ref)
    k_ref[...] = k
    v_ref[...] = v
    q_ref[...] = q


def _kvq_prompt_body(x_ref, gkv_ref, gq_ref, wkv_ref, wq_ref, cos_ref, sin_ref,
                     kt_ref, vt32_ref, qt_ref, kh_ref, vt_ref, km_ref):
    k, v, q = _kvq_compute(x_ref, gkv_ref, gq_ref, wkv_ref, wq_ref, cos_ref, sin_ref)
    kt_ref[0] = k.T
    v_t = v.T
    vt32_ref[0] = v_t
    qt_ref[0] = q.T
    vt = v_t.astype(BF16)
    blk = pl.program_id(1)
    col = lax.broadcasted_iota(jnp.int32, (MOBA_BLOCK, LANES - HEAD_DIM), 1)
    onehot = jnp.where(col == blk, 1.0, 0.0).astype(BF16)
    row = lax.broadcasted_iota(jnp.int32, (V_AUG - HEAD_DIM, MOBA_BLOCK), 0)
    ones_row = jnp.where(row == 0, 1.0, 0.0).astype(BF16)
    for kvh in range(N_KV_HEADS):
        kh_ref[0, 0, kvh] = jnp.concatenate(
            [k[:, kvh * HEAD_DIM:(kvh + 1) * HEAD_DIM].astype(BF16), onehot], axis=1)
        vt_ref[0, 0, kvh] = jnp.concatenate(
            [vt[kvh * HEAD_DIM:(kvh + 1) * HEAD_DIM, :], ones_row], axis=0)
    km_ref[0, 0] = jnp.sum(k, axis=0, keepdims=True) * (1.0 / MOBA_BLOCK)


def _kvq(x, gkv, gq, wkv, wq, cos, sin, tm):
    m = x.shape[0]
    row = lambda w: pl.BlockSpec((tm, w), lambda i: (i, 0))
    tab = pl.BlockSpec((tm, LANES), lambda i: (0, 0))
    return pl.pallas_call(
        _kvq_body,
        grid=(m // tm,),
        in_specs=[row(D_MODEL), _const_spec(gkv.shape), _const_spec(gq.shape),
                  _const_spec(wkv.shape), _const_spec(wq.shape), tab, tab],
        out_specs=[row(KV_DIM), row(KV_DIM), row(D_MODEL)],
        out_shape=[jax.ShapeDtypeStruct((m, KV_DIM), F32), jax.ShapeDtypeStruct((m, KV_DIM), F32),
                   jax.ShapeDtypeStruct((m, D_MODEL), F32)],
        compiler_params=_params(("parallel",)),
        name="kvq_proj",
    )(x, gkv, gq, wkv, wq, cos, sin)


def _kvq_prompt(x, b, t, gkv, gq, wkv, wq, cos, sin):
    nb = t // MOBA_BLOCK
    tm = MOBA_BLOCK
    row = lambda w: pl.BlockSpec((tm, w), lambda bi, i: (bi * nb + i, 0))
    tab = pl.BlockSpec((tm, LANES), lambda bi, i: (i, 0))
    return pl.pallas_call(
        _kvq_prompt_body,
        grid=(b, nb),
        in_specs=[row(D_MODEL), _const_spec(gkv.shape), _const_spec(gq.shape),
                  _const_spec(wkv.shape), _const_spec(wq.shape), tab, tab],
        out_specs=[pl.BlockSpec((1, KV_DIM, tm), lambda bi, i: (bi, 0, i)),
                   pl.BlockSpec((1, KV_DIM, tm), lambda bi, i: (bi, 0, i)),
                   pl.BlockSpec((1, D_MODEL, tm), lambda bi, i: (bi, 0, i)),
                   pl.BlockSpec((1, 1, N_KV_HEADS, tm, LANES), lambda bi, i: (bi, i, 0, 0, 0)),
                   pl.BlockSpec((1, 1, N_KV_HEADS, V_AUG, tm), lambda bi, i: (bi, i, 0, 0, 0)),
                   pl.BlockSpec((1, 1, 1, KV_DIM), lambda bi, i: (bi, i, 0, 0))],
        out_shape=[jax.ShapeDtypeStruct((b, KV_DIM, t), F32),
                   jax.ShapeDtypeStruct((b, KV_DIM, t), F32),
                   jax.ShapeDtypeStruct((b, D_MODEL, t), F32),
                   jax.ShapeDtypeStruct((b, nb, N_KV_HEADS, tm, LANES), BF16),
                   jax.ShapeDtypeStruct((b, nb, N_KV_HEADS, V_AUG, tm), BF16),
                   jax.ShapeDtypeStruct((b, nb, 1, KV_DIM), F32)],
        compiler_params=_params(("parallel", "parallel")),
        name="kvq_proj_prompt",
    )(x, gkv, gq, wkv, wq, cos, sin)


def _moba_prompt_body(qt_ref, kh_ref, vt_ref, km_ref, o_ref, s_a, s_b, acc_a, acc_b, *, nb):
    i = pl.program_id(1)
    g_per_unit = ATTN_UNIT_HEADS
    rows = g_per_unit * MOBA_BLOCK
    n_units = N_HEADS // g_per_unit
    scale = (HEAD_DIM ** -0.5) * LOG2_E
    n_pairs = (i + 1) // 2
    blk_row = lax.broadcasted_iota(jnp.int32, (nb, rows), 0)
    key_i = lax.broadcasted_iota(jnp.int32, (MOBA_BLOCK, rows), 0)
    tok_i = lax.broadcasted_iota(jnp.int32, (MOBA_BLOCK, rows), 1) % MOBA_BLOCK
    own_ok = key_i <= tok_i
    fold = lambda s: jnp.max(s.reshape(MOBA_BLOCK // SUBLANES, SUBLANES, rows), axis=0)

    def setup(u):
        kvh = (u * g_per_unit) // Q_PER_KV
        lo = kvh * HEAD_DIM
        base = u * g_per_unit * HEAD_DIM
        qt2 = qt_ref[0, base:base + g_per_unit * HEAD_DIM, :]
        qt = jnp.concatenate([qt2[g * HEAD_DIM:(g + 1) * HEAD_DIM, :] for g in range(g_per_unit)],
                             axis=1)
        km = km_ref[0, :, lo:lo + HEAD_DIM]
        gate = jnp.dot(km, qt, preferred_element_type=F32,
                       precision=lax.Precision.HIGHEST)
        sel = _top3_mask(gate, blk_row < i, axis=0)
        pen = jnp.where(sel, 0.0, MASKED).astype(BF16)
        qb = (qt * scale).astype(BF16)
        q_own = jnp.concatenate([qb, jnp.zeros((LANES - HEAD_DIM, rows), BF16)], axis=0)
        q_aug = jnp.concatenate([qb, pen, jnp.zeros((LANES - HEAD_DIM - nb, rows), BF16)], axis=0)
        s_own = jnp.dot(kh_ref[0, i, kvh], q_own, preferred_element_type=F32)
        s_own = jnp.where(own_ok, s_own, MASKED)
        s_bufs[u % 2][nb] = s_own
        return kvh, q_aug, fold(s_own)

    def scores_pair(u, kvh, q_aug, jj, m8):
        for t in range(2):
            j = 2 * jj + t
            s = jnp.dot(kh_ref[0, j, kvh], q_aug, preferred_element_type=F32)
            s_bufs[u % 2][j] = s
            m8 = jnp.maximum(m8, fold(s))
        return m8

    def weighted_pair(u, kvh, m, jj):
        j = 2 * jj
        s_buf = s_bufs[u % 2]
        p = jnp.concatenate([jnp.exp2(s_buf[j] - m).astype(BF16),
                             jnp.exp2(s_buf[j + 1] - m).astype(BF16)], axis=0)
        v2 = jnp.concatenate([vt_ref[0, j, kvh], vt_ref[0, j + 1, kvh]], axis=1)
        acc_bufs[u % 2][...] += jnp.dot(v2, p, preferred_element_type=F32)

    s_bufs = (s_a, s_b)
    acc_bufs = (acc_a, acc_b)
    kvh, q_aug, m8 = setup(0)
    m8 = lax.fori_loop(0, n_pairs, functools.partial(scores_pair, 0, kvh, q_aug), m8)
    for u in range(n_units):
        m = jnp.max(m8, axis=0, keepdims=True)
        p_own = jnp.exp2(s_bufs[u % 2][nb] - m).astype(BF16)
        acc_bufs[u % 2][...] = jnp.dot(vt_ref[0, i, kvh], p_own, preferred_element_type=F32)
        if u + 1 < n_units:
            kvh_n, q_aug_n, m8_n = setup(u + 1)

            def both(jj, m8c, u=u, kvh=kvh, m=m, kvh_n=kvh_n, q_aug_n=q_aug_n):
                m8c = scores_pair(u + 1, kvh_n, q_aug_n, jj, m8c)
                weighted_pair(u, kvh, m, jj)
                return m8c

            m8_n = lax.fori_loop(0, n_pairs, both, m8_n)
        else:
            def last(jj, carry, u=u, kvh=kvh, m=m):
                weighted_pair(u, kvh, m, jj)
                return carry

            lax.fori_loop(0, n_pairs, last, 0)
        acc = acc_bufs[u % 2][...]
        out_t = acc[:HEAD_DIM, :] / acc[HEAD_DIM:HEAD_DIM + 1, :]
        out2 = jnp.concatenate([out_t[:, g * MOBA_BLOCK:(g + 1) * MOBA_BLOCK]
                                for g in range(g_per_unit)], axis=0)
        base = u * g_per_unit * HEAD_DIM
        o_ref[0, :, base:base + g_per_unit * HEAD_DIM] = out2.T.astype(o_ref.dtype)
        if u + 1 < n_units:
            kvh, m8 = kvh_n, m8_n


def _moba_prompt(qt, kh, vt, km):
    b, t = qt.shape[0], qt.shape[2]
    nb = t // MOBA_BLOCK
    rows = ATTN_UNIT_HEADS * MOBA_BLOCK
    return pl.pallas_call(
        functools.partial(_moba_prompt_body, nb=nb),
        grid=(b, nb),
        in_specs=[pl.BlockSpec((1, D_MODEL, MOBA_BLOCK), lambda bi, i: (bi, 0, i)),
                  pl.BlockSpec((1, nb, N_KV_HEADS, MOBA_BLOCK, LANES),
                               lambda bi, i: (bi, 0, 0, 0, 0), pipeline_mode=pl.Buffered(1)),
                  pl.BlockSpec((1, nb, N_KV_HEADS, V_AUG, MOBA_BLOCK),
                               lambda bi, i: (bi, 0, 0, 0, 0), pipeline_mode=pl.Buffered(1)),
                  pl.BlockSpec((1, nb, KV_DIM), lambda bi, i: (bi, 0, 0))],
        out_specs=pl.BlockSpec((1, MOBA_BLOCK, D_MODEL), lambda bi, i: (bi, i, 0)),
        out_shape=jax.ShapeDtypeStruct((b, t, D_MODEL), BF16),
        scratch_shapes=[pltpu.VMEM((nb + 1, MOBA_BLOCK, rows), F32),
                        pltpu.VMEM((nb + 1, MOBA_BLOCK, rows), F32),
                        pltpu.VMEM((V_AUG, rows), F32), pltpu.VMEM((V_AUG, rows), F32)],
        compiler_params=_params(("parallel", "arbitrary")),
        name="moba_prompt",
    )(qt, kh, vt, km)


def _page_copy(cache_hbm, buf, sem, page, slot, p):
    return pltpu.make_async_copy(cache_hbm.at[page], buf.at[slot, p], sem)


def _moba_sample_body(pt_ref, q_ref, kn_ref, vn_ref, ck_hbm, cv_hbm, o_ref,
                      kbuf, vbuf, s_scr, new_scr, sems, *, n_pages, n_new):
    b = pl.program_id(0)
    slot = b % 2
    rows = q_ref.shape[1]
    ppb = MOBA_BLOCK // PAGE_SIZE
    nb = n_pages // ppb
    scale = HEAD_DIM ** -0.5

    def fetch(seq, to_slot):
        def start_page(p, carry):
            page = pt_ref[seq, p]
            _page_copy(ck_hbm, kbuf, sems.at[0, to_slot], page, to_slot, p).start()
            _page_copy(cv_hbm, vbuf, sems.at[1, to_slot], page, to_slot, p).start()
            return carry
        lax.fori_loop(0, n_pages, start_page, 0)

    @pl.when(b == 0)
    def _():
        fetch(0, 0)

    @pl.when(b + 1 < pl.num_programs(0))
    def _():
        fetch(b + 1, 1 - slot)

    q64 = q_ref[0]
    qt = jnp.concatenate([q64] * N_KV_HEADS, axis=1)
    r_i = lax.broadcasted_iota(jnp.int32, qt.shape, 0)
    c_i = lax.broadcasted_iota(jnp.int32, qt.shape, 1)
    qpad = jnp.where(c_i // HEAD_DIM == r_i // (Q_PER_KV * n_new), qt, 0.0)
    qb = (qpad * scale).astype(BF16)

    def wait_k(p, carry):
        _page_copy(ck_hbm, kbuf, sems.at[0, slot], 0, slot, p).wait()
        return carry

    lax.fori_loop(0, n_pages, wait_k, 0)

    blk_lane = lax.broadcasted_iota(jnp.int32, (KV_DIM, LANES), 1)
    km_t = jnp.zeros((KV_DIM, LANES), F32)
    for j in range(nb):
        kt = jnp.concatenate([kbuf[slot, ppb * j + u] for u in range(ppb)], axis=1)
        km_col = jnp.sum(kt, axis=1, keepdims=True) * (1.0 / MOBA_BLOCK)
        km_t = jnp.where(blk_lane == j, km_col, km_t)
        s_scr[:, j * MOBA_BLOCK:(j + 1) * MOBA_BLOCK] = jnp.dot(
            qb, kt.astype(BF16), preferred_element_type=F32)

    gate = jnp.dot(qpad, km_t, preferred_element_type=F32,
                   precision=lax.Precision.HIGHEST)
    sel = _top3_mask(gate, lax.broadcasted_iota(jnp.int32, gate.shape, 1) < nb)
    pen = jnp.where(sel, 0.0, NEG_INF)

    new_scr[...] = jnp.zeros(new_scr.shape, F32)
    new_scr[0:n_new, :] = kn_ref[0]
    s_new = lax.dot_general(qb, new_scr[...].astype(BF16), NT_DIMS, preferred_element_type=F32)
    nr = lax.broadcasted_iota(jnp.int32, s_new.shape, 0)
    ncol = lax.broadcasted_iota(jnp.int32, s_new.shape, 1)
    s_new = jnp.where(ncol <= nr % n_new, s_new, NEG_INF)

    def lane_fold(x, op):
        out = x[:, :LANES]
        for c in range(1, x.shape[1] // LANES):
            out = op(out, x[:, c * LANES:(c + 1) * LANES])
        return out

    m_part = s_new
    for j in range(nb):
        sj = s_scr[:, j * MOBA_BLOCK:(j + 1) * MOBA_BLOCK] + pen[:, j:j + 1]
        m_part = jnp.maximum(m_part, lane_fold(sj, jnp.maximum))
    m = jnp.max(m_part, axis=1, keepdims=True)

    def wait_v(p, carry):
        _page_copy(cv_hbm, vbuf, sems.at[1, slot], 0, slot, p).wait()
        return carry

    lax.fori_loop(0, n_pages, wait_v, 0)

    p_new = jnp.exp(s_new - m)
    l_part = p_new
    new_scr[0:n_new, :] = vn_ref[0]
    acc = jnp.dot(p_new.astype(BF16), new_scr[...].astype(BF16), preferred_element_type=F32)
    for j in range(nb):
        pj = jnp.exp(s_scr[:, j * MOBA_BLOCK:(j + 1) * MOBA_BLOCK] + pen[:, j:j + 1] - m)
        l_part = l_part + lane_fold(pj, jnp.add)
        vt = jnp.concatenate([vbuf[slot, ppb * j + u] for u in range(ppb)], axis=1).astype(BF16)
        acc = acc + lax.dot_general(pj.astype(BF16), vt, NT_DIMS, preferred_element_type=F32)
    acc = acc / jnp.sum(l_part, axis=1, keepdims=True)
    ro = lax.broadcasted_iota(jnp.int32, (rows, HEAD_DIM), 0) // (Q_PER_KV * n_new)
    out = jnp.zeros((rows, HEAD_DIM), F32)
    for kvh in range(N_KV_HEADS):
        out = jnp.where(ro == kvh, acc[:, kvh * HEAD_DIM:(kvh + 1) * HEAD_DIM], out)
    o_ref[0] = out


def _moba_sample(page_table, q_rows, k_new, v_new, cache_k, cache_v):
    nseq, n_pages = page_table.shape
    rows = q_rows.shape[1]
    n_new = k_new.shape[1]
    past = n_pages * PAGE_SIZE
    grid_spec = pltpu.PrefetchScalarGridSpec(
        num_scalar_prefetch=1,
        grid=(nseq,),
        in_specs=[pl.BlockSpec((1, rows, HEAD_DIM), lambda b, pt: (b, 0, 0)),
                  pl.BlockSpec((1, n_new, KV_DIM), lambda b, pt: (b, 0, 0)),
                  pl.BlockSpec((1, n_new, KV_DIM), lambda b, pt: (b, 0, 0)),
                  pl.BlockSpec(memory_space=pl.ANY),
                  pl.BlockSpec(memory_space=pl.ANY)],
        out_specs=pl.BlockSpec((1, rows, HEAD_DIM), lambda b, pt: (b, 0, 0)),
        scratch_shapes=[pltpu.VMEM((2, n_pages, KV_DIM, PAGE_SIZE), F32),
                        pltpu.VMEM((2, n_pages, KV_DIM, PAGE_SIZE), F32),
                        pltpu.VMEM((rows, past), F32),
                        pltpu.VMEM((LANES, KV_DIM), F32),
                        pltpu.SemaphoreType.DMA((2, 2))],
    )
    return pl.pallas_call(
        functools.partial(_moba_sample_body, n_pages=n_pages, n_new=n_new),
        grid_spec=grid_spec,
        out_shape=jax.ShapeDtypeStruct((nseq, rows, HEAD_DIM), F32),
        compiler_params=_params(("arbitrary",)),
        name="moba_sample",
    )(page_table, q_rows, k_new, v_new, cache_k, cache_v)


def _rope_tables(pos):
    half = HEAD_DIM // 2
    inv = ROPE_THETA ** (-jnp.arange(half, dtype=F32) / half)
    ang = pos.astype(F32)[:, None] * inv[None, :]
    cos, sin = jnp.cos(ang), jnp.sin(ang)
    cos_h = jnp.concatenate([cos, cos], axis=1)
    sin_h = jnp.concatenate([-sin, sin], axis=1)
    reps = LANES // HEAD_DIM
    return jnp.tile(cos_h, (1, reps)), jnp.tile(sin_h, (1, reps))


def _prep_weights(norm_mix, norm_ffn, w_in_ssm, conv_w, conv_b, dt_bias, a_log, d_skip, norm_ssm,
                  w_out_ssm, norm_kv, w_kv, w_q, w_o, w_gu, w_down, norm_final):
    pad_h = LANES - N_SSM_HEADS
    w_in = w_in_ssm[0]
    return dict(
        g_mix0=norm_mix[0][None], g_mix1=norm_mix[1][None],
        g_ffn0=norm_ffn[0][None], g_ffn1=norm_ffn[1][None],
        wz=w_in[:, :D_INNER].astype(BF16),
        wx=w_in[:, D_INNER:D_INNER + CONV_DIM].astype(BF16),
        wdt=jnp.pad(w_in[:, D_INNER + CONV_DIM:], ((0, 0), (0, pad_h))).astype(BF16),
        cw=conv_w[0], cb=conv_b[0][None],
        dtb=jnp.pad(dt_bias[0], (0, pad_h))[None], alog=jnp.pad(a_log[0], (0, pad_h))[None],
        dsk=jnp.repeat(d_skip[0], SSM_HEAD_DIM)[None],
        gn=norm_ssm[0][None], wo_ssm=w_out_ssm[0].astype(BF16),
        g_kv=norm_kv[None], wkv=w_kv.astype(BF16), wq=w_q[0].astype(BF16),
        wo=w_o[0].astype(BF16),
        wg0=w_gu[0][:, :D_FF].astype(BF16), wu0=w_gu[0][:, D_FF:].astype(BF16),
        wd0=w_down[0].astype(BF16),
        wg1=w_gu[1][:, :D_FF].astype(BF16), wu1=w_gu[1][:, D_FF:].astype(BF16),
        wd1=w_down[1].astype(BF16),
        g_fin=norm_final[None],
    )


def _ssd_layer(x, conv_in, ssm0, w, tm, L, valid):
    b, t, _ = x.shape
    xf = x.reshape(b * t, D_MODEL)
    z, xbc, dtr = _inproj(xf, w["g_mix0"], w["wz"], w["wx"], w["wdt"], tm)
    tp = -(-t // L) * L
    xbc3 = xbc.reshape(b, t, CONV_DIM)
    dtr3 = dtr.reshape(b, t, LANES)
    if tp != t:
        xbc3 = jnp.pad(xbc3, ((0, 0), (0, tp - t), (0, 0)))
        dtr3 = jnp.pad(dtr3, ((0, 0), (0, tp - t), (0, 0)))
    conv0 = jnp.pad(conv_in, ((0, 0), (SUBLANES - (D_CONV - 1), 0), (0, 0)))
    y, ssm_new = _ssd(xbc3, dtr3, conv0, ssm0, w["cw"], w["cb"], w["dtb"], w["alog"], w["dsk"],
                      L, valid)
    y = y[:, :t].reshape(b * t, D_INNER)
    x2 = _ssm_out_ffn(y, z, xf, w["gn"], w["wo_ssm"], w["g_ffn0"], w["wg0"], w["wu0"], w["wd0"], tm)
    keep = D_CONV - 1
    conv_new = jnp.concatenate([conv_in[:, t:], xbc.reshape(b, t, CONV_DIM)[:, max(0, t - keep):]],
                               axis=1)
    return x2, conv_new, ssm_new


def kernel(x_prompt, x_sample, state_conv, state_ssm, cache_k, cache_v, page_table, norm_mix,
           norm_ffn, w_in_ssm, conv_w, conv_b, dt_bias, a_log, d_skip, norm_ssm, w_out_ssm,
           norm_kv, w_kv, w_q, w_o, w_gu, w_down, norm_final):
    w = _prep_weights(norm_mix, norm_ffn, w_in_ssm, conv_w, conv_b, dt_bias, a_log, d_skip,
                      norm_ssm, w_out_ssm, norm_kv, w_kv, w_q, w_o, w_gu, w_down, norm_final)
    bp, tp, _ = x_prompt.shape
    bs, ts, _ = x_sample.shape
    past_len = page_table.shape[1] * PAGE_SIZE

    tm_p = 256
    conv0_p = jnp.zeros((bp, D_CONV - 1, CONV_DIM), F32)
    ssm0_p = jnp.zeros((bp, N_SSM_HEADS, SSM_HEAD_DIM, D_STATE), F32)
    x2_p, conv_p, ssm_p = _ssd_layer(x_prompt, conv0_p, ssm0_p, w, tm_p, SSD_CHUNK, SSD_CHUNK)
    cos_p, sin_p = _rope_tables(jnp.arange(tp, dtype=jnp.int32))
    kt_p, vt32_p, qt_p, kh_p, vt_p, km_p = _kvq_prompt(x2_p, bp, tp, w["g_kv"], w["g_mix1"],
                                                       w["wkv"], w["wq"], cos_p, sin_p)
    k_p = kt_p.reshape(bp, N_KV_HEADS, HEAD_DIM, tp).transpose(0, 3, 1, 2)
    v_p = vt32_p.reshape(bp, N_KV_HEADS, HEAD_DIM, tp).transpose(0, 3, 1, 2)
    attn_p = _moba_prompt(qt_p, kh_p, vt_p, km_p.reshape(bp, tp // MOBA_BLOCK, KV_DIM))
    y_p = _attn_out_ffn(attn_p.reshape(bp * tp, D_MODEL), x2_p, w["wo"], w["g_ffn1"], w["wg1"],
                        w["wu1"], w["wd1"], w["g_fin"], tm_p)

    tm_s = bs * ts
    x2_s, conv_s, ssm_s = _ssd_layer(x_sample, state_conv[0], state_ssm[0], w, tm_s, SUBLANES, ts)
    pos_s = past_len + jnp.tile(jnp.arange(ts, dtype=jnp.int32), bs)
    cos_s, sin_s = _rope_tables(pos_s)
    k_s, v_s, q_s = _kvq(x2_s, w["g_kv"], w["g_mix1"], w["wkv"], w["wq"], cos_s, sin_s, tm_s)
    q_rows = q_s.reshape(bs, ts, N_HEADS, HEAD_DIM).transpose(0, 2, 1, 3).reshape(
        bs, N_HEADS * ts, HEAD_DIM)
    q_rows = jnp.pad(q_rows, ((0, 0), (0, LANES - N_HEADS * ts), (0, 0)))
    attn_rows = _moba_sample(page_table, q_rows, k_s.reshape(bs, ts, KV_DIM),
                             v_s.reshape(bs, ts, KV_DIM),
                             cache_k.transpose(0, 2, 3, 1).reshape(-1, KV_DIM, PAGE_SIZE),
                             cache_v.transpose(0, 2, 3, 1).reshape(-1, KV_DIM, PAGE_SIZE))
    attn_s = attn_rows[:, :N_HEADS * ts].reshape(bs, N_HEADS, ts, HEAD_DIM).transpose(
        0, 2, 1, 3).reshape(bs * ts, D_MODEL).astype(BF16)
    y_s = _attn_out_ffn(attn_s, x2_s, w["wo"], w["g_ffn1"], w["wg1"], w["wu1"], w["wd1"],
                        w["g_fin"], tm_s)

    return (y_p.reshape(bp, tp, D_MODEL), y_s.reshape(bs, ts, D_MODEL),
            conv_p[None], ssm_p[None],
            k_p, v_p,
            conv_s[None], ssm_s[None],
            k_s.reshape(bs, ts, N_KV_HEADS, HEAD_DIM), v_s.reshape(bs, ts, N_KV_HEADS, HEAD_DIM))
```

```python
import functools

import jax
import jax.numpy as jnp
from jax import lax
from jax.experimental import pallas as pl
from jax.experimental.pallas import tpu as pltpu

F32 = jnp.float32
BF16 = jnp.bfloat16

D_MODEL = 1024
D_INNER = 2048
SSM_HEAD_DIM = 64
N_SSM_HEADS = 32
N_SSM_GROUPS = 4
HEADS_PER_GROUP = 8
D_STATE = 128
D_CONV = 4
SSD_CHUNK = 128
GN = N_SSM_GROUPS * D_STATE
CONV_DIM = D_INNER + 2 * GN
HEAD_DIM = 64
N_HEADS = 16
N_KV_HEADS = 4
Q_PER_KV = 4
KV_DIM = N_KV_HEADS * HEAD_DIM
MOBA_BLOCK = 256
MOBA_TOP_K = 3
ROPE_THETA = 10000.0
D_FF = 2816
EPS = 1e-6
PAGE_SIZE = 128

LANES = 128
SUBLANES = 8
VMEM_LIMIT = 56 * 1024 * 1024

NT_DIMS = (((1,), (1,)), ((), ()))
TN_DIMS = (((0,), (0,)), ((), ()))
NEG_INF = float("-inf")
MASKED = -1e30
LOG2_E = 1.4426950408889634
V_AUG = HEAD_DIM + 16
ATTN_UNIT_HEADS = 4


def _params(sem):
    return pltpu.CompilerParams(dimension_semantics=sem, vmem_limit_bytes=VMEM_LIMIT)


def _const_spec(shape):
    nd = len(shape)
    return pl.BlockSpec(shape, lambda *_: (0,) * nd, pipeline_mode=pl.Buffered(1))


def _silu(x):
    return x * (1.0 / (1.0 + jnp.exp(-x)))


def _inv_rms(x):
    return lax.rsqrt(jnp.mean(x * x, axis=-1, keepdims=True) + EPS)


def _split3(x):
    hi = x.astype(BF16)
    r = x - hi.astype(F32)
    mid = r.astype(BF16)
    lo = (r - mid.astype(F32)).astype(BF16)
    return hi, mid, lo


def _dot01(a01, x, dims, a_is_lhs=True):
    out = None
    for p in _split3(x):
        t = (lax.dot_general(a01, p, dims, preferred_element_type=F32) if a_is_lhs
             else lax.dot_general(p, a01, dims, preferred_element_type=F32))
        out = t if out is None else out + t
    return out


def _top3_mask(gate, valid, axis=1):
    nb = gate.shape[axis]
    col = lax.broadcasted_iota(jnp.int32, gate.shape, axis).astype(F32)
    g = jnp.where(valid, gate, NEG_INF)
    sel = jnp.zeros(gate.shape, F32)
    for _ in range(MOBA_TOP_K):
        m = jnp.max(g, axis=axis, keepdims=True)
        idx = jnp.min(jnp.where(g == m, col, float(nb)), axis=axis, keepdims=True)
        pick = col == idx
        sel = jnp.where(pick, 1.0, sel)
        g = jnp.where(pick, NEG_INF, g)
    return jnp.logical_and(sel > 0.5, valid)


def _inproj_body(x_ref, g_ref, wz_ref, wx_ref, wdt_ref, z_ref, xbc_ref, dt_ref):
    x = x_ref[...]
    h = (x * _inv_rms(x) * g_ref[...]).astype(BF16)
    z_ref[...] = jnp.dot(h, wz_ref[...], preferred_element_type=F32)
    xbc_ref[...] = jnp.dot(h, wx_ref[...], preferred_element_type=F32)
    dt_ref[...] = jnp.dot(h, wdt_ref[...], preferred_element_type=F32)


def _inproj(x, g, wz, wx, wdt, tm):
    m = x.shape[0]
    row = lambda w: pl.BlockSpec((tm, w), lambda i: (i, 0))
    return pl.pallas_call(
        _inproj_body,
        grid=(m // tm,),
        in_specs=[row(D_MODEL), _const_spec((1, D_MODEL)), _const_spec(wz.shape),
                  _const_spec(wx.shape), _const_spec(wdt.shape)],
        out_specs=[row(D_INNER), row(CONV_DIM), row(LANES)],
        out_shape=[jax.ShapeDtypeStruct((m, D_INNER), F32),
                   jax.ShapeDtypeStruct((m, CONV_DIM), F32),
                   jax.ShapeDtypeStruct((m, LANES), F32)],
        compiler_params=_params(("parallel",)),
        name="ssd_inproj",
    )(x, g, wz, wx, wdt)


def _causal_conv(pad_ref, xbc, cw_ref, cb_ref, rows, cols=slice(None)):
    pad_ref[SUBLANES:SUBLANES + rows, cols] = xbc
    conv = cb_ref[:, cols]
    for i in range(D_CONV):
        off = SUBLANES - (D_CONV - 1) + i
        conv = conv + pad_ref[off:off + rows, cols] * cw_ref[i:i + 1, cols]
    pad_ref[0:SUBLANES, cols] = pad_ref[rows:rows + SUBLANES, cols]
    return conv


def _ssd_body(xbc_ref, dtr_ref, conv0_ref, ssm0_ref, cw_ref, cb_ref, dtb_ref, alog_ref, dsk_ref,
              y_ref, st_ref, pad_ref, *, L, valid):
    c = pl.program_id(1)

    @pl.when(c == 0)
    def _():
        pad_ref[0:SUBLANES, :] = conv0_ref[0]
        st_ref[0] = ssm0_ref[0]

    act = _silu(_causal_conv(pad_ref, xbc_ref[0], cw_ref, cb_ref, L))
    xs = act[:, :D_INNER]
    bm = act[:, D_INNER:D_INNER + GN].astype(BF16)
    cm = act[:, D_INNER + GN:].astype(BF16)

    t = dtr_ref[0] + dtb_ref[...]
    dt = jnp.maximum(t, 0.0) + jnp.log1p(jnp.exp(-jnp.abs(t)))
    if valid < L:
        rows = lax.broadcasted_iota(jnp.int32, dt.shape, 0)
        dt = jnp.where(rows < valid, dt, 0.0)
    a = -jnp.exp(alog_ref[...])
    dta = dt * a

    r_i = lax.broadcasted_iota(jnp.int32, (L, L), 0)
    c_i = lax.broadcasted_iota(jnp.int32, (L, L), 1)
    causal = r_i >= c_i
    tril = jnp.where(causal, 1.0, 0.0).astype(BF16)
    e_r = lax.broadcasted_iota(jnp.int32, (LANES, LANES), 0)
    e_c = lax.broadcasted_iota(jnp.int32, (LANES, LANES), 1)
    eye = jnp.where(e_r == e_c, 1.0, 0.0).astype(BF16)

    acs = _dot01(tril, dta, (((1,), (0,)), ((), ())))
    acs_t = _dot01(eye, acs, NT_DIMS)
    last = acs[L - 1:L, :]
    dec_end = jnp.exp(last - acs)
    e_acs = jnp.exp(acs)
    dec_chunk = jnp.exp(last)

    for g in range(N_SSM_GROUPS):
        bg = bm[:, g * D_STATE:(g + 1) * D_STATE]
        cg = cm[:, g * D_STATE:(g + 1) * D_STATE]
        cb = lax.dot_general(cg, bg, NT_DIMS, preferred_element_type=F32)
        for jp in range(HEADS_PER_GROUP // 2):
            pair = []
            for h in (g * HEADS_PER_GROUP + 2 * jp, g * HEADS_PER_GROUP + 2 * jp + 1):
                diff = acs[:, h:h + 1] - acs_t[h:h + 1, :]
                seg = jnp.exp(jnp.where(causal, diff, NEG_INF))
                w = (cb * seg).astype(BF16)
                xs_h = xs[:, h * SSM_HEAD_DIM:(h + 1) * SSM_HEAD_DIM]
                xd_h = xs_h * dt[:, h:h + 1]
                y_diag = jnp.dot(w, xd_h.astype(BF16), preferred_element_type=F32)
                st_h = st_ref[0, h]
                y_off = lax.dot_general(cg, st_h.astype(BF16), NT_DIMS,
                                        preferred_element_type=F32) * e_acs[:, h:h + 1]
                xdd = (xd_h * dec_end[:, h:h + 1]).astype(BF16)
                cs = lax.dot_general(xdd, bg, TN_DIMS, preferred_element_type=F32)
                st_ref[0, h] = st_h * dec_chunk[:, h:h + 1] + cs
                pair.append(y_diag + y_off
                            + dsk_ref[:, h * SSM_HEAD_DIM:(h + 1) * SSM_HEAD_DIM] * xs_h)
            lo = (g * HEADS_PER_GROUP + 2 * jp) * SSM_HEAD_DIM
            y_ref[0, :, lo:lo + 2 * SSM_HEAD_DIM] = jnp.concatenate(pair, axis=1)


def _ssd_t_body(xbc_ref, dtr_ref, conv0_ref, ssm0_ref, cw_ref, cb_ref, dtb_ref, alog_ref, dsk_ref,
                y_ref, st_ref, pad_ref, yt_ref, *, L):
    hd = SSM_HEAD_DIM
    gw = HEADS_PER_GROUP * hd

    @pl.when(pl.program_id(1) == 0)
    def _():
        pad_ref[0:SUBLANES, :] = conv0_ref[0]
        st_ref[0] = ssm0_ref[0]

    act = _silu(_causal_conv(pad_ref, xbc_ref[0], cw_ref, cb_ref, L))
    xs_t = act[:, :D_INNER].T
    bm = act[:, D_INNER:D_INNER + GN].astype(BF16)
    cm = act[:, D_INNER + GN:].astype(BF16)

    t = dtr_ref[0] + dtb_ref[...]
    dt = jnp.maximum(t, 0.0) + jnp.log1p(jnp.exp(-jnp.abs(t)))
    dta = dt * (-jnp.exp(alog_ref[...]))

    r_i = lax.broadcasted_iota(jnp.int32, (L, L), 0)
    c_i = lax.broadcasted_iota(jnp.int32, (L, L), 1)
    tril = jnp.where(r_i >= c_i, 1.0, 0.0).astype(BF16)
    causal_t = c_i >= r_i

    acs = _dot01(tril, dta, (((1,), (0,)), ((), ())))
    acs_t = acs.T
    dt_t = dt.T
    dec_end_t = jnp.exp(acs_t[:, L - 1:L] - acs_t)
    e_acs_t = jnp.exp(acs_t)
    dec_chunk = jnp.exp(acs[L - 1:L, :])
    scale_in = dt_t
    scale_st = dt_t * dec_end_t

    for g in range(N_SSM_GROUPS):
        bg = bm[:, g * D_STATE:(g + 1) * D_STATE]
        cg = cm[:, g * D_STATE:(g + 1) * D_STATE]
        cb_t = lax.dot_general(bg, cg, NT_DIMS, preferred_element_type=F32)
        h0 = g * HEADS_PER_GROUP
        xs_g = xs_t[g * gw:(g + 1) * gw, :]
        xd_parts, xdd_parts = [], []
        for j in range(HEADS_PER_GROUP):
            h = h0 + j
            xd = xs_g[j * hd:(j + 1) * hd, :] * scale_in[h:h + 1, :]
            xd_parts.append(xd.astype(BF16))
            xdd_parts.append((xs_g[j * hd:(j + 1) * hd, :] * scale_st[h:h + 1, :]).astype(BF16))
        st_g = st_ref[0, h0:h0 + HEADS_PER_GROUP].reshape(gw, D_STATE)
        y_off = lax.dot_general(st_g.astype(BF16), cg, NT_DIMS,
                                preferred_element_type=F32)
        cs = jnp.dot(jnp.concatenate(xdd_parts, axis=0), bg,
                     preferred_element_type=F32)
        for j in range(HEADS_PER_GROUP):
            h = h0 + j
            diff = acs_t[h:h + 1, :] - acs[:, h:h + 1]
            seg_t = jnp.exp(jnp.where(causal_t, diff, NEG_INF))
            w_t = (cb_t * seg_t).astype(BF16)
            y_diag = jnp.dot(xd_parts[j], w_t, preferred_element_type=F32)
            rows = slice(j * hd, (j + 1) * hd)
            yt_ref[h * hd:(h + 1) * hd, :] = (
                y_diag + y_off[rows, :] * e_acs_t[h:h + 1, :]
                + dsk_ref[h * hd:(h + 1) * hd, :] * xs_g[rows, :])
            st_ref[0, h] = st_g[rows, :] * dec_chunk[:, h:h + 1] + cs[rows, :]
    y_ref[0] = yt_ref[...].T


def _ssd(xbc, dtr, conv0, ssm0, cw, cb, dtb, alog, dsk, L, valid):
    nb, t = xbc.shape[0], xbc.shape[1]
    nc = t // L
    st_spec = pl.BlockSpec((1, N_SSM_HEADS, SSM_HEAD_DIM, D_STATE), lambda b, c: (b, 0, 0, 0))
    return pl.pallas_call(
        functools.partial(_ssd_body, L=L, valid=valid),
        grid=(nb, nc),
        in_specs=[pl.BlockSpec((1, L, CONV_DIM), lambda b, c: (b, c, 0)),
                  pl.BlockSpec((1, L, LANES), lambda b, c: (b, c, 0)),
                  pl.BlockSpec((1, SUBLANES, CONV_DIM), lambda b, c: (b, 0, 0)),
                  st_spec,
                  _const_spec(cw.shape), _const_spec(cb.shape), _const_spec(dtb.shape),
                  _const_spec(alog.shape), _const_spec(dsk.shape)],
        out_specs=[pl.BlockSpec((1, L, D_INNER), lambda b, c: (b, c, 0)), st_spec],
        out_shape=[jax.ShapeDtypeStruct((nb, t, D_INNER), F32),
                   jax.ShapeDtypeStruct(ssm0.shape, F32)],
        scratch_shapes=[pltpu.VMEM((L + SUBLANES, CONV_DIM), F32)],
        compiler_params=_params(("parallel", "arbitrary")),
        name="ssd_scan",
    )(xbc, dtr, conv0, ssm0, cw, cb, dtb, alog, dsk)


def _ssd_t(xbc, dtr, conv0, ssm0, cw, cb, dtb, alog, dsk):
    nb, t = xbc.shape[0], xbc.shape[1]
    L = SSD_CHUNK
    assert L == LANES and t % L == 0
    dsk_t = jnp.broadcast_to(dsk.reshape(D_INNER, 1), (D_INNER, LANES))
    st_spec = pl.BlockSpec((1, N_SSM_HEADS, SSM_HEAD_DIM, D_STATE), lambda b, c: (b, 0, 0, 0))
    return pl.pallas_call(
        functools.partial(_ssd_t_body, L=L),
        grid=(nb, t // L),
        in_specs=[pl.BlockSpec((1, L, CONV_DIM), lambda b, c: (b, c, 0)),
                  pl.BlockSpec((1, L, LANES), lambda b, c: (b, c, 0)),
                  pl.BlockSpec((1, SUBLANES, CONV_DIM), lambda b, c: (b, 0, 0)),
                  st_spec, _const_spec(cw.shape), _const_spec(cb.shape),
                  _const_spec(dtb.shape), _const_spec(alog.shape), _const_spec(dsk_t.shape)],
        out_specs=[pl.BlockSpec((1, L, D_INNER), lambda b, c: (b, c, 0)), st_spec],
        out_shape=[jax.ShapeDtypeStruct((nb, t, D_INNER), F32),
                   jax.ShapeDtypeStruct(ssm0.shape, F32)],
        scratch_shapes=[pltpu.VMEM((L + SUBLANES, CONV_DIM), F32), pltpu.VMEM((D_INNER, L), F32)],
        compiler_params=_params(("parallel", "arbitrary")),
        name="ssd_scan_t",
    )(xbc, dtr, conv0, ssm0, cw, cb, dtb, alog, dsk_t)


def _ffn(x1, nf_ref, wg_ref, wu_ref, wd_ref):
    h = (x1 * _inv_rms(x1) * nf_ref[...]).astype(BF16)
    gate = jnp.dot(h, wg_ref[...], preferred_element_type=F32)
    up = jnp.dot(h, wu_ref[...], preferred_element_type=F32)
    act = (_silu(gate) * up).astype(BF16)
    return x1 + jnp.dot(act, wd_ref[...], preferred_element_type=F32)


def _ssm_out_ffn_body(y_ref, z_ref, x_ref, gn_ref, wo_ref, nf_ref, wg_ref, wu_ref, wd_ref, o_ref):
    y = y_ref[...] * _silu(z_ref[...])
    gw = D_INNER // N_SSM_GROUPS
    parts = []
    for g in range(N_SSM_GROUPS):
        yg = y[:, g * gw:(g + 1) * gw]
        parts.append(yg * _inv_rms(yg))
    y = (jnp.concatenate(parts, axis=1) * gn_ref[...]).astype(BF16)
    x1 = x_ref[...] + jnp.dot(y, wo_ref[...], preferred_element_type=F32)
    o_ref[...] = _ffn(x1, nf_ref, wg_ref, wu_ref, wd_ref)


def _attn_out_ffn_body(a_ref, x_ref, wo_ref, nf_ref, wg_ref, wu_ref, wd_ref, nfin_ref, o_ref):
    x1 = x_ref[...] + jnp.dot(a_ref[...], wo_ref[...], preferred_element_type=F32)
    x2 = _ffn(x1, nf_ref, wg_ref, wu_ref, wd_ref)
    o_ref[...] = x2 * _inv_rms(x2) * nfin_ref[...]


def _ssm_out_ffn(y, z, x, gn, wo, nf, wg, wu, wd, tm):
    m = x.shape[0]
    row = lambda w: pl.BlockSpec((tm, w), lambda i: (i, 0))
    return pl.pallas_call(
        _ssm_out_ffn_body,
        grid=(m // tm,),
        in_specs=[row(D_INNER), row(D_INNER), row(D_MODEL), _const_spec(gn.shape),
                  _const_spec(wo.shape), _const_spec(nf.shape), _const_spec(wg.shape),
                  _const_spec(wu.shape), _const_spec(wd.shape)],
        out_specs=row(D_MODEL),
        out_shape=jax.ShapeDtypeStruct((m, D_MODEL), F32),
        compiler_params=_params(("parallel",)),
        name="ssm_out_ffn",
    )(y, z, x, gn, wo, nf, wg, wu, wd)


def _attn_out_ffn(a, x, wo, nf, wg, wu, wd, nfin, tm):
    m = x.shape[0]
    row = lambda w: pl.BlockSpec((tm, w), lambda i: (i, 0))
    return pl.pallas_call(
        _attn_out_ffn_body,
        grid=(m // tm,),
        in_specs=[row(D_MODEL), row(D_MODEL), _const_spec(wo.shape), _const_spec(nf.shape),
                  _const_spec(wg.shape), _const_spec(wu.shape), _const_spec(wd.shape),
                  _const_spec(nfin.shape)],
        out_specs=row(D_MODEL),
        out_shape=jax.ShapeDtypeStruct((m, D_MODEL), F32),
        compiler_params=_params(("parallel",)),
        name="attn_out_ffn",
    )(a, x, wo, nf, wg, wu, wd, nfin)


def _rope(x, cos, sin_signed, first_half):
    outs = []
    for c in range(x.shape[1] // LANES):
        xc = x[:, c * LANES:(c + 1) * LANES]
        partner = jnp.where(first_half, pltpu.roll(xc, LANES - HEAD_DIM // 2, 1),
                            pltpu.roll(xc, HEAD_DIM // 2, 1))
        outs.append(xc * cos + partner * sin_signed)
    return jnp.concatenate(outs, axis=1)


def _kvq_compute(x_ref, gkv_ref, gq_ref, wkv_ref, wq_ref, cos_ref, sin_ref):
    x = x_ref[...]
    xn = x * _inv_rms(x)
    hk = (xn * gkv_ref[...]).astype(BF16)
    hq = (xn * gq_ref[...]).astype(BF16)
    cos = cos_ref[...]
    sin = sin_ref[...]
    lane = lax.broadcasted_iota(jnp.int32, cos.shape, 1)
    first_half = (lane % HEAD_DIM) < (HEAD_DIM // 2)
    kv = jnp.dot(hk, wkv_ref[...], preferred_element_type=F32)
    k = _rope(kv[:, :KV_DIM], cos, sin, first_half)
    q = _rope(jnp.dot(hq, wq_ref[...], preferred_element_type=F32), cos, sin, first_half)
    return k, kv[:, KV_DIM:], q


def _kvq_body(x_ref, gkv_ref, gq_ref, wkv_ref, wq_ref, cos_ref, sin_ref, k_ref, v_ref, q_ref):
    k, v, q = _kvq_compute(x_ref, gkv_ref, gq_ref, wkv_ref, wq_ref, cos_ref, sin_ref)
    k_ref[...] = k
    v_ref[...] = v
    q_ref[...] = q


def _kvq_prompt_body(x_ref, gkv_ref, gq_ref, wkv_ref, wq_ref, cos_ref, sin_ref,
                     kt_ref, vt32_ref, qt_ref, kh_ref, vt_ref, km_ref):
    k, v, q = _kvq_compute(x_ref, gkv_ref, gq_ref, wkv_ref, wq_ref, cos_ref, sin_ref)
    kt_ref[0] = k.T
    v_t = v.T
    vt32_ref[0] = v_t
    qt_ref[0] = q.T
    vt = v_t.astype(BF16)
    blk = pl.program_id(1)
    col = lax.broadcasted_iota(jnp.int32, (MOBA_BLOCK, LANES - HEAD_DIM), 1)
    onehot = jnp.where(col == blk, 1.0, 0.0).astype(BF16)
    row = lax.broadcasted_iota(jnp.int32, (V_AUG - HEAD_DIM, MOBA_BLOCK), 0)
    ones_row = jnp.where(row == 0, 1.0, 0.0).astype(BF16)
    for kvh in range(N_KV_HEADS):
        kh_ref[0, 0, kvh] = jnp.concatenate(
            [k[:, kvh * HEAD_DIM:(kvh + 1) * HEAD_DIM].astype(BF16), onehot], axis=1)
        vt_ref[0, 0, kvh] = jnp.concatenate(
            [vt[kvh * HEAD_DIM:(kvh + 1) * HEAD_DIM, :], ones_row], axis=0)
    km_ref[0, 0] = jnp.sum(k, axis=0, keepdims=True) * (1.0 / MOBA_BLOCK)


def _kvq(x, gkv, gq, wkv, wq, cos, sin, tm):
    m = x.shape[0]
    row = lambda w: pl.BlockSpec((tm, w), lambda i: (i, 0))
    tab = pl.BlockSpec((tm, LANES), lambda i: (0, 0))
    return pl.pallas_call(
        _kvq_body,
        grid=(m // tm,),
        in_specs=[row(D_MODEL), _const_spec(gkv.shape), _const_spec(gq.shape),
                  _const_spec(wkv.shape), _const_spec(wq.shape), tab, tab],
        out_specs=[row(KV_DIM), row(KV_DIM), row(D_MODEL)],
        out_shape=[jax.ShapeDtypeStruct((m, KV_DIM), F32), jax.ShapeDtypeStruct((m, KV_DIM), F32),
                   jax.ShapeDtypeStruct((m, D_MODEL), F32)],
        compiler_params=_params(("parallel",)),
        name="kvq_proj",
    )(x, gkv, gq, wkv, wq, cos, sin)


def _kvq_prompt(x, b, t, gkv, gq, wkv, wq, cos, sin):
    nb = t // MOBA_BLOCK
    tm = MOBA_BLOCK
    row = lambda w: pl.BlockSpec((tm, w), lambda bi, i: (bi * nb + i, 0))
    tab = pl.BlockSpec((tm, LANES), lambda bi, i: (i, 0))
    return pl.pallas_call(
        _kvq_prompt_body,
        grid=(b, nb),
        in_specs=[row(D_MODEL), _const_spec(gkv.shape), _const_spec(gq.shape),
                  _const_spec(wkv.shape), _const_spec(wq.shape), tab, tab],
        out_specs=[pl.BlockSpec((1, KV_DIM, tm), lambda bi, i: (bi, 0, i)),
                   pl.BlockSpec((1, KV_DIM, tm), lambda bi, i: (bi, 0, i)),
                   pl.BlockSpec((1, D_MODEL, tm), lambda bi, i: (bi, 0, i)),
                   pl.BlockSpec((1, 1, N_KV_HEADS, tm, LANES), lambda bi, i: (bi, i, 0, 0, 0)),
                   pl.BlockSpec((1, 1, N_KV_HEADS, V_AUG, tm), lambda bi, i: (bi, i, 0, 0, 0)),
                   pl.BlockSpec((1, 1, 1, KV_DIM), lambda bi, i: (bi, i, 0, 0))],
        out_shape=[jax.ShapeDtypeStruct((b, KV_DIM, t), F32),
                   jax.ShapeDtypeStruct((b, KV_DIM, t), F32),
                   jax.ShapeDtypeStruct((b, D_MODEL, t), F32),
                   jax.ShapeDtypeStruct((b, nb, N_KV_HEADS, tm, LANES), BF16),
                   jax.ShapeDtypeStruct((b, nb, N_KV_HEADS, V_AUG, tm), BF16),
                   jax.ShapeDtypeStruct((b, nb, 1, KV_DIM), F32)],
        compiler_params=_params(("parallel", "parallel")),
        name="kvq_proj_prompt",
    )(x, gkv, gq, wkv, wq, cos, sin)


def _moba_prompt_body(qt_ref, kh_ref, vt_ref, km_ref, o_ref, s_a, s_b, acc_a, acc_b, *, nb):
    i = pl.program_id(1)
    g_per_unit = ATTN_UNIT_HEADS
    rows = g_per_unit * MOBA_BLOCK
    n_units = N_HEADS // g_per_unit
    scale = (HEAD_DIM ** -0.5) * LOG2_E
    n_pairs = (i + 1) // 2
    blk_row = lax.broadcasted_iota(jnp.int32, (nb, rows), 0)
    key_i = lax.broadcasted_iota(jnp.int32, (MOBA_BLOCK, rows), 0)
    tok_i = lax.broadcasted_iota(jnp.int32, (MOBA_BLOCK, rows), 1) % MOBA_BLOCK
    own_ok = key_i <= tok_i
    fold = lambda s: jnp.max(s.reshape(MOBA_BLOCK // SUBLANES, SUBLANES, rows), axis=0)

    def setup(u):
        kvh = (u * g_per_unit) // Q_PER_KV
        lo = kvh * HEAD_DIM
        base = u * g_per_unit * HEAD_DIM
        qt2 = qt_ref[0, base:base + g_per_unit * HEAD_DIM, :]
        qt = jnp.concatenate([qt2[g * HEAD_DIM:(g + 1) * HEAD_DIM, :] for g in range(g_per_unit)],
                             axis=1)
        km = km_ref[0, :, lo:lo + HEAD_DIM]
        gate = jnp.dot(km, qt, preferred_element_type=F32,
                       precision=lax.Precision.HIGHEST)
        sel = _top3_mask(gate, blk_row < i, axis=0)
        pen = jnp.where(sel, 0.0, MASKED).astype(BF16)
        qb = (qt * scale).astype(BF16)
        q_own = jnp.concatenate([qb, jnp.zeros((LANES - HEAD_DIM, rows), BF16)], axis=0)
        q_aug = jnp.concatenate([qb, pen, jnp.zeros((LANES - HEAD_DIM - nb, rows), BF16)], axis=0)
        s_own = jnp.dot(kh_ref[0, i, kvh], q_own, preferred_element_type=F32)
        s_own = jnp.where(own_ok, s_own, MASKED)
        s_bufs[u % 2][nb] = s_own
        return kvh, q_aug, fold(s_own)

    def scores_pair(u, kvh, q_aug, jj, m8):
        for t in range(2):
            j = 2 * jj + t
            s = jnp.dot(kh_ref[0, j, kvh], q_aug, preferred_element_type=F32)
            s_bufs[u % 2][j] = s
            m8 = jnp.maximum(m8, fold(s))
        return m8

    def weighted_pair(u, kvh, m, jj):
        j = 2 * jj
        s_buf = s_bufs[u % 2]
        p = jnp.concatenate([jnp.exp2(s_buf[j] - m).astype(BF16),
                             jnp.exp2(s_buf[j + 1] - m).astype(BF16)], axis=0)
        v2 = jnp.concatenate([vt_ref[0, j, kvh], vt_ref[0, j + 1, kvh]], axis=1)
        acc_bufs[u % 2][...] += jnp.dot(v2, p, preferred_element_type=F32)

    def over_pairs(step, carry):
        n_double = n_pairs // 2

        def double(q, c):
            return step(2 * q + 1, step(2 * q, c))

        carry = lax.fori_loop(0, n_double, double, carry)
        return lax.fori_loop(2 * n_double, n_pairs, step, carry)

    s_bufs = (s_a, s_b)
    acc_bufs = (acc_a, acc_b)
    kvh, q_aug, m8 = setup(0)
    m8 = over_pairs(functools.partial(scores_pair, 0, kvh, q_aug), m8)
    for u in range(n_units):
        m = jnp.max(m8, axis=0, keepdims=True)
        p_own = jnp.exp2(s_bufs[u % 2][nb] - m).astype(BF16)
        acc_bufs[u % 2][...] = jnp.dot(vt_ref[0, i, kvh], p_own, preferred_element_type=F32)
        if u + 1 < n_units:
            kvh_n, q_aug_n, m8_n = setup(u + 1)

            def both(jj, m8c, u=u, kvh=kvh, m=m, kvh_n=kvh_n, q_aug_n=q_aug_n):
                m8c = scores_pair(u + 1, kvh_n, q_aug_n, jj, m8c)
                weighted_pair(u, kvh, m, jj)
                return m8c

            m8_n = over_pairs(both, m8_n)
        else:
            def last(jj, carry, u=u, kvh=kvh, m=m):
                weighted_pair(u, kvh, m, jj)
                return carry

            over_pairs(last, 0)
        acc = acc_bufs[u % 2][...]
        out_t = acc[:HEAD_DIM, :] / acc[HEAD_DIM:HEAD_DIM + 1, :]
        out2 = jnp.concatenate([out_t[:, g * MOBA_BLOCK:(g + 1) * MOBA_BLOCK]
                                for g in range(g_per_unit)], axis=0)
        base = u * g_per_unit * HEAD_DIM
        o_ref[0, :, base:base + g_per_unit * HEAD_DIM] = out2.T.astype(o_ref.dtype)
        if u + 1 < n_units:
            kvh, m8 = kvh_n, m8_n


def _moba_prompt(qt, kh, vt, km):
    b, t = qt.shape[0], qt.shape[2]
    nb = t // MOBA_BLOCK
    rows = ATTN_UNIT_HEADS * MOBA_BLOCK
    return pl.pallas_call(
        functools.partial(_moba_prompt_body, nb=nb),
        grid=(b, nb),
        in_specs=[pl.BlockSpec((1, D_MODEL, MOBA_BLOCK), lambda bi, i: (bi, 0, i)),
                  pl.BlockSpec((1, nb, N_KV_HEADS, MOBA_BLOCK, LANES),
                               lambda bi, i: (bi, 0, 0, 0, 0), pipeline_mode=pl.Buffered(1)),
                  pl.BlockSpec((1, nb, N_KV_HEADS, V_AUG, MOBA_BLOCK),
                               lambda bi, i: (bi, 0, 0, 0, 0), pipeline_mode=pl.Buffered(1)),
                  pl.BlockSpec((1, nb, KV_DIM), lambda bi, i: (bi, 0, 0))],
        out_specs=pl.BlockSpec((1, MOBA_BLOCK, D_MODEL), lambda bi, i: (bi, i, 0)),
        out_shape=jax.ShapeDtypeStruct((b, t, D_MODEL), BF16),
        scratch_shapes=[pltpu.VMEM((nb + 1, MOBA_BLOCK, rows), F32),
                        pltpu.VMEM((nb + 1, MOBA_BLOCK, rows), F32),
                        pltpu.VMEM((V_AUG, rows), F32), pltpu.VMEM((V_AUG, rows), F32)],
        compiler_params=_params(("parallel", "arbitrary")),
        name="moba_prompt",
    )(qt, kh, vt, km)


def _page_copy(cache_hbm, buf, sem, page, slot, p):
    return pltpu.make_async_copy(cache_hbm.at[page], buf.at[slot, p], sem)


def _moba_sample_body(pt_ref, q_ref, kn_ref, vn_ref, ck_hbm, cv_hbm, o_ref,
                      kbuf, vbuf, s_scr, new_scr, sems, *, n_pages, n_new):
    b = pl.program_id(0)
    slot = b % 2
    rows = q_ref.shape[1]
    ppb = MOBA_BLOCK // PAGE_SIZE
    nb = n_pages // ppb
    scale = HEAD_DIM ** -0.5

    def fetch(seq, to_slot):
        def start_page(p, carry):
            page = pt_ref[seq, p]
            _page_copy(ck_hbm, kbuf, sems.at[0, to_slot], page, to_slot, p).start()
            _page_copy(cv_hbm, vbuf, sems.at[1, to_slot], page, to_slot, p).start()
            return carry
        lax.fori_loop(0, n_pages, start_page, 0)

    @pl.when(b == 0)
    def _():
        fetch(0, 0)

    @pl.when(b + 1 < pl.num_programs(0))
    def _():
        fetch(b + 1, 1 - slot)

    q64 = q_ref[0]
    qt = jnp.concatenate([q64] * N_KV_HEADS, axis=1)
    r_i = lax.broadcasted_iota(jnp.int32, qt.shape, 0)
    c_i = lax.broadcasted_iota(jnp.int32, qt.shape, 1)
    qpad = jnp.where(c_i // HEAD_DIM == r_i // (Q_PER_KV * n_new), qt, 0.0)
    qb = (qpad * scale).astype(BF16)

    def wait_k(p, carry):
        _page_copy(ck_hbm, kbuf, sems.at[0, slot], 0, slot, p).wait()
        return carry

    lax.fori_loop(0, n_pages, wait_k, 0)

    blk_lane = lax.broadcasted_iota(jnp.int32, (KV_DIM, LANES), 1)
    km_t = jnp.zeros((KV_DIM, LANES), F32)
    for j in range(nb):
        kt = jnp.concatenate([kbuf[slot, ppb * j + u] for u in range(ppb)], axis=1)
        km_col = jnp.sum(kt, axis=1, keepdims=True) * (1.0 / MOBA_BLOCK)
        km_t = jnp.where(blk_lane == j, km_col, km_t)
        s_scr[:, j * MOBA_BLOCK:(j + 1) * MOBA_BLOCK] = jnp.dot(
            qb, kt.astype(BF16), preferred_element_type=F32)

    gate = jnp.dot(qpad, km_t, preferred_element_type=F32,
                   precision=lax.Precision.HIGHEST)
    sel = _top3_mask(gate, lax.broadcasted_iota(jnp.int32, gate.shape, 1) < nb)
    pen = jnp.where(sel, 0.0, NEG_INF)

    new_scr[...] = jnp.zeros(new_scr.shape, F32)
    new_scr[0:n_new, :] = kn_ref[0]
    s_new = lax.dot_general(qb, new_scr[...].astype(BF16), NT_DIMS, preferred_element_type=F32)
    nr = lax.broadcasted_iota(jnp.int32, s_new.shape, 0)
    ncol = lax.broadcasted_iota(jnp.int32, s_new.shape, 1)
    s_new = jnp.where(ncol <= nr % n_new, s_new, NEG_INF)

    def lane_fold(x, op):
        out = x[:, :LANES]
        for c in range(1, x.shape[1] // LANES):
            out = op(out, x[:, c * LANES:(c + 1) * LANES])
        return out

    m_part = s_new
    for j in range(nb):
        sj = s_scr[:, j * MOBA_BLOCK:(j + 1) * MOBA_BLOCK] + pen[:, j:j + 1]
        m_part = jnp.maximum(m_part, lane_fold(sj, jnp.maximum))
    m = jnp.max(m_part, axis=1, keepdims=True)

    def wait_v(p, carry):
        _page_copy(cv_hbm, vbuf, sems.at[1, slot], 0, slot, p).wait()
        return carry

    lax.fori_loop(0, n_pages, wait_v, 0)

    p_new = jnp.exp(s_new - m)
    l_part = p_new
    new_scr[0:n_new, :] = vn_ref[0]
    acc = jnp.dot(p_new.astype(BF16), new_scr[...].astype(BF16), preferred_element_type=F32)
    for j in range(nb):
        pj = jnp.exp(s_scr[:, j * MOBA_BLOCK:(j + 1) * MOBA_BLOCK] + pen[:, j:j + 1] - m)
        l_part = l_part + lane_fold(pj, jnp.add)
        vt = jnp.concatenate([vbuf[slot, ppb * j + u] for u in range(ppb)], axis=1).astype(BF16)
        acc = acc + lax.dot_general(pj.astype(BF16), vt, NT_DIMS, preferred_element_type=F32)
    acc = acc / jnp.sum(l_part, axis=1, keepdims=True)
    ro = lax.broadcasted_iota(jnp.int32, (rows, HEAD_DIM), 0) // (Q_PER_KV * n_new)
    out = jnp.zeros((rows, HEAD_DIM), F32)
    for kvh in range(N_KV_HEADS):
        out = jnp.where(ro == kvh, acc[:, kvh * HEAD_DIM:(kvh + 1) * HEAD_DIM], out)
    o_ref[0] = out


def _moba_sample(page_table, q_rows, k_new, v_new, cache_k, cache_v):
    nseq, n_pages = page_table.shape
    rows = q_rows.shape[1]
    n_new = k_new.shape[1]
    past = n_pages * PAGE_SIZE
    grid_spec = pltpu.PrefetchScalarGridSpec(
        num_scalar_prefetch=1,
        grid=(nseq,),
        in_specs=[pl.BlockSpec((1, rows, HEAD_DIM), lambda b, pt: (b, 0, 0)),
                  pl.BlockSpec((1, n_new, KV_DIM), lambda b, pt: (b, 0, 0)),
                  pl.BlockSpec((1, n_new, KV_DIM), lambda b, pt: (b, 0, 0)),
                  pl.BlockSpec(memory_space=pl.ANY),
                  pl.BlockSpec(memory_space=pl.ANY)],
        out_specs=pl.BlockSpec((1, rows, HEAD_DIM), lambda b, pt: (b, 0, 0)),
        scratch_shapes=[pltpu.VMEM((2, n_pages, KV_DIM, PAGE_SIZE), F32),
                        pltpu.VMEM((2, n_pages, KV_DIM, PAGE_SIZE), F32),
                        pltpu.VMEM((rows, past), F32),
                        pltpu.VMEM((LANES, KV_DIM), F32),
                        pltpu.SemaphoreType.DMA((2, 2))],
    )
    return pl.pallas_call(
        functools.partial(_moba_sample_body, n_pages=n_pages, n_new=n_new),
        grid_spec=grid_spec,
        out_shape=jax.ShapeDtypeStruct((nseq, rows, HEAD_DIM), F32),
        compiler_params=_params(("arbitrary",)),
        name="moba_sample",
    )(page_table, q_rows, k_new, v_new, cache_k, cache_v)


def _rope_tables(pos):
    half = HEAD_DIM // 2
    inv = ROPE_THETA ** (-jnp.arange(half, dtype=F32) / half)
    ang = pos.astype(F32)[:, None] * inv[None, :]
    cos, sin = jnp.cos(ang), jnp.sin(ang)
    cos_h = jnp.concatenate([cos, cos], axis=1)
    sin_h = jnp.concatenate([-sin, sin], axis=1)
    reps = LANES // HEAD_DIM
    return jnp.tile(cos_h, (1, reps)), jnp.tile(sin_h, (1, reps))


def _prep_weights(norm_mix, norm_ffn, w_in_ssm, conv_w, conv_b, dt_bias, a_log, d_skip, norm_ssm,
                  w_out_ssm, norm_kv, w_kv, w_q, w_o, w_gu, w_down, norm_final):
    pad_h = LANES - N_SSM_HEADS
    w_in = w_in_ssm[0]
    return dict(
        g_mix0=norm_mix[0][None], g_mix1=norm_mix[1][None],
        g_ffn0=norm_ffn[0][None], g_ffn1=norm_ffn[1][None],
        wz=w_in[:, :D_INNER].astype(BF16),
        wx=w_in[:, D_INNER:D_INNER + CONV_DIM].astype(BF16),
        wdt=jnp.pad(w_in[:, D_INNER + CONV_DIM:], ((0, 0), (0, pad_h))).astype(BF16),
        cw=conv_w[0], cb=conv_b[0][None],
        dtb=jnp.pad(dt_bias[0], (0, pad_h))[None], alog=jnp.pad(a_log[0], (0, pad_h))[None],
        dsk=jnp.repeat(d_skip[0], SSM_HEAD_DIM)[None],
        gn=norm_ssm[0][None], wo_ssm=w_out_ssm[0].astype(BF16),
        g_kv=norm_kv[None], wkv=w_kv.astype(BF16), wq=w_q[0].astype(BF16),
        wo=w_o[0].astype(BF16),
        wg0=w_gu[0][:, :D_FF].astype(BF16), wu0=w_gu[0][:, D_FF:].astype(BF16),
        wd0=w_down[0].astype(BF16),
        wg1=w_gu[1][:, :D_FF].astype(BF16), wu1=w_gu[1][:, D_FF:].astype(BF16),
        wd1=w_down[1].astype(BF16),
        g_fin=norm_final[None],
    )


def _ssd_layer_long(x, conv_in, ssm0, w, tm):
    b, t, _ = x.shape
    keep = D_CONV - 1
    assert t >= keep
    xf = x.reshape(b * t, D_MODEL)
    z, xbc, dtr = _inproj(xf, w["g_mix0"], w["wz"], w["wx"], w["wdt"], tm)
    xbc3 = xbc.reshape(b, t, CONV_DIM)
    conv0 = jnp.pad(conv_in, ((0, 0), (SUBLANES - keep, 0), (0, 0)))
    y, ssm_new = _ssd_t(xbc3, dtr.reshape(b, t, LANES), conv0, ssm0, w["cw"], w["cb"], w["dtb"],
                        w["alog"], w["dsk"])
    x2 = _ssm_out_ffn(y.reshape(b * t, D_INNER), z, xf, w["gn"], w["wo_ssm"], w["g_ffn0"],
                      w["wg0"], w["wu0"], w["wd0"], tm)
    return x2, xbc3[:, t - keep:], ssm_new


def _ssd_layer(x, conv_in, ssm0, w, tm, L, valid):
    b, t, _ = x.shape
    xf = x.reshape(b * t, D_MODEL)
    z, xbc, dtr = _inproj(xf, w["g_mix0"], w["wz"], w["wx"], w["wdt"], tm)
    tp = -(-t // L) * L
    xbc3 = xbc.reshape(b, t, CONV_DIM)
    dtr3 = dtr.reshape(b, t, LANES)
    if tp != t:
        xbc3 = jnp.pad(xbc3, ((0, 0), (0, tp - t), (0, 0)))
        dtr3 = jnp.pad(dtr3, ((0, 0), (0, tp - t), (0, 0)))
    conv0 = jnp.pad(conv_in, ((0, 0), (SUBLANES - (D_CONV - 1), 0), (0, 0)))
    y, ssm_new = _ssd(xbc3, dtr3, conv0, ssm0, w["cw"], w["cb"], w["dtb"], w["alog"], w["dsk"],
                      L, valid)
    y = y[:, :t].reshape(b * t, D_INNER)
    x2 = _ssm_out_ffn(y, z, xf, w["gn"], w["wo_ssm"], w["g_ffn0"], w["wg0"], w["wu0"], w["wd0"], tm)
    keep = D_CONV - 1
    conv_new = jnp.concatenate([conv_in[:, t:], xbc.reshape(b, t, CONV_DIM)[:, max(0, t - keep):]],
                               axis=1)
    return x2, conv_new, ssm_new


def kernel(x_prompt, x_sample, state_conv, state_ssm, cache_k, cache_v, page_table, norm_mix,
           norm_ffn, w_in_ssm, conv_w, conv_b, dt_bias, a_log, d_skip, norm_ssm, w_out_ssm,
           norm_kv, w_kv, w_q, w_o, w_gu, w_down, norm_final):
    w = _prep_weights(norm_mix, norm_ffn, w_in_ssm, conv_w, conv_b, dt_bias, a_log, d_skip,
                      norm_ssm, w_out_ssm, norm_kv, w_kv, w_q, w_o, w_gu, w_down, norm_final)
    bp, tp, _ = x_prompt.shape
    bs, ts, _ = x_sample.shape
    past_len = page_table.shape[1] * PAGE_SIZE

    tm_p = 256
    conv0_p = jnp.zeros((bp, D_CONV - 1, CONV_DIM), F32)
    ssm0_p = jnp.zeros((bp, N_SSM_HEADS, SSM_HEAD_DIM, D_STATE), F32)
    x2_p, conv_p, ssm_p = _ssd_layer_long(x_prompt, conv0_p, ssm0_p, w, tm_p)
    cos_p, sin_p = _rope_tables(jnp.arange(tp, dtype=jnp.int32))
    kt_p, vt32_p, qt_p, kh_p, vt_p, km_p = _kvq_prompt(x2_p, bp, tp, w["g_kv"], w["g_mix1"],
                                                       w["wkv"], w["wq"], cos_p, sin_p)
    k_p = kt_p.reshape(bp, N_KV_HEADS, HEAD_DIM, tp).transpose(0, 3, 1, 2)
    v_p = vt32_p.reshape(bp, N_KV_HEADS, HEAD_DIM, tp).transpose(0, 3, 1, 2)
    attn_p = _moba_prompt(qt_p, kh_p, vt_p, km_p.reshape(bp, tp // MOBA_BLOCK, KV_DIM))
    y_p = _attn_out_ffn(attn_p.reshape(bp * tp, D_MODEL), x2_p, w["wo"], w["g_ffn1"], w["wg1"],
                        w["wu1"], w["wd1"], w["g_fin"], tm_p)

    tm_s = bs * ts
    x2_s, conv_s, ssm_s = _ssd_layer(x_sample, state_conv[0], state_ssm[0], w, tm_s, SUBLANES, ts)
    pos_s = past_len + jnp.tile(jnp.arange(ts, dtype=jnp.int32), bs)
    cos_s, sin_s = _rope_tables(pos_s)
    k_s, v_s, q_s = _kvq(x2_s, w["g_kv"], w["g_mix1"], w["wkv"], w["wq"], cos_s, sin_s, tm_s)
    q_rows = q_s.reshape(bs, ts, N_HEADS, HEAD_DIM).transpose(0, 2, 1, 3).reshape(
        bs, N_HEADS * ts, HEAD_DIM)
    q_rows = jnp.pad(q_rows, ((0, 0), (0, LANES - N_HEADS * ts), (0, 0)))
    attn_rows = _moba_sample(page_table, q_rows, k_s.reshape(bs, ts, KV_DIM),
                             v_s.reshape(bs, ts, KV_DIM),
                             cache_k.transpose(0, 2, 3, 1).reshape(-1, KV_DIM, PAGE_SIZE),
                             cache_v.transpose(0, 2, 3, 1).reshape(-1, KV_DIM, PAGE_SIZE))
    attn_s = attn_rows[:, :N_HEADS * ts].reshape(bs, N_HEADS, ts, HEAD_DIM).transpose(
        0, 2, 1, 3).reshape(bs * ts, D_MODEL).astype(BF16)
    y_s = _attn_out_ffn(attn_s, x2_s, w["wo"], w["g_ffn1"], w["wg1"], w["wu1"], w["wd1"],
                        w["g_fin"], tm_s)

    return (y_p.reshape(bp, tp, D_MODEL), y_s.reshape(bs, ts, D_MODEL),
            conv_p[None], ssm_p[None],
            k_p, v_p,
            conv_s[None], ssm_s[None],
            k_s.reshape(bs, ts, N_KV_HEADS, HEAD_DIM), v_s.reshape(bs, ts, N_KV_HEADS, HEAD_DIM))
```

```python
import functools

import jax
import jax.numpy as jnp
from jax import lax
from jax.experimental import pallas as pl
from jax.experimental.pallas import tpu as pltpu

F32 = jnp.float32
BF16 = jnp.bfloat16

D_MODEL = 1024
D_INNER = 2048
SSM_HEAD_DIM = 64
N_SSM_HEADS = 32
N_SSM_GROUPS = 4
HEADS_PER_GROUP = 8
D_STATE = 128
D_CONV = 4
SSD_CHUNK = 128
GN = N_SSM_GROUPS * D_STATE
CONV_DIM = D_INNER + 2 * GN
HEAD_DIM = 64
N_HEADS = 16
N_KV_HEADS = 4
Q_PER_KV = 4
KV_DIM = N_KV_HEADS * HEAD_DIM
MOBA_BLOCK = 256
MOBA_TOP_K = 3
ROPE_THETA = 10000.0
D_FF = 2816
EPS = 1e-6
PAGE_SIZE = 128

LANES = 128
SUBLANES = 8
VMEM_LIMIT = 56 * 1024 * 1024

NT_DIMS = (((1,), (1,)), ((), ()))
TN_DIMS = (((0,), (0,)), ((), ()))
NEG_INF = float("-inf")
MASKED = -1e30
LOG2_E = 1.4426950408889634
V_AUG = HEAD_DIM + 16
PROMPT_ROW_TILE = 512
FFN_CHUNK = 256
SSD_CHUNKS_PER_STEP = 2
ATTN_UNIT_HEADS = 4


def _params(sem):
    return pltpu.CompilerParams(dimension_semantics=sem, vmem_limit_bytes=VMEM_LIMIT)


def _const_spec(shape):
    nd = len(shape)
    return pl.BlockSpec(shape, lambda *_: (0,) * nd, pipeline_mode=pl.Buffered(1))


def _silu(x):
    return x * (1.0 / (1.0 + jnp.exp(-x)))


def _inv_rms(x):
    return lax.rsqrt(jnp.mean(x * x, axis=-1, keepdims=True) + EPS)


def _split3(x):
    hi = x.astype(BF16)
    r = x - hi.astype(F32)
    mid = r.astype(BF16)
    lo = (r - mid.astype(F32)).astype(BF16)
    return hi, mid, lo


def _dot01(a01, x, dims, a_is_lhs=True):
    out = None
    for p in _split3(x):
        t = (lax.dot_general(a01, p, dims, preferred_element_type=F32) if a_is_lhs
             else lax.dot_general(p, a01, dims, preferred_element_type=F32))
        out = t if out is None else out + t
    return out


def _top3_mask(gate, valid, axis=1):
    nb = gate.shape[axis]
    col = lax.broadcasted_iota(jnp.int32, gate.shape, axis).astype(F32)
    g = jnp.where(valid, gate, NEG_INF)
    sel = jnp.zeros(gate.shape, F32)
    for _ in range(MOBA_TOP_K):
        m = jnp.max(g, axis=axis, keepdims=True)
        idx = jnp.min(jnp.where(g == m, col, float(nb)), axis=axis, keepdims=True)
        pick = col == idx
        sel = jnp.where(pick, 1.0, sel)
        g = jnp.where(pick, NEG_INF, g)
    return jnp.logical_and(sel > 0.5, valid)


def _inproj_body(x_ref, g_ref, wz_ref, wx_ref, wdt_ref, z_ref, xbc_ref, dt_ref):
    x = x_ref[...]
    h = (x * _inv_rms(x) * g_ref[...]).astype(BF16)
    z_ref[...] = jnp.dot(h, wz_ref[...], preferred_element_type=F32)
    xbc_ref[...] = jnp.dot(h, wx_ref[...], preferred_element_type=F32)
    dt_ref[...] = jnp.dot(h, wdt_ref[...], preferred_element_type=F32)


def _inproj(x, g, wz, wx, wdt, tm):
    m = x.shape[0]
    row = lambda w: pl.BlockSpec((tm, w), lambda i: (i, 0))
    return pl.pallas_call(
        _inproj_body,
        grid=(m // tm,),
        in_specs=[row(D_MODEL), _const_spec((1, D_MODEL)), _const_spec(wz.shape),
                  _const_spec(wx.shape), _const_spec(wdt.shape)],
        out_specs=[row(D_INNER), row(CONV_DIM), row(LANES)],
        out_shape=[jax.ShapeDtypeStruct((m, D_INNER), F32),
                   jax.ShapeDtypeStruct((m, CONV_DIM), F32),
                   jax.ShapeDtypeStruct((m, LANES), F32)],
        compiler_params=_params(("parallel",)),
        name="ssd_inproj",
    )(x, g, wz, wx, wdt)


def _causal_conv(pad_ref, xbc, cw_ref, cb_ref, rows, cols=slice(None)):
    pad_ref[SUBLANES:SUBLANES + rows, cols] = xbc
    conv = cb_ref[:, cols]
    for i in range(D_CONV):
        off = SUBLANES - (D_CONV - 1) + i
        conv = conv + pad_ref[off:off + rows, cols] * cw_ref[i:i + 1, cols]
    pad_ref[0:SUBLANES, cols] = pad_ref[rows:rows + SUBLANES, cols]
    return conv


def _ssd_body(xbc_ref, dtr_ref, conv0_ref, ssm0_ref, cw_ref, cb_ref, dtb_ref, alog_ref, dsk_ref,
              y_ref, st_ref, pad_ref, *, L, valid):
    c = pl.program_id(1)

    @pl.when(c == 0)
    def _():
        pad_ref[0:SUBLANES, :] = conv0_ref[0]
        st_ref[0] = ssm0_ref[0]

    act = _silu(_causal_conv(pad_ref, xbc_ref[0], cw_ref, cb_ref, L))
    xs = act[:, :D_INNER]
    bm = act[:, D_INNER:D_INNER + GN].astype(BF16)
    cm = act[:, D_INNER + GN:].astype(BF16)

    t = dtr_ref[0] + dtb_ref[...]
    dt = jnp.maximum(t, 0.0) + jnp.log1p(jnp.exp(-jnp.abs(t)))
    if valid < L:
        rows = lax.broadcasted_iota(jnp.int32, dt.shape, 0)
        dt = jnp.where(rows < valid, dt, 0.0)
    a = -jnp.exp(alog_ref[...])
    dta = dt * a

    r_i = lax.broadcasted_iota(jnp.int32, (L, L), 0)
    c_i = lax.broadcasted_iota(jnp.int32, (L, L), 1)
    causal = r_i >= c_i
    tril = jnp.where(causal, 1.0, 0.0).astype(BF16)
    e_r = lax.broadcasted_iota(jnp.int32, (LANES, LANES), 0)
    e_c = lax.broadcasted_iota(jnp.int32, (LANES, LANES), 1)
    eye = jnp.where(e_r == e_c, 1.0, 0.0).astype(BF16)

    acs = _dot01(tril, dta, (((1,), (0,)), ((), ())))
    acs_t = _dot01(eye, acs, NT_DIMS)
    last = acs[L - 1:L, :]
    dec_end = jnp.exp(last - acs)
    e_acs = jnp.exp(acs)
    dec_chunk = jnp.exp(last)

    for g in range(N_SSM_GROUPS):
        bg = bm[:, g * D_STATE:(g + 1) * D_STATE]
        cg = cm[:, g * D_STATE:(g + 1) * D_STATE]
        cb = lax.dot_general(cg, bg, NT_DIMS, preferred_element_type=F32)
        for jp in range(HEADS_PER_GROUP // 2):
            pair = []
            for h in (g * HEADS_PER_GROUP + 2 * jp, g * HEADS_PER_GROUP + 2 * jp + 1):
                diff = acs[:, h:h + 1] - acs_t[h:h + 1, :]
                seg = jnp.exp(jnp.where(causal, diff, NEG_INF))
                w = (cb * seg).astype(BF16)
                xs_h = xs[:, h * SSM_HEAD_DIM:(h + 1) * SSM_HEAD_DIM]
                xd_h = xs_h * dt[:, h:h + 1]
                y_diag = jnp.dot(w, xd_h.astype(BF16), preferred_element_type=F32)
                st_h = st_ref[0, h]
                y_off = lax.dot_general(cg, st_h.astype(BF16), NT_DIMS,
                                        preferred_element_type=F32) * e_acs[:, h:h + 1]
                xdd = (xd_h * dec_end[:, h:h + 1]).astype(BF16)
                cs = lax.dot_general(xdd, bg, TN_DIMS, preferred_element_type=F32)
                st_ref[0, h] = st_h * dec_chunk[:, h:h + 1] + cs
                pair.append(y_diag + y_off
                            + dsk_ref[:, h * SSM_HEAD_DIM:(h + 1) * SSM_HEAD_DIM] * xs_h)
            lo = (g * HEADS_PER_GROUP + 2 * jp) * SSM_HEAD_DIM
            y_ref[0, :, lo:lo + 2 * SSM_HEAD_DIM] = jnp.concatenate(pair, axis=1)


def _ssd_t_body(xbc_ref, dtr_ref, conv0_ref, ssm0_ref, cw_ref, cb_ref, dtb_ref, alog_ref, dsk_ref,
                y_ref, st_ref, pad_ref, yt_ref, *, L, n_sub):
    @pl.when(pl.program_id(1) == 0)
    def _():
        pad_ref[0:SUBLANES, :] = conv0_ref[0]
        st_ref[0] = ssm0_ref[0]

    def chunk(ci, carry):
        rows = pl.ds(pl.multiple_of(ci * L, L), L)
        y_ref[0, rows, :] = _ssd_t_chunk(xbc_ref[0, rows, :], dtr_ref[0, rows, :], cw_ref, cb_ref,
                                         dtb_ref, alog_ref, dsk_ref, st_ref, pad_ref, yt_ref, L)
        return carry

    lax.fori_loop(0, n_sub, chunk, 0)


def _ssd_t_chunk(xbc, dtr, cw_ref, cb_ref, dtb_ref, alog_ref, dsk_ref, st_ref, pad_ref, yt_ref, L):
    hd = SSM_HEAD_DIM
    gw = HEADS_PER_GROUP * hd
    act = _silu(_causal_conv(pad_ref, xbc, cw_ref, cb_ref, L))
    xs_t = act[:, :D_INNER].T
    bm = act[:, D_INNER:D_INNER + GN].astype(BF16)
    cm = act[:, D_INNER + GN:].astype(BF16)

    t = dtr + dtb_ref[...]
    dt = jnp.maximum(t, 0.0) + jnp.log1p(jnp.exp(-jnp.abs(t)))
    dta = dt * (-jnp.exp(alog_ref[...]))

    r_i = lax.broadcasted_iota(jnp.int32, (L, L), 0)
    c_i = lax.broadcasted_iota(jnp.int32, (L, L), 1)
    tril = jnp.where(r_i >= c_i, 1.0, 0.0).astype(BF16)
    causal_t = c_i >= r_i

    acs = _dot01(tril, dta, (((1,), (0,)), ((), ())))
    acs_t = acs.T
    dt_t = dt.T
    dec_end_t = jnp.exp(acs_t[:, L - 1:L] - acs_t)
    e_acs_t = jnp.exp(acs_t)
    dec_chunk = jnp.exp(acs[L - 1:L, :])
    scale_in = dt_t
    scale_st = dt_t * dec_end_t

    for g in range(N_SSM_GROUPS):
        bg = bm[:, g * D_STATE:(g + 1) * D_STATE]
        cg = cm[:, g * D_STATE:(g + 1) * D_STATE]
        cb_t = lax.dot_general(bg, cg, NT_DIMS, preferred_element_type=F32)
        h0 = g * HEADS_PER_GROUP
        xs_g = xs_t[g * gw:(g + 1) * gw, :]
        xd_parts, xdd_parts = [], []
        for j in range(HEADS_PER_GROUP):
            h = h0 + j
            xd = xs_g[j * hd:(j + 1) * hd, :] * scale_in[h:h + 1, :]
            xd_parts.append(xd.astype(BF16))
            xdd_parts.append((xs_g[j * hd:(j + 1) * hd, :] * scale_st[h:h + 1, :]).astype(BF16))
        st_g = st_ref[0, h0:h0 + HEADS_PER_GROUP].reshape(gw, D_STATE)
        y_off = lax.dot_general(st_g.astype(BF16), cg, NT_DIMS,
                                preferred_element_type=F32)
        cs = jnp.dot(jnp.concatenate(xdd_parts, axis=0), bg,
                     preferred_element_type=F32)
        for j in range(HEADS_PER_GROUP):
            h = h0 + j
            diff = acs_t[h:h + 1, :] - acs[:, h:h + 1]
            seg_t = jnp.exp(jnp.where(causal_t, diff, NEG_INF))
            w_t = (cb_t * seg_t).astype(BF16)
            y_diag = jnp.dot(xd_parts[j], w_t, preferred_element_type=F32)
            rows = slice(j * hd, (j + 1) * hd)
            yt_ref[h * hd:(h + 1) * hd, :] = (
                y_diag + y_off[rows, :] * e_acs_t[h:h + 1, :]
                + dsk_ref[h * hd:(h + 1) * hd, :] * xs_g[rows, :])
            st_ref[0, h] = st_g[rows, :] * dec_chunk[:, h:h + 1] + cs[rows, :]
    return yt_ref[...].T


def _ssd(xbc, dtr, conv0, ssm0, cw, cb, dtb, alog, dsk, L, valid):
    nb, t = xbc.shape[0], xbc.shape[1]
    nc = t // L
    st_spec = pl.BlockSpec((1, N_SSM_HEADS, SSM_HEAD_DIM, D_STATE), lambda b, c: (b, 0, 0, 0))
    return pl.pallas_call(
        functools.partial(_ssd_body, L=L, valid=valid),
        grid=(nb, nc),
        in_specs=[pl.BlockSpec((1, L, CONV_DIM), lambda b, c: (b, c, 0)),
                  pl.BlockSpec((1, L, LANES), lambda b, c: (b, c, 0)),
                  pl.BlockSpec((1, SUBLANES, CONV_DIM), lambda b, c: (b, 0, 0)),
                  st_spec,
                  _const_spec(cw.shape), _const_spec(cb.shape), _const_spec(dtb.shape),
                  _const_spec(alog.shape), _const_spec(dsk.shape)],
        out_specs=[pl.BlockSpec((1, L, D_INNER), lambda b, c: (b, c, 0)), st_spec],
        out_shape=[jax.ShapeDtypeStruct((nb, t, D_INNER), F32),
                   jax.ShapeDtypeStruct(ssm0.shape, F32)],
        scratch_shapes=[pltpu.VMEM((L + SUBLANES, CONV_DIM), F32)],
        compiler_params=_params(("parallel", "arbitrary")),
        name="ssd_scan",
    )(xbc, dtr, conv0, ssm0, cw, cb, dtb, alog, dsk)


def _ssd_t(xbc, dtr, conv0, ssm0, cw, cb, dtb, alog, dsk):
    nb, t = xbc.shape[0], xbc.shape[1]
    L = SSD_CHUNK
    n_sub = SSD_CHUNKS_PER_STEP if t % (L * SSD_CHUNKS_PER_STEP) == 0 else 1
    rows = L * n_sub
    assert L == LANES and t % rows == 0
    dsk_t = jnp.broadcast_to(dsk.reshape(D_INNER, 1), (D_INNER, LANES))
    st_spec = pl.BlockSpec((1, N_SSM_HEADS, SSM_HEAD_DIM, D_STATE), lambda b, c: (b, 0, 0, 0))
    return pl.pallas_call(
        functools.partial(_ssd_t_body, L=L, n_sub=n_sub),
        grid=(nb, t // rows),
        in_specs=[pl.BlockSpec((1, rows, CONV_DIM), lambda b, c: (b, c, 0)),
                  pl.BlockSpec((1, rows, LANES), lambda b, c: (b, c, 0)),
                  pl.BlockSpec((1, SUBLANES, CONV_DIM), lambda b, c: (b, 0, 0)),
                  st_spec, _const_spec(cw.shape), _const_spec(cb.shape),
                  _const_spec(dtb.shape), _const_spec(alog.shape), _const_spec(dsk_t.shape)],
        out_specs=[pl.BlockSpec((1, rows, D_INNER), lambda b, c: (b, c, 0)), st_spec],
        out_shape=[jax.ShapeDtypeStruct((nb, t, D_INNER), F32),
                   jax.ShapeDtypeStruct(ssm0.shape, F32)],
        scratch_shapes=[pltpu.VMEM((L + SUBLANES, CONV_DIM), F32), pltpu.VMEM((D_INNER, L), F32)],
        compiler_params=_params(("parallel", "arbitrary")),
        name="ssd_scan_t",
    )(xbc, dtr, conv0, ssm0, cw, cb, dtb, alog, dsk_t)


def _ffn(x1, nf_ref, wg_ref, wu_ref, wd_ref):
    h = (x1 * _inv_rms(x1) * nf_ref[...]).astype(BF16)
    out = x1
    for c in range(D_FF // FFN_CHUNK):
        cols = slice(c * FFN_CHUNK, (c + 1) * FFN_CHUNK)
        gate = jnp.dot(h, wg_ref[:, cols], preferred_element_type=F32)
        up = jnp.dot(h, wu_ref[:, cols], preferred_element_type=F32)
        act = (_silu(gate) * up).astype(BF16)
        out = out + jnp.dot(act, wd_ref[cols, :], preferred_element_type=F32)
    return out


def _ssm_out_ffn_body(y_ref, z_ref, x_ref, gn_ref, wo_ref, nf_ref, wg_ref, wu_ref, wd_ref, o_ref):
    y = y_ref[...] * _silu(z_ref[...])
    gw = D_INNER // N_SSM_GROUPS
    parts = []
    for g in range(N_SSM_GROUPS):
        yg = y[:, g * gw:(g + 1) * gw]
        parts.append(yg * _inv_rms(yg))
    y = (jnp.concatenate(parts, axis=1) * gn_ref[...]).astype(BF16)
    x1 = x_ref[...] + jnp.dot(y, wo_ref[...], preferred_element_type=F32)
    o_ref[...] = _ffn(x1, nf_ref, wg_ref, wu_ref, wd_ref)


def _attn_out_ffn_body(a_ref, x_ref, wo_ref, nf_ref, wg_ref, wu_ref, wd_ref, nfin_ref, o_ref):
    x1 = x_ref[...] + jnp.dot(a_ref[...], wo_ref[...], preferred_element_type=F32)
    x2 = _ffn(x1, nf_ref, wg_ref, wu_ref, wd_ref)
    o_ref[...] = x2 * _inv_rms(x2) * nfin_ref[...]


def _ssm_out_ffn(y, z, x, gn, wo, nf, wg, wu, wd, tm):
    m = x.shape[0]
    row = lambda w: pl.BlockSpec((tm, w), lambda i: (i, 0))
    return pl.pallas_call(
        _ssm_out_ffn_body,
        grid=(m // tm,),
        in_specs=[row(D_INNER), row(D_INNER), row(D_MODEL), _const_spec(gn.shape),
                  _const_spec(wo.shape), _const_spec(nf.shape), _const_spec(wg.shape),
                  _const_spec(wu.shape), _const_spec(wd.shape)],
        out_specs=row(D_MODEL),
        out_shape=jax.ShapeDtypeStruct((m, D_MODEL), F32),
        compiler_params=_params(("parallel",)),
        name="ssm_out_ffn",
    )(y, z, x, gn, wo, nf, wg, wu, wd)


def _attn_out_ffn(a, x, wo, nf, wg, wu, wd, nfin, tm):
    m = x.shape[0]
    row = lambda w: pl.BlockSpec((tm, w), lambda i: (i, 0))
    return pl.pallas_call(
        _attn_out_ffn_body,
        grid=(m // tm,),
        in_specs=[row(D_MODEL), row(D_MODEL), _const_spec(wo.shape), _const_spec(nf.shape),
                  _const_spec(wg.shape), _const_spec(wu.shape), _const_spec(wd.shape),
                  _const_spec(nfin.shape)],
        out_specs=row(D_MODEL),
        out_shape=jax.ShapeDtypeStruct((m, D_MODEL), F32),
        compiler_params=_params(("parallel",)),
        name="attn_out_ffn",
    )(a, x, wo, nf, wg, wu, wd, nfin)


def _rope(x, cos, sin_signed, first_half):
    outs = []
    for c in range(x.shape[1] // LANES):
        xc = x[:, c * LANES:(c + 1) * LANES]
        partner = jnp.where(first_half, pltpu.roll(xc, LANES - HEAD_DIM // 2, 1),
                            pltpu.roll(xc, HEAD_DIM // 2, 1))
        outs.append(xc * cos + partner * sin_signed)
    return jnp.concatenate(outs, axis=1)


def _kvq_compute(x_ref, gkv_ref, gq_ref, wkv_ref, wq_ref, cos_ref, sin_ref):
    x = x_ref[...]
    xn = x * _inv_rms(x)
    hk = (xn * gkv_ref[...]).astype(BF16)
    hq = (xn * gq_ref[...]).astype(BF16)
    cos = cos_ref[...]
    sin = sin_ref[...]
    lane = lax.broadcasted_iota(jnp.int32, cos.shape, 1)
    first_half = (lane % HEAD_DIM) < (HEAD_DIM // 2)
    kv = jnp.dot(hk, wkv_ref[...], preferred_element_type=F32)
    k = _rope(kv[:, :KV_DIM], cos, sin, first_half)
    q = _rope(jnp.dot(hq, wq_ref[...], preferred_element_type=F32), cos, sin, first_half)
    return k, kv[:, KV_DIM:], q


def _kvq_body(x_ref, gkv_ref, gq_ref, wkv_ref, wq_ref, cos_ref, sin_ref, k_ref, v_ref, q_ref):
    k, v, q = _kvq_compute(x_ref, gkv_ref, gq_ref, wkv_ref, wq_ref, cos_ref, sin_ref)
    k_ref[...] = k
    v_ref[...] = v
    q_ref[...] = q


def _kvq_prompt_body(x_ref, gkv_ref, gq_ref, wkv_ref, wq_ref, cos_ref, sin_ref,
                     kt_ref, vt32_ref, qt_ref, kh_ref, vt_ref, km_ref):
    k, v, q = _kvq_compute(x_ref, gkv_ref, gq_ref, wkv_ref, wq_ref, cos_ref, sin_ref)
    kt_ref[0] = k.T
    v_t = v.T
    vt32_ref[0] = v_t
    qt_ref[0] = q.T
    vt = v_t.astype(BF16)
    blk = pl.program_id(1)
    col = lax.broadcasted_iota(jnp.int32, (MOBA_BLOCK, LANES - HEAD_DIM), 1)
    onehot = jnp.where(col == blk, 1.0, 0.0).astype(BF16)
    row = lax.broadcasted_iota(jnp.int32, (V_AUG - HEAD_DIM, MOBA_BLOCK), 0)
    ones_row = jnp.where(row == 0, 1.0, 0.0).astype(BF16)
    for kvh in range(N_KV_HEADS):
        kh_ref[0, 0, kvh] = jnp.concatenate(
            [k[:, kvh * HEAD_DIM:(kvh + 1) * HEAD_DIM].astype(BF16), onehot], axis=1)
        vt_ref[0, 0, kvh] = jnp.concatenate(
            [vt[kvh * HEAD_DIM:(kvh + 1) * HEAD_DIM, :], ones_row], axis=0)
    km_ref[0, 0] = jnp.sum(k, axis=0, keepdims=True) * (1.0 / MOBA_BLOCK)


def _kvq(x, gkv, gq, wkv, wq, cos, sin, tm):
    m = x.shape[0]
    row = lambda w: pl.BlockSpec((tm, w), lambda i: (i, 0))
    tab = pl.BlockSpec((tm, LANES), lambda i: (0, 0))
    return pl.pallas_call(
        _kvq_body,
        grid=(m // tm,),
        in_specs=[row(D_MODEL), _const_spec(gkv.shape), _const_spec(gq.shape),
                  _const_spec(wkv.shape), _const_spec(wq.shape), tab, tab],
        out_specs=[row(KV_DIM), row(KV_DIM), row(D_MODEL)],
        out_shape=[jax.ShapeDtypeStruct((m, KV_DIM), F32), jax.ShapeDtypeStruct((m, KV_DIM), F32),
                   jax.ShapeDtypeStruct((m, D_MODEL), F32)],
        compiler_params=_params(("parallel",)),
        name="kvq_proj",
    )(x, gkv, gq, wkv, wq, cos, sin)


def _kvq_prompt(x, b, t, gkv, gq, wkv, wq, cos, sin):
    nb = t // MOBA_BLOCK
    tm = MOBA_BLOCK
    row = lambda w: pl.BlockSpec((tm, w), lambda bi, i: (bi * nb + i, 0))
    tab = pl.BlockSpec((tm, LANES), lambda bi, i: (i, 0))
    return pl.pallas_call(
        _kvq_prompt_body,
        grid=(b, nb),
        in_specs=[row(D_MODEL), _const_spec(gkv.shape), _const_spec(gq.shape),
                  _const_spec(wkv.shape), _const_spec(wq.shape), tab, tab],
        out_specs=[pl.BlockSpec((1, KV_DIM, tm), lambda bi, i: (bi, 0, i)),
                   pl.BlockSpec((1, KV_DIM, tm), lambda bi, i: (bi, 0, i)),
                   pl.BlockSpec((1, D_MODEL, tm), lambda bi, i: (bi, 0, i)),
                   pl.BlockSpec((1, 1, N_KV_HEADS, tm, LANES), lambda bi, i: (bi, i, 0, 0, 0)),
                   pl.BlockSpec((1, 1, N_KV_HEADS, V_AUG, tm), lambda bi, i: (bi, i, 0, 0, 0)),
                   pl.BlockSpec((1, 1, 1, KV_DIM), lambda bi, i: (bi, i, 0, 0))],
        out_shape=[jax.ShapeDtypeStruct((b, KV_DIM, t), F32),
                   jax.ShapeDtypeStruct((b, KV_DIM, t), F32),
                   jax.ShapeDtypeStruct((b, D_MODEL, t), F32),
                   jax.ShapeDtypeStruct((b, nb, N_KV_HEADS, tm, LANES), BF16),
                   jax.ShapeDtypeStruct((b, nb, N_KV_HEADS, V_AUG, tm), BF16),
                   jax.ShapeDtypeStruct((b, nb, 1, KV_DIM), F32)],
        compiler_params=_params(("parallel", "parallel")),
        name="kvq_proj_prompt",
    )(x, gkv, gq, wkv, wq, cos, sin)


def _moba_prompt_body(qt_ref, kh_ref, vt_ref, km_ref, o_ref, s_a, s_b, acc_a, acc_b, *, nb):
    i = pl.program_id(1)
    g_per_unit = ATTN_UNIT_HEADS
    rows = g_per_unit * MOBA_BLOCK
    n_units = N_HEADS // g_per_unit
    scale = (HEAD_DIM ** -0.5) * LOG2_E
    n_pairs = (i + 1) // 2
    blk_row = lax.broadcasted_iota(jnp.int32, (nb, rows), 0)
    key_i = lax.broadcasted_iota(jnp.int32, (MOBA_BLOCK, rows), 0)
    tok_i = lax.broadcasted_iota(jnp.int32, (MOBA_BLOCK, rows), 1) % MOBA_BLOCK
    own_ok = key_i <= tok_i
    fold = lambda s: jnp.max(s.reshape(MOBA_BLOCK // SUBLANES, SUBLANES, rows), axis=0)

    def setup(u):
        kvh = (u * g_per_unit) // Q_PER_KV
        lo = kvh * HEAD_DIM
        base = u * g_per_unit * HEAD_DIM
        qt2 = qt_ref[0, base:base + g_per_unit * HEAD_DIM, :]
        qt = jnp.concatenate([qt2[g * HEAD_DIM:(g + 1) * HEAD_DIM, :] for g in range(g_per_unit)],
                             axis=1)
        km = km_ref[0, :, lo:lo + HEAD_DIM]
        gate = jnp.dot(km, qt, preferred_element_type=F32,
                       precision=lax.Precision.HIGHEST)
        sel = _top3_mask(gate, blk_row < i, axis=0)
        pen = jnp.where(sel, 0.0, MASKED).astype(BF16)
        qb = (qt * scale).astype(BF16)
        q_own = jnp.concatenate([qb, jnp.zeros((LANES - HEAD_DIM, rows), BF16)], axis=0)
        q_aug = jnp.concatenate([qb, pen, jnp.zeros((LANES - HEAD_DIM - nb, rows), BF16)], axis=0)
        s_own = jnp.dot(kh_ref[0, i, kvh], q_own, preferred_element_type=F32)
        s_own = jnp.where(own_ok, s_own, MASKED)
        s_bufs[u % 2][nb] = s_own
        return kvh, q_aug, fold(s_own)

    def scores_pair(u, kvh, q_aug, jj, m8):
        for t in range(2):
            j = 2 * jj + t
            s = jnp.dot(kh_ref[0, j, kvh], q_aug, preferred_element_type=F32)
            s_bufs[u % 2][j] = s
            m8 = jnp.maximum(m8, fold(s))
        return m8

    def weighted_pair(u, kvh, m, jj):
        j = 2 * jj
        s_buf = s_bufs[u % 2]
        p = jnp.concatenate([jnp.exp2(s_buf[j] - m).astype(BF16),
                             jnp.exp2(s_buf[j + 1] - m).astype(BF16)], axis=0)
        v2 = jnp.concatenate([vt_ref[0, j, kvh], vt_ref[0, j + 1, kvh]], axis=1)
        acc_bufs[u % 2][...] += jnp.dot(v2, p, preferred_element_type=F32)

    def over_pairs(step, carry):
        n_double = n_pairs // 2

        def double(q, c):
            return step(2 * q + 1, step(2 * q, c))

        carry = lax.fori_loop(0, n_double, double, carry)
        return lax.fori_loop(2 * n_double, n_pairs, step, carry)

    s_bufs = (s_a, s_b)
    acc_bufs = (acc_a, acc_b)
    kvh, q_aug, m8 = setup(0)
    m8 = over_pairs(functools.partial(scores_pair, 0, kvh, q_aug), m8)
    for u in range(n_units):
        m = jnp.max(m8, axis=0, keepdims=True)
        p_own = jnp.exp2(s_bufs[u % 2][nb] - m).astype(BF16)
        acc_bufs[u % 2][...] = jnp.dot(vt_ref[0, i, kvh], p_own, preferred_element_type=F32)
        if u + 1 < n_units:
            kvh_n, q_aug_n, m8_n = setup(u + 1)

            def both(jj, m8c, u=u, kvh=kvh, m=m, kvh_n=kvh_n, q_aug_n=q_aug_n):
                m8c = scores_pair(u + 1, kvh_n, q_aug_n, jj, m8c)
                weighted_pair(u, kvh, m, jj)
                return m8c

            m8_n = over_pairs(both, m8_n)
        else:
            def last(jj, carry, u=u, kvh=kvh, m=m):
                weighted_pair(u, kvh, m, jj)
                return carry

            over_pairs(last, 0)
        acc = acc_bufs[u % 2][...]
        out_t = acc[:HEAD_DIM, :] / acc[HEAD_DIM:HEAD_DIM + 1, :]
        out2 = jnp.concatenate([out_t[:, g * MOBA_BLOCK:(g + 1) * MOBA_BLOCK]
                                for g in range(g_per_unit)], axis=0)
        base = u * g_per_unit * HEAD_DIM
        o_ref[0, :, base:base + g_per_unit * HEAD_DIM] = out2.T.astype(o_ref.dtype)
        if u + 1 < n_units:
            kvh, m8 = kvh_n, m8_n


def _moba_prompt(qt, kh, vt, km):
    b, t = qt.shape[0], qt.shape[2]
    nb = t // MOBA_BLOCK
    rows = ATTN_UNIT_HEADS * MOBA_BLOCK
    return pl.pallas_call(
        functools.partial(_moba_prompt_body, nb=nb),
        grid=(b, nb),
        in_specs=[pl.BlockSpec((1, D_MODEL, MOBA_BLOCK), lambda bi, i: (bi, 0, i)),
                  pl.BlockSpec((1, nb, N_KV_HEADS, MOBA_BLOCK, LANES),
                               lambda bi, i: (bi, 0, 0, 0, 0), pipeline_mode=pl.Buffered(1)),
                  pl.BlockSpec((1, nb, N_KV_HEADS, V_AUG, MOBA_BLOCK),
                               lambda bi, i: (bi, 0, 0, 0, 0), pipeline_mode=pl.Buffered(1)),
                  pl.BlockSpec((1, nb, KV_DIM), lambda bi, i: (bi, 0, 0))],
        out_specs=pl.BlockSpec((1, MOBA_BLOCK, D_MODEL), lambda bi, i: (bi, i, 0)),
        out_shape=jax.ShapeDtypeStruct((b, t, D_MODEL), BF16),
        scratch_shapes=[pltpu.VMEM((nb + 1, MOBA_BLOCK, rows), F32),
                        pltpu.VMEM((nb + 1, MOBA_BLOCK, rows), F32),
                        pltpu.VMEM((V_AUG, rows), F32), pltpu.VMEM((V_AUG, rows), F32)],
        compiler_params=_params(("parallel", "arbitrary")),
        name="moba_prompt",
    )(qt, kh, vt, km)


def _page_copy(cache_hbm, buf, sem, page, slot, p):
    return pltpu.make_async_copy(cache_hbm.at[page], buf.at[slot, p], sem)


def _moba_sample_body(pt_ref, q_ref, kn_ref, vn_ref, ck_hbm, cv_hbm, o_ref,
                      kbuf, vbuf, s_scr, new_scr, sems, *, n_pages, n_new):
    b = pl.program_id(0)
    slot = b % 2
    rows = q_ref.shape[1]
    ppb = MOBA_BLOCK // PAGE_SIZE
    nb = n_pages // ppb
    scale = HEAD_DIM ** -0.5

    def fetch(seq, to_slot):
        def start_page(p, carry):
            page = pt_ref[seq, p]
            _page_copy(ck_hbm, kbuf, sems.at[0, to_slot], page, to_slot, p).start()
            _page_copy(cv_hbm, vbuf, sems.at[1, to_slot], page, to_slot, p).start()
            return carry
        lax.fori_loop(0, n_pages, start_page, 0)

    @pl.when(b == 0)
    def _():
        fetch(0, 0)

    @pl.when(b + 1 < pl.num_programs(0))
    def _():
        fetch(b + 1, 1 - slot)

    q64 = q_ref[0]
    qt = jnp.concatenate([q64] * N_KV_HEADS, axis=1)
    r_i = lax.broadcasted_iota(jnp.int32, qt.shape, 0)
    c_i = lax.broadcasted_iota(jnp.int32, qt.shape, 1)
    qpad = jnp.where(c_i // HEAD_DIM == r_i // (Q_PER_KV * n_new), qt, 0.0)
    qb = (qpad * scale).astype(BF16)

    def wait_k(p, carry):
        _page_copy(ck_hbm, kbuf, sems.at[0, slot], 0, slot, p).wait()
        return carry

    lax.fori_loop(0, n_pages, wait_k, 0)

    blk_lane = lax.broadcasted_iota(jnp.int32, (KV_DIM, LANES), 1)
    km_t = jnp.zeros((KV_DIM, LANES), F32)
    for j in range(nb):
        kt = jnp.concatenate([kbuf[slot, ppb * j + u] for u in range(ppb)], axis=1)
        km_col = jnp.sum(kt, axis=1, keepdims=True) * (1.0 / MOBA_BLOCK)
        km_t = jnp.where(blk_lane == j, km_col, km_t)
        s_scr[:, j * MOBA_BLOCK:(j + 1) * MOBA_BLOCK] = jnp.dot(
            qb, kt.astype(BF16), preferred_element_type=F32)

    gate = jnp.dot(qpad, km_t, preferred_element_type=F32,
                   precision=lax.Precision.HIGHEST)
    sel = _top3_mask(gate, lax.broadcasted_iota(jnp.int32, gate.shape, 1) < nb)
    pen = jnp.where(sel, 0.0, NEG_INF)

    new_scr[...] = jnp.zeros(new_scr.shape, F32)
    new_scr[0:n_new, :] = kn_ref[0]
    s_new = lax.dot_general(qb, new_scr[...].astype(BF16), NT_DIMS, preferred_element_type=F32)
    nr = lax.broadcasted_iota(jnp.int32, s_new.shape, 0)
    ncol = lax.broadcasted_iota(jnp.int32, s_new.shape, 1)
    s_new = jnp.where(ncol <= nr % n_new, s_new, NEG_INF)

    def lane_fold(x, op):
        out = x[:, :LANES]
        for c in range(1, x.shape[1] // LANES):
            out = op(out, x[:, c * LANES:(c + 1) * LANES])
        return out

    m_part = s_new
    for j in range(nb):
        sj = s_scr[:, j * MOBA_BLOCK:(j + 1) * MOBA_BLOCK] + pen[:, j:j + 1]
        m_part = jnp.maximum(m_part, lane_fold(sj, jnp.maximum))
    m = jnp.max(m_part, axis=1, keepdims=True)

    def wait_v(p, carry):
        _page_copy(cv_hbm, vbuf, sems.at[1, slot], 0, slot, p).wait()
        return carry

    lax.fori_loop(0, n_pages, wait_v, 0)

    p_new = jnp.exp(s_new - m)
    l_part = p_new
    new_scr[0:n_new, :] = vn_ref[0]
    acc = jnp.dot(p_new.astype(BF16), new_scr[...].astype(BF16), preferred_element_type=F32)
    for j in range(nb):
        pj = jnp.exp(s_scr[:, j * MOBA_BLOCK:(j + 1) * MOBA_BLOCK] + pen[:, j:j + 1] - m)
        l_part = l_part + lane_fold(pj, jnp.add)
        vt = jnp.concatenate([vbuf[slot, ppb * j + u] for u in range(ppb)], axis=1).astype(BF16)
        acc = acc + lax.dot_general(pj.astype(BF16), vt, NT_DIMS, preferred_element_type=F32)
    acc = acc / jnp.sum(l_part, axis=1, keepdims=True)
    ro = lax.broadcasted_iota(jnp.int32, (rows, HEAD_DIM), 0) // (Q_PER_KV * n_new)
    out = jnp.zeros((rows, HEAD_DIM), F32)
    for kvh in range(N_KV_HEADS):
        out = jnp.where(ro == kvh, acc[:, kvh * HEAD_DIM:(kvh + 1) * HEAD_DIM], out)
    o_ref[0] = out


def _moba_sample(page_table, q_rows, k_new, v_new, cache_k, cache_v):
    nseq, n_pages = page_table.shape
    rows = q_rows.shape[1]
    n_new = k_new.shape[1]
    past = n_pages * PAGE_SIZE
    grid_spec = pltpu.PrefetchScalarGridSpec(
        num_scalar_prefetch=1,
        grid=(nseq,),
        in_specs=[pl.BlockSpec((1, rows, HEAD_DIM), lambda b, pt: (b, 0, 0)),
                  pl.BlockSpec((1, n_new, KV_DIM), lambda b, pt: (b, 0, 0)),
                  pl.BlockSpec((1, n_new, KV_DIM), lambda b, pt: (b, 0, 0)),
                  pl.BlockSpec(memory_space=pl.ANY),
                  pl.BlockSpec(memory_space=pl.ANY)],
        out_specs=pl.BlockSpec((1, rows, HEAD_DIM), lambda b, pt: (b, 0, 0)),
        scratch_shapes=[pltpu.VMEM((2, n_pages, KV_DIM, PAGE_SIZE), F32),
                        pltpu.VMEM((2, n_pages, KV_DIM, PAGE_SIZE), F32),
                        pltpu.VMEM((rows, past), F32),
                        pltpu.VMEM((LANES, KV_DIM), F32),
                        pltpu.SemaphoreType.DMA((2, 2))],
    )
    return pl.pallas_call(
        functools.partial(_moba_sample_body, n_pages=n_pages, n_new=n_new),
        grid_spec=grid_spec,
        out_shape=jax.ShapeDtypeStruct((nseq, rows, HEAD_DIM), F32),
        compiler_params=_params(("arbitrary",)),
        name="moba_sample",
    )(page_table, q_rows, k_new, v_new, cache_k, cache_v)


def _rope_tables(pos):
    half = HEAD_DIM // 2
    inv = ROPE_THETA ** (-jnp.arange(half, dtype=F32) / half)
    ang = pos.astype(F32)[:, None] * inv[None, :]
    cos, sin = jnp.cos(ang), jnp.sin(ang)
    cos_h = jnp.concatenate([cos, cos], axis=1)
    sin_h = jnp.concatenate([-sin, sin], axis=1)
    reps = LANES // HEAD_DIM
    return jnp.tile(cos_h, (1, reps)), jnp.tile(sin_h, (1, reps))


def _prep_weights(norm_mix, norm_ffn, w_in_ssm, conv_w, conv_b, dt_bias, a_log, d_skip, norm_ssm,
                  w_out_ssm, norm_kv, w_kv, w_q, w_o, w_gu, w_down, norm_final):
    pad_h = LANES - N_SSM_HEADS
    w_in = w_in_ssm[0]
    return dict(
        g_mix0=norm_mix[0][None], g_mix1=norm_mix[1][None],
        g_ffn0=norm_ffn[0][None], g_ffn1=norm_ffn[1][None],
        wz=w_in[:, :D_INNER].astype(BF16),
        wx=w_in[:, D_INNER:D_INNER + CONV_DIM].astype(BF16),
        wdt=jnp.pad(w_in[:, D_INNER + CONV_DIM:], ((0, 0), (0, pad_h))).astype(BF16),
        cw=conv_w[0], cb=conv_b[0][None],
        dtb=jnp.pad(dt_bias[0], (0, pad_h))[None], alog=jnp.pad(a_log[0], (0, pad_h))[None],
        dsk=jnp.repeat(d_skip[0], SSM_HEAD_DIM)[None],
        gn=norm_ssm[0][None], wo_ssm=w_out_ssm[0].astype(BF16),
        g_kv=norm_kv[None], wkv=w_kv.astype(BF16), wq=w_q[0].astype(BF16),
        wo=w_o[0].astype(BF16),
        wg0=w_gu[0][:, :D_FF].astype(BF16), wu0=w_gu[0][:, D_FF:].astype(BF16),
        wd0=w_down[0].astype(BF16),
        wg1=w_gu[1][:, :D_FF].astype(BF16), wu1=w_gu[1][:, D_FF:].astype(BF16),
        wd1=w_down[1].astype(BF16),
        g_fin=norm_final[None],
    )


def _ssd_layer_long(x, conv_in, ssm0, w, tm):
    b, t, _ = x.shape
    keep = D_CONV - 1
    assert t >= keep
    xf = x.reshape(b * t, D_MODEL)
    z, xbc, dtr = _inproj(xf, w["g_mix0"], w["wz"], w["wx"], w["wdt"], tm)
    xbc3 = xbc.reshape(b, t, CONV_DIM)
    conv0 = jnp.pad(conv_in, ((0, 0), (SUBLANES - keep, 0), (0, 0)))
    y, ssm_new = _ssd_t(xbc3, dtr.reshape(b, t, LANES), conv0, ssm0, w["cw"], w["cb"], w["dtb"],
                        w["alog"], w["dsk"])
    x2 = _ssm_out_ffn(y.reshape(b * t, D_INNER), z, xf, w["gn"], w["wo_ssm"], w["g_ffn0"],
                      w["wg0"], w["wu0"], w["wd0"], tm)
    return x2, xbc3[:, t - keep:], ssm_new


def _ssd_layer(x, conv_in, ssm0, w, tm, L, valid):
    b, t, _ = x.shape
    xf = x.reshape(b * t, D_MODEL)
    z, xbc, dtr = _inproj(xf, w["g_mix0"], w["wz"], w["wx"], w["wdt"], tm)
    tp = -(-t // L) * L
    xbc3 = xbc.reshape(b, t, CONV_DIM)
    dtr3 = dtr.reshape(b, t, LANES)
    if tp != t:
        xbc3 = jnp.pad(xbc3, ((0, 0), (0, tp - t), (0, 0)))
        dtr3 = jnp.pad(dtr3, ((0, 0), (0, tp - t), (0, 0)))
    conv0 = jnp.pad(conv_in, ((0, 0), (SUBLANES - (D_CONV - 1), 0), (0, 0)))
    y, ssm_new = _ssd(xbc3, dtr3, conv0, ssm0, w["cw"], w["cb"], w["dtb"], w["alog"], w["dsk"],
                      L, valid)
    y = y[:, :t].reshape(b * t, D_INNER)
    x2 = _ssm_out_ffn(y, z, xf, w["gn"], w["wo_ssm"], w["g_ffn0"], w["wg0"], w["wu0"], w["wd0"], tm)
    keep = D_CONV - 1
    conv_new = jnp.concatenate([conv_in[:, t:], xbc.reshape(b, t, CONV_DIM)[:, max(0, t - keep):]],
                               axis=1)
    return x2, conv_new, ssm_new


def kernel(x_prompt, x_sample, state_conv, state_ssm, cache_k, cache_v, page_table, norm_mix,
           norm_ffn, w_in_ssm, conv_w, conv_b, dt_bias, a_log, d_skip, norm_ssm, w_out_ssm,
           norm_kv, w_kv, w_q, w_o, w_gu, w_down, norm_final):
    w = _prep_weights(norm_mix, norm_ffn, w_in_ssm, conv_w, conv_b, dt_bias, a_log, d_skip,
                      norm_ssm, w_out_ssm, norm_kv, w_kv, w_q, w_o, w_gu, w_down, norm_final)
    bp, tp, _ = x_prompt.shape
    bs, ts, _ = x_sample.shape
    past_len = page_table.shape[1] * PAGE_SIZE

    tm_p = PROMPT_ROW_TILE
    conv0_p = jnp.zeros((bp, D_CONV - 1, CONV_DIM), F32)
    ssm0_p = jnp.zeros((bp, N_SSM_HEADS, SSM_HEAD_DIM, D_STATE), F32)
    x2_p, conv_p, ssm_p = _ssd_layer_long(x_prompt, conv0_p, ssm0_p, w, tm_p)
    cos_p, sin_p = _rope_tables(jnp.arange(tp, dtype=jnp.int32))
    kt_p, vt32_p, qt_p, kh_p, vt_p, km_p = _kvq_prompt(x2_p, bp, tp, w["g_kv"], w["g_mix1"],
                                                       w["wkv"], w["wq"], cos_p, sin_p)
    k_p = kt_p.reshape(bp, N_KV_HEADS, HEAD_DIM, tp).transpose(0, 3, 1, 2)
    v_p = vt32_p.reshape(bp, N_KV_HEADS, HEAD_DIM, tp).transpose(0, 3, 1, 2)
    attn_p = _moba_prompt(qt_p, kh_p, vt_p, km_p.reshape(bp, tp // MOBA_BLOCK, KV_DIM))
    y_p = _attn_out_ffn(attn_p.reshape(bp * tp, D_MODEL), x2_p, w["wo"], w["g_ffn1"], w["wg1"],
                        w["wu1"], w["wd1"], w["g_fin"], tm_p)

    tm_s = bs * ts
    x2_s, conv_s, ssm_s = _ssd_layer(x_sample, state_conv[0], state_ssm[0], w, tm_s, SUBLANES, ts)
    pos_s = past_len + jnp.tile(jnp.arange(ts, dtype=jnp.int32), bs)
    cos_s, sin_s = _rope_tables(pos_s)
    k_s, v_s, q_s = _kvq(x2_s, w["g_kv"], w["g_mix1"], w["wkv"], w["wq"], cos_s, sin_s, tm_s)
    q_rows = q_s.reshape(bs, ts, N_HEADS, HEAD_DIM).transpose(0, 2, 1, 3).reshape(
        bs, N_HEADS * ts, HEAD_DIM)
    q_rows = jnp.pad(q_rows, ((0, 0), (0, LANES - N_HEADS * ts), (0, 0)))
    attn_rows = _moba_sample(page_table, q_rows, k_s.reshape(bs, ts, KV_DIM),
                             v_s.reshape(bs, ts, KV_DIM),
                             cache_k.transpose(0, 2, 3, 1).reshape(-1, KV_DIM, PAGE_SIZE),
                             cache_v.transpose(0, 2, 3, 1).reshape(-1, KV_DIM, PAGE_SIZE))
    attn_s = attn_rows[:, :N_HEADS * ts].reshape(bs, N_HEADS, ts, HEAD_DIM).transpose(
        0, 2, 1, 3).reshape(bs * ts, D_MODEL).astype(BF16)
    y_s = _attn_out_ffn(attn_s, x2_s, w["wo"], w["g_ffn1"], w["wg1"], w["wu1"], w["wd1"],
                        w["g_fin"], tm_s)

    return (y_p.reshape(bp, tp, D_MODEL), y_s.reshape(bs, ts, D_MODEL),
            conv_p[None], ssm_p[None],
            k_p, v_p,
            conv_s[None], ssm_s[None],
            k_s.reshape(bs, ts, N_KV_HEADS, HEAD_DIM), v_s.reshape(bs, ts, N_KV_HEADS, HEAD_DIM))
```

```python
import functools

import jax
import jax.numpy as jnp
import numpy as np
from jax import lax
from jax.experimental import pallas as pl
from jax.experimental.pallas import tpu as pltpu

F32 = jnp.float32
BF16 = jnp.bfloat16

D_MODEL = 1024
D_INNER = 2048
SSM_HEAD_DIM = 64
N_SSM_HEADS = 32
N_SSM_GROUPS = 4
HEADS_PER_GROUP = 8
D_STATE = 128
D_CONV = 4
SSD_CHUNK = 128
GN = N_SSM_GROUPS * D_STATE
CONV_DIM = D_INNER + 2 * GN
HEAD_DIM = 64
N_HEADS = 16
N_KV_HEADS = 4
Q_PER_KV = 4
KV_DIM = N_KV_HEADS * HEAD_DIM
MOBA_BLOCK = 256
MOBA_TOP_K = 3
ROPE_THETA = 10000.0
D_FF = 2816
EPS = 1e-6
PAGE_SIZE = 128

LANES = 128
SUBLANES = 8
VMEM_LIMIT = 56 * 1024 * 1024

NT_DIMS = (((1,), (1,)), ((), ()))
TN_DIMS = (((0,), (0,)), ((), ()))
NEG_INF = float("-inf")
MASKED = -1e30
LOG2_E = 1.4426950408889634
V_AUG = HEAD_DIM + 16
PROMPT_ROW_TILE = 512
FFN_CHUNK = 256
SSD_CHUNKS_PER_STEP = 2
ATTN_UNIT_HEADS = 4


def _params(sem):
    return pltpu.CompilerParams(dimension_semantics=sem, vmem_limit_bytes=VMEM_LIMIT)


def _const_spec(shape):
    nd = len(shape)
    return pl.BlockSpec(shape, lambda *_: (0,) * nd, pipeline_mode=pl.Buffered(1))


def _layer_spec(shape, layer):
    nd = len(shape)
    return pl.BlockSpec((1,) + tuple(shape[1:]), lambda *_: (layer,) + (0,) * (nd - 1),
                        pipeline_mode=pl.Buffered(1))


def _silu(x):
    return x * (1.0 / (1.0 + jnp.exp(-x)))


def _inv_rms(x):
    return lax.rsqrt(jnp.mean(x * x, axis=-1, keepdims=True) + EPS)


def _split3(x):
    hi = x.astype(BF16)
    r = x - hi.astype(F32)
    mid = r.astype(BF16)
    lo = (r - mid.astype(F32)).astype(BF16)
    return hi, mid, lo


def _dot01(a01, x, dims, a_is_lhs=True):
    out = None
    for p in _split3(x):
        t = (lax.dot_general(a01, p, dims, preferred_element_type=F32) if a_is_lhs
             else lax.dot_general(p, a01, dims, preferred_element_type=F32))
        out = t if out is None else out + t
    return out


def _top3_mask(gate, valid, axis=1):
    nb = gate.shape[axis]
    col = lax.broadcasted_iota(jnp.int32, gate.shape, axis).astype(F32)
    g = jnp.where(valid, gate, NEG_INF)
    sel = jnp.zeros(gate.shape, F32)
    for _ in range(MOBA_TOP_K):
        m = jnp.max(g, axis=axis, keepdims=True)
        idx = jnp.min(jnp.where(g == m, col, float(nb)), axis=axis, keepdims=True)
        pick = col == idx
        sel = jnp.where(pick, 1.0, sel)
        g = jnp.where(pick, NEG_INF, g)
    return jnp.logical_and(sel > 0.5, valid)


def _inproj_body(x_ref, g_ref, w_ref, z_ref, xbc_ref, dt_ref):
    x = x_ref[...]
    h = (x * _inv_rms(x) * g_ref[...]).astype(BF16)
    z_ref[...] = jnp.dot(h, w_ref[0, :, :D_INNER], preferred_element_type=F32)
    xbc_ref[...] = jnp.dot(h, w_ref[0, :, D_INNER:D_INNER + CONV_DIM], preferred_element_type=F32)
    dt = jnp.dot(h, w_ref[0, :, D_INNER + CONV_DIM:], preferred_element_type=F32)
    dt_ref[...] = jnp.concatenate(
        [dt, jnp.zeros((dt.shape[0], LANES - N_SSM_HEADS), F32)], axis=1)


def _inproj(x, g, w_in, layer, tm):
    m = x.shape[0]
    row = lambda w: pl.BlockSpec((tm, w), lambda i: (i, 0))
    return pl.pallas_call(
        _inproj_body,
        grid=(m // tm,),
        in_specs=[row(D_MODEL), _const_spec((1, D_MODEL)), _layer_spec(w_in.shape, layer)],
        out_specs=[row(D_INNER), row(CONV_DIM), row(LANES)],
        out_shape=[jax.ShapeDtypeStruct((m, D_INNER), F32),
                   jax.ShapeDtypeStruct((m, CONV_DIM), F32),
                   jax.ShapeDtypeStruct((m, LANES), F32)],
        compiler_params=_params(("parallel",)),
        name="ssd_inproj",
    )(x, g, w_in)


def _causal_conv(pad_ref, xbc, cw_ref, cb_ref, rows):
    pad_ref[SUBLANES:SUBLANES + rows, :] = xbc
    conv = cb_ref[...]
    for i in range(D_CONV):
        off = SUBLANES - (D_CONV - 1) + i
        conv = conv + pad_ref[off:off + rows, :] * cw_ref[i:i + 1, :]
    pad_ref[0:SUBLANES, :] = pad_ref[rows:rows + SUBLANES, :]
    return conv


def _ssd_body(xbc_ref, dtr_ref, conv0_ref, ssm0_ref, cw_ref, cb_ref, dtb_ref, alog_ref, dsk_ref,
              y_ref, st_ref, pad_ref, *, L, valid):
    c = pl.program_id(1)

    @pl.when(c == 0)
    def _():
        pad_ref[0:SUBLANES, :] = conv0_ref[0]
        st_ref[0] = ssm0_ref[0]

    act = _silu(_causal_conv(pad_ref, xbc_ref[0], cw_ref, cb_ref, L))
    xs = act[:, :D_INNER]
    bm = act[:, D_INNER:D_INNER + GN].astype(BF16)
    cm = act[:, D_INNER + GN:].astype(BF16)

    t = dtr_ref[0] + dtb_ref[...]
    dt = jnp.maximum(t, 0.0) + jnp.log1p(jnp.exp(-jnp.abs(t)))
    if valid < L:
        rows = lax.broadcasted_iota(jnp.int32, dt.shape, 0)
        dt = jnp.where(rows < valid, dt, 0.0)
    a = -jnp.exp(alog_ref[...])
    dta = dt * a

    r_i = lax.broadcasted_iota(jnp.int32, (L, L), 0)
    c_i = lax.broadcasted_iota(jnp.int32, (L, L), 1)
    causal = r_i >= c_i
    tril = jnp.where(causal, 1.0, 0.0).astype(BF16)
    e_r = lax.broadcasted_iota(jnp.int32, (LANES, LANES), 0)
    e_c = lax.broadcasted_iota(jnp.int32, (LANES, LANES), 1)
    eye = jnp.where(e_r == e_c, 1.0, 0.0).astype(BF16)

    acs = _dot01(tril, dta, (((1,), (0,)), ((), ())))
    acs_t = _dot01(eye, acs, NT_DIMS)
    last = acs[L - 1:L, :]
    dec_end = jnp.exp(last - acs)
    e_acs = jnp.exp(acs)
    dec_chunk = jnp.exp(last)

    for g in range(N_SSM_GROUPS):
        bg = bm[:, g * D_STATE:(g + 1) * D_STATE]
        cg = cm[:, g * D_STATE:(g + 1) * D_STATE]
        cb = lax.dot_general(cg, bg, NT_DIMS, preferred_element_type=F32)
        for jp in range(HEADS_PER_GROUP // 2):
            pair = []
            for h in (g * HEADS_PER_GROUP + 2 * jp, g * HEADS_PER_GROUP + 2 * jp + 1):
                diff = acs[:, h:h + 1] - acs_t[h:h + 1, :]
                seg = jnp.exp(jnp.where(causal, diff, NEG_INF))
                w = (cb * seg).astype(BF16)
                xs_h = xs[:, h * SSM_HEAD_DIM:(h + 1) * SSM_HEAD_DIM]
                xd_h = xs_h * dt[:, h:h + 1]
                y_diag = jnp.dot(w, xd_h.astype(BF16), preferred_element_type=F32)
                st_h = st_ref[0, h]
                y_off = lax.dot_general(cg, st_h.astype(BF16), NT_DIMS,
                                        preferred_element_type=F32) * e_acs[:, h:h + 1]
                xdd = (xd_h * dec_end[:, h:h + 1]).astype(BF16)
                cs = lax.dot_general(xdd, bg, TN_DIMS, preferred_element_type=F32)
                st_ref[0, h] = st_h * dec_chunk[:, h:h + 1] + cs
                pair.append(y_diag + y_off
                            + dsk_ref[:, h * SSM_HEAD_DIM:(h + 1) * SSM_HEAD_DIM] * xs_h)
            lo = (g * HEADS_PER_GROUP + 2 * jp) * SSM_HEAD_DIM
            y_ref[0, :, lo:lo + 2 * SSM_HEAD_DIM] = jnp.concatenate(pair, axis=1)


def _ssd_t_body(xbc_ref, dtr_ref, *refs, L, n_sub, has_init):
    if has_init:
        conv0_ref, ssm0_ref = refs[:2]
        refs = refs[2:]
    cw_ref, cb_ref, dtb_ref, alog_ref, dsk_ref, y_ref, st_ref, pad_ref, yt_ref = refs

    @pl.when(pl.program_id(1) == 0)
    def _():
        if has_init:
            pad_ref[0:SUBLANES, :] = conv0_ref[0]
            st_ref[0] = ssm0_ref[0]
        else:
            pad_ref[0:SUBLANES, :] = jnp.zeros((SUBLANES, CONV_DIM), F32)
            st_ref[0] = jnp.zeros(st_ref.shape[1:], F32)

    def chunk(ci, carry):
        rows = pl.ds(pl.multiple_of(ci * L, L), L)
        y_ref[0, rows, :] = _ssd_t_chunk(xbc_ref[0, rows, :], dtr_ref[0, rows, :], cw_ref, cb_ref,
                                         dtb_ref, alog_ref, dsk_ref, st_ref, pad_ref, yt_ref, L)
        return carry

    lax.fori_loop(0, n_sub, chunk, 0)


def _ssd_t_chunk(xbc, dtr, cw_ref, cb_ref, dtb_ref, alog_ref, dsk_ref, st_ref, pad_ref, yt_ref, L):
    hd = SSM_HEAD_DIM
    gw = HEADS_PER_GROUP * hd
    act = _silu(_causal_conv(pad_ref, xbc, cw_ref, cb_ref, L))
    xs_t = act[:, :D_INNER].T
    bm = act[:, D_INNER:D_INNER + GN].astype(BF16)
    cm = act[:, D_INNER + GN:].astype(BF16)

    t = dtr + dtb_ref[...]
    dt = jnp.maximum(t, 0.0) + jnp.log1p(jnp.exp(-jnp.abs(t)))
    dta = dt * (-jnp.exp(alog_ref[...]))

    r_i = lax.broadcasted_iota(jnp.int32, (L, L), 0)
    c_i = lax.broadcasted_iota(jnp.int32, (L, L), 1)
    tril = jnp.where(r_i >= c_i, 1.0, 0.0).astype(BF16)
    causal_t = c_i >= r_i

    acs = _dot01(tril, dta, (((1,), (0,)), ((), ())))
    acs_t = acs.T
    dt_t = dt.T
    dec_end_t = jnp.exp(acs_t[:, L - 1:L] - acs_t)
    e_acs_t = jnp.exp(acs_t)
    dec_chunk = jnp.exp(acs[L - 1:L, :])
    scale_in = dt_t
    scale_st = dt_t * dec_end_t

    for g in range(N_SSM_GROUPS):
        bg = bm[:, g * D_STATE:(g + 1) * D_STATE]
        cg = cm[:, g * D_STATE:(g + 1) * D_STATE]
        cb_t = lax.dot_general(bg, cg, NT_DIMS, preferred_element_type=F32)
        h0 = g * HEADS_PER_GROUP
        xs_g = xs_t[g * gw:(g + 1) * gw, :]
        xd_parts, xdd_parts = [], []
        for j in range(HEADS_PER_GROUP):
            h = h0 + j
            xd = xs_g[j * hd:(j + 1) * hd, :] * scale_in[h:h + 1, :]
            xd_parts.append(xd.astype(BF16))
            xdd_parts.append((xs_g[j * hd:(j + 1) * hd, :] * scale_st[h:h + 1, :]).astype(BF16))
        st_g = st_ref[0, h0:h0 + HEADS_PER_GROUP].reshape(gw, D_STATE)
        y_off = lax.dot_general(st_g.astype(BF16), cg, NT_DIMS,
                                preferred_element_type=F32)
        cs = jnp.dot(jnp.concatenate(xdd_parts, axis=0), bg,
                     preferred_element_type=F32)
        for j in range(HEADS_PER_GROUP):
            h = h0 + j
            diff = acs_t[h:h + 1, :] - acs[:, h:h + 1]
            seg_t = jnp.exp(jnp.where(causal_t, diff, NEG_INF))
            w_t = (cb_t * seg_t).astype(BF16)
            y_diag = jnp.dot(xd_parts[j], w_t, preferred_element_type=F32)
            rows = slice(j * hd, (j + 1) * hd)
            yt_ref[h * hd:(h + 1) * hd, :] = (
                y_diag + y_off[rows, :] * e_acs_t[h:h + 1, :]
                + dsk_ref[h * hd:(h + 1) * hd, :] * xs_g[rows, :])
            st_ref[0, h] = st_g[rows, :] * dec_chunk[:, h:h + 1] + cs[rows, :]
    return yt_ref[...].T


def _ssd(xbc, dtr, conv0, ssm0, cw, cb, dtb, alog, dsk, L, valid):
    nb, t = xbc.shape[0], xbc.shape[1]
    nc = t // L
    st_spec = pl.BlockSpec((1, N_SSM_HEADS, SSM_HEAD_DIM, D_STATE), lambda b, c: (b, 0, 0, 0))
    return pl.pallas_call(
        functools.partial(_ssd_body, L=L, valid=valid),
        grid=(nb, nc),
        in_specs=[pl.BlockSpec((1, L, CONV_DIM), lambda b, c: (b, c, 0)),
                  pl.BlockSpec((1, L, LANES), lambda b, c: (b, c, 0)),
                  pl.BlockSpec((1, SUBLANES, CONV_DIM), lambda b, c: (b, 0, 0)),
                  st_spec,
                  _const_spec(cw.shape), _const_spec(cb.shape), _const_spec(dtb.shape),
                  _const_spec(alog.shape), _const_spec(dsk.shape)],
        out_specs=[pl.BlockSpec((1, L, D_INNER), lambda b, c: (b, c, 0)), st_spec],
        out_shape=[jax.ShapeDtypeStruct((nb, t, D_INNER), F32),
                   jax.ShapeDtypeStruct(ssm0.shape, F32)],
        scratch_shapes=[pltpu.VMEM((L + SUBLANES, CONV_DIM), F32)],
        compiler_params=_params(("parallel", "arbitrary")),
        name="ssd_scan",
    )(xbc, dtr, conv0, ssm0, cw, cb, dtb, alog, dsk)


def _ssd_t(xbc, dtr, conv0, ssm0, cw, cb, dtb, alog, dsk):
    nb, t = xbc.shape[0], xbc.shape[1]
    L = SSD_CHUNK
    n_sub = SSD_CHUNKS_PER_STEP if t % (L * SSD_CHUNKS_PER_STEP) == 0 else 1
    rows = L * n_sub
    assert L == LANES and t % rows == 0
    dsk_t = jnp.broadcast_to(dsk.reshape(D_INNER, 1), (D_INNER, LANES))
    st_spec = pl.BlockSpec((1, N_SSM_HEADS, SSM_HEAD_DIM, D_STATE), lambda b, c: (b, 0, 0, 0))
    has_init = conv0 is not None
    init_args = (conv0, ssm0) if has_init else ()
    init_specs = [pl.BlockSpec((1, SUBLANES, CONV_DIM), lambda b, c: (b, 0, 0)),
                  st_spec] if has_init else []
    return pl.pallas_call(
        functools.partial(_ssd_t_body, L=L, n_sub=n_sub, has_init=has_init),
        grid=(nb, t // rows),
        in_specs=[pl.BlockSpec((1, rows, CONV_DIM), lambda b, c: (b, c, 0)),
                  pl.BlockSpec((1, rows, LANES), lambda b, c: (b, c, 0))] + init_specs + [
                  _const_spec(cw.shape), _const_spec(cb.shape),
                  _const_spec(dtb.shape), _const_spec(alog.shape), _const_spec(dsk_t.shape)],
        out_specs=[pl.BlockSpec((1, rows, D_INNER), lambda b, c: (b, c, 0)), st_spec],
        out_shape=[jax.ShapeDtypeStruct((nb, t, D_INNER), F32),
                   jax.ShapeDtypeStruct((nb, N_SSM_HEADS, SSM_HEAD_DIM, D_STATE), F32)],
        scratch_shapes=[pltpu.VMEM((L + SUBLANES, CONV_DIM), F32), pltpu.VMEM((D_INNER, L), F32)],
        compiler_params=_params(("parallel", "arbitrary")),
        name="ssd_scan_t",
    )(xbc, dtr, *init_args, cw, cb, dtb, alog, dsk_t)


def _ffn(x1, nf_ref, wgu_ref, wd_ref):
    h = (x1 * _inv_rms(x1) * nf_ref[...]).astype(BF16)
    out = x1
    for c in range(D_FF // FFN_CHUNK):
        lo = c * FFN_CHUNK
        gate = jnp.dot(h, wgu_ref[0, :, lo:lo + FFN_CHUNK], preferred_element_type=F32)
        up = jnp.dot(h, wgu_ref[0, :, D_FF + lo:D_FF + lo + FFN_CHUNK], preferred_element_type=F32)
        act = (_silu(gate) * up).astype(BF16)
        out = out + jnp.dot(act, wd_ref[0, lo:lo + FFN_CHUNK, :], preferred_element_type=F32)
    return out


def _ssm_out_ffn_body(y_ref, z_ref, x_ref, gn_ref, wo_ref, nf_ref, wgu_ref, wd_ref, o_ref):
    y = y_ref[...] * _silu(z_ref[...])
    gw = D_INNER // N_SSM_GROUPS
    parts = []
    for g in range(N_SSM_GROUPS):
        yg = y[:, g * gw:(g + 1) * gw]
        parts.append(yg * _inv_rms(yg))
    y = (jnp.concatenate(parts, axis=1) * gn_ref[...]).astype(BF16)
    x1 = x_ref[...] + jnp.dot(y, wo_ref[...], preferred_element_type=F32)
    o_ref[...] = _ffn(x1, nf_ref, wgu_ref, wd_ref)


def _attn_out_ffn_body(a_ref, x_ref, wo_ref, nf_ref, wgu_ref, wd_ref, nfin_ref, o_ref):
    x1 = x_ref[...] + jnp.dot(a_ref[...], wo_ref[...], preferred_element_type=F32)
    x2 = _ffn(x1, nf_ref, wgu_ref, wd_ref)
    o_ref[...] = x2 * _inv_rms(x2) * nfin_ref[...]


def _ssm_out_ffn(y, z, x, gn, wo, nf, wgu, wd, layer, tm):
    m = x.shape[0]
    row = lambda w: pl.BlockSpec((tm, w), lambda i: (i, 0))
    return pl.pallas_call(
        _ssm_out_ffn_body,
        grid=(m // tm,),
        in_specs=[row(D_INNER), row(D_INNER), row(D_MODEL), _const_spec(gn.shape),
                  _const_spec(wo.shape), _const_spec(nf.shape), _layer_spec(wgu.shape, layer),
                  _layer_spec(wd.shape, layer)],
        out_specs=row(D_MODEL),
        out_shape=jax.ShapeDtypeStruct((m, D_MODEL), F32),
        compiler_params=_params(("parallel",)),
        name="ssm_out_ffn",
    )(y, z, x, gn, wo, nf, wgu, wd)


def _attn_out_ffn(a, x, wo, nf, wgu, wd, layer, nfin, tm):
    m = x.shape[0]
    row = lambda w: pl.BlockSpec((tm, w), lambda i: (i, 0))
    return pl.pallas_call(
        _attn_out_ffn_body,
        grid=(m // tm,),
        in_specs=[row(D_MODEL), row(D_MODEL), _const_spec(wo.shape), _const_spec(nf.shape),
                  _layer_spec(wgu.shape, layer), _layer_spec(wd.shape, layer),
                  _const_spec(nfin.shape)],
        out_specs=row(D_MODEL),
        out_shape=jax.ShapeDtypeStruct((m, D_MODEL), F32),
        compiler_params=_params(("parallel",)),
        name="attn_out_ffn",
    )(a, x, wo, nf, wgu, wd, nfin)


def _rope(x, cos, sin_signed, first_half):
    outs = []
    for c in range(x.shape[1] // LANES):
        xc = x[:, c * LANES:(c + 1) * LANES]
        partner = jnp.where(first_half, pltpu.roll(xc, LANES - HEAD_DIM // 2, 1),
                            pltpu.roll(xc, HEAD_DIM // 2, 1))
        outs.append(xc * cos + partner * sin_signed)
    return jnp.concatenate(outs, axis=1)


def _kvq_compute(x_ref, gkv_ref, gq_ref, wkv_ref, wq_ref, cos_ref, sin_ref):
    x = x_ref[...]
    xn = x * _inv_rms(x)
    hk = (xn * gkv_ref[...]).astype(BF16)
    hq = (xn * gq_ref[...]).astype(BF16)
    cos = cos_ref[...]
    sin = sin_ref[...]
    lane = lax.broadcasted_iota(jnp.int32, cos.shape, 1)
    first_half = (lane % HEAD_DIM) < (HEAD_DIM // 2)
    kv = jnp.dot(hk, wkv_ref[...], preferred_element_type=F32)
    k = _rope(kv[:, :KV_DIM], cos, sin, first_half)
    q = _rope(jnp.dot(hq, wq_ref[...], preferred_element_type=F32), cos, sin, first_half)
    return k, kv[:, KV_DIM:], q


def _kvq_body(x_ref, gkv_ref, gq_ref, wkv_ref, wq_ref, cos_ref, sin_ref, k_ref, v_ref, q_ref):
    k, v, q = _kvq_compute(x_ref, gkv_ref, gq_ref, wkv_ref, wq_ref, cos_ref, sin_ref)
    k_ref[...] = k
    v_ref[...] = v
    q_ref[...] = q


def _kvq_prompt_body(x_ref, gkv_ref, gq_ref, wkv_ref, wq_ref, cos_ref, sin_ref,
                     kt_ref, vt32_ref, qt_ref, kh_ref, vt_ref, km_ref):
    k, v, q = _kvq_compute(x_ref, gkv_ref, gq_ref, wkv_ref, wq_ref, cos_ref, sin_ref)
    kt_ref[0] = k.T
    v_t = v.T
    vt32_ref[0] = v_t
    qt_ref[0] = q.T
    vt = v_t.astype(BF16)
    blk = pl.program_id(1)
    col = lax.broadcasted_iota(jnp.int32, (MOBA_BLOCK, LANES - HEAD_DIM), 1)
    onehot = jnp.where(col == blk, 1.0, 0.0).astype(BF16)
    row = lax.broadcasted_iota(jnp.int32, (V_AUG - HEAD_DIM, MOBA_BLOCK), 0)
    ones_row = jnp.where(row == 0, 1.0, 0.0).astype(BF16)
    for kvh in range(N_KV_HEADS):
        kh_ref[0, 0, kvh] = jnp.concatenate(
            [k[:, kvh * HEAD_DIM:(kvh + 1) * HEAD_DIM].astype(BF16), onehot], axis=1)
        vt_ref[0, 0, kvh] = jnp.concatenate(
            [vt[kvh * HEAD_DIM:(kvh + 1) * HEAD_DIM, :], ones_row], axis=0)
    km_ref[0, 0] = jnp.sum(k, axis=0, keepdims=True) * (1.0 / MOBA_BLOCK)


def _kvq(x, gkv, gq, wkv, wq, cos, sin, tm):
    m = x.shape[0]
    row = lambda w: pl.BlockSpec((tm, w), lambda i: (i, 0))
    tab = pl.BlockSpec((tm, LANES), lambda i: (0, 0))
    return pl.pallas_call(
        _kvq_body,
        grid=(m // tm,),
        in_specs=[row(D_MODEL), _const_spec(gkv.shape), _const_spec(gq.shape),
                  _const_spec(wkv.shape), _const_spec(wq.shape), tab, tab],
        out_specs=[row(KV_DIM), row(KV_DIM), row(D_MODEL)],
        out_shape=[jax.ShapeDtypeStruct((m, KV_DIM), F32), jax.ShapeDtypeStruct((m, KV_DIM), F32),
                   jax.ShapeDtypeStruct((m, D_MODEL), F32)],
        compiler_params=_params(("parallel",)),
        name="kvq_proj",
    )(x, gkv, gq, wkv, wq, cos, sin)


def _kvq_prompt(x, b, t, gkv, gq, wkv, wq, cos, sin):
    nb = t // MOBA_BLOCK
    tm = MOBA_BLOCK
    row = lambda w: pl.BlockSpec((tm, w), lambda bi, i: (bi * nb + i, 0))
    tab = pl.BlockSpec((tm, LANES), lambda bi, i: (i, 0))
    return pl.pallas_call(
        _kvq_prompt_body,
        grid=(b, nb),
        in_specs=[row(D_MODEL), _const_spec(gkv.shape), _const_spec(gq.shape),
                  _const_spec(wkv.shape), _const_spec(wq.shape), tab, tab],
        out_specs=[pl.BlockSpec((1, KV_DIM, tm), lambda bi, i: (bi, 0, i)),
                   pl.BlockSpec((1, KV_DIM, tm), lambda bi, i: (bi, 0, i)),
                   pl.BlockSpec((1, D_MODEL, tm), lambda bi, i: (bi, 0, i)),
                   pl.BlockSpec((1, 1, N_KV_HEADS, tm, LANES), lambda bi, i: (bi, i, 0, 0, 0)),
                   pl.BlockSpec((1, 1, N_KV_HEADS, V_AUG, tm), lambda bi, i: (bi, i, 0, 0, 0)),
                   pl.BlockSpec((1, 1, 1, KV_DIM), lambda bi, i: (bi, i, 0, 0))],
        out_shape=[jax.ShapeDtypeStruct((b, KV_DIM, t), F32),
                   jax.ShapeDtypeStruct((b, KV_DIM, t), F32),
                   jax.ShapeDtypeStruct((b, D_MODEL, t), F32),
                   jax.ShapeDtypeStruct((b, nb, N_KV_HEADS, tm, LANES), BF16),
                   jax.ShapeDtypeStruct((b, nb, N_KV_HEADS, V_AUG, tm), BF16),
                   jax.ShapeDtypeStruct((b, nb, 1, KV_DIM), F32)],
        compiler_params=_params(("parallel", "parallel")),
        name="kvq_proj_prompt",
    )(x, gkv, gq, wkv, wq, cos, sin)


def _moba_prompt_body(qt_ref, kh_ref, vt_ref, km_ref, o_ref, s_a, s_b, acc_a, acc_b, *, nb):
    i = pl.program_id(1)
    g_per_unit = ATTN_UNIT_HEADS
    rows = g_per_unit * MOBA_BLOCK
    n_units = N_HEADS // g_per_unit
    scale = (HEAD_DIM ** -0.5) * LOG2_E
    n_pairs = (i + 1) // 2
    blk_row = lax.broadcasted_iota(jnp.int32, (nb, rows), 0)
    key_i = lax.broadcasted_iota(jnp.int32, (MOBA_BLOCK, rows), 0)
    tok_i = lax.broadcasted_iota(jnp.int32, (MOBA_BLOCK, rows), 1) % MOBA_BLOCK
    own_ok = key_i <= tok_i
    fold = lambda s: jnp.max(s.reshape(MOBA_BLOCK // SUBLANES, SUBLANES, rows), axis=0)

    def setup(u):
        kvh = (u * g_per_unit) // Q_PER_KV
        lo = kvh * HEAD_DIM
        base = u * g_per_unit * HEAD_DIM
        qt2 = qt_ref[0, base:base + g_per_unit * HEAD_DIM, :]
        qt = jnp.concatenate([qt2[g * HEAD_DIM:(g + 1) * HEAD_DIM, :] for g in range(g_per_unit)],
                             axis=1)
        km = km_ref[0, :, lo:lo + HEAD_DIM]
        gate = jnp.dot(km, qt, preferred_element_type=F32,
                       precision=lax.Precision.HIGHEST)
        sel = _top3_mask(gate, blk_row < i, axis=0)
        pen = jnp.where(sel, 0.0, MASKED).astype(BF16)
        qb = (qt * scale).astype(BF16)
        q_own = jnp.concatenate([qb, jnp.zeros((LANES - HEAD_DIM, rows), BF16)], axis=0)
        q_aug = jnp.concatenate([qb, pen, jnp.zeros((LANES - HEAD_DIM - nb, rows), BF16)], axis=0)
        s_own = jnp.dot(kh_ref[0, i, kvh], q_own, preferred_element_type=F32)
        s_own = jnp.where(own_ok, s_own, MASKED)
        s_bufs[u % 2][nb] = s_own
        return kvh, q_aug, fold(s_own)

    def scores_pair(u, kvh, q_aug, jj, m8):
        for t in range(2):
            j = 2 * jj + t
            s = jnp.dot(kh_ref[0, j, kvh], q_aug, preferred_element_type=F32)
            s_bufs[u % 2][j] = s
            m8 = jnp.maximum(m8, fold(s))
        return m8

    def weighted_pair(u, kvh, m, jj):
        j = 2 * jj
        s_buf = s_bufs[u % 2]
        p = jnp.concatenate([jnp.exp2(s_buf[j] - m).astype(BF16),
                             jnp.exp2(s_buf[j + 1] - m).astype(BF16)], axis=0)
        v2 = jnp.concatenate([vt_ref[0, j, kvh], vt_ref[0, j + 1, kvh]], axis=1)
        acc_bufs[u % 2][...] += jnp.dot(v2, p, preferred_element_type=F32)

    def over_pairs(step, carry):
        n_double = n_pairs // 2

        def double(q, c):
            return step(2 * q + 1, step(2 * q, c))

        carry = lax.fori_loop(0, n_double, double, carry)
        return lax.fori_loop(2 * n_double, n_pairs, step, carry)

    s_bufs = (s_a, s_b)
    acc_bufs = (acc_a, acc_b)
    kvh, q_aug, m8 = setup(0)
    m8 = over_pairs(functools.partial(scores_pair, 0, kvh, q_aug), m8)
    for u in range(n_units):
        m = jnp.max(m8, axis=0, keepdims=True)
        p_own = jnp.exp2(s_bufs[u % 2][nb] - m).astype(BF16)
        acc_bufs[u % 2][...] = jnp.dot(vt_ref[0, i, kvh], p_own, preferred_element_type=F32)
        if u + 1 < n_units:
            kvh_n, q_aug_n, m8_n = setup(u + 1)

            def both(jj, m8c, u=u, kvh=kvh, m=m, kvh_n=kvh_n, q_aug_n=q_aug_n):
                m8c = scores_pair(u + 1, kvh_n, q_aug_n, jj, m8c)
                weighted_pair(u, kvh, m, jj)
                return m8c

            m8_n = over_pairs(both, m8_n)
        else:
            def last(jj, carry, u=u, kvh=kvh, m=m):
                weighted_pair(u, kvh, m, jj)
                return carry

            over_pairs(last, 0)
        acc = acc_bufs[u % 2][...]
        out_t = acc[:HEAD_DIM, :] / acc[HEAD_DIM:HEAD_DIM + 1, :]
        out2 = jnp.concatenate([out_t[:, g * MOBA_BLOCK:(g + 1) * MOBA_BLOCK]
                                for g in range(g_per_unit)], axis=0)
        base = u * g_per_unit * HEAD_DIM
        o_ref[0, :, base:base + g_per_unit * HEAD_DIM] = out2.T.astype(o_ref.dtype)
        if u + 1 < n_units:
            kvh, m8 = kvh_n, m8_n


def _moba_prompt(qt, kh, vt, km):
    b, t = qt.shape[0], qt.shape[2]
    nb = t // MOBA_BLOCK
    rows = ATTN_UNIT_HEADS * MOBA_BLOCK
    return pl.pallas_call(
        functools.partial(_moba_prompt_body, nb=nb),
        grid=(b, nb),
        in_specs=[pl.BlockSpec((1, D_MODEL, MOBA_BLOCK), lambda bi, i: (bi, 0, i)),
                  pl.BlockSpec((1, nb, N_KV_HEADS, MOBA_BLOCK, LANES),
                               lambda bi, i: (bi, 0, 0, 0, 0), pipeline_mode=pl.Buffered(1)),
                  pl.BlockSpec((1, nb, N_KV_HEADS, V_AUG, MOBA_BLOCK),
                               lambda bi, i: (bi, 0, 0, 0, 0), pipeline_mode=pl.Buffered(1)),
                  pl.BlockSpec((1, nb, KV_DIM), lambda bi, i: (bi, 0, 0))],
        out_specs=pl.BlockSpec((1, MOBA_BLOCK, D_MODEL), lambda bi, i: (bi, i, 0)),
        out_shape=jax.ShapeDtypeStruct((b, t, D_MODEL), BF16),
        scratch_shapes=[pltpu.VMEM((nb + 1, MOBA_BLOCK, rows), F32),
                        pltpu.VMEM((nb + 1, MOBA_BLOCK, rows), F32),
                        pltpu.VMEM((V_AUG, rows), F32), pltpu.VMEM((V_AUG, rows), F32)],
        compiler_params=_params(("parallel", "arbitrary")),
        name="moba_prompt",
    )(qt, kh, vt, km)


def _page_copy(cache_hbm, buf, sem, page, slot, p):
    return pltpu.make_async_copy(cache_hbm.at[page], buf.at[slot, p], sem)


def _moba_sample_body(pt_ref, q_ref, kn_ref, vn_ref, ck_hbm, cv_hbm, o_ref,
                      kbuf, vbuf, s_scr, new_scr, sems, *, n_pages, n_new):
    b = pl.program_id(0)
    slot = b % 2
    rows = q_ref.shape[1]
    ppb = MOBA_BLOCK // PAGE_SIZE
    nb = n_pages // ppb
    scale = HEAD_DIM ** -0.5

    def fetch(seq, to_slot):
        def start_page(p, carry):
            page = pt_ref[seq, p]
            _page_copy(ck_hbm, kbuf, sems.at[0, to_slot], page, to_slot, p).start()
            _page_copy(cv_hbm, vbuf, sems.at[1, to_slot], page, to_slot, p).start()
            return carry
        lax.fori_loop(0, n_pages, start_page, 0)

    @pl.when(b == 0)
    def _():
        fetch(0, 0)

    @pl.when(b + 1 < pl.num_programs(0))
    def _():
        fetch(b + 1, 1 - slot)

    q64 = q_ref[0]
    qt = jnp.concatenate([q64] * N_KV_HEADS, axis=1)
    r_i = lax.broadcasted_iota(jnp.int32, qt.shape, 0)
    c_i = lax.broadcasted_iota(jnp.int32, qt.shape, 1)
    qpad = jnp.where(c_i // HEAD_DIM == r_i // (Q_PER_KV * n_new), qt, 0.0)
    qb = (qpad * scale).astype(BF16)

    def wait_k(p, carry):
        _page_copy(ck_hbm, kbuf, sems.at[0, slot], 0, slot, p).wait()
        return carry

    lax.fori_loop(0, n_pages, wait_k, 0)

    blk_lane = lax.broadcasted_iota(jnp.int32, (KV_DIM, LANES), 1)
    km_t = jnp.zeros((KV_DIM, LANES), F32)
    for j in range(nb):
        kt = jnp.concatenate([kbuf[slot, ppb * j + u] for u in range(ppb)], axis=1)
        km_col = jnp.sum(kt, axis=1, keepdims=True) * (1.0 / MOBA_BLOCK)
        km_t = jnp.where(blk_lane == j, km_col, km_t)
        s_scr[:, j * MOBA_BLOCK:(j + 1) * MOBA_BLOCK] = jnp.dot(
            qb, kt.astype(BF16), preferred_element_type=F32)

    gate = jnp.dot(qpad, km_t, preferred_element_type=F32,
                   precision=lax.Precision.HIGHEST)
    sel = _top3_mask(gate, lax.broadcasted_iota(jnp.int32, gate.shape, 1) < nb)
    pen = jnp.where(sel, 0.0, NEG_INF)

    new_scr[...] = jnp.zeros(new_scr.shape, F32)
    new_scr[0:n_new, :] = kn_ref[0]
    s_new = lax.dot_general(qb, new_scr[...].astype(BF16), NT_DIMS, preferred_element_type=F32)
    nr = lax.broadcasted_iota(jnp.int32, s_new.shape, 0)
    ncol = lax.broadcasted_iota(jnp.int32, s_new.shape, 1)
    s_new = jnp.where(ncol <= nr % n_new, s_new, NEG_INF)

    def lane_fold(x, op):
        out = x[:, :LANES]
        for c in range(1, x.shape[1] // LANES):
            out = op(out, x[:, c * LANES:(c + 1) * LANES])
        return out

    m_part = s_new
    for j in range(nb):
        sj = s_scr[:, j * MOBA_BLOCK:(j + 1) * MOBA_BLOCK] + pen[:, j:j + 1]
        m_part = jnp.maximum(m_part, lane_fold(sj, jnp.maximum))
    m = jnp.max(m_part, axis=1, keepdims=True)

    def wait_v(p, carry):
        _page_copy(cv_hbm, vbuf, sems.at[1, slot], 0, slot, p).wait()
        return carry

    lax.fori_loop(0, n_pages, wait_v, 0)

    p_new = jnp.exp(s_new - m)
    l_part = p_new
    new_scr[0:n_new, :] = vn_ref[0]
    acc = jnp.dot(p_new.astype(BF16), new_scr[...].astype(BF16), preferred_element_type=F32)
    for j in range(nb):
        pj = jnp.exp(s_scr[:, j * MOBA_BLOCK:(j + 1) * MOBA_BLOCK] + pen[:, j:j + 1] - m)
        l_part = l_part + lane_fold(pj, jnp.add)
        vt = jnp.concatenate([vbuf[slot, ppb * j + u] for u in range(ppb)], axis=1).astype(BF16)
        acc = acc + lax.dot_general(pj.astype(BF16), vt, NT_DIMS, preferred_element_type=F32)
    acc = acc / jnp.sum(l_part, axis=1, keepdims=True)
    ro = lax.broadcasted_iota(jnp.int32, (rows, HEAD_DIM), 0) // (Q_PER_KV * n_new)
    out = jnp.zeros((rows, HEAD_DIM), F32)
    for kvh in range(N_KV_HEADS):
        out = jnp.where(ro == kvh, acc[:, kvh * HEAD_DIM:(kvh + 1) * HEAD_DIM], out)
    o_ref[0] = out


def _moba_sample(page_table, q_rows, k_new, v_new, cache_k, cache_v):
    nseq, n_pages = page_table.shape
    rows = q_rows.shape[1]
    n_new = k_new.shape[1]
    past = n_pages * PAGE_SIZE
    grid_spec = pltpu.PrefetchScalarGridSpec(
        num_scalar_prefetch=1,
        grid=(nseq,),
        in_specs=[pl.BlockSpec((1, rows, HEAD_DIM), lambda b, pt: (b, 0, 0)),
                  pl.BlockSpec((1, n_new, KV_DIM), lambda b, pt: (b, 0, 0)),
                  pl.BlockSpec((1, n_new, KV_DIM), lambda b, pt: (b, 0, 0)),
                  pl.BlockSpec(memory_space=pl.ANY),
                  pl.BlockSpec(memory_space=pl.ANY)],
        out_specs=pl.BlockSpec((1, rows, HEAD_DIM), lambda b, pt: (b, 0, 0)),
        scratch_shapes=[pltpu.VMEM((2, n_pages, KV_DIM, PAGE_SIZE), F32),
                        pltpu.VMEM((2, n_pages, KV_DIM, PAGE_SIZE), F32),
                        pltpu.VMEM((rows, past), F32),
                        pltpu.VMEM((LANES, KV_DIM), F32),
                        pltpu.SemaphoreType.DMA((2, 2))],
    )
    return pl.pallas_call(
        functools.partial(_moba_sample_body, n_pages=n_pages, n_new=n_new),
        grid_spec=grid_spec,
        out_shape=jax.ShapeDtypeStruct((nseq, rows, HEAD_DIM), F32),
        compiler_params=_params(("arbitrary",)),
        name="moba_sample",
    )(page_table, q_rows, k_new, v_new, cache_k, cache_v)


def _rope_tables(pos):
    half = HEAD_DIM // 2
    inv = ROPE_THETA ** (-np.arange(half, dtype=np.float64) / half)
    ang = np.asarray(pos, dtype=np.float64)[:, None] * inv[None, :]
    cos, sin = np.cos(ang), np.sin(ang)
    cos_h = np.concatenate([cos, cos], axis=1)
    sin_h = np.concatenate([-sin, sin], axis=1)
    reps = LANES // HEAD_DIM
    return (jnp.asarray(np.tile(cos_h, (1, reps)), F32), jnp.asarray(np.tile(sin_h, (1, reps)), F32))


def _prep_weights(norm_mix, norm_ffn, w_in_ssm, conv_w, conv_b, dt_bias, a_log, d_skip, norm_ssm,
                  w_out_ssm, norm_kv, w_kv, w_q, w_o, w_gu, w_down, norm_final):
    pad_h = LANES - N_SSM_HEADS
    return dict(
        g_mix0=norm_mix[0][None], g_mix1=norm_mix[1][None],
        g_ffn0=norm_ffn[0][None], g_ffn1=norm_ffn[1][None],
        w_in=w_in_ssm.astype(BF16),
        cw=conv_w[0], cb=conv_b[0][None],
        dtb=jnp.pad(dt_bias[0], (0, pad_h))[None], alog=jnp.pad(a_log[0], (0, pad_h))[None],
        dsk=jnp.repeat(d_skip[0], SSM_HEAD_DIM)[None],
        gn=norm_ssm[0][None], wo_ssm=w_out_ssm[0].astype(BF16),
        g_kv=norm_kv[None], wkv=w_kv.astype(BF16), wq=w_q[0].astype(BF16),
        wo=w_o[0].astype(BF16),
        wgu=w_gu.astype(BF16), wd=w_down.astype(BF16),
        g_fin=norm_final[None],
    )


def _ssd_layer_long(x, conv_in, ssm0, w, tm):
    b, t, _ = x.shape
    keep = D_CONV - 1
    assert t >= keep
    xf = x.reshape(b * t, D_MODEL)
    z, xbc, dtr = _inproj(xf, w["g_mix0"], w["w_in"], 0, tm)
    xbc3 = xbc.reshape(b, t, CONV_DIM)
    conv0 = None if conv_in is None else jnp.pad(conv_in, ((0, 0), (SUBLANES - keep, 0), (0, 0)))
    y, ssm_new = _ssd_t(xbc3, dtr.reshape(b, t, LANES), conv0, ssm0, w["cw"], w["cb"], w["dtb"],
                        w["alog"], w["dsk"])
    x2 = _ssm_out_ffn(y.reshape(b * t, D_INNER), z, xf, w["gn"], w["wo_ssm"], w["g_ffn0"],
                      w["wgu"], w["wd"], 0, tm)
    return x2, xbc3[:, t - keep:], ssm_new


def _ssd_layer(x, conv_in, ssm0, w, tm, L, valid):
    b, t, _ = x.shape
    xf = x.reshape(b * t, D_MODEL)
    z, xbc, dtr = _inproj(xf, w["g_mix0"], w["w_in"], 0, tm)
    tp = -(-t // L) * L
    xbc3 = xbc.reshape(b, t, CONV_DIM)
    dtr3 = dtr.reshape(b, t, LANES)
    if tp != t:
        xbc3 = jnp.pad(xbc3, ((0, 0), (0, tp - t), (0, 0)))
        dtr3 = jnp.pad(dtr3, ((0, 0), (0, tp - t), (0, 0)))
    conv0 = jnp.pad(conv_in, ((0, 0), (SUBLANES - (D_CONV - 1), 0), (0, 0)))
    y, ssm_new = _ssd(xbc3, dtr3, conv0, ssm0, w["cw"], w["cb"], w["dtb"], w["alog"], w["dsk"],
                      L, valid)
    y = y[:, :t].reshape(b * t, D_INNER)
    x2 = _ssm_out_ffn(y, z, xf, w["gn"], w["wo_ssm"], w["g_ffn0"], w["wgu"], w["wd"], 0, tm)
    keep = D_CONV - 1
    conv_new = jnp.concatenate([conv_in[:, t:], xbc.reshape(b, t, CONV_DIM)[:, max(0, t - keep):]],
                               axis=1)
    return x2, conv_new, ssm_new


def kernel(x_prompt, x_sample, state_conv, state_ssm, cache_k, cache_v, page_table, norm_mix,
           norm_ffn, w_in_ssm, conv_w, conv_b, dt_bias, a_log, d_skip, norm_ssm, w_out_ssm,
           norm_kv, w_kv, w_q, w_o, w_gu, w_down, norm_final):
    w = _prep_weights(norm_mix, norm_ffn, w_in_ssm, conv_w, conv_b, dt_bias, a_log, d_skip,
                      norm_ssm, w_out_ssm, norm_kv, w_kv, w_q, w_o, w_gu, w_down, norm_final)
    bp, tp, _ = x_prompt.shape
    bs, ts, _ = x_sample.shape
    past_len = page_table.shape[1] * PAGE_SIZE

    tm_p = PROMPT_ROW_TILE
    x2_p, conv_p, ssm_p = _ssd_layer_long(x_prompt, None, None, w, tm_p)
    cos_p, sin_p = _rope_tables(np.arange(tp))
    kt_p, vt32_p, qt_p, kh_p, vt_p, km_p = _kvq_prompt(x2_p, bp, tp, w["g_kv"], w["g_mix1"],
                                                       w["wkv"], w["wq"], cos_p, sin_p)
    k_p = kt_p.reshape(bp, N_KV_HEADS, HEAD_DIM, tp).transpose(0, 3, 1, 2)
    v_p = vt32_p.reshape(bp, N_KV_HEADS, HEAD_DIM, tp).transpose(0, 3, 1, 2)
    attn_p = _moba_prompt(qt_p, kh_p, vt_p, km_p.reshape(bp, tp // MOBA_BLOCK, KV_DIM))
    y_p = _attn_out_ffn(attn_p.reshape(bp * tp, D_MODEL), x2_p, w["wo"], w["g_ffn1"], w["wgu"],
                        w["wd"], 1, w["g_fin"], tm_p)

    tm_s = bs * ts
    x2_s, conv_s, ssm_s = _ssd_layer(x_sample, state_conv[0], state_ssm[0], w, tm_s, SUBLANES, ts)
    cos_s, sin_s = _rope_tables(past_len + np.tile(np.arange(ts), bs))
    k_s, v_s, q_s = _kvq(x2_s, w["g_kv"], w["g_mix1"], w["wkv"], w["wq"], cos_s, sin_s, tm_s)
    q_rows = q_s.reshape(bs, ts, N_HEADS, HEAD_DIM).transpose(0, 2, 1, 3).reshape(
        bs, N_HEADS * ts, HEAD_DIM)
    q_rows = jnp.pad(q_rows, ((0, 0), (0, LANES - N_HEADS * ts), (0, 0)))
    attn_rows = _moba_sample(page_table, q_rows, k_s.reshape(bs, ts, KV_DIM),
                             v_s.reshape(bs, ts, KV_DIM),
                             cache_k.transpose(0, 2, 3, 1).reshape(-1, KV_DIM, PAGE_SIZE),
                             cache_v.transpose(0, 2, 3, 1).reshape(-1, KV_DIM, PAGE_SIZE))
    attn_s = attn_rows[:, :N_HEADS * ts].reshape(bs, N_HEADS, ts, HEAD_DIM).transpose(
        0, 2, 1, 3).reshape(bs * ts, D_MODEL).astype(BF16)
    y_s = _attn_out_ffn(attn_s, x2_s, w["wo"], w["g_ffn1"], w["wgu"], w["wd"], 1,
                        w["g_fin"], tm_s)

    return (y_p.reshape(bp, tp, D_MODEL), y_s.reshape(bs, ts, D_MODEL),
            conv_p[None], ssm_p[None],
            k_p, v_p,
            conv_s[None], ssm_s[None],
            k_s.reshape(bs, ts, N_KV_HEADS, HEAD_DIM), v_s.reshape(bs, ts, N_KV_HEADS, HEAD_DIM))
```

```python
import functools

import jax
import jax.numpy as jnp
import numpy as np
from jax import lax
from jax.experimental import pallas as pl
from jax.experimental.pallas import tpu as pltpu

F32 = jnp.float32
BF16 = jnp.bfloat16

D_MODEL = 1024
D_INNER = 2048
SSM_HEAD_DIM = 64
N_SSM_HEADS = 32
N_SSM_GROUPS = 4
HEADS_PER_GROUP = 8
D_STATE = 128
D_CONV = 4
SSD_CHUNK = 128
GN = N_SSM_GROUPS * D_STATE
CONV_DIM = D_INNER + 2 * GN
HEAD_DIM = 64
N_HEADS = 16
N_KV_HEADS = 4
Q_PER_KV = 4
KV_DIM = N_KV_HEADS * HEAD_DIM
MOBA_BLOCK = 256
MOBA_TOP_K = 3
ROPE_THETA = 10000.0
D_FF = 2816
EPS = 1e-6
PAGE_SIZE = 128

LANES = 128
SUBLANES = 8
VMEM_LIMIT = 56 * 1024 * 1024

NT_DIMS = (((1,), (1,)), ((), ()))
TN_DIMS = (((0,), (0,)), ((), ()))
NEG_INF = float("-inf")
MASKED = -1e30
LOG2_E = 1.4426950408889634
V_AUG = HEAD_DIM + 16
PROMPT_ROW_TILE = 512
FFN_CHUNK = 256
SSD_CHUNKS_PER_STEP = 2
ATTN_UNIT_HEADS = 4


def _params(sem):
    return pltpu.CompilerParams(dimension_semantics=sem, vmem_limit_bytes=VMEM_LIMIT)


def _const_spec(shape):
    nd = len(shape)
    return pl.BlockSpec(shape, lambda *_: (0,) * nd, pipeline_mode=pl.Buffered(1))


def _layer_spec(shape, layer):
    nd = len(shape)
    return pl.BlockSpec((1,) + tuple(shape[1:]), lambda *_: (layer,) + (0,) * (nd - 1),
                        pipeline_mode=pl.Buffered(1))


def _silu(x):
    return x * (1.0 / (1.0 + jnp.exp(-x)))


def _inv_rms(x):
    return lax.rsqrt(jnp.mean(x * x, axis=-1, keepdims=True) + EPS)


def _split3(x):
    hi = x.astype(BF16)
    r = x - hi.astype(F32)
    mid = r.astype(BF16)
    lo = (r - mid.astype(F32)).astype(BF16)
    return hi, mid, lo


def _dot01(a01, x, dims, a_is_lhs=True):
    out = None
    for p in _split3(x):
        t = (lax.dot_general(a01, p, dims, preferred_element_type=F32) if a_is_lhs
             else lax.dot_general(p, a01, dims, preferred_element_type=F32))
        out = t if out is None else out + t
    return out


def _top3_mask(gate, valid, axis=1):
    nb = gate.shape[axis]
    col = lax.broadcasted_iota(jnp.int32, gate.shape, axis).astype(F32)
    g = jnp.where(valid, gate, NEG_INF)
    sel = jnp.zeros(gate.shape, F32)
    for _ in range(MOBA_TOP_K):
        m = jnp.max(g, axis=axis, keepdims=True)
        idx = jnp.min(jnp.where(g == m, col, float(nb)), axis=axis, keepdims=True)
        pick = col == idx
        sel = jnp.where(pick, 1.0, sel)
        g = jnp.where(pick, NEG_INF, g)
    return jnp.logical_and(sel > 0.5, valid)


def _inproj_body(x_ref, g_ref, w_ref, z_ref, xbc_ref, dt_ref):
    x = x_ref[...]
    h = (x * _inv_rms(x) * g_ref[...]).astype(BF16)
    z_ref[...] = jnp.dot(h, w_ref[0, :, :D_INNER], preferred_element_type=F32)
    xbc_ref[...] = jnp.dot(h, w_ref[0, :, D_INNER:D_INNER + CONV_DIM], preferred_element_type=F32)
    dt = jnp.dot(h, w_ref[0, :, D_INNER + CONV_DIM:], preferred_element_type=F32)
    dt_ref[...] = jnp.concatenate(
        [dt, jnp.zeros((dt.shape[0], LANES - N_SSM_HEADS), F32)], axis=1)


def _inproj(x, g, w_in, layer, tm):
    m = x.shape[0]
    row = lambda w: pl.BlockSpec((tm, w), lambda i: (i, 0))
    return pl.pallas_call(
        _inproj_body,
        grid=(m // tm,),
        in_specs=[row(D_MODEL), _const_spec((1, D_MODEL)), _layer_spec(w_in.shape, layer)],
        out_specs=[row(D_INNER), row(CONV_DIM), row(LANES)],
        out_shape=[jax.ShapeDtypeStruct((m, D_INNER), F32),
                   jax.ShapeDtypeStruct((m, CONV_DIM), F32),
                   jax.ShapeDtypeStruct((m, LANES), F32)],
        compiler_params=_params(("parallel",)),
        name="ssd_inproj",
    )(x, g, w_in)


def _causal_conv(pad_ref, xbc, cw_ref, cb_ref, rows):
    pad_ref[SUBLANES:SUBLANES + rows, :] = xbc
    padded = pad_ref[...]
    conv = cb_ref[...]
    for i in range(D_CONV):
        shift = D_CONV - 1 - i
        tap = padded if shift == 0 else pltpu.roll(padded, shift, 0)
        conv = conv + tap[SUBLANES:SUBLANES + rows, :] * cw_ref[i:i + 1, :]
    pad_ref[0:SUBLANES, :] = pad_ref[rows:rows + SUBLANES, :]
    return conv


def _ssd_body(xbc_ref, dtr_ref, conv0_ref, ssm0_ref, cw_ref, cb_ref, dtb_ref, alog_ref, dsk_ref,
              y_ref, st_ref, pad_ref, *, L, valid):
    c = pl.program_id(1)

    @pl.when(c == 0)
    def _():
        pad_ref[0:SUBLANES, :] = conv0_ref[0]
        st_ref[0] = ssm0_ref[0]

    act = _silu(_causal_conv(pad_ref, xbc_ref[0], cw_ref, cb_ref, L))
    xs = act[:, :D_INNER]
    bm = act[:, D_INNER:D_INNER + GN].astype(BF16)
    cm = act[:, D_INNER + GN:].astype(BF16)

    t = dtr_ref[0] + dtb_ref[...]
    dt = jnp.maximum(t, 0.0) + jnp.log1p(jnp.exp(-jnp.abs(t)))
    if valid < L:
        rows = lax.broadcasted_iota(jnp.int32, dt.shape, 0)
        dt = jnp.where(rows < valid, dt, 0.0)
    a = -jnp.exp(alog_ref[...])
    dta = dt * a

    r_i = lax.broadcasted_iota(jnp.int32, (L, L), 0)
    c_i = lax.broadcasted_iota(jnp.int32, (L, L), 1)
    causal = r_i >= c_i
    tril = jnp.where(causal, 1.0, 0.0).astype(BF16)
    e_r = lax.broadcasted_iota(jnp.int32, (LANES, LANES), 0)
    e_c = lax.broadcasted_iota(jnp.int32, (LANES, LANES), 1)
    eye = jnp.where(e_r == e_c, 1.0, 0.0).astype(BF16)

    acs = _dot01(tril, dta, (((1,), (0,)), ((), ())))
    acs_t = _dot01(eye, acs, NT_DIMS)
    last = acs[L - 1:L, :]
    dec_end = jnp.exp(last - acs)
    e_acs = jnp.exp(acs)
    dec_chunk = jnp.exp(last)

    for g in range(N_SSM_GROUPS):
        bg = bm[:, g * D_STATE:(g + 1) * D_STATE]
        cg = cm[:, g * D_STATE:(g + 1) * D_STATE]
        cb = lax.dot_general(cg, bg, NT_DIMS, preferred_element_type=F32)
        for jp in range(HEADS_PER_GROUP // 2):
            pair = []
            for h in (g * HEADS_PER_GROUP + 2 * jp, g * HEADS_PER_GROUP + 2 * jp + 1):
                diff = acs[:, h:h + 1] - acs_t[h:h + 1, :]
                seg = jnp.exp(jnp.where(causal, diff, NEG_INF))
                w = (cb * seg).astype(BF16)
                xs_h = xs[:, h * SSM_HEAD_DIM:(h + 1) * SSM_HEAD_DIM]
                xd_h = xs_h * dt[:, h:h + 1]
                y_diag = jnp.dot(w, xd_h.astype(BF16), preferred_element_type=F32)
                st_h = st_ref[0, h]
                y_off = lax.dot_general(cg, st_h.astype(BF16), NT_DIMS,
                                        preferred_element_type=F32) * e_acs[:, h:h + 1]
                xdd = (xd_h * dec_end[:, h:h + 1]).astype(BF16)
                cs = lax.dot_general(xdd, bg, TN_DIMS, preferred_element_type=F32)
                st_ref[0, h] = st_h * dec_chunk[:, h:h + 1] + cs
                pair.append(y_diag + y_off
                            + dsk_ref[:, h * SSM_HEAD_DIM:(h + 1) * SSM_HEAD_DIM] * xs_h)
            lo = (g * HEADS_PER_GROUP + 2 * jp) * SSM_HEAD_DIM
            y_ref[0, :, lo:lo + 2 * SSM_HEAD_DIM] = jnp.concatenate(pair, axis=1)


def _ssd_t_body(xbc_ref, dtr_ref, *refs, L, n_sub, has_init):
    if has_init:
        conv0_ref, ssm0_ref = refs[:2]
        refs = refs[2:]
    cw_ref, cb_ref, dtb_ref, alog_ref, dsk_ref, y_ref, st_ref, pad_ref, yt_ref = refs

    @pl.when(pl.program_id(1) == 0)
    def _():
        if has_init:
            pad_ref[0:SUBLANES, :] = conv0_ref[0]
            st_ref[0] = ssm0_ref[0]
        else:
            pad_ref[0:SUBLANES, :] = jnp.zeros((SUBLANES, CONV_DIM), F32)
            st_ref[0] = jnp.zeros(st_ref.shape[1:], F32)

    def chunk(ci, carry):
        rows = pl.ds(pl.multiple_of(ci * L, L), L)
        y_ref[0, rows, :] = _ssd_t_chunk(xbc_ref[0, rows, :], dtr_ref[0, rows, :], cw_ref, cb_ref,
                                         dtb_ref, alog_ref, dsk_ref, st_ref, pad_ref, yt_ref, L)
        return carry

    lax.fori_loop(0, n_sub, chunk, 0)


def _ssd_t_chunk(xbc, dtr, cw_ref, cb_ref, dtb_ref, alog_ref, dsk_ref, st_ref, pad_ref, yt_ref, L):
    hd = SSM_HEAD_DIM
    gw = HEADS_PER_GROUP * hd
    act = _silu(_causal_conv(pad_ref, xbc, cw_ref, cb_ref, L))
    xs_t = act[:, :D_INNER].T
    bm = act[:, D_INNER:D_INNER + GN].astype(BF16)
    cm = act[:, D_INNER + GN:].astype(BF16)

    t = dtr + dtb_ref[...]
    dt = jnp.maximum(t, 0.0) + jnp.log1p(jnp.exp(-jnp.abs(t)))
    dta = dt * (-jnp.exp(alog_ref[...]))

    r_i = lax.broadcasted_iota(jnp.int32, (L, L), 0)
    c_i = lax.broadcasted_iota(jnp.int32, (L, L), 1)
    tril = jnp.where(r_i >= c_i, 1.0, 0.0).astype(BF16)
    causal_t = c_i >= r_i

    acs = _dot01(tril, dta, (((1,), (0,)), ((), ())))
    acs_t = acs.T
    dt_t = dt.T
    dec_end_t = jnp.exp(acs_t[:, L - 1:L] - acs_t)
    e_acs_t = jnp.exp(acs_t)
    dec_chunk = jnp.exp(acs[L - 1:L, :])
    scale_in = dt_t
    scale_st = dt_t * dec_end_t

    for g in range(N_SSM_GROUPS):
        bg = bm[:, g * D_STATE:(g + 1) * D_STATE]
        cg = cm[:, g * D_STATE:(g + 1) * D_STATE]
        cb_t = lax.dot_general(bg, cg, NT_DIMS, preferred_element_type=F32)
        h0 = g * HEADS_PER_GROUP
        xs_g = xs_t[g * gw:(g + 1) * gw, :]
        xd_parts, xdd_parts = [], []
        for j in range(HEADS_PER_GROUP):
            h = h0 + j
            xd = xs_g[j * hd:(j + 1) * hd, :] * scale_in[h:h + 1, :]
            xd_parts.append(xd.astype(BF16))
            xdd_parts.append((xs_g[j * hd:(j + 1) * hd, :] * scale_st[h:h + 1, :]).astype(BF16))
        st_g = st_ref[0, h0:h0 + HEADS_PER_GROUP].reshape(gw, D_STATE)
        y_off = lax.dot_general(st_g.astype(BF16), cg, NT_DIMS,
                                preferred_element_type=F32)
        cs = jnp.dot(jnp.concatenate(xdd_parts, axis=0), bg,
                     preferred_element_type=F32)
        for j in range(HEADS_PER_GROUP):
            h = h0 + j
            diff = acs_t[h:h + 1, :] - acs[:, h:h + 1]
            seg_t = jnp.exp(jnp.where(causal_t, diff, NEG_INF))
            w_t = (cb_t * seg_t).astype(BF16)
            y_diag = jnp.dot(xd_parts[j], w_t, preferred_element_type=F32)
            rows = slice(j * hd, (j + 1) * hd)
            yt_ref[h * hd:(h + 1) * hd, :] = (
                y_diag + y_off[rows, :] * e_acs_t[h:h + 1, :]
                + dsk_ref[h * hd:(h + 1) * hd, :] * xs_g[rows, :])
            st_ref[0, h] = st_g[rows, :] * dec_chunk[:, h:h + 1] + cs[rows, :]
    return yt_ref[...].T


def _ssd(xbc, dtr, conv0, ssm0, cw, cb, dtb, alog, dsk, L, valid):
    nb, t = xbc.shape[0], xbc.shape[1]
    nc = t // L
    st_spec = pl.BlockSpec((1, N_SSM_HEADS, SSM_HEAD_DIM, D_STATE), lambda b, c: (b, 0, 0, 0))
    return pl.pallas_call(
        functools.partial(_ssd_body, L=L, valid=valid),
        grid=(nb, nc),
        in_specs=[pl.BlockSpec((1, L, CONV_DIM), lambda b, c: (b, c, 0)),
                  pl.BlockSpec((1, L, LANES), lambda b, c: (b, c, 0)),
                  pl.BlockSpec((1, SUBLANES, CONV_DIM), lambda b, c: (b, 0, 0)),
                  st_spec,
                  _const_spec(cw.shape), _const_spec(cb.shape), _const_spec(dtb.shape),
                  _const_spec(alog.shape), _const_spec(dsk.shape)],
        out_specs=[pl.BlockSpec((1, L, D_INNER), lambda b, c: (b, c, 0)), st_spec],
        out_shape=[jax.ShapeDtypeStruct((nb, t, D_INNER), F32),
                   jax.ShapeDtypeStruct(ssm0.shape, F32)],
        scratch_shapes=[pltpu.VMEM((L + SUBLANES, CONV_DIM), F32)],
        compiler_params=_params(("parallel", "arbitrary")),
        name="ssd_scan",
    )(xbc, dtr, conv0, ssm0, cw, cb, dtb, alog, dsk)


def _ssd_t(xbc, dtr, conv0, ssm0, cw, cb, dtb, alog, dsk):
    nb, t = xbc.shape[0], xbc.shape[1]
    L = SSD_CHUNK
    n_sub = SSD_CHUNKS_PER_STEP if t % (L * SSD_CHUNKS_PER_STEP) == 0 else 1
    rows = L * n_sub
    assert L == LANES and t % rows == 0
    dsk_t = jnp.broadcast_to(dsk.reshape(D_INNER, 1), (D_INNER, LANES))
    st_spec = pl.BlockSpec((1, N_SSM_HEADS, SSM_HEAD_DIM, D_STATE), lambda b, c: (b, 0, 0, 0))
    has_init = conv0 is not None
    init_args = (conv0, ssm0) if has_init else ()
    init_specs = [pl.BlockSpec((1, SUBLANES, CONV_DIM), lambda b, c: (b, 0, 0)),
                  st_spec] if has_init else []
    return pl.pallas_call(
        functools.partial(_ssd_t_body, L=L, n_sub=n_sub, has_init=has_init),
        grid=(nb, t // rows),
        in_specs=[pl.BlockSpec((1, rows, CONV_DIM), lambda b, c: (b, c, 0)),
                  pl.BlockSpec((1, rows, LANES), lambda b, c: (b, c, 0))] + init_specs + [
                  _const_spec(cw.shape), _const_spec(cb.shape),
                  _const_spec(dtb.shape), _const_spec(alog.shape), _const_spec(dsk_t.shape)],
        out_specs=[pl.BlockSpec((1, rows, D_INNER), lambda b, c: (b, c, 0)), st_spec],
        out_shape=[jax.ShapeDtypeStruct((nb, t, D_INNER), F32),
                   jax.ShapeDtypeStruct((nb, N_SSM_HEADS, SSM_HEAD_DIM, D_STATE), F32)],
        scratch_shapes=[pltpu.VMEM((L + SUBLANES, CONV_DIM), F32), pltpu.VMEM((D_INNER, L), F32)],
        compiler_params=_params(("parallel", "arbitrary")),
        name="ssd_scan_t",
    )(xbc, dtr, *init_args, cw, cb, dtb, alog, dsk_t)


def _ffn(x1, nf_ref, wgu_ref, wd_ref):
    h = (x1 * _inv_rms(x1) * nf_ref[...]).astype(BF16)
    out = x1
    for c in range(D_FF // FFN_CHUNK):
        lo = c * FFN_CHUNK
        gate = jnp.dot(h, wgu_ref[0, :, lo:lo + FFN_CHUNK], preferred_element_type=F32)
        up = jnp.dot(h, wgu_ref[0, :, D_FF + lo:D_FF + lo + FFN_CHUNK], preferred_element_type=F32)
        act = (_silu(gate) * up).astype(BF16)
        out = out + jnp.dot(act, wd_ref[0, lo:lo + FFN_CHUNK, :], preferred_element_type=F32)
    return out


def _ssm_out_ffn_body(y_ref, z_ref, x_ref, gn_ref, wo_ref, nf_ref, wgu_ref, wd_ref, o_ref):
    y = y_ref[...] * _silu(z_ref[...])
    gw = D_INNER // N_SSM_GROUPS
    parts = []
    for g in range(N_SSM_GROUPS):
        yg = y[:, g * gw:(g + 1) * gw]
        parts.append(yg * _inv_rms(yg))
    y = (jnp.concatenate(parts, axis=1) * gn_ref[...]).astype(BF16)
    x1 = x_ref[...] + jnp.dot(y, wo_ref[...], preferred_element_type=F32)
    o_ref[...] = _ffn(x1, nf_ref, wgu_ref, wd_ref)


def _attn_out_ffn_body(a_ref, x_ref, wo_ref, nf_ref, wgu_ref, wd_ref, nfin_ref, o_ref):
    x1 = x_ref[...] + jnp.dot(a_ref[...], wo_ref[...], preferred_element_type=F32)
    x2 = _ffn(x1, nf_ref, wgu_ref, wd_ref)
    o_ref[...] = x2 * _inv_rms(x2) * nfin_ref[...]


def _ssm_out_ffn(y, z, x, gn, wo, nf, wgu, wd, layer, tm):
    m = x.shape[0]
    row = lambda w: pl.BlockSpec((tm, w), lambda i: (i, 0))
    return pl.pallas_call(
        _ssm_out_ffn_body,
        grid=(m // tm,),
        in_specs=[row(D_INNER), row(D_INNER), row(D_MODEL), _const_spec(gn.shape),
                  _const_spec(wo.shape), _const_spec(nf.shape), _layer_spec(wgu.shape, layer),
                  _layer_spec(wd.shape, layer)],
        out_specs=row(D_MODEL),
        out_shape=jax.ShapeDtypeStruct((m, D_MODEL), F32),
        compiler_params=_params(("parallel",)),
        name="ssm_out_ffn",
    )(y, z, x, gn, wo, nf, wgu, wd)


def _attn_out_ffn(a, x, wo, nf, wgu, wd, layer, nfin, tm):
    m = x.shape[0]
    row = lambda w: pl.BlockSpec((tm, w), lambda i: (i, 0))
    return pl.pallas_call(
        _attn_out_ffn_body,
        grid=(m // tm,),
        in_specs=[row(D_MODEL), row(D_MODEL), _const_spec(wo.shape), _const_spec(nf.shape),
                  _layer_spec(wgu.shape, layer), _layer_spec(wd.shape, layer),
                  _const_spec(nfin.shape)],
        out_specs=row(D_MODEL),
        out_shape=jax.ShapeDtypeStruct((m, D_MODEL), F32),
        compiler_params=_params(("parallel",)),
        name="attn_out_ffn",
    )(a, x, wo, nf, wgu, wd, nfin)


def _rope(x, cos, sin_signed, first_half):
    outs = []
    for c in range(x.shape[1] // LANES):
        xc = x[:, c * LANES:(c + 1) * LANES]
        partner = jnp.where(first_half, pltpu.roll(xc, LANES - HEAD_DIM // 2, 1),
                            pltpu.roll(xc, HEAD_DIM // 2, 1))
        outs.append(xc * cos + partner * sin_signed)
    return jnp.concatenate(outs, axis=1)


def _kvq_compute(x_ref, gkv_ref, gq_ref, wkv_ref, wq_ref, cos_ref, sin_ref):
    x = x_ref[...]
    xn = x * _inv_rms(x)
    hk = (xn * gkv_ref[...]).astype(BF16)
    hq = (xn * gq_ref[...]).astype(BF16)
    cos = cos_ref[...]
    sin = sin_ref[...]
    lane = lax.broadcasted_iota(jnp.int32, cos.shape, 1)
    first_half = (lane % HEAD_DIM) < (HEAD_DIM // 2)
    kv = jnp.dot(hk, wkv_ref[...], preferred_element_type=F32)
    k = _rope(kv[:, :KV_DIM], cos, sin, first_half)
    q = _rope(jnp.dot(hq, wq_ref[...], preferred_element_type=F32), cos, sin, first_half)
    return k, kv[:, KV_DIM:], q


def _kvq_body(x_ref, gkv_ref, gq_ref, wkv_ref, wq_ref, cos_ref, sin_ref, k_ref, v_ref, q_ref):
    k, v, q = _kvq_compute(x_ref, gkv_ref, gq_ref, wkv_ref, wq_ref, cos_ref, sin_ref)
    k_ref[...] = k
    v_ref[...] = v
    q_ref[...] = q


def _kvq_prompt_body(x_ref, gkv_ref, gq_ref, wkv_ref, wq_ref, cos_ref, sin_ref,
                     kt_ref, vt32_ref, qt_ref, kh_ref, vt_ref, km_ref):
    k, v, q = _kvq_compute(x_ref, gkv_ref, gq_ref, wkv_ref, wq_ref, cos_ref, sin_ref)
    kt_ref[0] = k.T
    v_t = v.T
    vt32_ref[0] = v_t
    qt_ref[0] = q.T
    vt = v_t.astype(BF16)
    blk = pl.program_id(1)
    col = lax.broadcasted_iota(jnp.int32, (MOBA_BLOCK, LANES - HEAD_DIM), 1)
    onehot = jnp.where(col == blk, 1.0, 0.0).astype(BF16)
    row = lax.broadcasted_iota(jnp.int32, (V_AUG - HEAD_DIM, MOBA_BLOCK), 0)
    ones_row = jnp.where(row == 0, 1.0, 0.0).astype(BF16)
    for kvh in range(N_KV_HEADS):
        kh_ref[0, 0, kvh] = jnp.concatenate(
            [k[:, kvh * HEAD_DIM:(kvh + 1) * HEAD_DIM].astype(BF16), onehot], axis=1)
        vt_ref[0, 0, kvh] = jnp.concatenate(
            [vt[kvh * HEAD_DIM:(kvh + 1) * HEAD_DIM, :], ones_row], axis=0)
    km_ref[0, 0] = jnp.sum(k, axis=0, keepdims=True) * (1.0 / MOBA_BLOCK)


def _kvq(x, gkv, gq, wkv, wq, cos, sin, tm):
    m = x.shape[0]
    row = lambda w: pl.BlockSpec((tm, w), lambda i: (i, 0))
    tab = pl.BlockSpec((tm, LANES), lambda i: (0, 0))
    return pl.pallas_call(
        _kvq_body,
        grid=(m // tm,),
        in_specs=[row(D_MODEL), _const_spec(gkv.shape), _const_spec(gq.shape),
                  _const_spec(wkv.shape), _const_spec(wq.shape), tab, tab],
        out_specs=[row(KV_DIM), row(KV_DIM), row(D_MODEL)],
        out_shape=[jax.ShapeDtypeStruct((m, KV_DIM), F32), jax.ShapeDtypeStruct((m, KV_DIM), F32),
                   jax.ShapeDtypeStruct((m, D_MODEL), F32)],
        compiler_params=_params(("parallel",)),
        name="kvq_proj",
    )(x, gkv, gq, wkv, wq, cos, sin)


def _kvq_prompt(x, b, t, gkv, gq, wkv, wq, cos, sin):
    nb = t // MOBA_BLOCK
    tm = MOBA_BLOCK
    row = lambda w: pl.BlockSpec((tm, w), lambda bi, i: (bi * nb + i, 0))
    tab = pl.BlockSpec((tm, LANES), lambda bi, i: (i, 0))
    return pl.pallas_call(
        _kvq_prompt_body,
        grid=(b, nb),
        in_specs=[row(D_MODEL), _const_spec(gkv.shape), _const_spec(gq.shape),
                  _const_spec(wkv.shape), _const_spec(wq.shape), tab, tab],
        out_specs=[pl.BlockSpec((1, KV_DIM, tm), lambda bi, i: (bi, 0, i)),
                   pl.BlockSpec((1, KV_DIM, tm), lambda bi, i: (bi, 0, i)),
                   pl.BlockSpec((1, D_MODEL, tm), lambda bi, i: (bi, 0, i)),
                   pl.BlockSpec((1, 1, N_KV_HEADS, tm, LANES), lambda bi, i: (bi, i, 0, 0, 0)),
                   pl.BlockSpec((1, 1, N_KV_HEADS, V_AUG, tm), lambda bi, i: (bi, i, 0, 0, 0)),
                   pl.BlockSpec((1, 1, 1, KV_DIM), lambda bi, i: (bi, i, 0, 0))],
        out_shape=[jax.ShapeDtypeStruct((b, KV_DIM, t), F32),
                   jax.ShapeDtypeStruct((b, KV_DIM, t), F32),
                   jax.ShapeDtypeStruct((b, D_MODEL, t), F32),
                   jax.ShapeDtypeStruct((b, nb, N_KV_HEADS, tm, LANES), BF16),
                   jax.ShapeDtypeStruct((b, nb, N_KV_HEADS, V_AUG, tm), BF16),
                   jax.ShapeDtypeStruct((b, nb, 1, KV_DIM), F32)],
        compiler_params=_params(("parallel", "parallel")),
        name="kvq_proj_prompt",
    )(x, gkv, gq, wkv, wq, cos, sin)


def _moba_prompt_body(qt_ref, kh_ref, vt_ref, km_ref, o_ref, s_a, s_b, acc_a, acc_b, *, nb):
    i = pl.program_id(1)
    g_per_unit = ATTN_UNIT_HEADS
    rows = g_per_unit * MOBA_BLOCK
    n_units = N_HEADS // g_per_unit
    scale = (HEAD_DIM ** -0.5) * LOG2_E
    blk_row = lax.broadcasted_iota(jnp.int32, (nb, rows), 0)
    key_i = lax.broadcasted_iota(jnp.int32, (MOBA_BLOCK, rows), 0)
    tok_i = lax.broadcasted_iota(jnp.int32, (MOBA_BLOCK, rows), 1) % MOBA_BLOCK
    own_ok = key_i <= tok_i
    fold = lambda s: jnp.max(s.reshape(MOBA_BLOCK // SUBLANES, SUBLANES, rows), axis=0)

    def setup(u):
        kvh = (u * g_per_unit) // Q_PER_KV
        lo = kvh * HEAD_DIM
        base = u * g_per_unit * HEAD_DIM
        qt2 = qt_ref[0, base:base + g_per_unit * HEAD_DIM, :]
        qt = jnp.concatenate([qt2[g * HEAD_DIM:(g + 1) * HEAD_DIM, :] for g in range(g_per_unit)],
                             axis=1)
        km = km_ref[0, :, lo:lo + HEAD_DIM]
        gate = jnp.dot(km, qt, preferred_element_type=F32,
                       precision=lax.Precision.HIGHEST)
        sel = _top3_mask(gate, blk_row < i, axis=0)
        pen = jnp.where(sel, 0.0, MASKED).astype(BF16)
        qb = (qt * scale).astype(BF16)
        q_own = jnp.concatenate([qb, jnp.zeros((LANES - HEAD_DIM, rows), BF16)], axis=0)
        q_aug = jnp.concatenate([qb, pen, jnp.zeros((LANES - HEAD_DIM - nb, rows), BF16)], axis=0)
        s_own = jnp.dot(kh_ref[0, i, kvh], q_own, preferred_element_type=F32)
        s_own = jnp.where(own_ok, s_own, MASKED)
        s_bufs[u % 2][nb] = s_own
        return kvh, q_aug, fold(s_own)

    def scores(u, kvh, q_aug, j0, n, m8):
        for t in range(n):
            s = jnp.dot(kh_ref[0, j0 + t, kvh], q_aug, preferred_element_type=F32)
            s_bufs[u % 2][j0 + t] = s
            m8 = jnp.maximum(m8, fold(s))
        return m8

    def weighted(u, kvh, m, j0, n):
        s_buf = s_bufs[u % 2]
        p = jnp.concatenate([jnp.exp2(s_buf[j0 + t] - m).astype(BF16) for t in range(n)], axis=0)
        v = jnp.concatenate([vt_ref[0, j0 + t, kvh] for t in range(n)], axis=1)
        acc_bufs[u % 2][...] += jnp.dot(v, p, preferred_element_type=F32)

    def over_blocks(step, carry):
        n4, n2 = i // 4, i // 2
        carry = lax.fori_loop(0, n4, lambda q, c: step(4 * q + 2, 2, step(4 * q, 2, c)), carry)
        carry = lax.fori_loop(2 * n4, n2, lambda q, c: step(2 * q, 2, c), carry)
        return lax.fori_loop(2 * n2, i, lambda j, c: step(j, 1, c), carry)

    s_bufs = (s_a, s_b)
    acc_bufs = (acc_a, acc_b)
    kvh, q_aug, m8 = setup(0)
    m8 = over_blocks(functools.partial(scores, 0, kvh, q_aug), m8)
    for u in range(n_units):
        m = jnp.max(m8, axis=0, keepdims=True)
        p_own = jnp.exp2(s_bufs[u % 2][nb] - m).astype(BF16)
        acc_bufs[u % 2][...] = jnp.dot(vt_ref[0, i, kvh], p_own, preferred_element_type=F32)
        if u + 1 < n_units:
            kvh_n, q_aug_n, m8_n = setup(u + 1)

            def both(j0, n, m8c, u=u, kvh=kvh, m=m, kvh_n=kvh_n, q_aug_n=q_aug_n):
                m8c = scores(u + 1, kvh_n, q_aug_n, j0, n, m8c)
                weighted(u, kvh, m, j0, n)
                return m8c

            m8_n = over_blocks(both, m8_n)
        else:
            def last(j0, n, carry, u=u, kvh=kvh, m=m):
                weighted(u, kvh, m, j0, n)
                return carry

            over_blocks(last, 0)
        acc = acc_bufs[u % 2][...]
        out_t = acc[:HEAD_DIM, :] / acc[HEAD_DIM:HEAD_DIM + 1, :]
        out2 = jnp.concatenate([out_t[:, g * MOBA_BLOCK:(g + 1) * MOBA_BLOCK]
                                for g in range(g_per_unit)], axis=0)
        base = u * g_per_unit * HEAD_DIM
        o_ref[0, :, base:base + g_per_unit * HEAD_DIM] = out2.T.astype(o_ref.dtype)
        if u + 1 < n_units:
            kvh, m8 = kvh_n, m8_n


def _moba_prompt(qt, kh, vt, km):
    b, t = qt.shape[0], qt.shape[2]
    nb = t // MOBA_BLOCK
    rows = ATTN_UNIT_HEADS * MOBA_BLOCK
    return pl.pallas_call(
        functools.partial(_moba_prompt_body, nb=nb),
        grid=(b, nb),
        in_specs=[pl.BlockSpec((1, D_MODEL, MOBA_BLOCK), lambda bi, i: (bi, 0, i)),
                  pl.BlockSpec((1, nb, N_KV_HEADS, MOBA_BLOCK, LANES),
                               lambda bi, i: (bi, 0, 0, 0, 0), pipeline_mode=pl.Buffered(1)),
                  pl.BlockSpec((1, nb, N_KV_HEADS, V_AUG, MOBA_BLOCK),
                               lambda bi, i: (bi, 0, 0, 0, 0), pipeline_mode=pl.Buffered(1)),
                  pl.BlockSpec((1, nb, KV_DIM), lambda bi, i: (bi, 0, 0))],
        out_specs=pl.BlockSpec((1, MOBA_BLOCK, D_MODEL), lambda bi, i: (bi, i, 0)),
        out_shape=jax.ShapeDtypeStruct((b, t, D_MODEL), BF16),
        scratch_shapes=[pltpu.VMEM((nb + 1, MOBA_BLOCK, rows), F32),
                        pltpu.VMEM((nb + 1, MOBA_BLOCK, rows), F32),
                        pltpu.VMEM((V_AUG, rows), F32), pltpu.VMEM((V_AUG, rows), F32)],
        compiler_params=_params(("parallel", "arbitrary")),
        name="moba_prompt",
    )(qt, kh, vt, km)


def _page_copy(cache_hbm, buf, sem, page, slot, p):
    return pltpu.make_async_copy(cache_hbm.at[page], buf.at[slot, p], sem)


def _moba_sample_body(pt_ref, q_ref, kn_ref, vn_ref, ck_hbm, cv_hbm, o_ref,
                      kbuf, vbuf, s_scr, new_scr, sems, *, n_pages, n_new):
    b = pl.program_id(0)
    slot = b % 2
    rows = q_ref.shape[1]
    ppb = MOBA_BLOCK // PAGE_SIZE
    nb = n_pages // ppb
    scale = HEAD_DIM ** -0.5

    def fetch(seq, to_slot):
        def start_page(p, carry):
            page = pt_ref[seq, p]
            _page_copy(ck_hbm, kbuf, sems.at[0, to_slot], page, to_slot, p).start()
            _page_copy(cv_hbm, vbuf, sems.at[1, to_slot], page, to_slot, p).start()
            return carry
        lax.fori_loop(0, n_pages, start_page, 0)

    @pl.when(b == 0)
    def _():
        fetch(0, 0)

    @pl.when(b + 1 < pl.num_programs(0))
    def _():
        fetch(b + 1, 1 - slot)

    q64 = q_ref[0]
    qt = jnp.concatenate([q64] * N_KV_HEADS, axis=1)
    r_i = lax.broadcasted_iota(jnp.int32, qt.shape, 0)
    c_i = lax.broadcasted_iota(jnp.int32, qt.shape, 1)
    qpad = jnp.where(c_i // HEAD_DIM == r_i // (Q_PER_KV * n_new), qt, 0.0)
    qb = (qpad * scale).astype(BF16)

    def wait_k(p, carry):
        _page_copy(ck_hbm, kbuf, sems.at[0, slot], 0, slot, p).wait()
        return carry

    lax.fori_loop(0, n_pages, wait_k, 0)

    blk_lane = lax.broadcasted_iota(jnp.int32, (KV_DIM, LANES), 1)
    km_t = jnp.zeros((KV_DIM, LANES), F32)
    for j in range(nb):
        kt = jnp.concatenate([kbuf[slot, ppb * j + u] for u in range(ppb)], axis=1)
        km_col = jnp.sum(kt, axis=1, keepdims=True) * (1.0 / MOBA_BLOCK)
        km_t = jnp.where(blk_lane == j, km_col, km_t)
        s_scr[:, j * MOBA_BLOCK:(j + 1) * MOBA_BLOCK] = jnp.dot(
            qb, kt.astype(BF16), preferred_element_type=F32)

    gate = jnp.dot(qpad, km_t, preferred_element_type=F32,
                   precision=lax.Precision.HIGHEST)
    sel = _top3_mask(gate, lax.broadcasted_iota(jnp.int32, gate.shape, 1) < nb)
    pen = jnp.where(sel, 0.0, NEG_INF)

    new_scr[...] = jnp.zeros(new_scr.shape, F32)
    new_scr[0:n_new, :] = kn_ref[0]
    s_new = lax.dot_general(qb, new_scr[...].astype(BF16), NT_DIMS, preferred_element_type=F32)
    nr = lax.broadcasted_iota(jnp.int32, s_new.shape, 0)
    ncol = lax.broadcasted_iota(jnp.int32, s_new.shape, 1)
    s_new = jnp.where(ncol <= nr % n_new, s_new, NEG_INF)

    def lane_fold(x, op):
        out = x[:, :LANES]
        for c in range(1, x.shape[1] // LANES):
            out = op(out, x[:, c * LANES:(c + 1) * LANES])
        return out

    m_part = s_new
    for j in range(nb):
        sj = s_scr[:, j * MOBA_BLOCK:(j + 1) * MOBA_BLOCK] + pen[:, j:j + 1]
        m_part = jnp.maximum(m_part, lane_fold(sj, jnp.maximum))
    m = jnp.max(m_part, axis=1, keepdims=True)

    def wait_v(p, carry):
        _page_copy(cv_hbm, vbuf, sems.at[1, slot], 0, slot, p).wait()
        return carry

    lax.fori_loop(0, n_pages, wait_v, 0)

    p_new = jnp.exp(s_new - m)
    l_part = p_new
    new_scr[0:n_new, :] = vn_ref[0]
    acc = jnp.dot(p_new.astype(BF16), new_scr[...].astype(BF16), preferred_element_type=F32)
    for j in range(nb):
        pj = jnp.exp(s_scr[:, j * MOBA_BLOCK:(j + 1) * MOBA_BLOCK] + pen[:, j:j + 1] - m)
        l_part = l_part + lane_fold(pj, jnp.add)
        vt = jnp.concatenate([vbuf[slot, ppb * j + u] for u in range(ppb)], axis=1).astype(BF16)
        acc = acc + lax.dot_general(pj.astype(BF16), vt, NT_DIMS, preferred_element_type=F32)
    acc = acc / jnp.sum(l_part, axis=1, keepdims=True)
    ro = lax.broadcasted_iota(jnp.int32, (rows, HEAD_DIM), 0) // (Q_PER_KV * n_new)
    out = jnp.zeros((rows, HEAD_DIM), F32)
    for kvh in range(N_KV_HEADS):
        out = jnp.where(ro == kvh, acc[:, kvh * HEAD_DIM:(kvh + 1) * HEAD_DIM], out)
    o_ref[0] = out


def _moba_sample(page_table, q_rows, k_new, v_new, cache_k, cache_v):
    nseq, n_pages = page_table.shape
    rows = q_rows.shape[1]
    n_new = k_new.shape[1]
    past = n_pages * PAGE_SIZE
    grid_spec = pltpu.PrefetchScalarGridSpec(
        num_scalar_prefetch=1,
        grid=(nseq,),
        in_specs=[pl.BlockSpec((1, rows, HEAD_DIM), lambda b, pt: (b, 0, 0)),
                  pl.BlockSpec((1, n_new, KV_DIM), lambda b, pt: (b, 0, 0)),
                  pl.BlockSpec((1, n_new, KV_DIM), lambda b, pt: (b, 0, 0)),
                  pl.BlockSpec(memory_space=pl.ANY),
                  pl.BlockSpec(memory_space=pl.ANY)],
        out_specs=pl.BlockSpec((1, rows, HEAD_DIM), lambda b, pt: (b, 0, 0)),
        scratch_shapes=[pltpu.VMEM((2, n_pages, KV_DIM, PAGE_SIZE), F32),
                        pltpu.VMEM((2, n_pages, KV_DIM, PAGE_SIZE), F32),
                        pltpu.VMEM((rows, past), F32),
                        pltpu.VMEM((LANES, KV_DIM), F32),
                        pltpu.SemaphoreType.DMA((2, 2))],
    )
    return pl.pallas_call(
        functools.partial(_moba_sample_body, n_pages=n_pages, n_new=n_new),
        grid_spec=grid_spec,
        out_shape=jax.ShapeDtypeStruct((nseq, rows, HEAD_DIM), F32),
        compiler_params=_params(("arbitrary",)),
        name="moba_sample",
    )(page_table, q_rows, k_new, v_new, cache_k, cache_v)


def _rope_tables(pos):
    half = HEAD_DIM // 2
    inv = ROPE_THETA ** (-np.arange(half, dtype=np.float64) / half)
    ang = np.asarray(pos, dtype=np.float64)[:, None] * inv[None, :]
    cos, sin = np.cos(ang), np.sin(ang)
    cos_h = np.concatenate([cos, cos], axis=1)
    sin_h = np.concatenate([-sin, sin], axis=1)
    reps = LANES // HEAD_DIM
    return (jnp.asarray(np.tile(cos_h, (1, reps)), F32), jnp.asarray(np.tile(sin_h, (1, reps)), F32))


def _prep_weights(norm_mix, norm_ffn, w_in_ssm, conv_w, conv_b, dt_bias, a_log, d_skip, norm_ssm,
                  w_out_ssm, norm_kv, w_kv, w_q, w_o, w_gu, w_down, norm_final):
    pad_h = LANES - N_SSM_HEADS
    return dict(
        g_mix0=norm_mix[0][None], g_mix1=norm_mix[1][None],
        g_ffn0=norm_ffn[0][None], g_ffn1=norm_ffn[1][None],
        w_in=w_in_ssm.astype(BF16),
        cw=conv_w[0], cb=conv_b[0][None],
        dtb=jnp.pad(dt_bias[0], (0, pad_h))[None], alog=jnp.pad(a_log[0], (0, pad_h))[None],
        dsk=jnp.repeat(d_skip[0], SSM_HEAD_DIM)[None],
        gn=norm_ssm[0][None], wo_ssm=w_out_ssm[0].astype(BF16),
        g_kv=norm_kv[None], wkv=w_kv.astype(BF16), wq=w_q[0].astype(BF16),
        wo=w_o[0].astype(BF16),
        wgu=w_gu.astype(BF16), wd=w_down.astype(BF16),
        g_fin=norm_final[None],
    )


def _ssd_layer_long(x, conv_in, ssm0, w, tm):
    b, t, _ = x.shape
    keep = D_CONV - 1
    assert t >= keep
    xf = x.reshape(b * t, D_MODEL)
    z, xbc, dtr = _inproj(xf, w["g_mix0"], w["w_in"], 0, tm)
    xbc3 = xbc.reshape(b, t, CONV_DIM)
    conv0 = None if conv_in is None else jnp.pad(conv_in, ((0, 0), (SUBLANES - keep, 0), (0, 0)))
    y, ssm_new = _ssd_t(xbc3, dtr.reshape(b, t, LANES), conv0, ssm0, w["cw"], w["cb"], w["dtb"],
                        w["alog"], w["dsk"])
    x2 = _ssm_out_ffn(y.reshape(b * t, D_INNER), z, xf, w["gn"], w["wo_ssm"], w["g_ffn0"],
                      w["wgu"], w["wd"], 0, tm)
    return x2, xbc3[:, t - keep:], ssm_new


def _ssd_layer(x, conv_in, ssm0, w, tm, L, valid):
    b, t, _ = x.shape
    xf = x.reshape(b * t, D_MODEL)
    z, xbc, dtr = _inproj(xf, w["g_mix0"], w["w_in"], 0, tm)
    tp = -(-t // L) * L
    xbc3 = xbc.reshape(b, t, CONV_DIM)
    dtr3 = dtr.reshape(b, t, LANES)
    if tp != t:
        xbc3 = jnp.pad(xbc3, ((0, 0), (0, tp - t), (0, 0)))
        dtr3 = jnp.pad(dtr3, ((0, 0), (0, tp - t), (0, 0)))
    conv0 = jnp.pad(conv_in, ((0, 0), (SUBLANES - (D_CONV - 1), 0), (0, 0)))
    y, ssm_new = _ssd(xbc3, dtr3, conv0, ssm0, w["cw"], w["cb"], w["dtb"], w["alog"], w["dsk"],
                      L, valid)
    y = y[:, :t].reshape(b * t, D_INNER)
    x2 = _ssm_out_ffn(y, z, xf, w["gn"], w["wo_ssm"], w["g_ffn0"], w["wgu"], w["wd"], 0, tm)
    keep = D_CONV - 1
    conv_new = jnp.concatenate([conv_in[:, t:], xbc.reshape(b, t, CONV_DIM)[:, max(0, t - keep):]],
                               axis=1)
    return x2, conv_new, ssm_new


def kernel(x_prompt, x_sample, state_conv, state_ssm, cache_k, cache_v, page_table, norm_mix,
           norm_ffn, w_in_ssm, conv_w, conv_b, dt_bias, a_log, d_skip, norm_ssm, w_out_ssm,
           norm_kv, w_kv, w_q, w_o, w_gu, w_down, norm_final):
    w = _prep_weights(norm_mix, norm_ffn, w_in_ssm, conv_w, conv_b, dt_bias, a_log, d_skip,
                      norm_ssm, w_out_ssm, norm_kv, w_kv, w_q, w_o, w_gu, w_down, norm_final)
    bp, tp, _ = x_prompt.shape
    bs, ts, _ = x_sample.shape
    past_len = page_table.shape[1] * PAGE_SIZE

    tm_p = PROMPT_ROW_TILE
    x2_p, conv_p, ssm_p = _ssd_layer_long(x_prompt, None, None, w, tm_p)
    cos_p, sin_p = _rope_tables(np.arange(tp))
    kt_p, vt32_p, qt_p, kh_p, vt_p, km_p = _kvq_prompt(x2_p, bp, tp, w["g_kv"], w["g_mix1"],
                                                       w["wkv"], w["wq"], cos_p, sin_p)
    k_p = kt_p.reshape(bp, N_KV_HEADS, HEAD_DIM, tp).transpose(0, 3, 1, 2)
    v_p = vt32_p.reshape(bp, N_KV_HEADS, HEAD_DIM, tp).transpose(0, 3, 1, 2)
    attn_p = _moba_prompt(qt_p, kh_p, vt_p, km_p.reshape(bp, tp // MOBA_BLOCK, KV_DIM))
    y_p = _attn_out_ffn(attn_p.reshape(bp * tp, D_MODEL), x2_p, w["wo"], w["g_ffn1"], w["wgu"],
                        w["wd"], 1, w["g_fin"], tm_p)

    tm_s = bs * ts
    x2_s, conv_s, ssm_s = _ssd_layer(x_sample, state_conv[0], state_ssm[0], w, tm_s, SUBLANES, ts)
    cos_s, sin_s = _rope_tables(past_len + np.tile(np.arange(ts), bs))
    k_s, v_s, q_s = _kvq(x2_s, w["g_kv"], w["g_mix1"], w["wkv"], w["wq"], cos_s, sin_s, tm_s)
    q_rows = q_s.reshape(bs, ts, N_HEADS, HEAD_DIM).transpose(0, 2, 1, 3).reshape(
        bs, N_HEADS * ts, HEAD_DIM)
    q_rows = jnp.pad(q_rows, ((0, 0), (0, LANES - N_HEADS * ts), (0, 0)))
    attn_rows = _moba_sample(page_table, q_rows, k_s.reshape(bs, ts, KV_DIM),
                             v_s.reshape(bs, ts, KV_DIM),
                             cache_k.transpose(0, 2, 3, 1).reshape(-1, KV_DIM, PAGE_SIZE),
                             cache_v.transpose(0, 2, 3, 1).reshape(-1, KV_DIM, PAGE_SIZE))
    attn_s = attn_rows[:, :N_HEADS * ts].reshape(bs, N_HEADS, ts, HEAD_DIM).transpose(
        0, 2, 1, 3).reshape(bs * ts, D_MODEL).astype(BF16)
    y_s = _attn_out_ffn(attn_s, x2_s, w["wo"], w["g_ffn1"], w["wgu"], w["wd"], 1,
                        w["g_fin"], tm_s)

    return (y_p.reshape(bp, tp, D_MODEL), y_s.reshape(bs, ts, D_MODEL),
            conv_p[None], ssm_p[None],
            k_p, v_p,
            conv_s[None], ssm_s[None],
            k_s.reshape(bs, ts, N_KV_HEADS, HEAD_DIM), v_s.reshape(bs, ts, N_KV_HEADS, HEAD_DIM))
```

```python
import functools

import jax
import jax.numpy as jnp
import numpy as np
from jax import lax
from jax.experimental import pallas as pl
from jax.experimental.pallas import tpu as pltpu

F32 = jnp.float32
BF16 = jnp.bfloat16

D_MODEL = 1024
D_INNER = 2048
SSM_HEAD_DIM = 64
N_SSM_HEADS = 32
N_SSM_GROUPS = 4
HEADS_PER_GROUP = 8
D_STATE = 128
D_CONV = 4
SSD_CHUNK = 128
GN = N_SSM_GROUPS * D_STATE
CONV_DIM = D_INNER + 2 * GN
HEAD_DIM = 64
N_HEADS = 16
N_KV_HEADS = 4
Q_PER_KV = 4
KV_DIM = N_KV_HEADS * HEAD_DIM
MOBA_BLOCK = 256
MOBA_TOP_K = 3
ROPE_THETA = 10000.0
D_FF = 2816
EPS = 1e-6
PAGE_SIZE = 128

LANES = 128
SUBLANES = 8
VMEM_LIMIT = 56 * 1024 * 1024

NT_DIMS = (((1,), (1,)), ((), ()))
TN_DIMS = (((0,), (0,)), ((), ()))
NEG_INF = float("-inf")
MASKED = -1e30
LOG2_E = 1.4426950408889634
V_AUG = HEAD_DIM + 16
PROMPT_ROW_TILE = 512
FFN_CHUNK = 256
SSD_CHUNKS_PER_STEP = 2
KVQ_BLOCKS_PER_STEP = 4
ATTN_UNIT_HEADS = 4


def _params(sem):
    return pltpu.CompilerParams(dimension_semantics=sem, vmem_limit_bytes=VMEM_LIMIT)


def _const_spec(shape):
    nd = len(shape)
    return pl.BlockSpec(shape, lambda *_: (0,) * nd, pipeline_mode=pl.Buffered(1))


def _layer_spec(shape, layer):
    nd = len(shape)
    return pl.BlockSpec((1,) + tuple(shape[1:]), lambda *_: (layer,) + (0,) * (nd - 1),
                        pipeline_mode=pl.Buffered(1))


def _silu(x):
    return x * (1.0 / (1.0 + jnp.exp(-x)))


def _inv_rms(x):
    return lax.rsqrt(jnp.mean(x * x, axis=-1, keepdims=True) + EPS)


def _split3(x):
    hi = x.astype(BF16)
    r = x - hi.astype(F32)
    mid = r.astype(BF16)
    lo = (r - mid.astype(F32)).astype(BF16)
    return hi, mid, lo


def _dot01(a01, x, dims, a_is_lhs=True):
    out = None
    for p in _split3(x):
        t = (lax.dot_general(a01, p, dims, preferred_element_type=F32) if a_is_lhs
             else lax.dot_general(p, a01, dims, preferred_element_type=F32))
        out = t if out is None else out + t
    return out


def _top3_mask(gate, valid, axis=1):
    nb = gate.shape[axis]
    col = lax.broadcasted_iota(jnp.int32, gate.shape, axis).astype(F32)
    g = jnp.where(valid, gate, NEG_INF)
    sel = jnp.zeros(gate.shape, F32)
    for _ in range(MOBA_TOP_K):
        m = jnp.max(g, axis=axis, keepdims=True)
        idx = jnp.min(jnp.where(g == m, col, float(nb)), axis=axis, keepdims=True)
        pick = col == idx
        sel = jnp.where(pick, 1.0, sel)
        g = jnp.where(pick, NEG_INF, g)
    return jnp.logical_and(sel > 0.5, valid)


def _inproj_body(x_ref, g_ref, w_ref, z_ref, xbc_ref, dt_ref):
    x = x_ref[...]
    h = (x * _inv_rms(x) * g_ref[...]).astype(BF16)
    z_ref[...] = jnp.dot(h, w_ref[0, :, :D_INNER], preferred_element_type=F32)
    xbc_ref[...] = jnp.dot(h, w_ref[0, :, D_INNER:D_INNER + CONV_DIM], preferred_element_type=F32)
    dt = jnp.dot(h, w_ref[0, :, D_INNER + CONV_DIM:], preferred_element_type=F32)
    dt_ref[...] = jnp.concatenate(
        [dt, jnp.zeros((dt.shape[0], LANES - N_SSM_HEADS), F32)], axis=1)


def _inproj(x, g, w_in, layer, tm):
    m = x.shape[0]
    row = lambda w: pl.BlockSpec((tm, w), lambda i: (i, 0))
    return pl.pallas_call(
        _inproj_body,
        grid=(m // tm,),
        in_specs=[row(D_MODEL), _const_spec((1, D_MODEL)), _layer_spec(w_in.shape, layer)],
        out_specs=[row(D_INNER), row(CONV_DIM), row(LANES)],
        out_shape=[jax.ShapeDtypeStruct((m, D_INNER), F32),
                   jax.ShapeDtypeStruct((m, CONV_DIM), F32),
                   jax.ShapeDtypeStruct((m, LANES), F32)],
        compiler_params=_params(("parallel",)),
        name="ssd_inproj",
    )(x, g, w_in)


def _causal_conv(pad_ref, xbc, cw_ref, cb_ref, rows):
    pad_ref[SUBLANES:SUBLANES + rows, :] = xbc
    padded = pad_ref[...]
    conv = cb_ref[...]
    for i in range(D_CONV):
        shift = D_CONV - 1 - i
        tap = padded if shift == 0 else pltpu.roll(padded, shift, 0)
        conv = conv + tap[SUBLANES:SUBLANES + rows, :] * cw_ref[i:i + 1, :]
    pad_ref[0:SUBLANES, :] = pad_ref[rows:rows + SUBLANES, :]
    return conv


def _ssd_body(xbc_ref, dtr_ref, conv0_ref, ssm0_ref, cw_ref, cb_ref, dtb_ref, alog_ref, dsk_ref,
              y_ref, st_ref, pad_ref, *, L, valid):
    c = pl.program_id(1)

    @pl.when(c == 0)
    def _():
        pad_ref[0:SUBLANES, :] = conv0_ref[0]
        st_ref[0] = ssm0_ref[0]

    act = _silu(_causal_conv(pad_ref, xbc_ref[0], cw_ref, cb_ref, L))
    xs = act[:, :D_INNER]
    bm = act[:, D_INNER:D_INNER + GN].astype(BF16)
    cm = act[:, D_INNER + GN:].astype(BF16)

    t = dtr_ref[0] + dtb_ref[...]
    dt = jnp.maximum(t, 0.0) + jnp.log1p(jnp.exp(-jnp.abs(t)))
    if valid < L:
        rows = lax.broadcasted_iota(jnp.int32, dt.shape, 0)
        dt = jnp.where(rows < valid, dt, 0.0)
    a = -jnp.exp(alog_ref[...])
    dta = dt * a

    r_i = lax.broadcasted_iota(jnp.int32, (L, L), 0)
    c_i = lax.broadcasted_iota(jnp.int32, (L, L), 1)
    causal = r_i >= c_i
    tril = jnp.where(causal, 1.0, 0.0).astype(BF16)
    e_r = lax.broadcasted_iota(jnp.int32, (LANES, LANES), 0)
    e_c = lax.broadcasted_iota(jnp.int32, (LANES, LANES), 1)
    eye = jnp.where(e_r == e_c, 1.0, 0.0).astype(BF16)

    acs = _dot01(tril, dta, (((1,), (0,)), ((), ())))
    acs_t = _dot01(eye, acs, NT_DIMS)
    last = acs[L - 1:L, :]
    dec_end = jnp.exp(last - acs)
    e_acs = jnp.exp(acs)
    dec_chunk = jnp.exp(last)

    for g in range(N_SSM_GROUPS):
        bg = bm[:, g * D_STATE:(g + 1) * D_STATE]
        cg = cm[:, g * D_STATE:(g + 1) * D_STATE]
        cb = lax.dot_general(cg, bg, NT_DIMS, preferred_element_type=F32)
        for jp in range(HEADS_PER_GROUP // 2):
            pair = []
            for h in (g * HEADS_PER_GROUP + 2 * jp, g * HEADS_PER_GROUP + 2 * jp + 1):
                diff = acs[:, h:h + 1] - acs_t[h:h + 1, :]
                seg = jnp.exp(jnp.where(causal, diff, NEG_INF))
                w = (cb * seg).astype(BF16)
                xs_h = xs[:, h * SSM_HEAD_DIM:(h + 1) * SSM_HEAD_DIM]
                xd_h = xs_h * dt[:, h:h + 1]
                y_diag = jnp.dot(w, xd_h.astype(BF16), preferred_element_type=F32)
                st_h = st_ref[0, h]
                y_off = lax.dot_general(cg, st_h.astype(BF16), NT_DIMS,
                                        preferred_element_type=F32) * e_acs[:, h:h + 1]
                xdd = (xd_h * dec_end[:, h:h + 1]).astype(BF16)
                cs = lax.dot_general(xdd, bg, TN_DIMS, preferred_element_type=F32)
                st_ref[0, h] = st_h * dec_chunk[:, h:h + 1] + cs
                pair.append(y_diag + y_off
                            + dsk_ref[:, h * SSM_HEAD_DIM:(h + 1) * SSM_HEAD_DIM] * xs_h)
            lo = (g * HEADS_PER_GROUP + 2 * jp) * SSM_HEAD_DIM
            y_ref[0, :, lo:lo + 2 * SSM_HEAD_DIM] = jnp.concatenate(pair, axis=1)


def _ssd_t_body(xbc_ref, dtr_ref, *refs, L, n_sub, has_init):
    if has_init:
        conv0_ref, ssm0_ref = refs[:2]
        refs = refs[2:]
    cw_ref, cb_ref, dtb_ref, alog_ref, dsk_ref, y_ref, st_ref, pad_ref, yt_ref = refs

    @pl.when(pl.program_id(1) == 0)
    def _():
        if has_init:
            pad_ref[0:SUBLANES, :] = conv0_ref[0]
            st_ref[0] = ssm0_ref[0]
        else:
            pad_ref[0:SUBLANES, :] = jnp.zeros((SUBLANES, CONV_DIM), F32)
            st_ref[0] = jnp.zeros(st_ref.shape[1:], F32)

    def chunk(ci, carry):
        rows = pl.ds(pl.multiple_of(ci * L, L), L)
        y_ref[0, rows, :] = _ssd_t_chunk(xbc_ref[0, rows, :], dtr_ref[0, rows, :], cw_ref, cb_ref,
                                         dtb_ref, alog_ref, dsk_ref, st_ref, pad_ref, yt_ref, L)
        return carry

    lax.fori_loop(0, n_sub, chunk, 0)


def _ssd_t_chunk(xbc, dtr, cw_ref, cb_ref, dtb_ref, alog_ref, dsk_ref, st_ref, pad_ref, yt_ref, L):
    hd = SSM_HEAD_DIM
    gw = HEADS_PER_GROUP * hd
    act = _silu(_causal_conv(pad_ref, xbc, cw_ref, cb_ref, L))
    xs_t = act[:, :D_INNER].T
    bm = act[:, D_INNER:D_INNER + GN].astype(BF16)
    cm = act[:, D_INNER + GN:].astype(BF16)

    t = dtr + dtb_ref[...]
    dt = jnp.maximum(t, 0.0) + jnp.log1p(jnp.exp(-jnp.abs(t)))
    dta = dt * (-jnp.exp(alog_ref[...]))

    r_i = lax.broadcasted_iota(jnp.int32, (L, L), 0)
    c_i = lax.broadcasted_iota(jnp.int32, (L, L), 1)
    tril = jnp.where(r_i >= c_i, 1.0, 0.0).astype(BF16)
    causal_t = c_i >= r_i

    acs = _dot01(tril, dta, (((1,), (0,)), ((), ())))
    acs_t = acs.T
    dt_t = dt.T
    dec_end_t = jnp.exp(acs_t[:, L - 1:L] - acs_t)
    e_acs_t = jnp.exp(acs_t)
    dec_chunk = jnp.exp(acs[L - 1:L, :])
    scale_in = dt_t
    scale_st = dt_t * dec_end_t

    for g in range(N_SSM_GROUPS):
        bg = bm[:, g * D_STATE:(g + 1) * D_STATE]
        cg = cm[:, g * D_STATE:(g + 1) * D_STATE]
        cb_t = lax.dot_general(bg, cg, NT_DIMS, preferred_element_type=F32)
        h0 = g * HEADS_PER_GROUP
        xs_g = xs_t[g * gw:(g + 1) * gw, :]
        xd_parts, xdd_parts = [], []
        for j in range(HEADS_PER_GROUP):
            h = h0 + j
            xd = xs_g[j * hd:(j + 1) * hd, :] * scale_in[h:h + 1, :]
            xd_parts.append(xd.astype(BF16))
            xdd_parts.append((xs_g[j * hd:(j + 1) * hd, :] * scale_st[h:h + 1, :]).astype(BF16))
        st_g = st_ref[0, h0:h0 + HEADS_PER_GROUP].reshape(gw, D_STATE)
        y_off = lax.dot_general(st_g.astype(BF16), cg, NT_DIMS,
                                preferred_element_type=F32)
        cs = jnp.dot(jnp.concatenate(xdd_parts, axis=0), bg,
                     preferred_element_type=F32)
        for j in range(HEADS_PER_GROUP):
            h = h0 + j
            diff = acs_t[h:h + 1, :] - acs[:, h:h + 1]
            seg_t = jnp.exp(jnp.where(causal_t, diff, NEG_INF))
            w_t = (cb_t * seg_t).astype(BF16)
            y_diag = jnp.dot(xd_parts[j], w_t, preferred_element_type=F32)
            rows = slice(j * hd, (j + 1) * hd)
            yt_ref[h * hd:(h + 1) * hd, :] = (
                y_diag + y_off[rows, :] * e_acs_t[h:h + 1, :]
                + dsk_ref[h * hd:(h + 1) * hd, :] * xs_g[rows, :])
            st_ref[0, h] = st_g[rows, :] * dec_chunk[:, h:h + 1] + cs[rows, :]
    return yt_ref[...].T


def _ssd(xbc, dtr, conv0, ssm0, cw, cb, dtb, alog, dsk, L, valid):
    nb, t = xbc.shape[0], xbc.shape[1]
    nc = t // L
    st_spec = pl.BlockSpec((1, N_SSM_HEADS, SSM_HEAD_DIM, D_STATE), lambda b, c: (b, 0, 0, 0))
    return pl.pallas_call(
        functools.partial(_ssd_body, L=L, valid=valid),
        grid=(nb, nc),
        in_specs=[pl.BlockSpec((1, L, CONV_DIM), lambda b, c: (b, c, 0)),
                  pl.BlockSpec((1, L, LANES), lambda b, c: (b, c, 0)),
                  pl.BlockSpec((1, SUBLANES, CONV_DIM), lambda b, c: (b, 0, 0)),
                  st_spec,
                  _const_spec(cw.shape), _const_spec(cb.shape), _const_spec(dtb.shape),
                  _const_spec(alog.shape), _const_spec(dsk.shape)],
        out_specs=[pl.BlockSpec((1, L, D_INNER), lambda b, c: (b, c, 0)), st_spec],
        out_shape=[jax.ShapeDtypeStruct((nb, t, D_INNER), F32),
                   jax.ShapeDtypeStruct(ssm0.shape, F32)],
        scratch_shapes=[pltpu.VMEM((L + SUBLANES, CONV_DIM), F32)],
        compiler_params=_params(("parallel", "arbitrary")),
        name="ssd_scan",
    )(xbc, dtr, conv0, ssm0, cw, cb, dtb, alog, dsk)


def _ssd_t(xbc, dtr, conv0, ssm0, cw, cb, dtb, alog, dsk):
    nb, t = xbc.shape[0], xbc.shape[1]
    L = SSD_CHUNK
    n_sub = SSD_CHUNKS_PER_STEP if t % (L * SSD_CHUNKS_PER_STEP) == 0 else 1
    rows = L * n_sub
    assert L == LANES and t % rows == 0
    dsk_t = jnp.broadcast_to(dsk.reshape(D_INNER, 1), (D_INNER, LANES))
    st_spec = pl.BlockSpec((1, N_SSM_HEADS, SSM_HEAD_DIM, D_STATE), lambda b, c: (b, 0, 0, 0))
    has_init = conv0 is not None
    init_args = (conv0, ssm0) if has_init else ()
    init_specs = [pl.BlockSpec((1, SUBLANES, CONV_DIM), lambda b, c: (b, 0, 0)),
                  st_spec] if has_init else []
    return pl.pallas_call(
        functools.partial(_ssd_t_body, L=L, n_sub=n_sub, has_init=has_init),
        grid=(nb, t // rows),
        in_specs=[pl.BlockSpec((1, rows, CONV_DIM), lambda b, c: (b, c, 0)),
                  pl.BlockSpec((1, rows, LANES), lambda b, c: (b, c, 0))] + init_specs + [
                  _const_spec(cw.shape), _const_spec(cb.shape),
                  _const_spec(dtb.shape), _const_spec(alog.shape), _const_spec(dsk_t.shape)],
        out_specs=[pl.BlockSpec((1, rows, D_INNER), lambda b, c: (b, c, 0)), st_spec],
        out_shape=[jax.ShapeDtypeStruct((nb, t, D_INNER), F32),
                   jax.ShapeDtypeStruct((nb, N_SSM_HEADS, SSM_HEAD_DIM, D_STATE), F32)],
        scratch_shapes=[pltpu.VMEM((L + SUBLANES, CONV_DIM), F32), pltpu.VMEM((D_INNER, L), F32)],
        compiler_params=_params(("parallel", "arbitrary")),
        name="ssd_scan_t",
    )(xbc, dtr, *init_args, cw, cb, dtb, alog, dsk_t)


def _ffn(x1, nf_ref, wgu_ref, wd_ref):
    h = (x1 * _inv_rms(x1) * nf_ref[...]).astype(BF16)
    out = x1
    for c in range(D_FF // FFN_CHUNK):
        lo = c * FFN_CHUNK
        gate = jnp.dot(h, wgu_ref[0, :, lo:lo + FFN_CHUNK], preferred_element_type=F32)
        up = jnp.dot(h, wgu_ref[0, :, D_FF + lo:D_FF + lo + FFN_CHUNK], preferred_element_type=F32)
        act = (_silu(gate) * up).astype(BF16)
        out = out + jnp.dot(act, wd_ref[0, lo:lo + FFN_CHUNK, :], preferred_element_type=F32)
    return out


def _ssm_out_ffn_body(y_ref, z_ref, x_ref, gn_ref, wo_ref, nf_ref, wgu_ref, wd_ref, o_ref):
    y = y_ref[...] * _silu(z_ref[...])
    gw = D_INNER // N_SSM_GROUPS
    parts = []
    for g in range(N_SSM_GROUPS):
        yg = y[:, g * gw:(g + 1) * gw]
        parts.append(yg * _inv_rms(yg))
    y = (jnp.concatenate(parts, axis=1) * gn_ref[...]).astype(BF16)
    x1 = x_ref[...] + jnp.dot(y, wo_ref[...], preferred_element_type=F32)
    o_ref[...] = _ffn(x1, nf_ref, wgu_ref, wd_ref)


def _attn_out_ffn_body(a_ref, x_ref, wo_ref, nf_ref, wgu_ref, wd_ref, nfin_ref, o_ref):
    x1 = x_ref[...] + jnp.dot(a_ref[...], wo_ref[...], preferred_element_type=F32)
    x2 = _ffn(x1, nf_ref, wgu_ref, wd_ref)
    o_ref[...] = x2 * _inv_rms(x2) * nfin_ref[...]


def _ssm_out_ffn(y, z, x, gn, wo, nf, wgu, wd, layer, tm):
    m = x.shape[0]
    row = lambda w: pl.BlockSpec((tm, w), lambda i: (i, 0))
    return pl.pallas_call(
        _ssm_out_ffn_body,
        grid=(m // tm,),
        in_specs=[row(D_INNER), row(D_INNER), row(D_MODEL), _const_spec(gn.shape),
                  _const_spec(wo.shape), _const_spec(nf.shape), _layer_spec(wgu.shape, layer),
                  _layer_spec(wd.shape, layer)],
        out_specs=row(D_MODEL),
        out_shape=jax.ShapeDtypeStruct((m, D_MODEL), F32),
        compiler_params=_params(("parallel",)),
        name="ssm_out_ffn",
    )(y, z, x, gn, wo, nf, wgu, wd)


def _attn_out_ffn(a, x, wo, nf, wgu, wd, layer, nfin, tm):
    m = x.shape[0]
    row = lambda w: pl.BlockSpec((tm, w), lambda i: (i, 0))
    return pl.pallas_call(
        _attn_out_ffn_body,
        grid=(m // tm,),
        in_specs=[row(D_MODEL), row(D_MODEL), _const_spec(wo.shape), _const_spec(nf.shape),
                  _layer_spec(wgu.shape, layer), _layer_spec(wd.shape, layer),
                  _const_spec(nfin.shape)],
        out_specs=row(D_MODEL),
        out_shape=jax.ShapeDtypeStruct((m, D_MODEL), F32),
        compiler_params=_params(("parallel",)),
        name="attn_out_ffn",
    )(a, x, wo, nf, wgu, wd, nfin)


def _rope(x, cos, sin_signed, first_half):
    outs = []
    for c in range(x.shape[1] // LANES):
        xc = x[:, c * LANES:(c + 1) * LANES]
        partner = jnp.where(first_half, pltpu.roll(xc, LANES - HEAD_DIM // 2, 1),
                            pltpu.roll(xc, HEAD_DIM // 2, 1))
        outs.append(xc * cos + partner * sin_signed)
    return jnp.concatenate(outs, axis=1)


def _kvq_compute(x_ref, gkv_ref, gq_ref, wkv_ref, wq_ref, cos_ref, sin_ref):
    x = x_ref[...]
    xn = x * _inv_rms(x)
    hk = (xn * gkv_ref[...]).astype(BF16)
    hq = (xn * gq_ref[...]).astype(BF16)
    cos = cos_ref[...]
    sin = sin_ref[...]
    lane = lax.broadcasted_iota(jnp.int32, cos.shape, 1)
    first_half = (lane % HEAD_DIM) < (HEAD_DIM // 2)
    kv = jnp.dot(hk, wkv_ref[...], preferred_element_type=F32)
    k = _rope(kv[:, :KV_DIM], cos, sin, first_half)
    q = _rope(jnp.dot(hq, wq_ref[...], preferred_element_type=F32), cos, sin, first_half)
    return k, kv[:, KV_DIM:], q


def _kvq_body(x_ref, gkv_ref, gq_ref, wkv_ref, wq_ref, cos_ref, sin_ref, k_ref, v_ref, q_ref):
    k, v, q = _kvq_compute(x_ref, gkv_ref, gq_ref, wkv_ref, wq_ref, cos_ref, sin_ref)
    k_ref[...] = k
    v_ref[...] = v
    q_ref[...] = q


def _kvq_prompt_body(x_ref, gkv_ref, gq_ref, wkv_ref, wq_ref, cos_ref, sin_ref,
                     kt_ref, vt32_ref, qt_ref, kh_ref, vt_ref, km_ref):
    k, v, q = _kvq_compute(x_ref, gkv_ref, gq_ref, wkv_ref, wq_ref, cos_ref, sin_ref)
    kt_ref[0] = k.T
    v_t = v.T
    vt32_ref[0] = v_t
    qt_ref[0] = q.T
    vt = v_t.astype(BF16)
    col = lax.broadcasted_iota(jnp.int32, (MOBA_BLOCK, LANES - HEAD_DIM), 1)
    row = lax.broadcasted_iota(jnp.int32, (V_AUG - HEAD_DIM, MOBA_BLOCK), 0)
    ones_row = jnp.where(row == 0, 1.0, 0.0).astype(BF16)
    for sub in range(KVQ_BLOCKS_PER_STEP):
        blk = pl.program_id(1) * KVQ_BLOCKS_PER_STEP + sub
        toks = slice(sub * MOBA_BLOCK, (sub + 1) * MOBA_BLOCK)
        onehot = jnp.where(col == blk, 1.0, 0.0).astype(BF16)
        for kvh in range(N_KV_HEADS):
            kh_ref[0, sub, kvh] = jnp.concatenate(
                [k[toks, kvh * HEAD_DIM:(kvh + 1) * HEAD_DIM].astype(BF16), onehot], axis=1)
            vt_ref[0, sub, kvh] = jnp.concatenate(
                [vt[kvh * HEAD_DIM:(kvh + 1) * HEAD_DIM, toks], ones_row], axis=0)
        km_ref[0, sub] = jnp.sum(k[toks, :], axis=0, keepdims=True) * (1.0 / MOBA_BLOCK)


def _kvq(x, gkv, gq, wkv, wq, cos, sin, tm):
    m = x.shape[0]
    row = lambda w: pl.BlockSpec((tm, w), lambda i: (i, 0))
    tab = pl.BlockSpec((tm, LANES), lambda i: (0, 0))
    return pl.pallas_call(
        _kvq_body,
        grid=(m // tm,),
        in_specs=[row(D_MODEL), _const_spec(gkv.shape), _const_spec(gq.shape),
                  _const_spec(wkv.shape), _const_spec(wq.shape), tab, tab],
        out_specs=[row(KV_DIM), row(KV_DIM), row(D_MODEL)],
        out_shape=[jax.ShapeDtypeStruct((m, KV_DIM), F32), jax.ShapeDtypeStruct((m, KV_DIM), F32),
                   jax.ShapeDtypeStruct((m, D_MODEL), F32)],
        compiler_params=_params(("parallel",)),
        name="kvq_proj",
    )(x, gkv, gq, wkv, wq, cos, sin)


def _kvq_prompt(x, b, t, gkv, gq, wkv, wq, cos, sin):
    nb = t // MOBA_BLOCK
    per = KVQ_BLOCKS_PER_STEP
    assert nb % per == 0
    steps = nb // per
    tm = per * MOBA_BLOCK
    row = lambda w: pl.BlockSpec((tm, w), lambda bi, i: (bi * steps + i, 0))
    tab = pl.BlockSpec((tm, LANES), lambda bi, i: (i, 0))
    return pl.pallas_call(
        _kvq_prompt_body,
        grid=(b, steps),
        in_specs=[row(D_MODEL), _const_spec(gkv.shape), _const_spec(gq.shape),
                  _const_spec(wkv.shape), _const_spec(wq.shape), tab, tab],
        out_specs=[pl.BlockSpec((1, KV_DIM, tm), lambda bi, i: (bi, 0, i)),
                   pl.BlockSpec((1, KV_DIM, tm), lambda bi, i: (bi, 0, i)),
                   pl.BlockSpec((1, D_MODEL, tm), lambda bi, i: (bi, 0, i)),
                   pl.BlockSpec((1, per, N_KV_HEADS, MOBA_BLOCK, LANES),
                                lambda bi, i: (bi, i, 0, 0, 0)),
                   pl.BlockSpec((1, per, N_KV_HEADS, V_AUG, MOBA_BLOCK),
                                lambda bi, i: (bi, i, 0, 0, 0)),
                   pl.BlockSpec((1, per, 1, KV_DIM), lambda bi, i: (bi, i, 0, 0))],
        out_shape=[jax.ShapeDtypeStruct((b, KV_DIM, t), F32),
                   jax.ShapeDtypeStruct((b, KV_DIM, t), F32),
                   jax.ShapeDtypeStruct((b, D_MODEL, t), F32),
                   jax.ShapeDtypeStruct((b, nb, N_KV_HEADS, MOBA_BLOCK, LANES), BF16),
                   jax.ShapeDtypeStruct((b, nb, N_KV_HEADS, V_AUG, MOBA_BLOCK), BF16),
                   jax.ShapeDtypeStruct((b, nb, 1, KV_DIM), F32)],
        compiler_params=_params(("parallel", "parallel")),
        name="kvq_proj_prompt",
    )(x, gkv, gq, wkv, wq, cos, sin)


def _moba_prompt_body(qt_ref, kh_ref, vt_ref, km_ref, o_ref, s_a, s_b, acc_a, acc_b, *, nb):
    i = pl.program_id(1)
    g_per_unit = ATTN_UNIT_HEADS
    rows = g_per_unit * MOBA_BLOCK
    n_units = N_HEADS // g_per_unit
    scale = (HEAD_DIM ** -0.5) * LOG2_E
    blk_row = lax.broadcasted_iota(jnp.int32, (nb, rows), 0)
    key_i = lax.broadcasted_iota(jnp.int32, (MOBA_BLOCK, rows), 0)
    tok_i = lax.broadcasted_iota(jnp.int32, (MOBA_BLOCK, rows), 1) % MOBA_BLOCK
    own_ok = key_i <= tok_i
    fold = lambda s: jnp.max(s.reshape(MOBA_BLOCK // SUBLANES, SUBLANES, rows), axis=0)

    def setup(u):
        kvh = (u * g_per_unit) // Q_PER_KV
        lo = kvh * HEAD_DIM
        base = u * g_per_unit * HEAD_DIM
        qt2 = qt_ref[0, base:base + g_per_unit * HEAD_DIM, :]
        qt = jnp.concatenate([qt2[g * HEAD_DIM:(g + 1) * HEAD_DIM, :] for g in range(g_per_unit)],
                             axis=1)
        km = km_ref[0, :, lo:lo + HEAD_DIM]
        gate = jnp.dot(km, qt, preferred_element_type=F32,
                       precision=lax.Precision.HIGHEST)
        sel = _top3_mask(gate, blk_row < i, axis=0)
        pen = jnp.where(sel, 0.0, MASKED).astype(BF16)
        qb = (qt * scale).astype(BF16)
        q_own = jnp.concatenate([qb, jnp.zeros((LANES - HEAD_DIM, rows), BF16)], axis=0)
        q_aug = jnp.concatenate([qb, pen, jnp.zeros((LANES - HEAD_DIM - nb, rows), BF16)], axis=0)
        s_own = jnp.dot(kh_ref[0, i, kvh], q_own, preferred_element_type=F32)
        s_own = jnp.where(own_ok, s_own, MASKED)
        s_bufs[u % 2][nb] = s_own
        return kvh, q_aug, fold(s_own)

    def scores(u, kvh, q_aug, j0, n, m8):
        for t in range(n):
            s = jnp.dot(kh_ref[0, j0 + t, kvh], q_aug, preferred_element_type=F32)
            s_bufs[u % 2][j0 + t] = s
            m8 = jnp.maximum(m8, fold(s))
        return m8

    def weighted(u, kvh, m, j0, n):
        s_buf = s_bufs[u % 2]
        p = jnp.concatenate([jnp.exp2(s_buf[j0 + t] - m).astype(BF16) for t in range(n)], axis=0)
        v = jnp.concatenate([vt_ref[0, j0 + t, kvh] for t in range(n)], axis=1)
        acc_bufs[u % 2][...] += jnp.dot(v, p, preferred_element_type=F32)

    def over_blocks(step, carry):
        n4, n2 = i // 4, i // 2
        carry = lax.fori_loop(0, n4, lambda q, c: step(4 * q + 2, 2, step(4 * q, 2, c)), carry)
        carry = lax.fori_loop(2 * n4, n2, lambda q, c: step(2 * q, 2, c), carry)
        return lax.fori_loop(2 * n2, i, lambda j, c: step(j, 1, c), carry)

    s_bufs = (s_a, s_b)
    acc_bufs = (acc_a, acc_b)
    kvh, q_aug, m8 = setup(0)
    m8 = over_blocks(functools.partial(scores, 0, kvh, q_aug), m8)
    for u in range(n_units):
        m = jnp.max(m8, axis=0, keepdims=True)
        p_own = jnp.exp2(s_bufs[u % 2][nb] - m).astype(BF16)
        acc_bufs[u % 2][...] = jnp.dot(vt_ref[0, i, kvh], p_own, preferred_element_type=F32)
        if u + 1 < n_units:
            kvh_n, q_aug_n, m8_n = setup(u + 1)

            def both(j0, n, m8c, u=u, kvh=kvh, m=m, kvh_n=kvh_n, q_aug_n=q_aug_n):
                m8c = scores(u + 1, kvh_n, q_aug_n, j0, n, m8c)
                weighted(u, kvh, m, j0, n)
                return m8c

            m8_n = over_blocks(both, m8_n)
        else:
            def last(j0, n, carry, u=u, kvh=kvh, m=m):
                weighted(u, kvh, m, j0, n)
                return carry

            over_blocks(last, 0)
        acc = acc_bufs[u % 2][...]
        out_t = acc[:HEAD_DIM, :] / acc[HEAD_DIM:HEAD_DIM + 1, :]
        out2 = jnp.concatenate([out_t[:, g * MOBA_BLOCK:(g + 1) * MOBA_BLOCK]
                                for g in range(g_per_unit)], axis=0)
        base = u * g_per_unit * HEAD_DIM
        o_ref[0, :, base:base + g_per_unit * HEAD_DIM] = out2.T.astype(o_ref.dtype)
        if u + 1 < n_units:
            kvh, m8 = kvh_n, m8_n


def _moba_prompt(qt, kh, vt, km):
    b, t = qt.shape[0], qt.shape[2]
    nb = t // MOBA_BLOCK
    rows = ATTN_UNIT_HEADS * MOBA_BLOCK
    return pl.pallas_call(
        functools.partial(_moba_prompt_body, nb=nb),
        grid=(b, nb),
        in_specs=[pl.BlockSpec((1, D_MODEL, MOBA_BLOCK), lambda bi, i: (bi, 0, i)),
                  pl.BlockSpec((1, nb, N_KV_HEADS, MOBA_BLOCK, LANES),
                               lambda bi, i: (bi, 0, 0, 0, 0), pipeline_mode=pl.Buffered(1)),
                  pl.BlockSpec((1, nb, N_KV_HEADS, V_AUG, MOBA_BLOCK),
                               lambda bi, i: (bi, 0, 0, 0, 0), pipeline_mode=pl.Buffered(1)),
                  pl.BlockSpec((1, nb, KV_DIM), lambda bi, i: (bi, 0, 0))],
        out_specs=pl.BlockSpec((1, MOBA_BLOCK, D_MODEL), lambda bi, i: (bi, i, 0)),
        out_shape=jax.ShapeDtypeStruct((b, t, D_MODEL), BF16),
        scratch_shapes=[pltpu.VMEM((nb + 1, MOBA_BLOCK, rows), F32),
                        pltpu.VMEM((nb + 1, MOBA_BLOCK, rows), F32),
                        pltpu.VMEM((V_AUG, rows), F32), pltpu.VMEM((V_AUG, rows), F32)],
        compiler_params=_params(("parallel", "arbitrary")),
        name="moba_prompt",
    )(qt, kh, vt, km)


def _page_copy(cache_hbm, buf, sem, page, slot, p):
    return pltpu.make_async_copy(cache_hbm.at[page], buf.at[slot, p], sem)


def _moba_sample_body(pt_ref, q_ref, kn_ref, vn_ref, ck_hbm, cv_hbm, o_ref,
                      kbuf, vbuf, s_scr, km_scr, new_scr, sems, *, n_pages, n_new):
    b = pl.program_id(0)
    slot = b % 2
    rows = q_ref.shape[1]
    ppb = MOBA_BLOCK // PAGE_SIZE
    nb = n_pages // ppb
    scale = HEAD_DIM ** -0.5

    def fetch(seq, to_slot):
        def start_page(p, carry):
            page = pt_ref[seq, p]
            _page_copy(ck_hbm, kbuf, sems.at[0, to_slot], page, to_slot, p).start()
            _page_copy(cv_hbm, vbuf, sems.at[1, to_slot], page, to_slot, p).start()
            return carry
        lax.fori_loop(0, n_pages, start_page, 0, unroll=8)

    @pl.when(b == 0)
    def _():
        fetch(0, 0)

    @pl.when(b + 1 < pl.num_programs(0))
    def _():
        fetch(b + 1, 1 - slot)

    q64 = q_ref[0]
    qt = jnp.concatenate([q64] * N_KV_HEADS, axis=1)
    r_i = lax.broadcasted_iota(jnp.int32, qt.shape, 0)
    c_i = lax.broadcasted_iota(jnp.int32, qt.shape, 1)
    qpad = jnp.where(c_i // HEAD_DIM == r_i // (Q_PER_KV * n_new), qt, 0.0)
    qb = (qpad * scale).astype(BF16)

    def wait_k(p, carry):
        _page_copy(ck_hbm, kbuf, sems.at[0, slot], 0, slot, p).wait()
        return carry

    lax.fori_loop(0, n_pages, wait_k, 0)

    km_scr[...] = jnp.zeros(km_scr.shape, F32)
    for j in range(nb):
        kt = jnp.concatenate([kbuf[slot, ppb * j + u] for u in range(ppb)], axis=1)
        km_scr[:, j:j + 1] = jnp.sum(kt, axis=1, keepdims=True) * (1.0 / MOBA_BLOCK)
        s_scr[:, j * MOBA_BLOCK:(j + 1) * MOBA_BLOCK] = jnp.dot(
            qb, kt.astype(BF16), preferred_element_type=F32)

    gate = jnp.dot(qpad, km_scr[...], preferred_element_type=F32,
                   precision=lax.Precision.HIGHEST)
    sel = _top3_mask(gate, lax.broadcasted_iota(jnp.int32, gate.shape, 1) < nb)
    pen = jnp.where(sel, 0.0, NEG_INF)

    new_scr[...] = jnp.zeros(new_scr.shape, F32)
    new_scr[0:n_new, :] = kn_ref[0]
    s_new = lax.dot_general(qb, new_scr[...].astype(BF16), NT_DIMS, preferred_element_type=F32)
    nr = lax.broadcasted_iota(jnp.int32, s_new.shape, 0)
    ncol = lax.broadcasted_iota(jnp.int32, s_new.shape, 1)
    s_new = jnp.where(ncol <= nr % n_new, s_new, NEG_INF)

    def lane_fold(x, op):
        out = x[:, :LANES]
        for c in range(1, x.shape[1] // LANES):
            out = op(out, x[:, c * LANES:(c + 1) * LANES])
        return out

    m_part = s_new
    for j in range(nb):
        sj = s_scr[:, j * MOBA_BLOCK:(j + 1) * MOBA_BLOCK] + pen[:, j:j + 1]
        m_part = jnp.maximum(m_part, lane_fold(sj, jnp.maximum))
    m = jnp.max(m_part, axis=1, keepdims=True)

    def wait_v(p, carry):
        _page_copy(cv_hbm, vbuf, sems.at[1, slot], 0, slot, p).wait()
        return carry

    lax.fori_loop(0, n_pages, wait_v, 0)

    p_new = jnp.exp(s_new - m)
    l_part = p_new
    new_scr[0:n_new, :] = vn_ref[0]
    acc = jnp.dot(p_new.astype(BF16), new_scr[...].astype(BF16), preferred_element_type=F32)
    for j in range(nb):
        pj = jnp.exp(s_scr[:, j * MOBA_BLOCK:(j + 1) * MOBA_BLOCK] + pen[:, j:j + 1] - m)
        l_part = l_part + lane_fold(pj, jnp.add)
        vt = jnp.concatenate([vbuf[slot, ppb * j + u] for u in range(ppb)], axis=1).astype(BF16)
        acc = acc + lax.dot_general(pj.astype(BF16), vt, NT_DIMS, preferred_element_type=F32)
    acc = acc / jnp.sum(l_part, axis=1, keepdims=True)
    ro = lax.broadcasted_iota(jnp.int32, (rows, HEAD_DIM), 0) // (Q_PER_KV * n_new)
    out = jnp.zeros((rows, HEAD_DIM), F32)
    for kvh in range(N_KV_HEADS):
        out = jnp.where(ro == kvh, acc[:, kvh * HEAD_DIM:(kvh + 1) * HEAD_DIM], out)
    o_ref[0] = out


def _moba_sample(page_table, q_rows, k_new, v_new, cache_k, cache_v):
    nseq, n_pages = page_table.shape
    rows = q_rows.shape[1]
    n_new = k_new.shape[1]
    past = n_pages * PAGE_SIZE
    grid_spec = pltpu.PrefetchScalarGridSpec(
        num_scalar_prefetch=1,
        grid=(nseq,),
        in_specs=[pl.BlockSpec((1, rows, HEAD_DIM), lambda b, pt: (b, 0, 0)),
                  pl.BlockSpec((1, n_new, KV_DIM), lambda b, pt: (b, 0, 0)),
                  pl.BlockSpec((1, n_new, KV_DIM), lambda b, pt: (b, 0, 0)),
                  pl.BlockSpec(memory_space=pl.ANY),
                  pl.BlockSpec(memory_space=pl.ANY)],
        out_specs=pl.BlockSpec((1, rows, HEAD_DIM), lambda b, pt: (b, 0, 0)),
        scratch_shapes=[pltpu.VMEM((2, n_pages, KV_DIM, PAGE_SIZE), F32),
                        pltpu.VMEM((2, n_pages, KV_DIM, PAGE_SIZE), F32),
                        pltpu.VMEM((rows, past), F32),
                        pltpu.VMEM((KV_DIM, LANES), F32),
                        pltpu.VMEM((LANES, KV_DIM), F32),
                        pltpu.SemaphoreType.DMA((2, 2))],
    )
    return pl.pallas_call(
        functools.partial(_moba_sample_body, n_pages=n_pages, n_new=n_new),
        grid_spec=grid_spec,
        out_shape=jax.ShapeDtypeStruct((nseq, rows, HEAD_DIM), F32),
        compiler_params=_params(("arbitrary",)),
        name="moba_sample",
    )(page_table, q_rows, k_new, v_new, cache_k, cache_v)


def _rope_tables(pos):
    half = HEAD_DIM // 2
    inv = ROPE_THETA ** (-np.arange(half, dtype=np.float64) / half)
    ang = np.asarray(pos, dtype=np.float64)[:, None] * inv[None, :]
    cos, sin = np.cos(ang), np.sin(ang)
    cos_h = np.concatenate([cos, cos], axis=1)
    sin_h = np.concatenate([-sin, sin], axis=1)
    reps = LANES // HEAD_DIM
    return (jnp.asarray(np.tile(cos_h, (1, reps)), F32), jnp.asarray(np.tile(sin_h, (1, reps)), F32))


def _prep_weights(norm_mix, norm_ffn, w_in_ssm, conv_w, conv_b, dt_bias, a_log, d_skip, norm_ssm,
                  w_out_ssm, norm_kv, w_kv, w_q, w_o, w_gu, w_down, norm_final):
    pad_h = LANES - N_SSM_HEADS
    return dict(
        g_mix0=norm_mix[0][None], g_mix1=norm_mix[1][None],
        g_ffn0=norm_ffn[0][None], g_ffn1=norm_ffn[1][None],
        w_in=w_in_ssm.astype(BF16),
        cw=conv_w[0], cb=conv_b[0][None],
        dtb=jnp.pad(dt_bias[0], (0, pad_h))[None], alog=jnp.pad(a_log[0], (0, pad_h))[None],
        dsk=jnp.repeat(d_skip[0], SSM_HEAD_DIM)[None],
        gn=norm_ssm[0][None], wo_ssm=w_out_ssm[0].astype(BF16),
        g_kv=norm_kv[None], wkv=w_kv.astype(BF16), wq=w_q[0].astype(BF16),
        wo=w_o[0].astype(BF16),
        wgu=w_gu.astype(BF16), wd=w_down.astype(BF16),
        g_fin=norm_final[None],
    )


def _ssd_layer_long(x, conv_in, ssm0, w, tm):
    b, t, _ = x.shape
    keep = D_CONV - 1
    assert t >= keep
    xf = x.reshape(b * t, D_MODEL)
    z, xbc, dtr = _inproj(xf, w["g_mix0"], w["w_in"], 0, tm)
    xbc3 = xbc.reshape(b, t, CONV_DIM)
    conv0 = None if conv_in is None else jnp.pad(conv_in, ((0, 0), (SUBLANES - keep, 0), (0, 0)))
    y, ssm_new = _ssd_t(xbc3, dtr.reshape(b, t, LANES), conv0, ssm0, w["cw"], w["cb"], w["dtb"],
                        w["alog"], w["dsk"])
    x2 = _ssm_out_ffn(y.reshape(b * t, D_INNER), z, xf, w["gn"], w["wo_ssm"], w["g_ffn0"],
                      w["wgu"], w["wd"], 0, tm)
    return x2, xbc3[:, t - keep:], ssm_new


def _ssd_layer(x, conv_in, ssm0, w, tm, L, valid):
    b, t, _ = x.shape
    xf = x.reshape(b * t, D_MODEL)
    z, xbc, dtr = _inproj(xf, w["g_mix0"], w["w_in"], 0, tm)
    tp = -(-t // L) * L
    xbc3 = xbc.reshape(b, t, CONV_DIM)
    dtr3 = dtr.reshape(b, t, LANES)
    if tp != t:
        xbc3 = jnp.pad(xbc3, ((0, 0), (0, tp - t), (0, 0)))
        dtr3 = jnp.pad(dtr3, ((0, 0), (0, tp - t), (0, 0)))
    conv0 = jnp.pad(conv_in, ((0, 0), (SUBLANES - (D_CONV - 1), 0), (0, 0)))
    y, ssm_new = _ssd(xbc3, dtr3, conv0, ssm0, w["cw"], w["cb"], w["dtb"], w["alog"], w["dsk"],
                      L, valid)
    y = y[:, :t].reshape(b * t, D_INNER)
    x2 = _ssm_out_ffn(y, z, xf, w["gn"], w["wo_ssm"], w["g_ffn0"], w["wgu"], w["wd"], 0, tm)
    keep = D_CONV - 1
    conv_new = jnp.concatenate([conv_in[:, t:], xbc.reshape(b, t, CONV_DIM)[:, max(0, t - keep):]],
                               axis=1)
    return x2, conv_new, ssm_new


def kernel(x_prompt, x_sample, state_conv, state_ssm, cache_k, cache_v, page_table, norm_mix,
           norm_ffn, w_in_ssm, conv_w, conv_b, dt_bias, a_log, d_skip, norm_ssm, w_out_ssm,
           norm_kv, w_kv, w_q, w_o, w_gu, w_down, norm_final):
    w = _prep_weights(norm_mix, norm_ffn, w_in_ssm, conv_w, conv_b, dt_bias, a_log, d_skip,
                      norm_ssm, w_out_ssm, norm_kv, w_kv, w_q, w_o, w_gu, w_down, norm_final)
    bp, tp, _ = x_prompt.shape
    bs, ts, _ = x_sample.shape
    past_len = page_table.shape[1] * PAGE_SIZE

    tm_p = PROMPT_ROW_TILE
    x2_p, conv_p, ssm_p = _ssd_layer_long(x_prompt, None, None, w, tm_p)
    cos_p, sin_p = _rope_tables(np.arange(tp))
    kt_p, vt32_p, qt_p, kh_p, vt_p, km_p = _kvq_prompt(x2_p, bp, tp, w["g_kv"], w["g_mix1"],
                                                       w["wkv"], w["wq"], cos_p, sin_p)
    k_p = kt_p.reshape(bp, N_KV_HEADS, HEAD_DIM, tp).transpose(0, 3, 1, 2)
    v_p = vt32_p.reshape(bp, N_KV_HEADS, HEAD_DIM, tp).transpose(0, 3, 1, 2)
    attn_p = _moba_prompt(qt_p, kh_p, vt_p, km_p.reshape(bp, tp // MOBA_BLOCK, KV_DIM))
    y_p = _attn_out_ffn(attn_p.reshape(bp * tp, D_MODEL), x2_p, w["wo"], w["g_ffn1"], w["wgu"],
                        w["wd"], 1, w["g_fin"], tm_p)

    tm_s = bs * ts
    x2_s, conv_s, ssm_s = _ssd_layer(x_sample, state_conv[0], state_ssm[0], w, tm_s, SUBLANES, ts)
    cos_s, sin_s = _rope_tables(past_len + np.tile(np.arange(ts), bs))
    k_s, v_s, q_s = _kvq(x2_s, w["g_kv"], w["g_mix1"], w["wkv"], w["wq"], cos_s, sin_s, tm_s)
    q_rows = q_s.reshape(bs, ts, N_HEADS, HEAD_DIM).transpose(0, 2, 1, 3).reshape(
        bs, N_HEADS * ts, HEAD_DIM)
    attn_rows = _moba_sample(page_table, q_rows, k_s.reshape(bs, ts, KV_DIM),
                             v_s.reshape(bs, ts, KV_DIM),
                             cache_k.transpose(0, 2, 3, 1).reshape(-1, KV_DIM, PAGE_SIZE),
                             cache_v.transpose(0, 2, 3, 1).reshape(-1, KV_DIM, PAGE_SIZE))
    attn_s = attn_rows.reshape(bs, N_HEADS, ts, HEAD_DIM).transpose(
        0, 2, 1, 3).reshape(bs * ts, D_MODEL).astype(BF16)
    y_s = _attn_out_ffn(attn_s, x2_s, w["wo"], w["g_ffn1"], w["wgu"], w["wd"], 1,
                        w["g_fin"], tm_s)

    return (y_p.reshape(bp, tp, D_MODEL), y_s.reshape(bs, ts, D_MODEL),
            conv_p[None], ssm_p[None],
            k_p, v_p,
            conv_s[None], ssm_s[None],
            k_s.reshape(bs, ts, N_KV_HEADS, HEAD_DIM), v_s.reshape(bs, ts, N_KV_HEADS, HEAD_DIM))
```

```python
import functools

import jax
import jax.numpy as jnp
import numpy as np
from jax import lax
from jax.experimental import pallas as pl
from jax.experimental.pallas import tpu as pltpu

F32 = jnp.float32
BF16 = jnp.bfloat16

D_MODEL = 1024
D_INNER = 2048
SSM_HEAD_DIM = 64
N_SSM_HEADS = 32
N_SSM_GROUPS = 4
HEADS_PER_GROUP = 8
D_STATE = 128
D_CONV = 4
SSD_CHUNK = 128
GN = N_SSM_GROUPS * D_STATE
CONV_DIM = D_INNER + 2 * GN
HEAD_DIM = 64
N_HEADS = 16
N_KV_HEADS = 4
Q_PER_KV = 4
KV_DIM = N_KV_HEADS * HEAD_DIM
MOBA_BLOCK = 256
MOBA_TOP_K = 3
ROPE_THETA = 10000.0
D_FF = 2816
EPS = 1e-6
PAGE_SIZE = 128

LANES = 128
SUBLANES = 8
VMEM_LIMIT = 56 * 1024 * 1024

NT_DIMS = (((1,), (1,)), ((), ()))
TN_DIMS = (((0,), (0,)), ((), ()))
NEG_INF = float("-inf")
MASKED = -1e30
LOG2_E = 1.4426950408889634
V_AUG = HEAD_DIM + 16
PROMPT_ROW_TILE = 512
FFN_CHUNK = 256
SSD_CHUNKS_PER_STEP = 2
KVQ_BLOCKS_PER_STEP = 4
ATTN_UNIT_HEADS = 4


def _params(sem):
    return pltpu.CompilerParams(dimension_semantics=sem, vmem_limit_bytes=VMEM_LIMIT)


def _const_spec(shape):
    nd = len(shape)
    return pl.BlockSpec(shape, lambda *_: (0,) * nd, pipeline_mode=pl.Buffered(1))


def _layer_spec(shape, layer):
    nd = len(shape)
    return pl.BlockSpec((1,) + tuple(shape[1:]), lambda *_: (layer,) + (0,) * (nd - 1),
                        pipeline_mode=pl.Buffered(1))


def _silu(x):
    return x * (1.0 / (1.0 + jnp.exp(-x)))


def _inv_rms(x):
    return lax.rsqrt(jnp.mean(x * x, axis=-1, keepdims=True) + EPS)


def _split3(x):
    hi = x.astype(BF16)
    r = x - hi.astype(F32)
    mid = r.astype(BF16)
    lo = (r - mid.astype(F32)).astype(BF16)
    return hi, mid, lo


def _dot01(a01, x, dims, a_is_lhs=True):
    out = None
    for p in _split3(x):
        t = (lax.dot_general(a01, p, dims, preferred_element_type=F32) if a_is_lhs
             else lax.dot_general(p, a01, dims, preferred_element_type=F32))
        out = t if out is None else out + t
    return out


def _top3_mask(gate, valid, axis=1):
    nb = gate.shape[axis]
    col = lax.broadcasted_iota(jnp.int32, gate.shape, axis).astype(F32)
    g = jnp.where(valid, gate, NEG_INF)
    sel = jnp.zeros(gate.shape, F32)
    for _ in range(MOBA_TOP_K):
        m = jnp.max(g, axis=axis, keepdims=True)
        idx = jnp.min(jnp.where(g == m, col, float(nb)), axis=axis, keepdims=True)
        pick = col == idx
        sel = jnp.where(pick, 1.0, sel)
        g = jnp.where(pick, NEG_INF, g)
    return jnp.logical_and(sel > 0.5, valid)


def _inproj_body(x_ref, g_ref, w_ref, z_ref, xbc_ref, dt_ref):
    x = x_ref[...]
    h = (x * _inv_rms(x) * g_ref[...]).astype(BF16)
    z_ref[...] = _silu(jnp.dot(h, w_ref[0, :, :D_INNER], preferred_element_type=F32))
    xbc_ref[...] = jnp.dot(h, w_ref[0, :, D_INNER:D_INNER + CONV_DIM], preferred_element_type=F32)
    dt = jnp.dot(h, w_ref[0, :, D_INNER + CONV_DIM:], preferred_element_type=F32)
    dt_ref[...] = jnp.concatenate(
        [dt, jnp.zeros((dt.shape[0], LANES - N_SSM_HEADS), F32)], axis=1)


def _inproj(x, g, w_in, layer, tm):
    m = x.shape[0]
    row = lambda w: pl.BlockSpec((tm, w), lambda i: (i, 0))
    return pl.pallas_call(
        _inproj_body,
        grid=(m // tm,),
        in_specs=[row(D_MODEL), _const_spec((1, D_MODEL)), _layer_spec(w_in.shape, layer)],
        out_specs=[row(D_INNER), row(CONV_DIM), row(LANES)],
        out_shape=[jax.ShapeDtypeStruct((m, D_INNER), F32),
                   jax.ShapeDtypeStruct((m, CONV_DIM), F32),
                   jax.ShapeDtypeStruct((m, LANES), F32)],
        compiler_params=_params(("parallel",)),
        name="ssd_inproj",
    )(x, g, w_in)


def _causal_conv(pad_ref, xbc, cw_ref, cb_ref, rows):
    pad_ref[SUBLANES:SUBLANES + rows, :] = xbc
    padded = pad_ref[...]
    conv = cb_ref[...]
    for i in range(D_CONV):
        shift = D_CONV - 1 - i
        tap = padded if shift == 0 else pltpu.roll(padded, shift, 0)
        conv = conv + tap[SUBLANES:SUBLANES + rows, :] * cw_ref[i:i + 1, :]
    pad_ref[0:SUBLANES, :] = pad_ref[rows:rows + SUBLANES, :]
    return conv


def _ssd_body(xbc_ref, dtr_ref, conv0_ref, ssm0_ref, cw_ref, cb_ref, dtb_ref, alog_ref, dsk_ref,
              y_ref, st_ref, pad_ref, *, L, valid):
    c = pl.program_id(1)

    @pl.when(c == 0)
    def _():
        pad_ref[0:SUBLANES, :] = conv0_ref[0]
        st_ref[0] = ssm0_ref[0]

    act = _silu(_causal_conv(pad_ref, xbc_ref[0], cw_ref, cb_ref, L))
    xs = act[:, :D_INNER]
    bm = act[:, D_INNER:D_INNER + GN].astype(BF16)
    cm = act[:, D_INNER + GN:].astype(BF16)

    t = dtr_ref[0] + dtb_ref[...]
    dt = jnp.maximum(t, 0.0) + jnp.log1p(jnp.exp(-jnp.abs(t)))
    if valid < L:
        rows = lax.broadcasted_iota(jnp.int32, dt.shape, 0)
        dt = jnp.where(rows < valid, dt, 0.0)
    a = -jnp.exp(alog_ref[...])
    dta = dt * a

    r_i = lax.broadcasted_iota(jnp.int32, (L, L), 0)
    c_i = lax.broadcasted_iota(jnp.int32, (L, L), 1)
    causal = r_i >= c_i
    tril = jnp.where(causal, 1.0, 0.0).astype(BF16)
    e_r = lax.broadcasted_iota(jnp.int32, (LANES, LANES), 0)
    e_c = lax.broadcasted_iota(jnp.int32, (LANES, LANES), 1)
    eye = jnp.where(e_r == e_c, 1.0, 0.0).astype(BF16)

    acs = _dot01(tril, dta, (((1,), (0,)), ((), ())))
    acs_t = _dot01(eye, acs, NT_DIMS)
    last = acs[L - 1:L, :]
    dec_end = jnp.exp(last - acs)
    e_acs = jnp.exp(acs)
    dec_chunk = jnp.exp(last)

    for g in range(N_SSM_GROUPS):
        bg = bm[:, g * D_STATE:(g + 1) * D_STATE]
        cg = cm[:, g * D_STATE:(g + 1) * D_STATE]
        cb = lax.dot_general(cg, bg, NT_DIMS, preferred_element_type=F32)
        for jp in range(HEADS_PER_GROUP // 2):
            pair = []
            for h in (g * HEADS_PER_GROUP + 2 * jp, g * HEADS_PER_GROUP + 2 * jp + 1):
                diff = acs[:, h:h + 1] - acs_t[h:h + 1, :]
                seg = jnp.exp(jnp.where(causal, diff, NEG_INF))
                w = (cb * seg).astype(BF16)
                xs_h = xs[:, h * SSM_HEAD_DIM:(h + 1) * SSM_HEAD_DIM]
                xd_h = xs_h * dt[:, h:h + 1]
                y_diag = jnp.dot(w, xd_h.astype(BF16), preferred_element_type=F32)
                st_h = st_ref[0, h]
                y_off = lax.dot_general(cg, st_h.astype(BF16), NT_DIMS,
                                        preferred_element_type=F32) * e_acs[:, h:h + 1]
                xdd = (xd_h * dec_end[:, h:h + 1]).astype(BF16)
                cs = lax.dot_general(xdd, bg, TN_DIMS, preferred_element_type=F32)
                st_ref[0, h] = st_h * dec_chunk[:, h:h + 1] + cs
                pair.append(y_diag + y_off
                            + dsk_ref[:, h * SSM_HEAD_DIM:(h + 1) * SSM_HEAD_DIM] * xs_h)
            lo = (g * HEADS_PER_GROUP + 2 * jp) * SSM_HEAD_DIM
            y_ref[0, :, lo:lo + 2 * SSM_HEAD_DIM] = jnp.concatenate(pair, axis=1)


def _ssd_t_body(xbc_ref, dtr_ref, *refs, L, n_sub, has_init):
    if has_init:
        conv0_ref, ssm0_ref = refs[:2]
        refs = refs[2:]
    cw_ref, cb_ref, dtb_ref, alog_ref, dsk_ref, y_ref, st_ref, pad_ref, yt_ref = refs

    @pl.when(pl.program_id(1) == 0)
    def _():
        if has_init:
            pad_ref[0:SUBLANES, :] = conv0_ref[0]
            st_ref[0] = ssm0_ref[0]
        else:
            pad_ref[0:SUBLANES, :] = jnp.zeros((SUBLANES, CONV_DIM), F32)
            st_ref[0] = jnp.zeros(st_ref.shape[1:], F32)

    def chunk(ci, carry):
        rows = pl.ds(pl.multiple_of(ci * L, L), L)
        y_ref[0, rows, :] = _ssd_t_chunk(xbc_ref[0, rows, :], dtr_ref[0, rows, :], cw_ref, cb_ref,
                                         dtb_ref, alog_ref, dsk_ref, st_ref, pad_ref, yt_ref, L)
        return carry

    lax.fori_loop(0, n_sub, chunk, 0)


def _ssd_t_chunk(xbc, dtr, cw_ref, cb_ref, dtb_ref, alog_ref, dsk_ref, st_ref, pad_ref, yt_ref, L):
    hd = SSM_HEAD_DIM
    gw = HEADS_PER_GROUP * hd
    act = _silu(_causal_conv(pad_ref, xbc, cw_ref, cb_ref, L))
    xs_t = act[:, :D_INNER].T
    bm = act[:, D_INNER:D_INNER + GN].astype(BF16)
    cm = act[:, D_INNER + GN:].astype(BF16)

    t = dtr + dtb_ref[...]
    dt = jnp.maximum(t, 0.0) + jnp.log1p(jnp.exp(-jnp.abs(t)))
    dta = dt * (-jnp.exp(alog_ref[...]))

    r_i = lax.broadcasted_iota(jnp.int32, (L, L), 0)
    c_i = lax.broadcasted_iota(jnp.int32, (L, L), 1)
    tril = jnp.where(r_i >= c_i, 1.0, 0.0).astype(BF16)
    causal_t = c_i >= r_i

    acs = _dot01(tril, dta, (((1,), (0,)), ((), ())))
    acs_t = acs.T
    dt_t = dt.T
    dec_end_t = jnp.exp(acs_t[:, L - 1:L] - acs_t)
    e_acs_t = jnp.exp(acs_t)
    dec_chunk = jnp.exp(acs[L - 1:L, :])
    scale_in = dt_t
    scale_st = dt_t * dec_end_t

    for g in range(N_SSM_GROUPS):
        bg = bm[:, g * D_STATE:(g + 1) * D_STATE]
        cg = cm[:, g * D_STATE:(g + 1) * D_STATE]
        cb_t = lax.dot_general(bg, cg, NT_DIMS, preferred_element_type=F32)
        h0 = g * HEADS_PER_GROUP
        xs_g = xs_t[g * gw:(g + 1) * gw, :]
        xd_parts, xdd_parts = [], []
        for j in range(HEADS_PER_GROUP):
            h = h0 + j
            xd = xs_g[j * hd:(j + 1) * hd, :] * scale_in[h:h + 1, :]
            xd_parts.append(xd.astype(BF16))
            xdd_parts.append((xs_g[j * hd:(j + 1) * hd, :] * scale_st[h:h + 1, :]).astype(BF16))
        st_g = st_ref[0, h0:h0 + HEADS_PER_GROUP].reshape(gw, D_STATE)
        y_off = lax.dot_general(st_g.astype(BF16), cg, NT_DIMS,
                                preferred_element_type=F32)
        cs = jnp.dot(jnp.concatenate(xdd_parts, axis=0), bg,
                     preferred_element_type=F32)
        for j in range(HEADS_PER_GROUP):
            h = h0 + j
            diff = acs_t[h:h + 1, :] - acs[:, h:h + 1]
            seg_t = jnp.exp(jnp.where(causal_t, diff, NEG_INF))
            w_t = (cb_t * seg_t).astype(BF16)
            y_diag = jnp.dot(xd_parts[j], w_t, preferred_element_type=F32)
            rows = slice(j * hd, (j + 1) * hd)
            yt_ref[h * hd:(h + 1) * hd, :] = (
                y_diag + y_off[rows, :] * e_acs_t[h:h + 1, :]
                + dsk_ref[h * hd:(h + 1) * hd, :] * xs_g[rows, :])
            st_ref[0, h] = st_g[rows, :] * dec_chunk[:, h:h + 1] + cs[rows, :]
    return yt_ref[...].T


def _ssd(xbc, dtr, conv0, ssm0, cw, cb, dtb, alog, dsk, L, valid):
    nb, t = xbc.shape[0], xbc.shape[1]
    nc = t // L
    st_spec = pl.BlockSpec((1, N_SSM_HEADS, SSM_HEAD_DIM, D_STATE), lambda b, c: (b, 0, 0, 0))
    return pl.pallas_call(
        functools.partial(_ssd_body, L=L, valid=valid),
        grid=(nb, nc),
        in_specs=[pl.BlockSpec((1, L, CONV_DIM), lambda b, c: (b, c, 0)),
                  pl.BlockSpec((1, L, LANES), lambda b, c: (b, c, 0)),
                  pl.BlockSpec((1, SUBLANES, CONV_DIM), lambda b, c: (b, 0, 0)),
                  st_spec,
                  _const_spec(cw.shape), _const_spec(cb.shape), _const_spec(dtb.shape),
                  _const_spec(alog.shape), _const_spec(dsk.shape)],
        out_specs=[pl.BlockSpec((1, L, D_INNER), lambda b, c: (b, c, 0)), st_spec],
        out_shape=[jax.ShapeDtypeStruct((nb, t, D_INNER), F32),
                   jax.ShapeDtypeStruct(ssm0.shape, F32)],
        scratch_shapes=[pltpu.VMEM((L + SUBLANES, CONV_DIM), F32)],
        compiler_params=_params(("parallel", "arbitrary")),
        name="ssd_scan",
    )(xbc, dtr, conv0, ssm0, cw, cb, dtb, alog, dsk)


def _ssd_t(xbc, dtr, conv0, ssm0, cw, cb, dtb, alog, dsk):
    nb, t = xbc.shape[0], xbc.shape[1]
    L = SSD_CHUNK
    n_sub = SSD_CHUNKS_PER_STEP if t % (L * SSD_CHUNKS_PER_STEP) == 0 else 1
    rows = L * n_sub
    assert L == LANES and t % rows == 0
    dsk_t = jnp.broadcast_to(dsk.reshape(D_INNER, 1), (D_INNER, LANES))
    st_spec = pl.BlockSpec((1, N_SSM_HEADS, SSM_HEAD_DIM, D_STATE), lambda b, c: (b, 0, 0, 0))
    has_init = conv0 is not None
    init_args = (conv0, ssm0) if has_init else ()
    init_specs = [pl.BlockSpec((1, SUBLANES, CONV_DIM), lambda b, c: (b, 0, 0)),
                  st_spec] if has_init else []
    return pl.pallas_call(
        functools.partial(_ssd_t_body, L=L, n_sub=n_sub, has_init=has_init),
        grid=(nb, t // rows),
        in_specs=[pl.BlockSpec((1, rows, CONV_DIM), lambda b, c: (b, c, 0)),
                  pl.BlockSpec((1, rows, LANES), lambda b, c: (b, c, 0))] + init_specs + [
                  _const_spec(cw.shape), _const_spec(cb.shape),
                  _const_spec(dtb.shape), _const_spec(alog.shape), _const_spec(dsk_t.shape)],
        out_specs=[pl.BlockSpec((1, rows, D_INNER), lambda b, c: (b, c, 0)), st_spec],
        out_shape=[jax.ShapeDtypeStruct((nb, t, D_INNER), F32),
                   jax.ShapeDtypeStruct((nb, N_SSM_HEADS, SSM_HEAD_DIM, D_STATE), F32)],
        scratch_shapes=[pltpu.VMEM((L + SUBLANES, CONV_DIM), F32), pltpu.VMEM((D_INNER, L), F32)],
        compiler_params=_params(("parallel", "arbitrary")),
        name="ssd_scan_t",
    )(xbc, dtr, *init_args, cw, cb, dtb, alog, dsk_t)


def _ffn(x1, nf_ref, wgu_ref, wd_ref):
    h = (x1 * _inv_rms(x1) * nf_ref[...]).astype(BF16)
    out = x1
    for c in range(D_FF // FFN_CHUNK):
        lo = c * FFN_CHUNK
        gate = jnp.dot(h, wgu_ref[0, :, lo:lo + FFN_CHUNK], preferred_element_type=F32)
        up = jnp.dot(h, wgu_ref[0, :, D_FF + lo:D_FF + lo + FFN_CHUNK], preferred_element_type=F32)
        act = (_silu(gate) * up).astype(BF16)
        out = out + jnp.dot(act, wd_ref[0, lo:lo + FFN_CHUNK, :], preferred_element_type=F32)
    return out


def _ssm_out_ffn_body(y_ref, z_ref, x_ref, gn_ref, wo_ref, nf_ref, wgu_ref, wd_ref, o_ref):
    y = y_ref[...] * z_ref[...]
    gw = D_INNER // N_SSM_GROUPS
    parts = []
    for g in range(N_SSM_GROUPS):
        yg = y[:, g * gw:(g + 1) * gw]
        parts.append(yg * _inv_rms(yg))
    y = (jnp.concatenate(parts, axis=1) * gn_ref[...]).astype(BF16)
    x1 = x_ref[...] + jnp.dot(y, wo_ref[...], preferred_element_type=F32)
    o_ref[...] = _ffn(x1, nf_ref, wgu_ref, wd_ref)


def _attn_out_ffn_body(a_ref, x_ref, wo_ref, nf_ref, wgu_ref, wd_ref, nfin_ref, o_ref):
    x1 = x_ref[...] + jnp.dot(a_ref[...], wo_ref[...], preferred_element_type=F32)
    x2 = _ffn(x1, nf_ref, wgu_ref, wd_ref)
    o_ref[...] = x2 * _inv_rms(x2) * nfin_ref[...]


def _ssm_out_ffn(y, z, x, gn, wo, nf, wgu, wd, layer, tm):
    m = x.shape[0]
    row = lambda w: pl.BlockSpec((tm, w), lambda i: (i, 0))
    return pl.pallas_call(
        _ssm_out_ffn_body,
        grid=(m // tm,),
        in_specs=[row(D_INNER), row(D_INNER), row(D_MODEL), _const_spec(gn.shape),
                  _const_spec(wo.shape), _const_spec(nf.shape), _layer_spec(wgu.shape, layer),
                  _layer_spec(wd.shape, layer)],
        out_specs=row(D_MODEL),
        out_shape=jax.ShapeDtypeStruct((m, D_MODEL), F32),
        compiler_params=_params(("parallel",)),
        name="ssm_out_ffn",
    )(y, z, x, gn, wo, nf, wgu, wd)


def _attn_out_ffn(a, x, wo, nf, wgu, wd, layer, nfin, tm):
    m = x.shape[0]
    row = lambda w: pl.BlockSpec((tm, w), lambda i: (i, 0))
    return pl.pallas_call(
        _attn_out_ffn_body,
        grid=(m // tm,),
        in_specs=[row(D_MODEL), row(D_MODEL), _const_spec(wo.shape), _const_spec(nf.shape),
                  _layer_spec(wgu.shape, layer), _layer_spec(wd.shape, layer),
                  _const_spec(nfin.shape)],
        out_specs=row(D_MODEL),
        out_shape=jax.ShapeDtypeStruct((m, D_MODEL), F32),
        compiler_params=_params(("parallel",)),
        name="attn_out_ffn",
    )(a, x, wo, nf, wgu, wd, nfin)


def _rope(x, cos, sin_signed, first_half):
    outs = []
    for c in range(x.shape[1] // LANES):
        xc = x[:, c * LANES:(c + 1) * LANES]
        partner = jnp.where(first_half, pltpu.roll(xc, LANES - HEAD_DIM // 2, 1),
                            pltpu.roll(xc, HEAD_DIM // 2, 1))
        outs.append(xc * cos + partner * sin_signed)
    return jnp.concatenate(outs, axis=1)


def _kvq_compute(x_ref, gkv_ref, gq_ref, wkv_ref, wq_ref, cos_ref, sin_ref):
    x = x_ref[...]
    xn = x * _inv_rms(x)
    hk = (xn * gkv_ref[...]).astype(BF16)
    hq = (xn * gq_ref[...]).astype(BF16)
    cos = cos_ref[...]
    sin = sin_ref[...]
    lane = lax.broadcasted_iota(jnp.int32, cos.shape, 1)
    first_half = (lane % HEAD_DIM) < (HEAD_DIM // 2)
    kv = jnp.dot(hk, wkv_ref[...], preferred_element_type=F32)
    k = _rope(kv[:, :KV_DIM], cos, sin, first_half)
    q = _rope(jnp.dot(hq, wq_ref[...], preferred_element_type=F32), cos, sin, first_half)
    return k, kv[:, KV_DIM:], q


def _kvq_body(x_ref, gkv_ref, gq_ref, wkv_ref, wq_ref, cos_ref, sin_ref, k_ref, v_ref, q_ref):
    k, v, q = _kvq_compute(x_ref, gkv_ref, gq_ref, wkv_ref, wq_ref, cos_ref, sin_ref)
    k_ref[...] = k
    v_ref[...] = v
    q_ref[...] = q


def _kvq_prompt_body(x_ref, gkv_ref, gq_ref, wkv_ref, wq_ref, cos_ref, sin_ref,
                     kt_ref, vt32_ref, qt_ref, kh_ref, vt_ref, km_ref):
    k, v, q = _kvq_compute(x_ref, gkv_ref, gq_ref, wkv_ref, wq_ref, cos_ref, sin_ref)
    kt_ref[0] = k.T
    v_t = v.T
    vt32_ref[0] = v_t
    qt_ref[0] = q.T
    vt = v_t.astype(BF16)
    col = lax.broadcasted_iota(jnp.int32, (MOBA_BLOCK, LANES - HEAD_DIM), 1)
    row = lax.broadcasted_iota(jnp.int32, (V_AUG - HEAD_DIM, MOBA_BLOCK), 0)
    ones_row = jnp.where(row == 0, 1.0, 0.0).astype(BF16)
    for sub in range(KVQ_BLOCKS_PER_STEP):
        blk = pl.program_id(1) * KVQ_BLOCKS_PER_STEP + sub
        toks = slice(sub * MOBA_BLOCK, (sub + 1) * MOBA_BLOCK)
        onehot = jnp.where(col == blk, 1.0, 0.0).astype(BF16)
        for kvh in range(N_KV_HEADS):
            kh_ref[0, sub, kvh] = jnp.concatenate(
                [k[toks, kvh * HEAD_DIM:(kvh + 1) * HEAD_DIM].astype(BF16), onehot], axis=1)
            vt_ref[0, sub, kvh] = jnp.concatenate(
                [vt[kvh * HEAD_DIM:(kvh + 1) * HEAD_DIM, toks], ones_row], axis=0)
        km_ref[0, sub] = jnp.sum(k[toks, :], axis=0, keepdims=True) * (1.0 / MOBA_BLOCK)


def _kvq(x, gkv, gq, wkv, wq, cos, sin, tm):
    m = x.shape[0]
    row = lambda w: pl.BlockSpec((tm, w), lambda i: (i, 0))
    tab = pl.BlockSpec((tm, LANES), lambda i: (0, 0))
    return pl.pallas_call(
        _kvq_body,
        grid=(m // tm,),
        in_specs=[row(D_MODEL), _const_spec(gkv.shape), _const_spec(gq.shape),
                  _const_spec(wkv.shape), _const_spec(wq.shape), tab, tab],
        out_specs=[row(KV_DIM), row(KV_DIM), row(D_MODEL)],
        out_shape=[jax.ShapeDtypeStruct((m, KV_DIM), F32), jax.ShapeDtypeStruct((m, KV_DIM), F32),
                   jax.ShapeDtypeStruct((m, D_MODEL), F32)],
        compiler_params=_params(("parallel",)),
        name="kvq_proj",
    )(x, gkv, gq, wkv, wq, cos, sin)


def _kvq_prompt(x, b, t, gkv, gq, wkv, wq, cos, sin):
    nb = t // MOBA_BLOCK
    per = KVQ_BLOCKS_PER_STEP
    assert nb % per == 0
    steps = nb // per
    tm = per * MOBA_BLOCK
    row = lambda w: pl.BlockSpec((tm, w), lambda bi, i: (bi * steps + i, 0))
    tab = pl.BlockSpec((tm, LANES), lambda bi, i: (i, 0))
    return pl.pallas_call(
        _kvq_prompt_body,
        grid=(b, steps),
        in_specs=[row(D_MODEL), _const_spec(gkv.shape), _const_spec(gq.shape),
                  _const_spec(wkv.shape), _const_spec(wq.shape), tab, tab],
        out_specs=[pl.BlockSpec((1, KV_DIM, tm), lambda bi, i: (bi, 0, i)),
                   pl.BlockSpec((1, KV_DIM, tm), lambda bi, i: (bi, 0, i)),
                   pl.BlockSpec((1, D_MODEL, tm), lambda bi, i: (bi, 0, i)),
                   pl.BlockSpec((1, per, N_KV_HEADS, MOBA_BLOCK, LANES),
                                lambda bi, i: (bi, i, 0, 0, 0)),
                   pl.BlockSpec((1, per, N_KV_HEADS, V_AUG, MOBA_BLOCK),
                                lambda bi, i: (bi, i, 0, 0, 0)),
                   pl.BlockSpec((1, per, 1, KV_DIM), lambda bi, i: (bi, i, 0, 0))],
        out_shape=[jax.ShapeDtypeStruct((b, KV_DIM, t), F32),
                   jax.ShapeDtypeStruct((b, KV_DIM, t), F32),
                   jax.ShapeDtypeStruct((b, D_MODEL, t), F32),
                   jax.ShapeDtypeStruct((b, nb, N_KV_HEADS, MOBA_BLOCK, LANES), BF16),
                   jax.ShapeDtypeStruct((b, nb, N_KV_HEADS, V_AUG, MOBA_BLOCK), BF16),
                   jax.ShapeDtypeStruct((b, nb, 1, KV_DIM), F32)],
        compiler_params=_params(("parallel", "parallel")),
        name="kvq_proj_prompt",
    )(x, gkv, gq, wkv, wq, cos, sin)


def _moba_prompt_body(qt_ref, kh_ref, vt_ref, km_ref, o_ref, s_a, s_b, acc_a, acc_b, *, nb):
    i = pl.program_id(1)
    g_per_unit = ATTN_UNIT_HEADS
    rows = g_per_unit * MOBA_BLOCK
    n_units = N_HEADS // g_per_unit
    scale = (HEAD_DIM ** -0.5) * LOG2_E
    blk_row = lax.broadcasted_iota(jnp.int32, (nb, rows), 0)
    key_i = lax.broadcasted_iota(jnp.int32, (MOBA_BLOCK, rows), 0)
    tok_i = lax.broadcasted_iota(jnp.int32, (MOBA_BLOCK, rows), 1) % MOBA_BLOCK
    own_ok = key_i <= tok_i
    fold = lambda s: jnp.max(s.reshape(MOBA_BLOCK // SUBLANES, SUBLANES, rows), axis=0)

    def setup(u):
        kvh = (u * g_per_unit) // Q_PER_KV
        lo = kvh * HEAD_DIM
        base = u * g_per_unit * HEAD_DIM
        qt2 = qt_ref[0, base:base + g_per_unit * HEAD_DIM, :]
        qt = jnp.concatenate([qt2[g * HEAD_DIM:(g + 1) * HEAD_DIM, :] for g in range(g_per_unit)],
                             axis=1)
        km = km_ref[0, :, lo:lo + HEAD_DIM]
        gate = jnp.dot(km, qt, preferred_element_type=F32,
                       precision=lax.Precision.HIGHEST)
        sel = _top3_mask(gate, blk_row < i, axis=0)
        pen = jnp.where(sel, 0.0, MASKED).astype(BF16)
        qb = (qt * scale).astype(BF16)
        q_own = jnp.concatenate([qb, jnp.zeros((LANES - HEAD_DIM, rows), BF16)], axis=0)
        q_aug = jnp.concatenate([qb, pen, jnp.zeros((LANES - HEAD_DIM - nb, rows), BF16)], axis=0)
        s_own = jnp.dot(kh_ref[0, i, kvh], q_own, preferred_element_type=F32)
        s_own = jnp.where(own_ok, s_own, MASKED)
        s_bufs[u % 2][nb] = s_own
        return kvh, q_aug, fold(s_own)

    def scores(u, kvh, q_aug, j0, n, m8):
        for t in range(n):
            s = jnp.dot(kh_ref[0, j0 + t, kvh], q_aug, preferred_element_type=F32)
            s_bufs[u % 2][j0 + t] = s
            m8 = jnp.maximum(m8, fold(s))
        return m8

    def weighted(u, kvh, m, j0, n):
        s_buf = s_bufs[u % 2]
        p = jnp.concatenate([jnp.exp2(s_buf[j0 + t] - m).astype(BF16) for t in range(n)], axis=0)
        v = jnp.concatenate([vt_ref[0, j0 + t, kvh] for t in range(n)], axis=1)
        acc_bufs[u % 2][...] += jnp.dot(v, p, preferred_element_type=F32)

    def over_blocks(step, carry):
        n4, n2 = i // 4, i // 2
        carry = lax.fori_loop(0, n4, lambda q, c: step(4 * q + 2, 2, step(4 * q, 2, c)), carry)
        carry = lax.fori_loop(2 * n4, n2, lambda q, c: step(2 * q, 2, c), carry)
        return lax.fori_loop(2 * n2, i, lambda j, c: step(j, 1, c), carry)

    s_bufs = (s_a, s_b)
    acc_bufs = (acc_a, acc_b)
    kvh, q_aug, m8 = setup(0)
    m8 = over_blocks(functools.partial(scores, 0, kvh, q_aug), m8)
    for u in range(n_units):
        m = jnp.max(m8, axis=0, keepdims=True)
        p_own = jnp.exp2(s_bufs[u % 2][nb] - m).astype(BF16)
        acc_bufs[u % 2][...] = jnp.dot(vt_ref[0, i, kvh], p_own, preferred_element_type=F32)
        if u + 1 < n_units:
            kvh_n, q_aug_n, m8_n = setup(u + 1)

            def both(j0, n, m8c, u=u, kvh=kvh, m=m, kvh_n=kvh_n, q_aug_n=q_aug_n):
                m8c = scores(u + 1, kvh_n, q_aug_n, j0, n, m8c)
                weighted(u, kvh, m, j0, n)
                return m8c

            m8_n = over_blocks(both, m8_n)
        else:
            def last(j0, n, carry, u=u, kvh=kvh, m=m):
                weighted(u, kvh, m, j0, n)
                return carry

            over_blocks(last, 0)
        acc = acc_bufs[u % 2][...]
        out_t = acc[:HEAD_DIM, :] / acc[HEAD_DIM:HEAD_DIM + 1, :]
        out2 = jnp.concatenate([out_t[:, g * MOBA_BLOCK:(g + 1) * MOBA_BLOCK]
                                for g in range(g_per_unit)], axis=0)
        base = u * g_per_unit * HEAD_DIM
        o_ref[0, :, base:base + g_per_unit * HEAD_DIM] = out2.T.astype(o_ref.dtype)
        if u + 1 < n_units:
            kvh, m8 = kvh_n, m8_n


def _moba_prompt(qt, kh, vt, km):
    b, t = qt.shape[0], qt.shape[2]
    nb = t // MOBA_BLOCK
    rows = ATTN_UNIT_HEADS * MOBA_BLOCK
    return pl.pallas_call(
        functools.partial(_moba_prompt_body, nb=nb),
        grid=(b, nb),
        in_specs=[pl.BlockSpec((1, D_MODEL, MOBA_BLOCK), lambda bi, i: (bi, 0, i)),
                  pl.BlockSpec((1, nb, N_KV_HEADS, MOBA_BLOCK, LANES),
                               lambda bi, i: (bi, 0, 0, 0, 0), pipeline_mode=pl.Buffered(1)),
                  pl.BlockSpec((1, nb, N_KV_HEADS, V_AUG, MOBA_BLOCK),
                               lambda bi, i: (bi, 0, 0, 0, 0), pipeline_mode=pl.Buffered(1)),
                  pl.BlockSpec((1, nb, KV_DIM), lambda bi, i: (bi, 0, 0))],
        out_specs=pl.BlockSpec((1, MOBA_BLOCK, D_MODEL), lambda bi, i: (bi, i, 0)),
        out_shape=jax.ShapeDtypeStruct((b, t, D_MODEL), BF16),
        scratch_shapes=[pltpu.VMEM((nb + 1, MOBA_BLOCK, rows), F32),
                        pltpu.VMEM((nb + 1, MOBA_BLOCK, rows), F32),
                        pltpu.VMEM((V_AUG, rows), F32), pltpu.VMEM((V_AUG, rows), F32)],
        compiler_params=_params(("parallel", "arbitrary")),
        name="moba_prompt",
    )(qt, kh, vt, km)


def _page_copy(cache_hbm, buf, sem, page, slot, p):
    return pltpu.make_async_copy(cache_hbm.at[page], buf.at[slot, p], sem)


def _moba_sample_body(pt_ref, q_ref, kn_ref, vn_ref, ck_hbm, cv_hbm, o_ref,
                      kbuf, vbuf, s_scr, km_scr, new_scr, sems, *, n_pages, n_new):
    b = pl.program_id(0)
    slot = b % 2
    rows = q_ref.shape[1]
    ppb = MOBA_BLOCK // PAGE_SIZE
    nb = n_pages // ppb
    scale = HEAD_DIM ** -0.5

    def fetch(seq, to_slot):
        def start_page(p, carry):
            page = pt_ref[seq, p]
            _page_copy(ck_hbm, kbuf, sems.at[0, to_slot], page, to_slot, p).start()
            _page_copy(cv_hbm, vbuf, sems.at[1, to_slot], page, to_slot, p).start()
            return carry
        lax.fori_loop(0, n_pages, start_page, 0, unroll=8)

    @pl.when(b == 0)
    def _():
        fetch(0, 0)

    @pl.when(b + 1 < pl.num_programs(0))
    def _():
        fetch(b + 1, 1 - slot)

    q64 = q_ref[0]
    qt = jnp.concatenate([q64] * N_KV_HEADS, axis=1)
    r_i = lax.broadcasted_iota(jnp.int32, qt.shape, 0)
    c_i = lax.broadcasted_iota(jnp.int32, qt.shape, 1)
    qpad = jnp.where(c_i // HEAD_DIM == r_i // (Q_PER_KV * n_new), qt, 0.0)
    qb = (qpad * scale).astype(BF16)

    def wait_k(p, carry):
        _page_copy(ck_hbm, kbuf, sems.at[0, slot], 0, slot, p).wait()
        return carry

    lax.fori_loop(0, n_pages, wait_k, 0)

    km_scr[...] = jnp.zeros(km_scr.shape, F32)
    for j in range(nb):
        kt = jnp.concatenate([kbuf[slot, ppb * j + u] for u in range(ppb)], axis=1)
        km_scr[:, j:j + 1] = jnp.sum(kt, axis=1, keepdims=True) * (1.0 / MOBA_BLOCK)
        s_scr[:, j * MOBA_BLOCK:(j + 1) * MOBA_BLOCK] = jnp.dot(
            qb, kt.astype(BF16), preferred_element_type=F32)

    gate = jnp.dot(qpad, km_scr[...], preferred_element_type=F32,
                   precision=lax.Precision.HIGHEST)
    sel = _top3_mask(gate, lax.broadcasted_iota(jnp.int32, gate.shape, 1) < nb)
    pen = jnp.where(sel, 0.0, NEG_INF)

    new_scr[...] = jnp.zeros(new_scr.shape, F32)
    new_scr[0:n_new, :] = kn_ref[0]
    s_new = lax.dot_general(qb, new_scr[...].astype(BF16), NT_DIMS, preferred_element_type=F32)
    nr = lax.broadcasted_iota(jnp.int32, s_new.shape, 0)
    ncol = lax.broadcasted_iota(jnp.int32, s_new.shape, 1)
    s_new = jnp.where(ncol <= nr % n_new, s_new, NEG_INF)

    def lane_fold(x, op):
        out = x[:, :LANES]
        for c in range(1, x.shape[1] // LANES):
            out = op(out, x[:, c * LANES:(c + 1) * LANES])
        return out

    m_part = s_new
    for j in range(nb):
        sj = s_scr[:, j * MOBA_BLOCK:(j + 1) * MOBA_BLOCK] + pen[:, j:j + 1]
        m_part = jnp.maximum(m_part, lane_fold(sj, jnp.maximum))
    m = jnp.max(m_part, axis=1, keepdims=True)

    def wait_v(p, carry):
        _page_copy(cv_hbm, vbuf, sems.at[1, slot], 0, slot, p).wait()
        return carry

    lax.fori_loop(0, n_pages, wait_v, 0)

    p_new = jnp.exp(s_new - m)
    l_part = p_new
    new_scr[0:n_new, :] = vn_ref[0]
    acc = jnp.dot(p_new.astype(BF16), new_scr[...].astype(BF16), preferred_element_type=F32)
    for j in range(nb):
        pj = jnp.exp(s_scr[:, j * MOBA_BLOCK:(j + 1) * MOBA_BLOCK] + pen[:, j:j + 1] - m)
        l_part = l_part + lane_fold(pj, jnp.add)
        vt = jnp.concatenate([vbuf[slot, ppb * j + u] for u in range(ppb)], axis=1).astype(BF16)
        acc = acc + lax.dot_general(pj.astype(BF16), vt, NT_DIMS, preferred_element_type=F32)
    acc = acc / jnp.sum(l_part, axis=1, keepdims=True)
    ro = lax.broadcasted_iota(jnp.int32, (rows, HEAD_DIM), 0) // (Q_PER_KV * n_new)
    out = jnp.zeros((rows, HEAD_DIM), F32)
    for kvh in range(N_KV_HEADS):
        out = jnp.where(ro == kvh, acc[:, kvh * HEAD_DIM:(kvh + 1) * HEAD_DIM], out)
    o_ref[0] = out


def _moba_sample(page_table, q_rows, k_new, v_new, cache_k, cache_v):
    nseq, n_pages = page_table.shape
    rows = q_rows.shape[1]
    n_new = k_new.shape[1]
    past = n_pages * PAGE_SIZE
    grid_spec = pltpu.PrefetchScalarGridSpec(
        num_scalar_prefetch=1,
        grid=(nseq,),
        in_specs=[pl.BlockSpec((1, rows, HEAD_DIM), lambda b, pt: (b, 0, 0)),
                  pl.BlockSpec((1, n_new, KV_DIM), lambda b, pt: (b, 0, 0)),
                  pl.BlockSpec((1, n_new, KV_DIM), lambda b, pt: (b, 0, 0)),
                  pl.BlockSpec(memory_space=pl.ANY),
                  pl.BlockSpec(memory_space=pl.ANY)],
        out_specs=pl.BlockSpec((1, rows, HEAD_DIM), lambda b, pt: (b, 0, 0)),
        scratch_shapes=[pltpu.VMEM((2, n_pages, KV_DIM, PAGE_SIZE), F32),
                        pltpu.VMEM((2, n_pages, KV_DIM, PAGE_SIZE), F32),
                        pltpu.VMEM((rows, past), F32),
                        pltpu.VMEM((KV_DIM, LANES), F32),
                        pltpu.VMEM((LANES, KV_DIM), F32),
                        pltpu.SemaphoreType.DMA((2, 2))],
    )
    return pl.pallas_call(
        functools.partial(_moba_sample_body, n_pages=n_pages, n_new=n_new),
        grid_spec=grid_spec,
        out_shape=jax.ShapeDtypeStruct((nseq, rows, HEAD_DIM), F32),
        compiler_params=_params(("arbitrary",)),
        name="moba_sample",
    )(page_table, q_rows, k_new, v_new, cache_k, cache_v)


def _rope_tables(pos):
    half = HEAD_DIM // 2
    inv = ROPE_THETA ** (-np.arange(half, dtype=np.float64) / half)
    ang = np.asarray(pos, dtype=np.float64)[:, None] * inv[None, :]
    cos, sin = np.cos(ang), np.sin(ang)
    cos_h = np.concatenate([cos, cos], axis=1)
    sin_h = np.concatenate([-sin, sin], axis=1)
    reps = LANES // HEAD_DIM
    return (jnp.asarray(np.tile(cos_h, (1, reps)), F32), jnp.asarray(np.tile(sin_h, (1, reps)), F32))


def _prep_weights(norm_mix, norm_ffn, w_in_ssm, conv_w, conv_b, dt_bias, a_log, d_skip, norm_ssm,
                  w_out_ssm, norm_kv, w_kv, w_q, w_o, w_gu, w_down, norm_final):
    pad_h = LANES - N_SSM_HEADS
    return dict(
        g_mix0=norm_mix[0][None], g_mix1=norm_mix[1][None],
        g_ffn0=norm_ffn[0][None], g_ffn1=norm_ffn[1][None],
        w_in=w_in_ssm.astype(BF16),
        cw=conv_w[0], cb=conv_b[0][None],
        dtb=jnp.pad(dt_bias[0], (0, pad_h))[None], alog=jnp.pad(a_log[0], (0, pad_h))[None],
        dsk=jnp.repeat(d_skip[0], SSM_HEAD_DIM)[None],
        gn=norm_ssm[0][None], wo_ssm=w_out_ssm[0].astype(BF16),
        g_kv=norm_kv[None], wkv=w_kv.astype(BF16), wq=w_q[0].astype(BF16),
        wo=w_o[0].astype(BF16),
        wgu=w_gu.astype(BF16), wd=w_down.astype(BF16),
        g_fin=norm_final[None],
    )


def _ssd_layer_long(x, conv_in, ssm0, w, tm):
    b, t, _ = x.shape
    keep = D_CONV - 1
    assert t >= keep
    xf = x.reshape(b * t, D_MODEL)
    z, xbc, dtr = _inproj(xf, w["g_mix0"], w["w_in"], 0, tm)
    xbc3 = xbc.reshape(b, t, CONV_DIM)
    conv0 = None if conv_in is None else jnp.pad(conv_in, ((0, 0), (SUBLANES - keep, 0), (0, 0)))
    y, ssm_new = _ssd_t(xbc3, dtr.reshape(b, t, LANES), conv0, ssm0, w["cw"], w["cb"], w["dtb"],
                        w["alog"], w["dsk"])
    x2 = _ssm_out_ffn(y.reshape(b * t, D_INNER), z, xf, w["gn"], w["wo_ssm"], w["g_ffn0"],
                      w["wgu"], w["wd"], 0, tm)
    return x2, xbc3[:, t - keep:], ssm_new


def _ssd_layer(x, conv_in, ssm0, w, tm, L, valid):
    b, t, _ = x.shape
    xf = x.reshape(b * t, D_MODEL)
    z, xbc, dtr = _inproj(xf, w["g_mix0"], w["w_in"], 0, tm)
    tp = -(-t // L) * L
    xbc3 = xbc.reshape(b, t, CONV_DIM)
    dtr3 = dtr.reshape(b, t, LANES)
    if tp != t:
        xbc3 = jnp.pad(xbc3, ((0, 0), (0, tp - t), (0, 0)))
        dtr3 = jnp.pad(dtr3, ((0, 0), (0, tp - t), (0, 0)))
    conv0 = jnp.pad(conv_in, ((0, 0), (SUBLANES - (D_CONV - 1), 0), (0, 0)))
    y, ssm_new = _ssd(xbc3, dtr3, conv0, ssm0, w["cw"], w["cb"], w["dtb"], w["alog"], w["dsk"],
                      L, valid)
    y = y[:, :t].reshape(b * t, D_INNER)
    x2 = _ssm_out_ffn(y, z, xf, w["gn"], w["wo_ssm"], w["g_ffn0"], w["wgu"], w["wd"], 0, tm)
    keep = D_CONV - 1
    conv_new = jnp.concatenate([conv_in[:, t:], xbc.reshape(b, t, CONV_DIM)[:, max(0, t - keep):]],
                               axis=1)
    return x2, conv_new, ssm_new


def kernel(x_prompt, x_sample, state_conv, state_ssm, cache_k, cache_v, page_table, norm_mix,
           norm_ffn, w_in_ssm, conv_w, conv_b, dt_bias, a_log, d_skip, norm_ssm, w_out_ssm,
           norm_kv, w_kv, w_q, w_o, w_gu, w_down, norm_final):
    w = _prep_weights(norm_mix, norm_ffn, w_in_ssm, conv_w, conv_b, dt_bias, a_log, d_skip,
                      norm_ssm, w_out_ssm, norm_kv, w_kv, w_q, w_o, w_gu, w_down, norm_final)
    bp, tp, _ = x_prompt.shape
    bs, ts, _ = x_sample.shape
    past_len = page_table.shape[1] * PAGE_SIZE

    tm_p = PROMPT_ROW_TILE
    x2_p, conv_p, ssm_p = _ssd_layer_long(x_prompt, None, None, w, tm_p)
    cos_p, sin_p = _rope_tables(np.arange(tp))
    kt_p, vt32_p, qt_p, kh_p, vt_p, km_p = _kvq_prompt(x2_p, bp, tp, w["g_kv"], w["g_mix1"],
                                                       w["wkv"], w["wq"], cos_p, sin_p)
    k_p = kt_p.reshape(bp, N_KV_HEADS, HEAD_DIM, tp).transpose(0, 3, 1, 2)
    v_p = vt32_p.reshape(bp, N_KV_HEADS, HEAD_DIM, tp).transpose(0, 3, 1, 2)
    attn_p = _moba_prompt(qt_p, kh_p, vt_p, km_p.reshape(bp, tp // MOBA_BLOCK, KV_DIM))
    y_p = _attn_out_ffn(attn_p.reshape(bp * tp, D_MODEL), x2_p, w["wo"], w["g_ffn1"], w["wgu"],
                        w["wd"], 1, w["g_fin"], tm_p)

    tm_s = bs * ts
    x2_s, conv_s, ssm_s = _ssd_layer(x_sample, state_conv[0], state_ssm[0], w, tm_s, SUBLANES, ts)
    cos_s, sin_s = _rope_tables(past_len + np.tile(np.arange(ts), bs))
    k_s, v_s, q_s = _kvq(x2_s, w["g_kv"], w["g_mix1"], w["wkv"], w["wq"], cos_s, sin_s, tm_s)
    q_rows = q_s.reshape(bs, ts, N_HEADS, HEAD_DIM).transpose(0, 2, 1, 3).reshape(
        bs, N_HEADS * ts, HEAD_DIM)
    attn_rows = _moba_sample(page_table, q_rows, k_s.reshape(bs, ts, KV_DIM),
                             v_s.reshape(bs, ts, KV_DIM),
                             cache_k.transpose(0, 2, 3, 1).reshape(-1, KV_DIM, PAGE_SIZE),
                             cache_v.transpose(0, 2, 3, 1).reshape(-1, KV_DIM, PAGE_SIZE))
    attn_s = attn_rows.reshape(bs, N_HEADS, ts, HEAD_DIM).transpose(
        0, 2, 1, 3).reshape(bs * ts, D_MODEL).astype(BF16)
    y_s = _attn_out_ffn(attn_s, x2_s, w["wo"], w["g_ffn1"], w["wgu"], w["wd"], 1,
                        w["g_fin"], tm_s)

    return (y_p.reshape(bp, tp, D_MODEL), y_s.reshape(bs, ts, D_MODEL),
            conv_p[None], ssm_p[None],
            k_p, v_p,
            conv_s[None], ssm_s[None],
            k_s.reshape(bs, ts, N_KV_HEADS, HEAD_DIM), v_s.reshape(bs, ts, N_KV_HEADS, HEAD_DIM))
```

```python
import functools

import jax
import jax.numpy as jnp
import numpy as np
from jax import lax
from jax.experimental import pallas as pl
from jax.experimental.pallas import tpu as pltpu

F32 = jnp.float32
BF16 = jnp.bfloat16

D_MODEL = 1024
D_INNER = 2048
SSM_HEAD_DIM = 64
N_SSM_HEADS = 32
N_SSM_GROUPS = 4
HEADS_PER_GROUP = 8
D_STATE = 128
D_CONV = 4
SSD_CHUNK = 128
GN = N_SSM_GROUPS * D_STATE
CONV_DIM = D_INNER + 2 * GN
HEAD_DIM = 64
N_HEADS = 16
N_KV_HEADS = 4
Q_PER_KV = 4
KV_DIM = N_KV_HEADS * HEAD_DIM
MOBA_BLOCK = 256
MOBA_TOP_K = 3
ROPE_THETA = 10000.0
D_FF = 2816
EPS = 1e-6
PAGE_SIZE = 128

LANES = 128
SUBLANES = 8
VMEM_LIMIT = 56 * 1024 * 1024

NT_DIMS = (((1,), (1,)), ((), ()))
TN_DIMS = (((0,), (0,)), ((), ()))
NEG_INF = float("-inf")
MASKED = -1e30
LOG2_E = 1.4426950408889634
V_AUG = HEAD_DIM + 16
PROMPT_ROW_TILE = 512
FFN_CHUNK = 256
SSD_CHUNKS_PER_STEP = 2
KVQ_BLOCKS_PER_STEP = 4
ATTN_UNIT_HEADS = 4


def _params(sem):
    return pltpu.CompilerParams(dimension_semantics=sem, vmem_limit_bytes=VMEM_LIMIT)


def _const_spec(shape):
    nd = len(shape)
    return pl.BlockSpec(shape, lambda *_: (0,) * nd, pipeline_mode=pl.Buffered(1))


def _layer_spec(shape, layer):
    nd = len(shape)
    return pl.BlockSpec((1,) + tuple(shape[1:]), lambda *_: (layer,) + (0,) * (nd - 1),
                        pipeline_mode=pl.Buffered(1))


def _silu(x):
    return x * (1.0 / (1.0 + jnp.exp(-x)))


def _inv_rms(x):
    return lax.rsqrt(jnp.mean(x * x, axis=-1, keepdims=True) + EPS)


def _split3(x):
    hi = x.astype(BF16)
    r = x - hi.astype(F32)
    mid = r.astype(BF16)
    lo = (r - mid.astype(F32)).astype(BF16)
    return hi, mid, lo


def _dot01(a01, x, dims):
    out = None
    for p in _split3(x):
        t = lax.dot_general(a01, p, dims, preferred_element_type=F32)
        out = t if out is None else out + t
    return out


def _top3_mask(gate, valid, axis=1):
    nb = gate.shape[axis]
    col = lax.broadcasted_iota(jnp.int32, gate.shape, axis).astype(F32)
    g = jnp.where(valid, gate, NEG_INF)
    sel = jnp.zeros(gate.shape, F32)
    for _ in range(MOBA_TOP_K):
        m = jnp.max(g, axis=axis, keepdims=True)
        idx = jnp.min(jnp.where(g == m, col, float(nb)), axis=axis, keepdims=True)
        pick = col == idx
        sel = jnp.where(pick, 1.0, sel)
        g = jnp.where(pick, NEG_INF, g)
    return jnp.logical_and(sel > 0.5, valid)


def _inproj_body(x_ref, g_ref, w_ref, z_ref, xbc_ref, dt_ref):
    x = x_ref[...]
    h = (x * _inv_rms(x) * g_ref[...]).astype(BF16)
    z_ref[...] = _silu(jnp.dot(h, w_ref[0, :, :D_INNER], preferred_element_type=F32))
    xbc_ref[...] = jnp.dot(h, w_ref[0, :, D_INNER:D_INNER + CONV_DIM], preferred_element_type=F32)
    dt = jnp.dot(h, w_ref[0, :, D_INNER + CONV_DIM:], preferred_element_type=F32)
    dt_ref[...] = jnp.concatenate(
        [dt, jnp.zeros((dt.shape[0], LANES - N_SSM_HEADS), F32)], axis=1)


def _inproj(x, g, w_in, layer, tm):
    m = x.shape[0]
    row = lambda w: pl.BlockSpec((tm, w), lambda i: (i, 0))
    return pl.pallas_call(
        _inproj_body,
        grid=(m // tm,),
        in_specs=[row(D_MODEL), _const_spec((1, D_MODEL)), _layer_spec(w_in.shape, layer)],
        out_specs=[row(D_INNER), row(CONV_DIM), row(LANES)],
        out_shape=[jax.ShapeDtypeStruct((m, D_INNER), F32),
                   jax.ShapeDtypeStruct((m, CONV_DIM), F32),
                   jax.ShapeDtypeStruct((m, LANES), F32)],
        compiler_params=_params(("parallel",)),
        name="ssd_inproj",
    )(x, g, w_in)


def _causal_conv(pad_ref, xbc, cw_ref, cb_ref, rows):
    pad_ref[SUBLANES:SUBLANES + rows, :] = xbc
    padded = pad_ref[...]
    conv = cb_ref[...]
    for i in range(D_CONV):
        shift = D_CONV - 1 - i
        tap = padded if shift == 0 else pltpu.roll(padded, shift, 0)
        conv = conv + tap[SUBLANES:SUBLANES + rows, :] * cw_ref[i:i + 1, :]
    pad_ref[0:SUBLANES, :] = pad_ref[rows:rows + SUBLANES, :]
    return conv


def _ssd_body(xbc_ref, dtr_ref, conv0_ref, ssm0_ref, cw_ref, cb_ref, dtb_ref, alog_ref, dsk_ref,
              y_ref, st_ref, pad_ref, *, L, valid):
    c = pl.program_id(1)

    @pl.when(c == 0)
    def _():
        pad_ref[0:SUBLANES, :] = conv0_ref[0]
        st_ref[0] = ssm0_ref[0]

    act = _silu(_causal_conv(pad_ref, xbc_ref[0], cw_ref, cb_ref, L))
    xs = act[:, :D_INNER]
    bm = act[:, D_INNER:D_INNER + GN].astype(BF16)
    cm = act[:, D_INNER + GN:].astype(BF16)

    t = dtr_ref[0] + dtb_ref[...]
    dt = jnp.maximum(t, 0.0) + jnp.log1p(jnp.exp(-jnp.abs(t)))
    if valid < L:
        rows = lax.broadcasted_iota(jnp.int32, dt.shape, 0)
        dt = jnp.where(rows < valid, dt, 0.0)
    a = -jnp.exp(alog_ref[...])
    dta = dt * a

    r_i = lax.broadcasted_iota(jnp.int32, (L, L), 0)
    c_i = lax.broadcasted_iota(jnp.int32, (L, L), 1)
    causal = r_i >= c_i
    tril = jnp.where(causal, 1.0, 0.0).astype(BF16)
    e_r = lax.broadcasted_iota(jnp.int32, (LANES, LANES), 0)
    e_c = lax.broadcasted_iota(jnp.int32, (LANES, LANES), 1)
    eye = jnp.where(e_r == e_c, 1.0, 0.0).astype(BF16)

    acs = _dot01(tril, dta, (((1,), (0,)), ((), ())))
    acs_t = _dot01(eye, acs, NT_DIMS)
    last = acs[L - 1:L, :]
    dec_end = jnp.exp(last - acs)
    e_acs = jnp.exp(acs)
    dec_chunk = jnp.exp(last)

    for g in range(N_SSM_GROUPS):
        bg = bm[:, g * D_STATE:(g + 1) * D_STATE]
        cg = cm[:, g * D_STATE:(g + 1) * D_STATE]
        cb = lax.dot_general(cg, bg, NT_DIMS, preferred_element_type=F32)
        for jp in range(HEADS_PER_GROUP // 2):
            pair = []
            for h in (g * HEADS_PER_GROUP + 2 * jp, g * HEADS_PER_GROUP + 2 * jp + 1):
                diff = acs[:, h:h + 1] - acs_t[h:h + 1, :]
                seg = jnp.exp(jnp.where(causal, diff, NEG_INF))
                w = (cb * seg).astype(BF16)
                xs_h = xs[:, h * SSM_HEAD_DIM:(h + 1) * SSM_HEAD_DIM]
                xd_h = xs_h * dt[:, h:h + 1]
                y_diag = jnp.dot(w, xd_h.astype(BF16), preferred_element_type=F32)
                st_h = st_ref[0, h]
                y_off = lax.dot_general(cg, st_h.astype(BF16), NT_DIMS,
                                        preferred_element_type=F32) * e_acs[:, h:h + 1]
                xdd = (xd_h * dec_end[:, h:h + 1]).astype(BF16)
                cs = lax.dot_general(xdd, bg, TN_DIMS, preferred_element_type=F32)
                st_ref[0, h] = st_h * dec_chunk[:, h:h + 1] + cs
                pair.append(y_diag + y_off
                            + dsk_ref[:, h * SSM_HEAD_DIM:(h + 1) * SSM_HEAD_DIM] * xs_h)
            lo = (g * HEADS_PER_GROUP + 2 * jp) * SSM_HEAD_DIM
            y_ref[0, :, lo:lo + 2 * SSM_HEAD_DIM] = jnp.concatenate(pair, axis=1)


def _ssd_t_body(xbc_ref, dtr_ref, *refs, L, n_sub, has_init):
    if has_init:
        conv0_ref, ssm0_ref = refs[:2]
        refs = refs[2:]
    cw_ref, cb_ref, dtb_ref, alog_ref, dsk_ref, y_ref, st_ref, pad_ref, yt_ref = refs

    @pl.when(pl.program_id(1) == 0)
    def _():
        if has_init:
            pad_ref[0:SUBLANES, :] = conv0_ref[0]
            st_ref[0] = ssm0_ref[0]
        else:
            pad_ref[0:SUBLANES, :] = jnp.zeros((SUBLANES, CONV_DIM), F32)
            st_ref[0] = jnp.zeros(st_ref.shape[1:], F32)

    def chunk(ci, carry):
        rows = pl.ds(pl.multiple_of(ci * L, L), L)
        y_ref[0, rows, :] = _ssd_t_chunk(xbc_ref[0, rows, :], dtr_ref[0, rows, :], cw_ref, cb_ref,
                                         dtb_ref, alog_ref, dsk_ref, st_ref, pad_ref, yt_ref, L)
        return carry

    lax.fori_loop(0, n_sub, chunk, 0)


def _ssd_t_chunk(xbc, dtr, cw_ref, cb_ref, dtb_ref, alog_ref, dsk_ref, st_ref, pad_ref, yt_ref, L):
    hd = SSM_HEAD_DIM
    gw = HEADS_PER_GROUP * hd
    act = _silu(_causal_conv(pad_ref, xbc, cw_ref, cb_ref, L))
    xs_t = act[:, :D_INNER].T
    bm = act[:, D_INNER:D_INNER + GN].astype(BF16)
    cm = act[:, D_INNER + GN:].astype(BF16)

    t = dtr + dtb_ref[...]
    dt = jnp.maximum(t, 0.0) + jnp.log1p(jnp.exp(-jnp.abs(t)))
    dta = dt * (-jnp.exp(alog_ref[...]))

    r_i = lax.broadcasted_iota(jnp.int32, (L, L), 0)
    c_i = lax.broadcasted_iota(jnp.int32, (L, L), 1)
    tril = jnp.where(r_i >= c_i, 1.0, 0.0).astype(BF16)
    causal_t = c_i >= r_i

    acs = _dot01(tril, dta, (((1,), (0,)), ((), ())))
    acs_t = acs.T
    dt_t = dt.T
    dec_end_t = jnp.exp(acs_t[:, L - 1:L] - acs_t)
    e_acs_t = jnp.exp(acs_t)
    dec_chunk = jnp.exp(acs[L - 1:L, :])
    scale_in = dt_t
    scale_st = dt_t * dec_end_t

    for g in range(N_SSM_GROUPS):
        bg = bm[:, g * D_STATE:(g + 1) * D_STATE]
        cg = cm[:, g * D_STATE:(g + 1) * D_STATE]
        cb_t = lax.dot_general(bg, cg, NT_DIMS, preferred_element_type=F32)
        h0 = g * HEADS_PER_GROUP
        xs_g = xs_t[g * gw:(g + 1) * gw, :]
        xd_parts, xdd_parts = [], []
        for j in range(HEADS_PER_GROUP):
            h = h0 + j
            xd = xs_g[j * hd:(j + 1) * hd, :] * scale_in[h:h + 1, :]
            xd_parts.append(xd.astype(BF16))
            xdd_parts.append((xs_g[j * hd:(j + 1) * hd, :] * scale_st[h:h + 1, :]).astype(BF16))
        st_g = st_ref[0, h0:h0 + HEADS_PER_GROUP].reshape(gw, D_STATE)
        y_off = lax.dot_general(st_g.astype(BF16), cg, NT_DIMS,
                                preferred_element_type=F32)
        cs = jnp.dot(jnp.concatenate(xdd_parts, axis=0), bg,
                     preferred_element_type=F32)
        for j in range(HEADS_PER_GROUP):
            h = h0 + j
            diff = acs_t[h:h + 1, :] - acs[:, h:h + 1]
            seg_t = jnp.exp(jnp.where(causal_t, diff, NEG_INF))
            w_t = (cb_t * seg_t).astype(BF16)
            y_diag = jnp.dot(xd_parts[j], w_t, preferred_element_type=F32)
            rows = slice(j * hd, (j + 1) * hd)
            yt_ref[h * hd:(h + 1) * hd, :] = (
                y_diag + y_off[rows, :] * e_acs_t[h:h + 1, :]
                + dsk_ref[h * hd:(h + 1) * hd, :] * xs_g[rows, :])
            st_ref[0, h] = st_g[rows, :] * dec_chunk[:, h:h + 1] + cs[rows, :]
    return yt_ref[...].T


def _ssd(xbc, dtr, conv0, ssm0, cw, cb, dtb, alog, dsk, L, valid):
    nb, t = xbc.shape[0], xbc.shape[1]
    nc = t // L
    st_spec = pl.BlockSpec((1, N_SSM_HEADS, SSM_HEAD_DIM, D_STATE), lambda b, c: (b, 0, 0, 0))
    return pl.pallas_call(
        functools.partial(_ssd_body, L=L, valid=valid),
        grid=(nb, nc),
        in_specs=[pl.BlockSpec((1, L, CONV_DIM), lambda b, c: (b, c, 0)),
                  pl.BlockSpec((1, L, LANES), lambda b, c: (b, c, 0)),
                  pl.BlockSpec((1, SUBLANES, CONV_DIM), lambda b, c: (b, 0, 0)),
                  st_spec,
                  _const_spec(cw.shape), _const_spec(cb.shape), _const_spec(dtb.shape),
                  _const_spec(alog.shape), _const_spec(dsk.shape)],
        out_specs=[pl.BlockSpec((1, L, D_INNER), lambda b, c: (b, c, 0)), st_spec],
        out_shape=[jax.ShapeDtypeStruct((nb, t, D_INNER), F32),
                   jax.ShapeDtypeStruct(ssm0.shape, F32)],
        scratch_shapes=[pltpu.VMEM((L + SUBLANES, CONV_DIM), F32)],
        compiler_params=_params(("parallel", "arbitrary")),
        name="ssd_scan",
    )(xbc, dtr, conv0, ssm0, cw, cb, dtb, alog, dsk)


def _ssd_t(xbc, dtr, conv0, ssm0, cw, cb, dtb, alog, dsk):
    nb, t = xbc.shape[0], xbc.shape[1]
    L = SSD_CHUNK
    n_sub = SSD_CHUNKS_PER_STEP if t % (L * SSD_CHUNKS_PER_STEP) == 0 else 1
    rows = L * n_sub
    assert L == LANES and t % rows == 0
    dsk_t = jnp.broadcast_to(dsk.reshape(D_INNER, 1), (D_INNER, LANES))
    st_spec = pl.BlockSpec((1, N_SSM_HEADS, SSM_HEAD_DIM, D_STATE), lambda b, c: (b, 0, 0, 0))
    has_init = conv0 is not None
    init_args = (conv0, ssm0) if has_init else ()
    init_specs = [pl.BlockSpec((1, SUBLANES, CONV_DIM), lambda b, c: (b, 0, 0)),
                  st_spec] if has_init else []
    return pl.pallas_call(
        functools.partial(_ssd_t_body, L=L, n_sub=n_sub, has_init=has_init),
        grid=(nb, t // rows),
        in_specs=[pl.BlockSpec((1, rows, CONV_DIM), lambda b, c: (b, c, 0)),
                  pl.BlockSpec((1, rows, LANES), lambda b, c: (b, c, 0))] + init_specs + [
                  _const_spec(cw.shape), _const_spec(cb.shape),
                  _const_spec(dtb.shape), _const_spec(alog.shape), _const_spec(dsk_t.shape)],
        out_specs=[pl.BlockSpec((1, rows, D_INNER), lambda b, c: (b, c, 0)), st_spec],
        out_shape=[jax.ShapeDtypeStruct((nb, t, D_INNER), F32),
                   jax.ShapeDtypeStruct((nb, N_SSM_HEADS, SSM_HEAD_DIM, D_STATE), F32)],
        scratch_shapes=[pltpu.VMEM((L + SUBLANES, CONV_DIM), F32), pltpu.VMEM((D_INNER, L), F32)],
        compiler_params=_params(("parallel", "arbitrary")),
        name="ssd_scan_t",
    )(xbc, dtr, *init_args, cw, cb, dtb, alog, dsk_t)


def _ffn(x1, nf_ref, wgu_ref, wd_ref):
    h = (x1 * _inv_rms(x1) * nf_ref[...]).astype(BF16)
    acts = []
    for c in range(D_FF // FFN_CHUNK):
        lo = c * FFN_CHUNK
        gate = jnp.dot(h, wgu_ref[0, :, lo:lo + FFN_CHUNK], preferred_element_type=F32)
        up = jnp.dot(h, wgu_ref[0, :, D_FF + lo:D_FF + lo + FFN_CHUNK], preferred_element_type=F32)
        acts.append((_silu(gate) * up).astype(BF16))
    return x1 + jnp.dot(jnp.concatenate(acts, axis=1), wd_ref[0], preferred_element_type=F32)


def _ssm_out_ffn_body(y_ref, z_ref, x_ref, gn_ref, wo_ref, nf_ref, wgu_ref, wd_ref, o_ref):
    y = y_ref[...] * z_ref[...]
    gw = D_INNER // N_SSM_GROUPS
    parts = []
    for g in range(N_SSM_GROUPS):
        yg = y[:, g * gw:(g + 1) * gw]
        parts.append(yg * _inv_rms(yg))
    y = (jnp.concatenate(parts, axis=1) * gn_ref[...]).astype(BF16)
    x1 = x_ref[...] + jnp.dot(y, wo_ref[...], preferred_element_type=F32)
    o_ref[...] = _ffn(x1, nf_ref, wgu_ref, wd_ref)


def _attn_out_ffn_body(a_ref, x_ref, wo_ref, nf_ref, wgu_ref, wd_ref, nfin_ref, o_ref):
    x1 = x_ref[...] + jnp.dot(a_ref[...], wo_ref[...], preferred_element_type=F32)
    x2 = _ffn(x1, nf_ref, wgu_ref, wd_ref)
    o_ref[...] = x2 * _inv_rms(x2) * nfin_ref[...]


def _ssm_out_ffn(y, z, x, gn, wo, nf, wgu, wd, layer, tm):
    m = x.shape[0]
    row = lambda w: pl.BlockSpec((tm, w), lambda i: (i, 0))
    return pl.pallas_call(
        _ssm_out_ffn_body,
        grid=(m // tm,),
        in_specs=[row(D_INNER), row(D_INNER), row(D_MODEL), _const_spec(gn.shape),
                  _const_spec(wo.shape), _const_spec(nf.shape), _layer_spec(wgu.shape, layer),
                  _layer_spec(wd.shape, layer)],
        out_specs=row(D_MODEL),
        out_shape=jax.ShapeDtypeStruct((m, D_MODEL), F32),
        compiler_params=_params(("parallel",)),
        name="ssm_out_ffn",
    )(y, z, x, gn, wo, nf, wgu, wd)


def _attn_out_ffn(a, x, wo, nf, wgu, wd, layer, nfin, tm):
    m = x.shape[0]
    row = lambda w: pl.BlockSpec((tm, w), lambda i: (i, 0))
    return pl.pallas_call(
        _attn_out_ffn_body,
        grid=(m // tm,),
        in_specs=[row(D_MODEL), row(D_MODEL), _const_spec(wo.shape), _const_spec(nf.shape),
                  _layer_spec(wgu.shape, layer), _layer_spec(wd.shape, layer),
                  _const_spec(nfin.shape)],
        out_specs=row(D_MODEL),
        out_shape=jax.ShapeDtypeStruct((m, D_MODEL), F32),
        compiler_params=_params(("parallel",)),
        name="attn_out_ffn",
    )(a, x, wo, nf, wgu, wd, nfin)


def _rope(x, cos, sin_signed, first_half):
    outs = []
    for c in range(x.shape[1] // LANES):
        xc = x[:, c * LANES:(c + 1) * LANES]
        partner = jnp.where(first_half, pltpu.roll(xc, LANES - HEAD_DIM // 2, 1),
                            pltpu.roll(xc, HEAD_DIM // 2, 1))
        outs.append(xc * cos + partner * sin_signed)
    return jnp.concatenate(outs, axis=1)


def _kvq_compute(x_ref, gkv_ref, gq_ref, wkv_ref, wq_ref, cos_ref, sin_ref):
    x = x_ref[...]
    xn = x * _inv_rms(x)
    hk = (xn * gkv_ref[...]).astype(BF16)
    hq = (xn * gq_ref[...]).astype(BF16)
    cos = cos_ref[...]
    sin = sin_ref[...]
    lane = lax.broadcasted_iota(jnp.int32, cos.shape, 1)
    first_half = (lane % HEAD_DIM) < (HEAD_DIM // 2)
    kv = jnp.dot(hk, wkv_ref[...], preferred_element_type=F32)
    k = _rope(kv[:, :KV_DIM], cos, sin, first_half)
    q = _rope(jnp.dot(hq, wq_ref[...], preferred_element_type=F32), cos, sin, first_half)
    return k, kv[:, KV_DIM:], q


def _kvq_body(x_ref, gkv_ref, gq_ref, wkv_ref, wq_ref, cos_ref, sin_ref, k_ref, v_ref, q_ref):
    k, v, q = _kvq_compute(x_ref, gkv_ref, gq_ref, wkv_ref, wq_ref, cos_ref, sin_ref)
    k_ref[...] = k
    v_ref[...] = v
    q_ref[...] = q


def _kvq_prompt_body(x_ref, gkv_ref, gq_ref, wkv_ref, wq_ref, cos_ref, sin_ref,
                     kt_ref, vt32_ref, qt_ref, kh_ref, vt_ref, km_ref):
    k, v, q = _kvq_compute(x_ref, gkv_ref, gq_ref, wkv_ref, wq_ref, cos_ref, sin_ref)
    kt_ref[0] = k.T
    v_t = v.T
    vt32_ref[0] = v_t
    qt_ref[0] = q.T
    vt = v_t.astype(BF16)
    col = lax.broadcasted_iota(jnp.int32, (MOBA_BLOCK, LANES - HEAD_DIM), 1)
    row = lax.broadcasted_iota(jnp.int32, (V_AUG - HEAD_DIM, MOBA_BLOCK), 0)
    ones_row = jnp.where(row == 0, 1.0, 0.0).astype(BF16)
    for sub in range(KVQ_BLOCKS_PER_STEP):
        blk = pl.program_id(1) * KVQ_BLOCKS_PER_STEP + sub
        toks = slice(sub * MOBA_BLOCK, (sub + 1) * MOBA_BLOCK)
        onehot = jnp.where(col == blk, 1.0, 0.0).astype(BF16)
        for kvh in range(N_KV_HEADS):
            kh_ref[0, sub, kvh] = jnp.concatenate(
                [k[toks, kvh * HEAD_DIM:(kvh + 1) * HEAD_DIM].astype(BF16), onehot], axis=1)
            vt_ref[0, sub, kvh] = jnp.concatenate(
                [vt[kvh * HEAD_DIM:(kvh + 1) * HEAD_DIM, toks], ones_row], axis=0)
        km_ref[0, sub] = jnp.sum(k[toks, :], axis=0, keepdims=True) * (1.0 / MOBA_BLOCK)


def _kvq(x, gkv, gq, wkv, wq, cos, sin, tm):
    m = x.shape[0]
    row = lambda w: pl.BlockSpec((tm, w), lambda i: (i, 0))
    tab = pl.BlockSpec((tm, LANES), lambda i: (0, 0))
    return pl.pallas_call(
        _kvq_body,
        grid=(m // tm,),
        in_specs=[row(D_MODEL), _const_spec(gkv.shape), _const_spec(gq.shape),
                  _const_spec(wkv.shape), _const_spec(wq.shape), tab, tab],
        out_specs=[row(KV_DIM), row(KV_DIM), row(D_MODEL)],
        out_shape=[jax.ShapeDtypeStruct((m, KV_DIM), F32), jax.ShapeDtypeStruct((m, KV_DIM), F32),
                   jax.ShapeDtypeStruct((m, D_MODEL), F32)],
        compiler_params=_params(("parallel",)),
        name="kvq_proj",
    )(x, gkv, gq, wkv, wq, cos, sin)


def _kvq_prompt(x, b, t, gkv, gq, wkv, wq, cos, sin):
    nb = t // MOBA_BLOCK
    per = KVQ_BLOCKS_PER_STEP
    assert nb % per == 0
    steps = nb // per
    tm = per * MOBA_BLOCK
    row = lambda w: pl.BlockSpec((tm, w), lambda bi, i: (bi * steps + i, 0))
    tab = pl.BlockSpec((tm, LANES), lambda bi, i: (i, 0))
    return pl.pallas_call(
        _kvq_prompt_body,
        grid=(b, steps),
        in_specs=[row(D_MODEL), _const_spec(gkv.shape), _const_spec(gq.shape),
                  _const_spec(wkv.shape), _const_spec(wq.shape), tab, tab],
        out_specs=[pl.BlockSpec((1, KV_DIM, tm), lambda bi, i: (bi, 0, i)),
                   pl.BlockSpec((1, KV_DIM, tm), lambda bi, i: (bi, 0, i)),
                   pl.BlockSpec((1, D_MODEL, tm), lambda bi, i: (bi, 0, i)),
                   pl.BlockSpec((1, per, N_KV_HEADS, MOBA_BLOCK, LANES),
                                lambda bi, i: (bi, i, 0, 0, 0)),
                   pl.BlockSpec((1, per, N_KV_HEADS, V_AUG, MOBA_BLOCK),
                                lambda bi, i: (bi, i, 0, 0, 0)),
                   pl.BlockSpec((1, per, 1, KV_DIM), lambda bi, i: (bi, i, 0, 0))],
        out_shape=[jax.ShapeDtypeStruct((b, KV_DIM, t), F32),
                   jax.ShapeDtypeStruct((b, KV_DIM, t), F32),
                   jax.ShapeDtypeStruct((b, D_MODEL, t), F32),
                   jax.ShapeDtypeStruct((b, nb, N_KV_HEADS, MOBA_BLOCK, LANES), BF16),
                   jax.ShapeDtypeStruct((b, nb, N_KV_HEADS, V_AUG, MOBA_BLOCK), BF16),
                   jax.ShapeDtypeStruct((b, nb, 1, KV_DIM), F32)],
        compiler_params=_params(("parallel", "parallel")),
        name="kvq_proj_prompt",
    )(x, gkv, gq, wkv, wq, cos, sin)


def _moba_prompt_body(qt_ref, kh_ref, vt_ref, km_ref, o_ref, s_a, s_b, acc_a, acc_b, *, nb):
    i = pl.program_id(1)
    g_per_unit = ATTN_UNIT_HEADS
    rows = g_per_unit * MOBA_BLOCK
    n_units = N_HEADS // g_per_unit
    scale = (HEAD_DIM ** -0.5) * LOG2_E
    blk_row = lax.broadcasted_iota(jnp.int32, (nb, rows), 0)
    key_i = lax.broadcasted_iota(jnp.int32, (MOBA_BLOCK, rows), 0)
    tok_i = lax.broadcasted_iota(jnp.int32, (MOBA_BLOCK, rows), 1) % MOBA_BLOCK
    own_ok = key_i <= tok_i
    fold = lambda s: jnp.max(s.reshape(MOBA_BLOCK // SUBLANES, SUBLANES, rows), axis=0)

    def setup(u):
        kvh = (u * g_per_unit) // Q_PER_KV
        lo = kvh * HEAD_DIM
        base = u * g_per_unit * HEAD_DIM
        qt2 = qt_ref[0, base:base + g_per_unit * HEAD_DIM, :]
        qt = jnp.concatenate([qt2[g * HEAD_DIM:(g + 1) * HEAD_DIM, :] for g in range(g_per_unit)],
                             axis=1)
        km = km_ref[0, :, lo:lo + HEAD_DIM]
        gate = jnp.dot(km, qt, preferred_element_type=F32,
                       precision=lax.Precision.HIGHEST)
        sel = _top3_mask(gate, blk_row < i, axis=0)
        pen = jnp.where(sel, 0.0, MASKED).astype(BF16)
        qb = (qt * scale).astype(BF16)
        q_own = jnp.concatenate([qb, jnp.zeros((LANES - HEAD_DIM, rows), BF16)], axis=0)
        q_aug = jnp.concatenate([qb, pen, jnp.zeros((LANES - HEAD_DIM - nb, rows), BF16)], axis=0)
        s_own = jnp.dot(kh_ref[0, i, kvh], q_own, preferred_element_type=F32)
        s_own = jnp.where(own_ok, s_own, MASKED)
        s_bufs[u % 2][nb] = s_own
        return kvh, q_aug, fold(s_own)

    def scores(u, kvh, q_aug, j0, n, m8):
        for t in range(n):
            s = jnp.dot(kh_ref[0, j0 + t, kvh], q_aug, preferred_element_type=F32)
            s_bufs[u % 2][j0 + t] = s
            m8 = jnp.maximum(m8, fold(s))
        return m8

    def weighted(u, kvh, m, j0, n):
        s_buf = s_bufs[u % 2]
        p = jnp.concatenate([jnp.exp2(s_buf[j0 + t] - m).astype(BF16) for t in range(n)], axis=0)
        v = jnp.concatenate([vt_ref[0, j0 + t, kvh] for t in range(n)], axis=1)
        acc_bufs[u % 2][...] += jnp.dot(v, p, preferred_element_type=F32)

    def over_blocks(step, carry):
        n4, n2 = i // 4, i // 2
        carry = lax.fori_loop(0, n4, lambda q, c: step(4 * q + 2, 2, step(4 * q, 2, c)), carry)
        carry = lax.fori_loop(2 * n4, n2, lambda q, c: step(2 * q, 2, c), carry)
        return lax.fori_loop(2 * n2, i, lambda j, c: step(j, 1, c), carry)

    s_bufs = (s_a, s_b)
    acc_bufs = (acc_a, acc_b)
    kvh, q_aug, m8 = setup(0)
    m8 = over_blocks(functools.partial(scores, 0, kvh, q_aug), m8)
    for u in range(n_units):
        m = jnp.max(m8, axis=0, keepdims=True)
        p_own = jnp.exp2(s_bufs[u % 2][nb] - m).astype(BF16)
        acc_bufs[u % 2][...] = jnp.dot(vt_ref[0, i, kvh], p_own, preferred_element_type=F32)
        if u + 1 < n_units:
            kvh_n, q_aug_n, m8_n = setup(u + 1)

            def both(j0, n, m8c, u=u, kvh=kvh, m=m, kvh_n=kvh_n, q_aug_n=q_aug_n):
                m8c = scores(u + 1, kvh_n, q_aug_n, j0, n, m8c)
                weighted(u, kvh, m, j0, n)
                return m8c

            m8_n = over_blocks(both, m8_n)
        else:
            def last(j0, n, carry, u=u, kvh=kvh, m=m):
                weighted(u, kvh, m, j0, n)
                return carry

            over_blocks(last, 0)
        acc = acc_bufs[u % 2][...]
        out_t = acc[:HEAD_DIM, :] / acc[HEAD_DIM:HEAD_DIM + 1, :]
        out2 = jnp.concatenate([out_t[:, g * MOBA_BLOCK:(g + 1) * MOBA_BLOCK]
                                for g in range(g_per_unit)], axis=0)
        base = u * g_per_unit * HEAD_DIM
        o_ref[0, :, base:base + g_per_unit * HEAD_DIM] = out2.T.astype(o_ref.dtype)
        if u + 1 < n_units:
            kvh, m8 = kvh_n, m8_n


def _moba_prompt(qt, kh, vt, km):
    b, t = qt.shape[0], qt.shape[2]
    nb = t // MOBA_BLOCK
    rows = ATTN_UNIT_HEADS * MOBA_BLOCK
    return pl.pallas_call(
        functools.partial(_moba_prompt_body, nb=nb),
        grid=(b, nb),
        in_specs=[pl.BlockSpec((1, D_MODEL, MOBA_BLOCK), lambda bi, i: (bi, 0, i)),
                  pl.BlockSpec((1, nb, N_KV_HEADS, MOBA_BLOCK, LANES),
                               lambda bi, i: (bi, 0, 0, 0, 0), pipeline_mode=pl.Buffered(1)),
                  pl.BlockSpec((1, nb, N_KV_HEADS, V_AUG, MOBA_BLOCK),
                               lambda bi, i: (bi, 0, 0, 0, 0), pipeline_mode=pl.Buffered(1)),
                  pl.BlockSpec((1, nb, KV_DIM), lambda bi, i: (bi, 0, 0))],
        out_specs=pl.BlockSpec((1, MOBA_BLOCK, D_MODEL), lambda bi, i: (bi, i, 0)),
        out_shape=jax.ShapeDtypeStruct((b, t, D_MODEL), BF16),
        scratch_shapes=[pltpu.VMEM((nb + 1, MOBA_BLOCK, rows), F32),
                        pltpu.VMEM((nb + 1, MOBA_BLOCK, rows), F32),
                        pltpu.VMEM((V_AUG, rows), F32), pltpu.VMEM((V_AUG, rows), F32)],
        compiler_params=_params(("parallel", "arbitrary")),
        name="moba_prompt",
    )(qt, kh, vt, km)


def _page_copy(cache_hbm, buf, sem, page, slot, p):
    return pltpu.make_async_copy(cache_hbm.at[page], buf.at[slot, p], sem)


def _moba_sample_body(pt_ref, q_ref, kn_ref, vn_ref, ck_hbm, cv_hbm, o_ref,
                      kbuf, vbuf, s_scr, km_scr, new_scr, sems, *, n_pages, n_new):
    b = pl.program_id(0)
    slot = b % 2
    rows = q_ref.shape[1]
    ppb = MOBA_BLOCK // PAGE_SIZE
    nb = n_pages // ppb
    scale = HEAD_DIM ** -0.5

    def fetch(seq, to_slot):
        def start_page(p, carry):
            page = pt_ref[seq, p]
            _page_copy(ck_hbm, kbuf, sems.at[0, to_slot], page, to_slot, p).start()
            _page_copy(cv_hbm, vbuf, sems.at[1, to_slot], page, to_slot, p).start()
            return carry
        lax.fori_loop(0, n_pages, start_page, 0, unroll=8)

    @pl.when(b == 0)
    def _():
        fetch(0, 0)

    @pl.when(b + 1 < pl.num_programs(0))
    def _():
        fetch(b + 1, 1 - slot)

    q64 = q_ref[0]
    qt = jnp.concatenate([q64] * N_KV_HEADS, axis=1)
    r_i = lax.broadcasted_iota(jnp.int32, qt.shape, 0)
    c_i = lax.broadcasted_iota(jnp.int32, qt.shape, 1)
    qpad = jnp.where(c_i // HEAD_DIM == r_i // (Q_PER_KV * n_new), qt, 0.0)
    qb = (qpad * scale).astype(BF16)

    def wait_k(p, carry):
        _page_copy(ck_hbm, kbuf, sems.at[0, slot], 0, slot, p).wait()
        return carry

    lax.fori_loop(0, n_pages, wait_k, 0)

    km_scr[...] = jnp.zeros(km_scr.shape, F32)
    for j in range(nb):
        kt = jnp.concatenate([kbuf[slot, ppb * j + u] for u in range(ppb)], axis=1)
        km_scr[:, j:j + 1] = jnp.sum(kt, axis=1, keepdims=True) * (1.0 / MOBA_BLOCK)
        s_scr[:, j * MOBA_BLOCK:(j + 1) * MOBA_BLOCK] = jnp.dot(
            qb, kt.astype(BF16), preferred_element_type=F32)

    gate = jnp.dot(qpad, km_scr[...], preferred_element_type=F32,
                   precision=lax.Precision.HIGHEST)
    sel = _top3_mask(gate, lax.broadcasted_iota(jnp.int32, gate.shape, 1) < nb)
    pen = jnp.where(sel, 0.0, NEG_INF)

    new_scr[...] = jnp.zeros(new_scr.shape, F32)
    new_scr[0:n_new, :] = kn_ref[0]
    s_new = lax.dot_general(qb, new_scr[...].astype(BF16), NT_DIMS, preferred_element_type=F32)
    nr = lax.broadcasted_iota(jnp.int32, s_new.shape, 0)
    ncol = lax.broadcasted_iota(jnp.int32, s_new.shape, 1)
    s_new = jnp.where(ncol <= nr % n_new, s_new, NEG_INF)

    def lane_fold(x, op):
        out = x[:, :LANES]
        for c in range(1, x.shape[1] // LANES):
            out = op(out, x[:, c * LANES:(c + 1) * LANES])
        return out

    m_part = s_new
    for j in range(nb):
        sj = s_scr[:, j * MOBA_BLOCK:(j + 1) * MOBA_BLOCK] + pen[:, j:j + 1]
        m_part = jnp.maximum(m_part, lane_fold(sj, jnp.maximum))
    m = jnp.max(m_part, axis=1, keepdims=True)

    def wait_v(p, carry):
        _page_copy(cv_hbm, vbuf, sems.at[1, slot], 0, slot, p).wait()
        return carry

    lax.fori_loop(0, n_pages, wait_v, 0)

    p_new = jnp.exp(s_new - m)
    l_part = p_new
    new_scr[0:n_new, :] = vn_ref[0]
    acc = jnp.dot(p_new.astype(BF16), new_scr[...].astype(BF16), preferred_element_type=F32)
    for j in range(nb):
        pj = jnp.exp(s_scr[:, j * MOBA_BLOCK:(j + 1) * MOBA_BLOCK] + pen[:, j:j + 1] - m)
        l_part = l_part + lane_fold(pj, jnp.add)
        vt = jnp.concatenate([vbuf[slot, ppb * j + u] for u in range(ppb)], axis=1).astype(BF16)
        acc = acc + lax.dot_general(pj.astype(BF16), vt, NT_DIMS, preferred_element_type=F32)
    acc = acc / jnp.sum(l_part, axis=1, keepdims=True)
    ro = lax.broadcasted_iota(jnp.int32, (rows, HEAD_DIM), 0) // (Q_PER_KV * n_new)
    out = jnp.zeros((rows, HEAD_DIM), F32)
    for kvh in range(N_KV_HEADS):
        out = jnp.where(ro == kvh, acc[:, kvh * HEAD_DIM:(kvh + 1) * HEAD_DIM], out)
    o_ref[0] = out


def _moba_sample(page_table, q_rows, k_new, v_new, cache_k, cache_v):
    nseq, n_pages = page_table.shape
    rows = q_rows.shape[1]
    n_new = k_new.shape[1]
    past = n_pages * PAGE_SIZE
    grid_spec = pltpu.PrefetchScalarGridSpec(
        num_scalar_prefetch=1,
        grid=(nseq,),
        in_specs=[pl.BlockSpec((1, rows, HEAD_DIM), lambda b, pt: (b, 0, 0)),
                  pl.BlockSpec((1, n_new, KV_DIM), lambda b, pt: (b, 0, 0)),
                  pl.BlockSpec((1, n_new, KV_DIM), lambda b, pt: (b, 0, 0)),
                  pl.BlockSpec(memory_space=pl.ANY),
                  pl.BlockSpec(memory_space=pl.ANY)],
        out_specs=pl.BlockSpec((1, rows, HEAD_DIM), lambda b, pt: (b, 0, 0)),
        scratch_shapes=[pltpu.VMEM((2, n_pages, KV_DIM, PAGE_SIZE), F32),
                        pltpu.VMEM((2, n_pages, KV_DIM, PAGE_SIZE), F32),
                        pltpu.VMEM((rows, past), F32),
                        pltpu.VMEM((KV_DIM, LANES), F32),
                        pltpu.VMEM((LANES, KV_DIM), F32),
                        pltpu.SemaphoreType.DMA((2, 2))],
    )
    return pl.pallas_call(
        functools.partial(_moba_sample_body, n_pages=n_pages, n_new=n_new),
        grid_spec=grid_spec,
        out_shape=jax.ShapeDtypeStruct((nseq, rows, HEAD_DIM), F32),
        compiler_params=_params(("arbitrary",)),
        name="moba_sample",
    )(page_table, q_rows, k_new, v_new, cache_k, cache_v)


def _rope_tables(pos):
    half = HEAD_DIM // 2
    inv = ROPE_THETA ** (-np.arange(half, dtype=np.float64) / half)
    ang = np.asarray(pos, dtype=np.float64)[:, None] * inv[None, :]
    cos, sin = np.cos(ang), np.sin(ang)
    cos_h = np.concatenate([cos, cos], axis=1)
    sin_h = np.concatenate([-sin, sin], axis=1)
    reps = LANES // HEAD_DIM
    return (jnp.asarray(np.tile(cos_h, (1, reps)), F32), jnp.asarray(np.tile(sin_h, (1, reps)), F32))


def _prep_weights(norm_mix, norm_ffn, w_in_ssm, conv_w, conv_b, dt_bias, a_log, d_skip, norm_ssm,
                  w_out_ssm, norm_kv, w_kv, w_q, w_o, w_gu, w_down, norm_final):
    pad_h = LANES - N_SSM_HEADS
    return dict(
        g_mix0=norm_mix[0][None], g_mix1=norm_mix[1][None],
        g_ffn0=norm_ffn[0][None], g_ffn1=norm_ffn[1][None],
        w_in=w_in_ssm.astype(BF16),
        cw=conv_w[0], cb=conv_b[0][None],
        dtb=jnp.pad(dt_bias[0], (0, pad_h))[None], alog=jnp.pad(a_log[0], (0, pad_h))[None],
        dsk=jnp.repeat(d_skip[0], SSM_HEAD_DIM)[None],
        gn=norm_ssm[0][None], wo_ssm=w_out_ssm[0].astype(BF16),
        g_kv=norm_kv[None], wkv=w_kv.astype(BF16), wq=w_q[0].astype(BF16),
        wo=w_o[0].astype(BF16),
        wgu=w_gu.astype(BF16), wd=w_down.astype(BF16),
        g_fin=norm_final[None],
    )


def _ssd_layer_long(x, conv_in, ssm0, w, tm):
    b, t, _ = x.shape
    keep = D_CONV - 1
    assert t >= keep
    xf = x.reshape(b * t, D_MODEL)
    z, xbc, dtr = _inproj(xf, w["g_mix0"], w["w_in"], 0, tm)
    xbc3 = xbc.reshape(b, t, CONV_DIM)
    conv0 = None if conv_in is None else jnp.pad(conv_in, ((0, 0), (SUBLANES - keep, 0), (0, 0)))
    y, ssm_new = _ssd_t(xbc3, dtr.reshape(b, t, LANES), conv0, ssm0, w["cw"], w["cb"], w["dtb"],
                        w["alog"], w["dsk"])
    x2 = _ssm_out_ffn(y.reshape(b * t, D_INNER), z, xf, w["gn"], w["wo_ssm"], w["g_ffn0"],
                      w["wgu"], w["wd"], 0, tm)
    return x2, xbc3[:, t - keep:], ssm_new


def _ssd_layer(x, conv_in, ssm0, w, tm, L, valid):
    b, t, _ = x.shape
    xf = x.reshape(b * t, D_MODEL)
    z, xbc, dtr = _inproj(xf, w["g_mix0"], w["w_in"], 0, tm)
    tp = -(-t // L) * L
    xbc3 = xbc.reshape(b, t, CONV_DIM)
    dtr3 = dtr.reshape(b, t, LANES)
    if tp != t:
        xbc3 = jnp.pad(xbc3, ((0, 0), (0, tp - t), (0, 0)))
        dtr3 = jnp.pad(dtr3, ((0, 0), (0, tp - t), (0, 0)))
    conv0 = jnp.pad(conv_in, ((0, 0), (SUBLANES - (D_CONV - 1), 0), (0, 0)))
    y, ssm_new = _ssd(xbc3, dtr3, conv0, ssm0, w["cw"], w["cb"], w["dtb"], w["alog"], w["dsk"],
                      L, valid)
    y = y[:, :t].reshape(b * t, D_INNER)
    x2 = _ssm_out_ffn(y, z, xf, w["gn"], w["wo_ssm"], w["g_ffn0"], w["wgu"], w["wd"], 0, tm)
    keep = D_CONV - 1
    conv_new = jnp.concatenate([conv_in[:, t:], xbc.reshape(b, t, CONV_DIM)[:, max(0, t - keep):]],
                               axis=1)
    return x2, conv_new, ssm_new


def kernel(x_prompt, x_sample, state_conv, state_ssm, cache_k, cache_v, page_table, norm_mix,
           norm_ffn, w_in_ssm, conv_w, conv_b, dt_bias, a_log, d_skip, norm_ssm, w_out_ssm,
           norm_kv, w_kv, w_q, w_o, w_gu, w_down, norm_final):
    w = _prep_weights(norm_mix, norm_ffn, w_in_ssm, conv_w, conv_b, dt_bias, a_log, d_skip,
                      norm_ssm, w_out_ssm, norm_kv, w_kv, w_q, w_o, w_gu, w_down, norm_final)
    bp, tp, _ = x_prompt.shape
    bs, ts, _ = x_sample.shape
    past_len = page_table.shape[1] * PAGE_SIZE

    tm_p = PROMPT_ROW_TILE
    x2_p, conv_p, ssm_p = _ssd_layer_long(x_prompt, None, None, w, tm_p)
    cos_p, sin_p = _rope_tables(np.arange(tp))
    kt_p, vt32_p, qt_p, kh_p, vt_p, km_p = _kvq_prompt(x2_p, bp, tp, w["g_kv"], w["g_mix1"],
                                                       w["wkv"], w["wq"], cos_p, sin_p)
    k_p = kt_p.reshape(bp, N_KV_HEADS, HEAD_DIM, tp).transpose(0, 3, 1, 2)
    v_p = vt32_p.reshape(bp, N_KV_HEADS, HEAD_DIM, tp).transpose(0, 3, 1, 2)
    attn_p = _moba_prompt(qt_p, kh_p, vt_p, km_p.reshape(bp, tp // MOBA_BLOCK, KV_DIM))
    y_p = _attn_out_ffn(attn_p.reshape(bp * tp, D_MODEL), x2_p, w["wo"], w["g_ffn1"], w["wgu"],
                        w["wd"], 1, w["g_fin"], tm_p)

    tm_s = bs * ts
    x2_s, conv_s, ssm_s = _ssd_layer(x_sample, state_conv[0], state_ssm[0], w, tm_s, SUBLANES, ts)
    cos_s, sin_s = _rope_tables(past_len + np.tile(np.arange(ts), bs))
    k_s, v_s, q_s = _kvq(x2_s, w["g_kv"], w["g_mix1"], w["wkv"], w["wq"], cos_s, sin_s, tm_s)
    q_rows = q_s.reshape(bs, ts, N_HEADS, HEAD_DIM).transpose(0, 2, 1, 3).reshape(
        bs, N_HEADS * ts, HEAD_DIM)
    attn_rows = _moba_sample(page_table, q_rows, k_s.reshape(bs, ts, KV_DIM),
                             v_s.reshape(bs, ts, KV_DIM),
                             cache_k.transpose(0, 2, 3, 1).reshape(-1, KV_DIM, PAGE_SIZE),
                             cache_v.transpose(0, 2, 3, 1).reshape(-1, KV_DIM, PAGE_SIZE))
    attn_s = attn_rows.reshape(bs, N_HEADS, ts, HEAD_DIM).transpose(
        0, 2, 1, 3).reshape(bs * ts, D_MODEL).astype(BF16)
    y_s = _attn_out_ffn(attn_s, x2_s, w["wo"], w["g_ffn1"], w["wgu"], w["wd"], 1,
                        w["g_fin"], tm_s)

    return (y_p.reshape(bp, tp, D_MODEL), y_s.reshape(bs, ts, D_MODEL),
            conv_p[None], ssm_p[None],
            k_p, v_p,
            conv_s[None], ssm_s[None],
            k_s.reshape(bs, ts, N_KV_HEADS, HEAD_DIM), v_s.reshape(bs, ts, N_KV_HEADS, HEAD_DIM))
```

```python
import functools

import jax
import jax.numpy as jnp
import numpy as np
from jax import lax
from jax.experimental import pallas as pl
from jax.experimental.pallas import tpu as pltpu

F32 = jnp.float32
BF16 = jnp.bfloat16

D_MODEL = 1024
D_INNER = 2048
SSM_HEAD_DIM = 64
N_SSM_HEADS = 32
N_SSM_GROUPS = 4
HEADS_PER_GROUP = 8
D_STATE = 128
D_CONV = 4
SSD_CHUNK = 128
GN = N_SSM_GROUPS * D_STATE
CONV_DIM = D_INNER + 2 * GN
HEAD_DIM = 64
N_HEADS = 16
N_KV_HEADS = 4
Q_PER_KV = 4
KV_DIM = N_KV_HEADS * HEAD_DIM
MOBA_BLOCK = 256
MOBA_TOP_K = 3
ROPE_THETA = 10000.0
D_FF = 2816
EPS = 1e-6
PAGE_SIZE = 128

LANES = 128
SUBLANES = 8
VMEM_LIMIT = 56 * 1024 * 1024

NT_DIMS = (((1,), (1,)), ((), ()))
TN_DIMS = (((0,), (0,)), ((), ()))
NEG_INF = float("-inf")
MASKED = -1e30
LOG2_E = 1.4426950408889634
V_AUG = HEAD_DIM + 16
PROMPT_ROW_TILE = 512
FFN_CHUNK = 256
SSD_CHUNKS_PER_STEP = 2
KVQ_BLOCKS_PER_STEP = 4
ATTN_UNIT_HEADS = 4


def _params(sem):
    return pltpu.CompilerParams(dimension_semantics=sem, vmem_limit_bytes=VMEM_LIMIT)


def _const_spec(shape):
    nd = len(shape)
    return pl.BlockSpec(shape, lambda *_: (0,) * nd, pipeline_mode=pl.Buffered(1))


def _layer_spec(shape, layer):
    nd = len(shape)
    return pl.BlockSpec((1,) + tuple(shape[1:]), lambda *_: (layer,) + (0,) * (nd - 1),
                        pipeline_mode=pl.Buffered(1))


def _silu(x):
    return x * (1.0 / (1.0 + jnp.exp(-x)))


def _inv_rms(x):
    return lax.rsqrt(jnp.mean(x * x, axis=-1, keepdims=True) + EPS)


def _split3(x):
    hi = x.astype(BF16)
    r = x - hi.astype(F32)
    mid = r.astype(BF16)
    lo = (r - mid.astype(F32)).astype(BF16)
    return hi, mid, lo


def _dot01(a01, x, dims):
    out = None
    for p in _split3(x):
        t = lax.dot_general(a01, p, dims, preferred_element_type=F32)
        out = t if out is None else out + t
    return out


def _top3_mask(gate, valid, axis=1):
    nb = gate.shape[axis]
    col = lax.broadcasted_iota(jnp.int32, gate.shape, axis).astype(F32)
    g = jnp.where(valid, gate, NEG_INF)
    sel = jnp.zeros(gate.shape, F32)
    for _ in range(MOBA_TOP_K):
        m = jnp.max(g, axis=axis, keepdims=True)
        idx = jnp.min(jnp.where(g == m, col, float(nb)), axis=axis, keepdims=True)
        pick = col == idx
        sel = jnp.where(pick, 1.0, sel)
        g = jnp.where(pick, NEG_INF, g)
    return jnp.logical_and(sel > 0.5, valid)


def _inproj_body(x_ref, g_ref, w_ref, z_ref, xbc_ref, dt_ref):
    x = x_ref[...]
    h = (x * _inv_rms(x) * g_ref[...]).astype(BF16)
    z_ref[...] = _silu(jnp.dot(h, w_ref[0, :, :D_INNER], preferred_element_type=F32))
    xbc_ref[...] = jnp.dot(h, w_ref[0, :, D_INNER:D_INNER + CONV_DIM], preferred_element_type=F32)
    dt = jnp.dot(h, w_ref[0, :, D_INNER + CONV_DIM:], preferred_element_type=F32)
    dt_ref[...] = jnp.concatenate(
        [dt, jnp.zeros((dt.shape[0], LANES - N_SSM_HEADS), F32)], axis=1)


def _inproj(x, g, w_in, layer, tm):
    m = x.shape[0]
    row = lambda w: pl.BlockSpec((tm, w), lambda i: (i, 0))
    return pl.pallas_call(
        _inproj_body,
        grid=(m // tm,),
        in_specs=[row(D_MODEL), _const_spec((1, D_MODEL)), _layer_spec(w_in.shape, layer)],
        out_specs=[row(D_INNER), row(CONV_DIM), row(LANES)],
        out_shape=[jax.ShapeDtypeStruct((m, D_INNER), F32),
                   jax.ShapeDtypeStruct((m, CONV_DIM), F32),
                   jax.ShapeDtypeStruct((m, LANES), F32)],
        compiler_params=_params(("parallel",)),
        name="ssd_inproj",
    )(x, g, w_in)


def _causal_conv(pad_ref, xbc, cw_ref, cb_ref, rows):
    pad_ref[SUBLANES:SUBLANES + rows, :] = xbc
    padded = pad_ref[...]
    conv = cb_ref[...]
    for i in range(D_CONV):
        shift = D_CONV - 1 - i
        tap = padded if shift == 0 else pltpu.roll(padded, shift, 0)
        conv = conv + tap[SUBLANES:SUBLANES + rows, :] * cw_ref[i:i + 1, :]
    pad_ref[0:SUBLANES, :] = pad_ref[rows:rows + SUBLANES, :]
    return conv


def _ssd_body(xbc_ref, dtr_ref, conv0_ref, ssm0_ref, cw_ref, cb_ref, dtb_ref, alog_ref, dsk_ref,
              y_ref, st_ref, pad_ref, *, L, valid):
    c = pl.program_id(1)

    @pl.when(c == 0)
    def _():
        pad_ref[0:SUBLANES, :] = conv0_ref[0]
        st_ref[0] = ssm0_ref[0]

    act = _silu(_causal_conv(pad_ref, xbc_ref[0], cw_ref, cb_ref, L))
    xs = act[:, :D_INNER]
    bm = act[:, D_INNER:D_INNER + GN].astype(BF16)
    cm = act[:, D_INNER + GN:].astype(BF16)

    t = dtr_ref[0] + dtb_ref[...]
    dt = jnp.maximum(t, 0.0) + jnp.log1p(jnp.exp(-jnp.abs(t)))
    if valid < L:
        rows = lax.broadcasted_iota(jnp.int32, dt.shape, 0)
        dt = jnp.where(rows < valid, dt, 0.0)
    a = -jnp.exp(alog_ref[...])
    dta = dt * a

    r_i = lax.broadcasted_iota(jnp.int32, (L, L), 0)
    c_i = lax.broadcasted_iota(jnp.int32, (L, L), 1)
    causal = r_i >= c_i
    tril = jnp.where(causal, 1.0, 0.0).astype(BF16)
    e_r = lax.broadcasted_iota(jnp.int32, (LANES, LANES), 0)
    e_c = lax.broadcasted_iota(jnp.int32, (LANES, LANES), 1)
    eye = jnp.where(e_r == e_c, 1.0, 0.0).astype(BF16)

    acs = _dot01(tril, dta, (((1,), (0,)), ((), ())))
    acs_t = _dot01(eye, acs, NT_DIMS)
    last = acs[L - 1:L, :]
    dec_end = jnp.exp(last - acs)
    e_acs = jnp.exp(acs)
    dec_chunk = jnp.exp(last)

    for g in range(N_SSM_GROUPS):
        bg = bm[:, g * D_STATE:(g + 1) * D_STATE]
        cg = cm[:, g * D_STATE:(g + 1) * D_STATE]
        cb = lax.dot_general(cg, bg, NT_DIMS, preferred_element_type=F32)
        for jp in range(HEADS_PER_GROUP // 2):
            pair = []
            for h in (g * HEADS_PER_GROUP + 2 * jp, g * HEADS_PER_GROUP + 2 * jp + 1):
                diff = acs[:, h:h + 1] - acs_t[h:h + 1, :]
                seg = jnp.exp(jnp.where(causal, diff, NEG_INF))
                w = (cb * seg).astype(BF16)
                xs_h = xs[:, h * SSM_HEAD_DIM:(h + 1) * SSM_HEAD_DIM]
                xd_h = xs_h * dt[:, h:h + 1]
                y_diag = jnp.dot(w, xd_h.astype(BF16), preferred_element_type=F32)
                st_h = st_ref[0, h]
                y_off = lax.dot_general(cg, st_h.astype(BF16), NT_DIMS,
                                        preferred_element_type=F32) * e_acs[:, h:h + 1]
                xdd = (xd_h * dec_end[:, h:h + 1]).astype(BF16)
                cs = lax.dot_general(xdd, bg, TN_DIMS, preferred_element_type=F32)
                st_ref[0, h] = st_h * dec_chunk[:, h:h + 1] + cs
                pair.append(y_diag + y_off
                            + dsk_ref[:, h * SSM_HEAD_DIM:(h + 1) * SSM_HEAD_DIM] * xs_h)
            lo = (g * HEADS_PER_GROUP + 2 * jp) * SSM_HEAD_DIM
            y_ref[0, :, lo:lo + 2 * SSM_HEAD_DIM] = jnp.concatenate(pair, axis=1)


def _ssd_t_body(xbc_ref, dtr_ref, *refs, L, n_sub, has_init):
    if has_init:
        conv0_ref, ssm0_ref = refs[:2]
        refs = refs[2:]
    cw_ref, cb_ref, dtb_ref, alog_ref, dsk_ref, y_ref, st_ref, pad_ref, yt_ref = refs

    @pl.when(pl.program_id(1) == 0)
    def _():
        if has_init:
            pad_ref[0:SUBLANES, :] = conv0_ref[0]
            st_ref[0] = ssm0_ref[0]
        else:
            pad_ref[0:SUBLANES, :] = jnp.zeros((SUBLANES, CONV_DIM), F32)
            st_ref[0] = jnp.zeros(st_ref.shape[1:], F32)

    def chunk(ci, carry):
        rows = pl.ds(pl.multiple_of(ci * L, L), L)
        y_ref[0, rows, :] = _ssd_t_chunk(xbc_ref[0, rows, :], dtr_ref[0, rows, :], cw_ref, cb_ref,
                                         dtb_ref, alog_ref, dsk_ref, st_ref, pad_ref, yt_ref, L)
        return carry

    lax.fori_loop(0, n_sub, chunk, 0)


def _ssd_t_chunk(xbc, dtr, cw_ref, cb_ref, dtb_ref, alog_ref, dsk_ref, st_ref, pad_ref, yt_ref, L):
    hd = SSM_HEAD_DIM
    gw = HEADS_PER_GROUP * hd
    act = _silu(_causal_conv(pad_ref, xbc, cw_ref, cb_ref, L))
    xs_t = act[:, :D_INNER].T
    bm = act[:, D_INNER:D_INNER + GN].astype(BF16)
    cm = act[:, D_INNER + GN:].astype(BF16)

    t = dtr + dtb_ref[...]
    dt = jnp.maximum(t, 0.0) + jnp.log1p(jnp.exp(-jnp.abs(t)))
    dta = dt * (-jnp.exp(alog_ref[...]))

    r_i = lax.broadcasted_iota(jnp.int32, (L, L), 0)
    c_i = lax.broadcasted_iota(jnp.int32, (L, L), 1)
    tril = jnp.where(r_i >= c_i, 1.0, 0.0).astype(BF16)
    causal_t = c_i >= r_i

    acs = _dot01(tril, dta, (((1,), (0,)), ((), ())))
    acs_t = acs.T
    dt_t = dt.T
    dec_end_t = jnp.exp(acs_t[:, L - 1:L] - acs_t)
    e_acs_t = jnp.exp(acs_t)
    dec_chunk = jnp.exp(acs[L - 1:L, :])
    scale_in = dt_t
    scale_st = dt_t * dec_end_t

    for g in range(N_SSM_GROUPS):
        bg = bm[:, g * D_STATE:(g + 1) * D_STATE]
        cg = cm[:, g * D_STATE:(g + 1) * D_STATE]
        cb_t = lax.dot_general(bg, cg, NT_DIMS, preferred_element_type=F32)
        h0 = g * HEADS_PER_GROUP
        xs_g = xs_t[g * gw:(g + 1) * gw, :]
        xd_parts, xdd_parts = [], []
        for j in range(HEADS_PER_GROUP):
            h = h0 + j
            xd = xs_g[j * hd:(j + 1) * hd, :] * scale_in[h:h + 1, :]
            xd_parts.append(xd.astype(BF16))
            xdd_parts.append((xs_g[j * hd:(j + 1) * hd, :] * scale_st[h:h + 1, :]).astype(BF16))
        st_g = st_ref[0, h0:h0 + HEADS_PER_GROUP].reshape(gw, D_STATE)
        y_off = lax.dot_general(st_g.astype(BF16), cg, NT_DIMS,
                                preferred_element_type=F32)
        cs = jnp.dot(jnp.concatenate(xdd_parts, axis=0), bg,
                     preferred_element_type=F32)
        for j in range(HEADS_PER_GROUP):
            h = h0 + j
            diff = acs_t[h:h + 1, :] - acs[:, h:h + 1]
            seg_t = jnp.exp(jnp.where(causal_t, diff, NEG_INF))
            w_t = (cb_t * seg_t).astype(BF16)
            y_diag = jnp.dot(xd_parts[j], w_t, preferred_element_type=F32)
            rows = slice(j * hd, (j + 1) * hd)
            yt_ref[h * hd:(h + 1) * hd, :] = (
                y_diag + y_off[rows, :] * e_acs_t[h:h + 1, :]
                + dsk_ref[h * hd:(h + 1) * hd, :] * xs_g[rows, :])
            st_ref[0, h] = st_g[rows, :] * dec_chunk[:, h:h + 1] + cs[rows, :]
    return yt_ref[...].T


def _ssd(xbc, dtr, conv0, ssm0, cw, cb, dtb, alog, dsk, L, valid):
    nb, t = xbc.shape[0], xbc.shape[1]
    nc = t // L
    st_spec = pl.BlockSpec((1, N_SSM_HEADS, SSM_HEAD_DIM, D_STATE), lambda b, c: (b, 0, 0, 0))
    return pl.pallas_call(
        functools.partial(_ssd_body, L=L, valid=valid),
        grid=(nb, nc),
        in_specs=[pl.BlockSpec((1, L, CONV_DIM), lambda b, c: (b, c, 0)),
                  pl.BlockSpec((1, L, LANES), lambda b, c: (b, c, 0)),
                  pl.BlockSpec((1, SUBLANES, CONV_DIM), lambda b, c: (b, 0, 0)),
                  st_spec,
                  _const_spec(cw.shape), _const_spec(cb.shape), _const_spec(dtb.shape),
                  _const_spec(alog.shape), _const_spec(dsk.shape)],
        out_specs=[pl.BlockSpec((1, L, D_INNER), lambda b, c: (b, c, 0)), st_spec],
        out_shape=[jax.ShapeDtypeStruct((nb, t, D_INNER), F32),
                   jax.ShapeDtypeStruct(ssm0.shape, F32)],
        scratch_shapes=[pltpu.VMEM((L + SUBLANES, CONV_DIM), F32)],
        compiler_params=_params(("parallel", "arbitrary")),
        name="ssd_scan",
    )(xbc, dtr, conv0, ssm0, cw, cb, dtb, alog, dsk)


def _ssd_t(xbc, dtr, conv0, ssm0, cw, cb, dtb, alog, dsk):
    nb, t = xbc.shape[0], xbc.shape[1]
    L = SSD_CHUNK
    n_sub = SSD_CHUNKS_PER_STEP if t % (L * SSD_CHUNKS_PER_STEP) == 0 else 1
    rows = L * n_sub
    assert L == LANES and t % rows == 0
    dsk_t = jnp.broadcast_to(dsk.reshape(D_INNER, 1), (D_INNER, LANES))
    st_spec = pl.BlockSpec((1, N_SSM_HEADS, SSM_HEAD_DIM, D_STATE), lambda b, c: (b, 0, 0, 0))
    has_init = conv0 is not None
    init_args = (conv0, ssm0) if has_init else ()
    init_specs = [pl.BlockSpec((1, SUBLANES, CONV_DIM), lambda b, c: (b, 0, 0)),
                  st_spec] if has_init else []
    return pl.pallas_call(
        functools.partial(_ssd_t_body, L=L, n_sub=n_sub, has_init=has_init),
        grid=(nb, t // rows),
        in_specs=[pl.BlockSpec((1, rows, CONV_DIM), lambda b, c: (b, c, 0)),
                  pl.BlockSpec((1, rows, LANES), lambda b, c: (b, c, 0))] + init_specs + [
                  _const_spec(cw.shape), _const_spec(cb.shape),
                  _const_spec(dtb.shape), _const_spec(alog.shape), _const_spec(dsk_t.shape)],
        out_specs=[pl.BlockSpec((1, rows, D_INNER), lambda b, c: (b, c, 0)), st_spec],
        out_shape=[jax.ShapeDtypeStruct((nb, t, D_INNER), F32),
                   jax.ShapeDtypeStruct((nb, N_SSM_HEADS, SSM_HEAD_DIM, D_STATE), F32)],
        scratch_shapes=[pltpu.VMEM((L + SUBLANES, CONV_DIM), F32), pltpu.VMEM((D_INNER, L), F32)],
        compiler_params=_params(("parallel", "arbitrary")),
        name="ssd_scan_t",
    )(xbc, dtr, *init_args, cw, cb, dtb, alog, dsk_t)


def _ffn(x1, nf_ref, wgu_ref, wd_ref):
    h = (x1 * _inv_rms(x1) * nf_ref[...]).astype(BF16)
    acts = []
    for c in range(D_FF // FFN_CHUNK):
        lo = c * FFN_CHUNK
        gate = jnp.dot(h, wgu_ref[0, :, lo:lo + FFN_CHUNK], preferred_element_type=F32)
        up = jnp.dot(h, wgu_ref[0, :, D_FF + lo:D_FF + lo + FFN_CHUNK], preferred_element_type=F32)
        acts.append((_silu(gate) * up).astype(BF16))
    return x1 + jnp.dot(jnp.concatenate(acts, axis=1), wd_ref[0], preferred_element_type=F32)


def _ssm_out_ffn_body(y_ref, z_ref, x_ref, gn_ref, wo_ref, nf_ref, wgu_ref, wd_ref, o_ref):
    y = y_ref[...] * z_ref[...]
    gw = D_INNER // N_SSM_GROUPS
    parts = []
    for g in range(N_SSM_GROUPS):
        yg = y[:, g * gw:(g + 1) * gw]
        parts.append(yg * _inv_rms(yg))
    y = (jnp.concatenate(parts, axis=1) * gn_ref[...]).astype(BF16)
    x1 = x_ref[...] + jnp.dot(y, wo_ref[...], preferred_element_type=F32)
    o_ref[...] = _ffn(x1, nf_ref, wgu_ref, wd_ref)


def _attn_out_ffn_body(a_ref, x_ref, wo_ref, nf_ref, wgu_ref, wd_ref, nfin_ref, o_ref):
    x1 = x_ref[...] + jnp.dot(a_ref[...], wo_ref[...], preferred_element_type=F32)
    x2 = _ffn(x1, nf_ref, wgu_ref, wd_ref)
    o_ref[...] = x2 * _inv_rms(x2) * nfin_ref[...]


def _ssm_out_ffn(y, z, x, gn, wo, nf, wgu, wd, layer, tm):
    m = x.shape[0]
    row = lambda w: pl.BlockSpec((tm, w), lambda i: (i, 0))
    return pl.pallas_call(
        _ssm_out_ffn_body,
        grid=(m // tm,),
        in_specs=[row(D_INNER), row(D_INNER), row(D_MODEL), _const_spec(gn.shape),
                  _const_spec(wo.shape), _const_spec(nf.shape), _layer_spec(wgu.shape, layer),
                  _layer_spec(wd.shape, layer)],
        out_specs=row(D_MODEL),
        out_shape=jax.ShapeDtypeStruct((m, D_MODEL), F32),
        compiler_params=_params(("parallel",)),
        name="ssm_out_ffn",
    )(y, z, x, gn, wo, nf, wgu, wd)


def _attn_out_ffn(a, x, wo, nf, wgu, wd, layer, nfin, tm):
    m = x.shape[0]
    row = lambda w: pl.BlockSpec((tm, w), lambda i: (i, 0))
    return pl.pallas_call(
        _attn_out_ffn_body,
        grid=(m // tm,),
        in_specs=[row(D_MODEL), row(D_MODEL), _const_spec(wo.shape), _const_spec(nf.shape),
                  _layer_spec(wgu.shape, layer), _layer_spec(wd.shape, layer),
                  _const_spec(nfin.shape)],
        out_specs=row(D_MODEL),
        out_shape=jax.ShapeDtypeStruct((m, D_MODEL), F32),
        compiler_params=_params(("parallel",)),
        name="attn_out_ffn",
    )(a, x, wo, nf, wgu, wd, nfin)


def _rope(x, cos, sin_signed, first_half):
    outs = []
    for c in range(x.shape[1] // LANES):
        xc = x[:, c * LANES:(c + 1) * LANES]
        partner = jnp.where(first_half, pltpu.roll(xc, LANES - HEAD_DIM // 2, 1),
                            pltpu.roll(xc, HEAD_DIM // 2, 1))
        outs.append(xc * cos + partner * sin_signed)
    return jnp.concatenate(outs, axis=1)


def _kvq_compute(x_ref, gkv_ref, gq_ref, wkv_ref, wq_ref, cos_ref, sin_ref):
    x = x_ref[...]
    xn = x * _inv_rms(x)
    hk = (xn * gkv_ref[...]).astype(BF16)
    hq = (xn * gq_ref[...]).astype(BF16)
    cos = cos_ref[...]
    sin = sin_ref[...]
    lane = lax.broadcasted_iota(jnp.int32, cos.shape, 1)
    first_half = (lane % HEAD_DIM) < (HEAD_DIM // 2)
    kv = jnp.dot(hk, wkv_ref[...], preferred_element_type=F32)
    k = _rope(kv[:, :KV_DIM], cos, sin, first_half)
    q = _rope(jnp.dot(hq, wq_ref[...], preferred_element_type=F32), cos, sin, first_half)
    return k, kv[:, KV_DIM:], q


def _kvq_body(x_ref, gkv_ref, gq_ref, wkv_ref, wq_ref, cos_ref, sin_ref, k_ref, v_ref, q_ref):
    k, v, q = _kvq_compute(x_ref, gkv_ref, gq_ref, wkv_ref, wq_ref, cos_ref, sin_ref)
    k_ref[...] = k
    v_ref[...] = v
    q_ref[...] = q


def _kvq_prompt_body(x_ref, gkv_ref, gq_ref, wkv_ref, wq_ref, cos_ref, sin_ref,
                     kt_ref, vt32_ref, qt_ref, kh_ref, vt_ref, km_ref):
    k, v, q = _kvq_compute(x_ref, gkv_ref, gq_ref, wkv_ref, wq_ref, cos_ref, sin_ref)
    kt_ref[0] = k.T
    v_t = v.T
    vt32_ref[0] = v_t
    qt_ref[0] = q.T
    vt = v_t.astype(BF16)
    col = lax.broadcasted_iota(jnp.int32, (MOBA_BLOCK, LANES - HEAD_DIM), 1)
    row = lax.broadcasted_iota(jnp.int32, (V_AUG - HEAD_DIM, MOBA_BLOCK), 0)
    ones_row = jnp.where(row == 0, 1.0, 0.0).astype(BF16)
    for sub in range(KVQ_BLOCKS_PER_STEP):
        blk = pl.program_id(1) * KVQ_BLOCKS_PER_STEP + sub
        toks = slice(sub * MOBA_BLOCK, (sub + 1) * MOBA_BLOCK)
        onehot = jnp.where(col == blk, 1.0, 0.0).astype(BF16)
        for kvh in range(N_KV_HEADS):
            kh_ref[0, sub, kvh] = jnp.concatenate(
                [k[toks, kvh * HEAD_DIM:(kvh + 1) * HEAD_DIM].astype(BF16), onehot], axis=1)
            vt_ref[0, sub, kvh] = jnp.concatenate(
                [vt[kvh * HEAD_DIM:(kvh + 1) * HEAD_DIM, toks], ones_row], axis=0)
        km_ref[0, sub] = jnp.sum(k[toks, :], axis=0, keepdims=True) * (1.0 / MOBA_BLOCK)


def _kvq(x, gkv, gq, wkv, wq, cos, sin, tm):
    m = x.shape[0]
    row = lambda w: pl.BlockSpec((tm, w), lambda i: (i, 0))
    tab = pl.BlockSpec((tm, LANES), lambda i: (0, 0))
    return pl.pallas_call(
        _kvq_body,
        grid=(m // tm,),
        in_specs=[row(D_MODEL), _const_spec(gkv.shape), _const_spec(gq.shape),
                  _const_spec(wkv.shape), _const_spec(wq.shape), tab, tab],
        out_specs=[row(KV_DIM), row(KV_DIM), row(D_MODEL)],
        out_shape=[jax.ShapeDtypeStruct((m, KV_DIM), F32), jax.ShapeDtypeStruct((m, KV_DIM), F32),
                   jax.ShapeDtypeStruct((m, D_MODEL), F32)],
        compiler_params=_params(("parallel",)),
        name="kvq_proj",
    )(x, gkv, gq, wkv, wq, cos, sin)


def _kvq_prompt(x, b, t, gkv, gq, wkv, wq, cos, sin):
    nb = t // MOBA_BLOCK
    per = KVQ_BLOCKS_PER_STEP
    assert nb % per == 0
    steps = nb // per
    tm = per * MOBA_BLOCK
    row = lambda w: pl.BlockSpec((tm, w), lambda bi, i: (bi * steps + i, 0))
    tab = pl.BlockSpec((tm, LANES), lambda bi, i: (i, 0))
    return pl.pallas_call(
        _kvq_prompt_body,
        grid=(b, steps),
        in_specs=[row(D_MODEL), _const_spec(gkv.shape), _const_spec(gq.shape),
                  _const_spec(wkv.shape), _const_spec(wq.shape), tab, tab],
        out_specs=[pl.BlockSpec((1, KV_DIM, tm), lambda bi, i: (bi, 0, i)),
                   pl.BlockSpec((1, KV_DIM, tm), lambda bi, i: (bi, 0, i)),
                   pl.BlockSpec((1, D_MODEL, tm), lambda bi, i: (bi, 0, i)),
                   pl.BlockSpec((1, per, N_KV_HEADS, MOBA_BLOCK, LANES),
                                lambda bi, i: (bi, i, 0, 0, 0)),
                   pl.BlockSpec((1, per, N_KV_HEADS, V_AUG, MOBA_BLOCK),
                                lambda bi, i: (bi, i, 0, 0, 0)),
                   pl.BlockSpec((1, per, 1, KV_DIM), lambda bi, i: (bi, i, 0, 0))],
        out_shape=[jax.ShapeDtypeStruct((b, KV_DIM, t), F32),
                   jax.ShapeDtypeStruct((b, KV_DIM, t), F32),
                   jax.ShapeDtypeStruct((b, D_MODEL, t), F32),
                   jax.ShapeDtypeStruct((b, nb, N_KV_HEADS, MOBA_BLOCK, LANES), BF16),
                   jax.ShapeDtypeStruct((b, nb, N_KV_HEADS, V_AUG, MOBA_BLOCK), BF16),
                   jax.ShapeDtypeStruct((b, nb, 1, KV_DIM), F32)],
        compiler_params=_params(("parallel", "parallel")),
        name="kvq_proj_prompt",
    )(x, gkv, gq, wkv, wq, cos, sin)


def _moba_prompt_body(qt_ref, kh_ref, vt_ref, km_ref, o_ref, s_a, s_b, acc_a, acc_b, *, nb):
    i = pl.program_id(1)
    g_per_unit = ATTN_UNIT_HEADS
    rows = g_per_unit * MOBA_BLOCK
    n_units = N_HEADS // g_per_unit
    scale = (HEAD_DIM ** -0.5) * LOG2_E
    blk_row = lax.broadcasted_iota(jnp.int32, (nb, rows), 0)
    key_i = lax.broadcasted_iota(jnp.int32, (MOBA_BLOCK, rows), 0)
    tok_i = lax.broadcasted_iota(jnp.int32, (MOBA_BLOCK, rows), 1) % MOBA_BLOCK
    own_ok = key_i <= tok_i
    fold = lambda s: jnp.max(s.reshape(MOBA_BLOCK // SUBLANES, SUBLANES, rows), axis=0)

    def setup(u):
        kvh = (u * g_per_unit) // Q_PER_KV
        lo = kvh * HEAD_DIM
        base = u * g_per_unit * HEAD_DIM
        qt2 = qt_ref[0, base:base + g_per_unit * HEAD_DIM, :]
        qt = jnp.concatenate([qt2[g * HEAD_DIM:(g + 1) * HEAD_DIM, :] for g in range(g_per_unit)],
                             axis=1)
        km = km_ref[0, :, lo:lo + HEAD_DIM]
        gate = jnp.dot(km, qt, preferred_element_type=F32,
                       precision=lax.Precision.HIGHEST)
        sel = _top3_mask(gate, blk_row < i, axis=0)
        pen = jnp.where(sel, 0.0, MASKED).astype(BF16)
        qb = (qt * scale).astype(BF16)
        q_own = jnp.concatenate([qb, jnp.zeros((LANES - HEAD_DIM, rows), BF16)], axis=0)
        q_aug = jnp.concatenate([qb, pen, jnp.zeros((LANES - HEAD_DIM - nb, rows), BF16)], axis=0)
        s_own = jnp.dot(kh_ref[0, i, kvh], q_own, preferred_element_type=F32)
        s_own = jnp.where(own_ok, s_own, MASKED)
        s_bufs[u % 2][nb] = s_own
        return kvh, q_aug, fold(s_own)

    def scores(u, kvh, q_aug, j0, n, m8):
        for t in range(n):
            s = jnp.dot(kh_ref[0, j0 + t, kvh], q_aug, preferred_element_type=F32)
            s_bufs[u % 2][j0 + t] = s
            m8 = jnp.maximum(m8, fold(s))
        return m8

    def weighted(u, kvh, m, j0, n):
        s_buf = s_bufs[u % 2]
        p = jnp.concatenate([jnp.exp2(s_buf[j0 + t] - m).astype(BF16) for t in range(n)], axis=0)
        v = jnp.concatenate([vt_ref[0, j0 + t, kvh] for t in range(n)], axis=1)
        acc_bufs[u % 2][...] += jnp.dot(v, p, preferred_element_type=F32)

    def over_blocks(step, carry):
        n4, n2 = i // 4, i // 2
        carry = lax.fori_loop(0, n4, lambda q, c: step(4 * q + 2, 2, step(4 * q, 2, c)), carry)
        carry = lax.fori_loop(2 * n4, n2, lambda q, c: step(2 * q, 2, c), carry)
        return lax.fori_loop(2 * n2, i, lambda j, c: step(j, 1, c), carry)

    s_bufs = (s_a, s_b)
    acc_bufs = (acc_a, acc_b)
    kvh, q_aug, m8 = setup(0)
    m8 = over_blocks(functools.partial(scores, 0, kvh, q_aug), m8)
    for u in range(n_units):
        m = jnp.max(m8, axis=0, keepdims=True)
        p_own = jnp.exp2(s_bufs[u % 2][nb] - m).astype(BF16)
        acc_bufs[u % 2][...] = jnp.dot(vt_ref[0, i, kvh], p_own, preferred_element_type=F32)
        if u + 1 < n_units:
            kvh_n, q_aug_n, m8_n = setup(u + 1)

            def both(j0, n, m8c, u=u, kvh=kvh, m=m, kvh_n=kvh_n, q_aug_n=q_aug_n):
                m8c = scores(u + 1, kvh_n, q_aug_n, j0, n, m8c)
                weighted(u, kvh, m, j0, n)
                return m8c

            m8_n = over_blocks(both, m8_n)
        else:
            def last(j0, n, carry, u=u, kvh=kvh, m=m):
                weighted(u, kvh, m, j0, n)
                return carry

            over_blocks(last, 0)
        acc = acc_bufs[u % 2][...]
        out_t = acc[:HEAD_DIM, :] / acc[HEAD_DIM:HEAD_DIM + 1, :]
        out2 = jnp.concatenate([out_t[:, g * MOBA_BLOCK:(g + 1) * MOBA_BLOCK]
                                for g in range(g_per_unit)], axis=0)
        base = u * g_per_unit * HEAD_DIM
        o_ref[0, :, base:base + g_per_unit * HEAD_DIM] = out2.T.astype(o_ref.dtype)
        if u + 1 < n_units:
            kvh, m8 = kvh_n, m8_n


def _moba_prompt(qt, kh, vt, km):
    b, t = qt.shape[0], qt.shape[2]
    nb = t // MOBA_BLOCK
    rows = ATTN_UNIT_HEADS * MOBA_BLOCK
    return pl.pallas_call(
        functools.partial(_moba_prompt_body, nb=nb),
        grid=(b, nb),
        in_specs=[pl.BlockSpec((1, D_MODEL, MOBA_BLOCK), lambda bi, i: (bi, 0, i)),
                  pl.BlockSpec((1, nb, N_KV_HEADS, MOBA_BLOCK, LANES),
                               lambda bi, i: (bi, 0, 0, 0, 0)),
                  pl.BlockSpec((1, nb, N_KV_HEADS, V_AUG, MOBA_BLOCK),
                               lambda bi, i: (bi, 0, 0, 0, 0)),
                  pl.BlockSpec((1, nb, KV_DIM), lambda bi, i: (bi, 0, 0))],
        out_specs=pl.BlockSpec((1, MOBA_BLOCK, D_MODEL), lambda bi, i: (bi, i, 0)),
        out_shape=jax.ShapeDtypeStruct((b, t, D_MODEL), BF16),
        scratch_shapes=[pltpu.VMEM((nb + 1, MOBA_BLOCK, rows), F32),
                        pltpu.VMEM((nb + 1, MOBA_BLOCK, rows), F32),
                        pltpu.VMEM((V_AUG, rows), F32), pltpu.VMEM((V_AUG, rows), F32)],
        compiler_params=_params(("parallel", "arbitrary")),
        name="moba_prompt",
    )(qt, kh, vt, km)


def _page_copy(cache_hbm, buf, sem, page, slot, p):
    return pltpu.make_async_copy(cache_hbm.at[page], buf.at[slot, p], sem)


def _moba_sample_body(pt_ref, q_ref, kn_ref, vn_ref, ck_hbm, cv_hbm, o_ref,
                      kbuf, vbuf, s_scr, km_scr, new_scr, sems, *, n_pages, n_new):
    b = pl.program_id(0)
    slot = b % 2
    rows = q_ref.shape[1]
    ppb = MOBA_BLOCK // PAGE_SIZE
    nb = n_pages // ppb
    scale = HEAD_DIM ** -0.5

    def fetch(seq, to_slot):
        def start_page(p, carry):
            page = pt_ref[seq, p]
            _page_copy(ck_hbm, kbuf, sems.at[0, to_slot], page, to_slot, p).start()
            _page_copy(cv_hbm, vbuf, sems.at[1, to_slot], page, to_slot, p).start()
            return carry
        lax.fori_loop(0, n_pages, start_page, 0, unroll=8)

    @pl.when(b == 0)
    def _():
        fetch(0, 0)

    @pl.when(b + 1 < pl.num_programs(0))
    def _():
        fetch(b + 1, 1 - slot)

    q64 = q_ref[0]
    qt = jnp.concatenate([q64] * N_KV_HEADS, axis=1)
    r_i = lax.broadcasted_iota(jnp.int32, qt.shape, 0)
    c_i = lax.broadcasted_iota(jnp.int32, qt.shape, 1)
    qpad = jnp.where(c_i // HEAD_DIM == r_i // (Q_PER_KV * n_new), qt, 0.0)
    qb = (qpad * scale).astype(BF16)

    def wait_k(p, carry):
        _page_copy(ck_hbm, kbuf, sems.at[0, slot], 0, slot, p).wait()
        return carry

    lax.fori_loop(0, n_pages, wait_k, 0)

    km_scr[...] = jnp.zeros(km_scr.shape, F32)
    for j in range(nb):
        kt = jnp.concatenate([kbuf[slot, ppb * j + u] for u in range(ppb)], axis=1)
        km_scr[:, j:j + 1] = jnp.sum(kt, axis=1, keepdims=True) * (1.0 / MOBA_BLOCK)
        s_scr[:, j * MOBA_BLOCK:(j + 1) * MOBA_BLOCK] = jnp.dot(
            qb, kt.astype(BF16), preferred_element_type=F32)

    gate = jnp.dot(qpad, km_scr[...], preferred_element_type=F32,
                   precision=lax.Precision.HIGHEST)
    sel = _top3_mask(gate, lax.broadcasted_iota(jnp.int32, gate.shape, 1) < nb)
    pen = jnp.where(sel, 0.0, NEG_INF)

    new_scr[...] = jnp.zeros(new_scr.shape, F32)
    new_scr[0:n_new, :] = kn_ref[0]
    s_new = lax.dot_general(qb, new_scr[...].astype(BF16), NT_DIMS, preferred_element_type=F32)
    nr = lax.broadcasted_iota(jnp.int32, s_new.shape, 0)
    ncol = lax.broadcasted_iota(jnp.int32, s_new.shape, 1)
    s_new = jnp.where(ncol <= nr % n_new, s_new, NEG_INF)

    def lane_fold(x, op):
        out = x[:, :LANES]
        for c in range(1, x.shape[1] // LANES):
            out = op(out, x[:, c * LANES:(c + 1) * LANES])
        return out

    m_part = s_new
    for j in range(nb):
        sj = s_scr[:, j * MOBA_BLOCK:(j + 1) * MOBA_BLOCK] + pen[:, j:j + 1]
        m_part = jnp.maximum(m_part, lane_fold(sj, jnp.maximum))
    m = jnp.max(m_part, axis=1, keepdims=True)

    def wait_v(p, carry):
        _page_copy(cv_hbm, vbuf, sems.at[1, slot], 0, slot, p).wait()
        return carry

    lax.fori_loop(0, n_pages, wait_v, 0)

    p_new = jnp.exp(s_new - m)
    l_part = p_new
    new_scr[0:n_new, :] = vn_ref[0]
    acc = jnp.dot(p_new.astype(BF16), new_scr[...].astype(BF16), preferred_element_type=F32)
    for j in range(nb):
        pj = jnp.exp(s_scr[:, j * MOBA_BLOCK:(j + 1) * MOBA_BLOCK] + pen[:, j:j + 1] - m)
        l_part = l_part + lane_fold(pj, jnp.add)
        vt = jnp.concatenate([vbuf[slot, ppb * j + u] for u in range(ppb)], axis=1).astype(BF16)
        acc = acc + lax.dot_general(pj.astype(BF16), vt, NT_DIMS, preferred_element_type=F32)
    acc = acc / jnp.sum(l_part, axis=1, keepdims=True)
    ro = lax.broadcasted_iota(jnp.int32, (rows, HEAD_DIM), 0) // (Q_PER_KV * n_new)
    out = jnp.zeros((rows, HEAD_DIM), F32)
    for kvh in range(N_KV_HEADS):
        out = jnp.where(ro == kvh, acc[:, kvh * HEAD_DIM:(kvh + 1) * HEAD_DIM], out)
    o_ref[0] = out


def _moba_sample(page_table, q_rows, k_new, v_new, cache_k, cache_v):
    nseq, n_pages = page_table.shape
    rows = q_rows.shape[1]
    n_new = k_new.shape[1]
    past = n_pages * PAGE_SIZE
    grid_spec = pltpu.PrefetchScalarGridSpec(
        num_scalar_prefetch=1,
        grid=(nseq,),
        in_specs=[pl.BlockSpec((1, rows, HEAD_DIM), lambda b, pt: (b, 0, 0)),
                  pl.BlockSpec((1, n_new, KV_DIM), lambda b, pt: (b, 0, 0)),
                  pl.BlockSpec((1, n_new, KV_DIM), lambda b, pt: (b, 0, 0)),
                  pl.BlockSpec(memory_space=pl.ANY),
                  pl.BlockSpec(memory_space=pl.ANY)],
        out_specs=pl.BlockSpec((1, rows, HEAD_DIM), lambda b, pt: (b, 0, 0)),
        scratch_shapes=[pltpu.VMEM((2, n_pages, KV_DIM, PAGE_SIZE), F32),
                        pltpu.VMEM((2, n_pages, KV_DIM, PAGE_SIZE), F32),
                        pltpu.VMEM((rows, past), F32),
                        pltpu.VMEM((KV_DIM, LANES), F32),
                        pltpu.VMEM((LANES, KV_DIM), F32),
                        pltpu.SemaphoreType.DMA((2, 2))],
    )
    return pl.pallas_call(
        functools.partial(_moba_sample_body, n_pages=n_pages, n_new=n_new),
        grid_spec=grid_spec,
        out_shape=jax.ShapeDtypeStruct((nseq, rows, HEAD_DIM), F32),
        compiler_params=_params(("arbitrary",)),
        name="moba_sample",
    )(page_table, q_rows, k_new, v_new, cache_k, cache_v)


def _rope_tables(pos):
    half = HEAD_DIM // 2
    inv = ROPE_THETA ** (-np.arange(half, dtype=np.float64) / half)
    ang = np.asarray(pos, dtype=np.float64)[:, None] * inv[None, :]
    cos, sin = np.cos(ang), np.sin(ang)
    cos_h = np.concatenate([cos, cos], axis=1)
    sin_h = np.concatenate([-sin, sin], axis=1)
    reps = LANES // HEAD_DIM
    return (jnp.asarray(np.tile(cos_h, (1, reps)), F32), jnp.asarray(np.tile(sin_h, (1, reps)), F32))


def _prep_weights(norm_mix, norm_ffn, w_in_ssm, conv_w, conv_b, dt_bias, a_log, d_skip, norm_ssm,
                  w_out_ssm, norm_kv, w_kv, w_q, w_o, w_gu, w_down, norm_final):
    pad_h = LANES - N_SSM_HEADS
    return dict(
        g_mix0=norm_mix[0][None], g_mix1=norm_mix[1][None],
        g_ffn0=norm_ffn[0][None], g_ffn1=norm_ffn[1][None],
        w_in=w_in_ssm.astype(BF16),
        cw=conv_w[0], cb=conv_b[0][None],
        dtb=jnp.pad(dt_bias[0], (0, pad_h))[None], alog=jnp.pad(a_log[0], (0, pad_h))[None],
        dsk=jnp.repeat(d_skip[0], SSM_HEAD_DIM)[None],
        gn=norm_ssm[0][None], wo_ssm=w_out_ssm[0].astype(BF16),
        g_kv=norm_kv[None], wkv=w_kv.astype(BF16), wq=w_q[0].astype(BF16),
        wo=w_o[0].astype(BF16),
        wgu=w_gu.astype(BF16), wd=w_down.astype(BF16),
        g_fin=norm_final[None],
    )


def _ssd_layer_long(x, conv_in, ssm0, w, tm):
    b, t, _ = x.shape
    keep = D_CONV - 1
    assert t >= keep
    xf = x.reshape(b * t, D_MODEL)
    z, xbc, dtr = _inproj(xf, w["g_mix0"], w["w_in"], 0, tm)
    xbc3 = xbc.reshape(b, t, CONV_DIM)
    conv0 = None if conv_in is None else jnp.pad(conv_in, ((0, 0), (SUBLANES - keep, 0), (0, 0)))
    y, ssm_new = _ssd_t(xbc3, dtr.reshape(b, t, LANES), conv0, ssm0, w["cw"], w["cb"], w["dtb"],
                        w["alog"], w["dsk"])
    x2 = _ssm_out_ffn(y.reshape(b * t, D_INNER), z, xf, w["gn"], w["wo_ssm"], w["g_ffn0"],
                      w["wgu"], w["wd"], 0, tm)
    return x2, xbc3[:, t - keep:], ssm_new


def _ssd_layer(x, conv_in, ssm0, w, tm, L, valid):
    b, t, _ = x.shape
    xf = x.reshape(b * t, D_MODEL)
    z, xbc, dtr = _inproj(xf, w["g_mix0"], w["w_in"], 0, tm)
    tp = -(-t // L) * L
    xbc3 = xbc.reshape(b, t, CONV_DIM)
    dtr3 = dtr.reshape(b, t, LANES)
    if tp != t:
        xbc3 = jnp.pad(xbc3, ((0, 0), (0, tp - t), (0, 0)))
        dtr3 = jnp.pad(dtr3, ((0, 0), (0, tp - t), (0, 0)))
    conv0 = jnp.pad(conv_in, ((0, 0), (SUBLANES - (D_CONV - 1), 0), (0, 0)))
    y, ssm_new = _ssd(xbc3, dtr3, conv0, ssm0, w["cw"], w["cb"], w["dtb"], w["alog"], w["dsk"],
                      L, valid)
    y = y[:, :t].reshape(b * t, D_INNER)
    x2 = _ssm_out_ffn(y, z, xf, w["gn"], w["wo_ssm"], w["g_ffn0"], w["wgu"], w["wd"], 0, tm)
    keep = D_CONV - 1
    conv_new = jnp.concatenate([conv_in[:, t:], xbc.reshape(b, t, CONV_DIM)[:, max(0, t - keep):]],
                               axis=1)
    return x2, conv_new, ssm_new


def kernel(x_prompt, x_sample, state_conv, state_ssm, cache_k, cache_v, page_table, norm_mix,
           norm_ffn, w_in_ssm, conv_w, conv_b, dt_bias, a_log, d_skip, norm_ssm, w_out_ssm,
           norm_kv, w_kv, w_q, w_o, w_gu, w_down, norm_final):
    w = _prep_weights(norm_mix, norm_ffn, w_in_ssm, conv_w, conv_b, dt_bias, a_log, d_skip,
                      norm_ssm, w_out_ssm, norm_kv, w_kv, w_q, w_o, w_gu, w_down, norm_final)
    bp, tp, _ = x_prompt.shape
    bs, ts, _ = x_sample.shape
    past_len = page_table.shape[1] * PAGE_SIZE

    tm_p = PROMPT_ROW_TILE
    x2_p, conv_p, ssm_p = _ssd_layer_long(x_prompt, None, None, w, tm_p)
    cos_p, sin_p = _rope_tables(np.arange(tp))
    kt_p, vt32_p, qt_p, kh_p, vt_p, km_p = _kvq_prompt(x2_p, bp, tp, w["g_kv"], w["g_mix1"],
                                                       w["wkv"], w["wq"], cos_p, sin_p)
    k_p = kt_p.reshape(bp, N_KV_HEADS, HEAD_DIM, tp).transpose(0, 3, 1, 2)
    v_p = vt32_p.reshape(bp, N_KV_HEADS, HEAD_DIM, tp).transpose(0, 3, 1, 2)
    attn_p = _moba_prompt(qt_p, kh_p, vt_p, km_p.reshape(bp, tp // MOBA_BLOCK, KV_DIM))
    y_p = _attn_out_ffn(attn_p.reshape(bp * tp, D_MODEL), x2_p, w["wo"], w["g_ffn1"], w["wgu"],
                        w["wd"], 1, w["g_fin"], tm_p)

    tm_s = bs * ts
    x2_s, conv_s, ssm_s = _ssd_layer(x_sample, state_conv[0], state_ssm[0], w, tm_s, SUBLANES, ts)
    cos_s, sin_s = _rope_tables(past_len + np.tile(np.arange(ts), bs))
    k_s, v_s, q_s = _kvq(x2_s, w["g_kv"], w["g_mix1"], w["wkv"], w["wq"], cos_s, sin_s, tm_s)
    q_rows = q_s.reshape(bs, ts, N_HEADS, HEAD_DIM).transpose(0, 2, 1, 3).reshape(
        bs, N_HEADS * ts, HEAD_DIM)
    attn_rows = _moba_sample(page_table, q_rows, k_s.reshape(bs, ts, KV_DIM),
                             v_s.reshape(bs, ts, KV_DIM),
                             cache_k.transpose(0, 2, 3, 1).reshape(-1, KV_DIM, PAGE_SIZE),
                             cache_v.transpose(0, 2, 3, 1).reshape(-1, KV_DIM, PAGE_SIZE))
    attn_s = attn_rows.reshape(bs, N_HEADS, ts, HEAD_DIM).transpose(
        0, 2, 1, 3).reshape(bs * ts, D_MODEL).astype(BF16)
    y_s = _attn_out_ffn(attn_s, x2_s, w["wo"], w["g_ffn1"], w["wgu"], w["wd"], 1,
                        w["g_fin"], tm_s)

    return (y_p.reshape(bp, tp, D_MODEL), y_s.reshape(bs, ts, D_MODEL),
            conv_p[None], ssm_p[None],
            k_p, v_p,
            conv_s[None], ssm_s[None],
            k_s.reshape(bs, ts, N_KV_HEADS, HEAD_DIM), v_s.reshape(bs, ts, N_KV_HEADS, HEAD_DIM))
```

```python
import functools

import jax
import jax.numpy as jnp
import numpy as np
from jax import lax
from jax.experimental import pallas as pl
from jax.experimental.pallas import tpu as pltpu

F32 = jnp.float32
BF16 = jnp.bfloat16

D_MODEL = 1024
D_INNER = 2048
SSM_HEAD_DIM = 64
N_SSM_HEADS = 32
N_SSM_GROUPS = 4
HEADS_PER_GROUP = 8
D_STATE = 128
D_CONV = 4
SSD_CHUNK = 128
GN = N_SSM_GROUPS * D_STATE
CONV_DIM = D_INNER + 2 * GN
HEAD_DIM = 64
N_HEADS = 16
N_KV_HEADS = 4
Q_PER_KV = 4
KV_DIM = N_KV_HEADS * HEAD_DIM
MOBA_BLOCK = 256
MOBA_TOP_K = 3
ROPE_THETA = 10000.0
D_FF = 2816
EPS = 1e-6
PAGE_SIZE = 128

LANES = 128
SUBLANES = 8
VMEM_LIMIT = 56 * 1024 * 1024

NT_DIMS = (((1,), (1,)), ((), ()))
TN_DIMS = (((0,), (0,)), ((), ()))
NEG_INF = float("-inf")
MASKED = -1e30
LOG2_E = 1.4426950408889634
V_AUG = HEAD_DIM + 16
PROMPT_ROW_TILE = 512
FFN_CHUNK = 256
SSD_CHUNKS_PER_STEP = 2
KVQ_BLOCKS_PER_STEP = 4
ATTN_UNIT_HEADS = 4


def _params(sem):
    return pltpu.CompilerParams(dimension_semantics=sem, vmem_limit_bytes=VMEM_LIMIT)


def _const_spec(shape):
    nd = len(shape)
    return pl.BlockSpec(shape, lambda *_: (0,) * nd, pipeline_mode=pl.Buffered(1))


def _layer_spec(shape, layer):
    nd = len(shape)
    return pl.BlockSpec((1,) + tuple(shape[1:]), lambda *_: (layer,) + (0,) * (nd - 1),
                        pipeline_mode=pl.Buffered(1))


def _silu(x):
    return x * (1.0 / (1.0 + jnp.exp(-x)))


def _inv_rms(x):
    return lax.rsqrt(jnp.mean(x * x, axis=-1, keepdims=True) + EPS)


def _split3(x):
    hi = x.astype(BF16)
    r = x - hi.astype(F32)
    mid = r.astype(BF16)
    lo = (r - mid.astype(F32)).astype(BF16)
    return hi, mid, lo


def _dot01(a01, x, dims):
    out = None
    for p in _split3(x):
        t = lax.dot_general(a01, p, dims, preferred_element_type=F32)
        out = t if out is None else out + t
    return out


def _top3_mask(gate, valid, axis=1):
    nb = gate.shape[axis]
    col = lax.broadcasted_iota(jnp.int32, gate.shape, axis).astype(F32)
    g = jnp.where(valid, gate, NEG_INF)
    sel = jnp.zeros(gate.shape, F32)
    for _ in range(MOBA_TOP_K):
        m = jnp.max(g, axis=axis, keepdims=True)
        idx = jnp.min(jnp.where(g == m, col, float(nb)), axis=axis, keepdims=True)
        pick = col == idx
        sel = jnp.where(pick, 1.0, sel)
        g = jnp.where(pick, NEG_INF, g)
    return jnp.logical_and(sel > 0.5, valid)


def _inproj_body(x_ref, g_ref, w_ref, z_ref, xbc_ref, dt_ref):
    x = x_ref[...]
    h = (x * _inv_rms(x) * g_ref[...]).astype(BF16)
    z_ref[...] = _silu(jnp.dot(h, w_ref[0, :, :D_INNER], preferred_element_type=F32))
    xbc_ref[...] = jnp.dot(h, w_ref[0, :, D_INNER:D_INNER + CONV_DIM], preferred_element_type=F32)
    dt = jnp.dot(h, w_ref[0, :, D_INNER + CONV_DIM:], preferred_element_type=F32)
    dt_ref[...] = jnp.concatenate(
        [dt, jnp.zeros((dt.shape[0], LANES - N_SSM_HEADS), F32)], axis=1)


def _inproj(x, g, w_in, layer, tm):
    m = x.shape[0]
    row = lambda w: pl.BlockSpec((tm, w), lambda i: (i, 0))
    return pl.pallas_call(
        _inproj_body,
        grid=(m // tm,),
        in_specs=[row(D_MODEL), _const_spec((1, D_MODEL)), _layer_spec(w_in.shape, layer)],
        out_specs=[row(D_INNER), row(CONV_DIM), row(LANES)],
        out_shape=[jax.ShapeDtypeStruct((m, D_INNER), F32),
                   jax.ShapeDtypeStruct((m, CONV_DIM), F32),
                   jax.ShapeDtypeStruct((m, LANES), F32)],
        compiler_params=_params(("parallel",)),
        name="ssd_inproj",
    )(x, g, w_in)


def _causal_conv(pad_ref, xbc, cw_ref, cb_ref, rows):
    pad_ref[SUBLANES:SUBLANES + rows, :] = xbc
    padded = pad_ref[...]
    conv = cb_ref[...]
    for i in range(D_CONV):
        shift = D_CONV - 1 - i
        tap = padded if shift == 0 else pltpu.roll(padded, shift, 0)
        conv = conv + tap[SUBLANES:SUBLANES + rows, :] * cw_ref[i:i + 1, :]
    pad_ref[0:SUBLANES, :] = pad_ref[rows:rows + SUBLANES, :]
    return conv


def _ssd_body(xbc_ref, dtr_ref, conv0_ref, ssm0_ref, cw_ref, cb_ref, dtb_ref, alog_ref, dsk_ref,
              y_ref, st_ref, pad_ref, *, L, valid):
    c = pl.program_id(1)

    @pl.when(c == 0)
    def _():
        pad_ref[0:SUBLANES, :] = conv0_ref[0]
        st_ref[0] = ssm0_ref[0]

    act = _silu(_causal_conv(pad_ref, xbc_ref[0], cw_ref, cb_ref, L))
    xs = act[:, :D_INNER]
    bm = act[:, D_INNER:D_INNER + GN].astype(BF16)
    cm = act[:, D_INNER + GN:].astype(BF16)

    t = dtr_ref[0] + dtb_ref[...]
    dt = jnp.maximum(t, 0.0) + jnp.log1p(jnp.exp(-jnp.abs(t)))
    if valid < L:
        rows = lax.broadcasted_iota(jnp.int32, dt.shape, 0)
        dt = jnp.where(rows < valid, dt, 0.0)
    a = -jnp.exp(alog_ref[...])
    dta = dt * a

    r_i = lax.broadcasted_iota(jnp.int32, (L, L), 0)
    c_i = lax.broadcasted_iota(jnp.int32, (L, L), 1)
    causal = r_i >= c_i
    tril = jnp.where(causal, 1.0, 0.0).astype(BF16)
    e_r = lax.broadcasted_iota(jnp.int32, (LANES, LANES), 0)
    e_c = lax.broadcasted_iota(jnp.int32, (LANES, LANES), 1)
    eye = jnp.where(e_r == e_c, 1.0, 0.0).astype(BF16)

    acs = _dot01(tril, dta, (((1,), (0,)), ((), ())))
    acs_t = _dot01(eye, acs, NT_DIMS)
    last = acs[L - 1:L, :]
    dec_end = jnp.exp(last - acs)
    e_acs = jnp.exp(acs)
    dec_chunk = jnp.exp(last)

    for g in range(N_SSM_GROUPS):
        bg = bm[:, g * D_STATE:(g + 1) * D_STATE]
        cg = cm[:, g * D_STATE:(g + 1) * D_STATE]
        cb = lax.dot_general(cg, bg, NT_DIMS, preferred_element_type=F32)
        for jp in range(HEADS_PER_GROUP // 2):
            pair = []
            for h in (g * HEADS_PER_GROUP + 2 * jp, g * HEADS_PER_GROUP + 2 * jp + 1):
                diff = acs[:, h:h + 1] - acs_t[h:h + 1, :]
                seg = jnp.exp(jnp.where(causal, diff, NEG_INF))
                w = (cb * seg).astype(BF16)
                xs_h = xs[:, h * SSM_HEAD_DIM:(h + 1) * SSM_HEAD_DIM]
                xd_h = xs_h * dt[:, h:h + 1]
                y_diag = jnp.dot(w, xd_h.astype(BF16), preferred_element_type=F32)
                st_h = st_ref[0, h]
                y_off = lax.dot_general(cg, st_h.astype(BF16), NT_DIMS,
                                        preferred_element_type=F32) * e_acs[:, h:h + 1]
                xdd = (xd_h * dec_end[:, h:h + 1]).astype(BF16)
                cs = lax.dot_general(xdd, bg, TN_DIMS, preferred_element_type=F32)
                st_ref[0, h] = st_h * dec_chunk[:, h:h + 1] + cs
                pair.append(y_diag + y_off
                            + dsk_ref[:, h * SSM_HEAD_DIM:(h + 1) * SSM_HEAD_DIM] * xs_h)
            lo = (g * HEADS_PER_GROUP + 2 * jp) * SSM_HEAD_DIM
            y_ref[0, :, lo:lo + 2 * SSM_HEAD_DIM] = jnp.concatenate(pair, axis=1)


def _ssd_t_body(xbc_ref, dtr_ref, *refs, L, n_sub, has_init):
    if has_init:
        conv0_ref, ssm0_ref = refs[:2]
        refs = refs[2:]
    cw_ref, cb_ref, dtb_ref, alog_ref, dsk_ref, y_ref, st_ref, pad_ref, yt_ref = refs

    @pl.when(pl.program_id(1) == 0)
    def _():
        if has_init:
            pad_ref[0:SUBLANES, :] = conv0_ref[0]
            st_ref[0] = ssm0_ref[0]
        else:
            pad_ref[0:SUBLANES, :] = jnp.zeros((SUBLANES, CONV_DIM), F32)
            st_ref[0] = jnp.zeros(st_ref.shape[1:], F32)

    def chunk(ci, carry):
        rows = pl.ds(pl.multiple_of(ci * L, L), L)
        y_ref[0, rows, :] = _ssd_t_chunk(xbc_ref[0, rows, :], dtr_ref[0, rows, :], cw_ref, cb_ref,
                                         dtb_ref, alog_ref, dsk_ref, st_ref, pad_ref, yt_ref, L)
        return carry

    lax.fori_loop(0, n_sub, chunk, 0)


def _ssd_t_chunk(xbc, dtr, cw_ref, cb_ref, dtb_ref, alog_ref, dsk_ref, st_ref, pad_ref, yt_ref, L):
    hd = SSM_HEAD_DIM
    gw = HEADS_PER_GROUP * hd
    act = _silu(_causal_conv(pad_ref, xbc, cw_ref, cb_ref, L))
    xs_t = act[:, :D_INNER].T
    bm = act[:, D_INNER:D_INNER + GN].astype(BF16)
    cm = act[:, D_INNER + GN:].astype(BF16)

    t = dtr + dtb_ref[...]
    dt = jnp.maximum(t, 0.0) + jnp.log1p(jnp.exp(-jnp.abs(t)))
    dta = dt * (-jnp.exp(alog_ref[...]))

    r_i = lax.broadcasted_iota(jnp.int32, (L, L), 0)
    c_i = lax.broadcasted_iota(jnp.int32, (L, L), 1)
    tril = jnp.where(r_i >= c_i, 1.0, 0.0).astype(BF16)
    causal_t = c_i >= r_i

    acs = _dot01(tril, dta, (((1,), (0,)), ((), ())))
    acs_t = acs.T
    dt_t = dt.T
    dec_end_t = jnp.exp(acs_t[:, L - 1:L] - acs_t)
    e_acs_t = jnp.exp(acs_t)
    dec_chunk = jnp.exp(acs[L - 1:L, :])
    scale_in = dt_t
    scale_st = dt_t * dec_end_t

    for g in range(N_SSM_GROUPS):
        bg = bm[:, g * D_STATE:(g + 1) * D_STATE]
        cg = cm[:, g * D_STATE:(g + 1) * D_STATE]
        cb_t = lax.dot_general(bg, cg, NT_DIMS, preferred_element_type=F32)
        h0 = g * HEADS_PER_GROUP
        xs_g = xs_t[g * gw:(g + 1) * gw, :]
        xd_parts, xdd_parts = [], []
        for j in range(HEADS_PER_GROUP):
            h = h0 + j
            xd = xs_g[j * hd:(j + 1) * hd, :] * scale_in[h:h + 1, :]
            xd_parts.append(xd.astype(BF16))
            xdd_parts.append((xs_g[j * hd:(j + 1) * hd, :] * scale_st[h:h + 1, :]).astype(BF16))
        st_g = st_ref[0, h0:h0 + HEADS_PER_GROUP].reshape(gw, D_STATE)
        y_off = lax.dot_general(st_g.astype(BF16), cg, NT_DIMS,
                                preferred_element_type=F32)
        cs = jnp.dot(jnp.concatenate(xdd_parts, axis=0), bg,
                     preferred_element_type=F32)
        for j in range(HEADS_PER_GROUP):
            h = h0 + j
            diff = acs_t[h:h + 1, :] - acs[:, h:h + 1]
            seg_t = jnp.exp(jnp.where(causal_t, diff, NEG_INF))
            w_t = (cb_t * seg_t).astype(BF16)
            y_diag = jnp.dot(xd_parts[j], w_t, preferred_element_type=F32)
            rows = slice(j * hd, (j + 1) * hd)
            yt_ref[h * hd:(h + 1) * hd, :] = (
                y_diag + y_off[rows, :] * e_acs_t[h:h + 1, :]
                + dsk_ref[h * hd:(h + 1) * hd, :] * xs_g[rows, :])
            st_ref[0, h] = st_g[rows, :] * dec_chunk[:, h:h + 1] + cs[rows, :]
    return yt_ref[...].T


def _ssd(xbc, dtr, conv0, ssm0, cw, cb, dtb, alog, dsk, L, valid):
    nb, t = xbc.shape[0], xbc.shape[1]
    nc = t // L
    st_spec = pl.BlockSpec((1, N_SSM_HEADS, SSM_HEAD_DIM, D_STATE), lambda b, c: (b, 0, 0, 0))
    return pl.pallas_call(
        functools.partial(_ssd_body, L=L, valid=valid),
        grid=(nb, nc),
        in_specs=[pl.BlockSpec((1, L, CONV_DIM), lambda b, c: (b, c, 0)),
                  pl.BlockSpec((1, L, LANES), lambda b, c: (b, c, 0)),
                  pl.BlockSpec((1, SUBLANES, CONV_DIM), lambda b, c: (b, 0, 0)),
                  st_spec,
                  _const_spec(cw.shape), _const_spec(cb.shape), _const_spec(dtb.shape),
                  _const_spec(alog.shape), _const_spec(dsk.shape)],
        out_specs=[pl.BlockSpec((1, L, D_INNER), lambda b, c: (b, c, 0)), st_spec],
        out_shape=[jax.ShapeDtypeStruct((nb, t, D_INNER), F32),
                   jax.ShapeDtypeStruct(ssm0.shape, F32)],
        scratch_shapes=[pltpu.VMEM((L + SUBLANES, CONV_DIM), F32)],
        compiler_params=_params(("parallel", "arbitrary")),
        name="ssd_scan",
    )(xbc, dtr, conv0, ssm0, cw, cb, dtb, alog, dsk)


def _ssd_t(xbc, dtr, conv0, ssm0, cw, cb, dtb, alog, dsk):
    nb, t = xbc.shape[0], xbc.shape[1]
    L = SSD_CHUNK
    n_sub = SSD_CHUNKS_PER_STEP if t % (L * SSD_CHUNKS_PER_STEP) == 0 else 1
    rows = L * n_sub
    assert L == LANES and t % rows == 0
    dsk_t = jnp.broadcast_to(dsk.reshape(D_INNER, 1), (D_INNER, LANES))
    st_spec = pl.BlockSpec((1, N_SSM_HEADS, SSM_HEAD_DIM, D_STATE), lambda b, c: (b, 0, 0, 0))
    has_init = conv0 is not None
    init_args = (conv0, ssm0) if has_init else ()
    init_specs = [pl.BlockSpec((1, SUBLANES, CONV_DIM), lambda b, c: (b, 0, 0)),
                  st_spec] if has_init else []
    return pl.pallas_call(
        functools.partial(_ssd_t_body, L=L, n_sub=n_sub, has_init=has_init),
        grid=(nb, t // rows),
        in_specs=[pl.BlockSpec((1, rows, CONV_DIM), lambda b, c: (b, c, 0)),
                  pl.BlockSpec((1, rows, LANES), lambda b, c: (b, c, 0))] + init_specs + [
                  _const_spec(cw.shape), _const_spec(cb.shape),
                  _const_spec(dtb.shape), _const_spec(alog.shape), _const_spec(dsk_t.shape)],
        out_specs=[pl.BlockSpec((1, rows, D_INNER), lambda b, c: (b, c, 0)), st_spec],
        out_shape=[jax.ShapeDtypeStruct((nb, t, D_INNER), F32),
                   jax.ShapeDtypeStruct((nb, N_SSM_HEADS, SSM_HEAD_DIM, D_STATE), F32)],
        scratch_shapes=[pltpu.VMEM((L + SUBLANES, CONV_DIM), F32), pltpu.VMEM((D_INNER, L), F32)],
        compiler_params=_params(("parallel", "arbitrary")),
        name="ssd_scan_t",
    )(xbc, dtr, *init_args, cw, cb, dtb, alog, dsk_t)


def _ffn(x1, nf_ref, wgu_ref, wd_ref):
    h = (x1 * _inv_rms(x1) * nf_ref[...]).astype(BF16)
    acts = []
    for c in range(D_FF // FFN_CHUNK):
        lo = c * FFN_CHUNK
        gate = jnp.dot(h, wgu_ref[0, :, lo:lo + FFN_CHUNK], preferred_element_type=F32)
        up = jnp.dot(h, wgu_ref[0, :, D_FF + lo:D_FF + lo + FFN_CHUNK], preferred_element_type=F32)
        acts.append((_silu(gate) * up).astype(BF16))
    return x1 + jnp.dot(jnp.concatenate(acts, axis=1), wd_ref[0], preferred_element_type=F32)


def _ssm_out_ffn_body(y_ref, z_ref, x_ref, gn_ref, wo_ref, nf_ref, wgu_ref, wd_ref, o_ref):
    y = y_ref[...] * z_ref[...]
    gw = D_INNER // N_SSM_GROUPS
    parts = []
    for g in range(N_SSM_GROUPS):
        yg = y[:, g * gw:(g + 1) * gw]
        parts.append(yg * _inv_rms(yg))
    y = (jnp.concatenate(parts, axis=1) * gn_ref[...]).astype(BF16)
    x1 = x_ref[...] + jnp.dot(y, wo_ref[...], preferred_element_type=F32)
    o_ref[...] = _ffn(x1, nf_ref, wgu_ref, wd_ref)


def _attn_out_ffn_body(a_ref, x_ref, wo_ref, nf_ref, wgu_ref, wd_ref, nfin_ref, o_ref):
    x1 = x_ref[...] + jnp.dot(a_ref[...], wo_ref[...], preferred_element_type=F32)
    x2 = _ffn(x1, nf_ref, wgu_ref, wd_ref)
    o_ref[...] = x2 * _inv_rms(x2) * nfin_ref[...]


def _ssm_out_ffn(y, z, x, gn, wo, nf, wgu, wd, layer, tm):
    m = x.shape[0]
    row = lambda w: pl.BlockSpec((tm, w), lambda i: (i, 0))
    return pl.pallas_call(
        _ssm_out_ffn_body,
        grid=(m // tm,),
        in_specs=[row(D_INNER), row(D_INNER), row(D_MODEL), _const_spec(gn.shape),
                  _const_spec(wo.shape), _const_spec(nf.shape), _layer_spec(wgu.shape, layer),
                  _layer_spec(wd.shape, layer)],
        out_specs=row(D_MODEL),
        out_shape=jax.ShapeDtypeStruct((m, D_MODEL), F32),
        compiler_params=_params(("parallel",)),
        name="ssm_out_ffn",
    )(y, z, x, gn, wo, nf, wgu, wd)


def _attn_out_ffn(a, x, wo, nf, wgu, wd, layer, nfin, tm):
    m = x.shape[0]
    row = lambda w: pl.BlockSpec((tm, w), lambda i: (i, 0))
    return pl.pallas_call(
        _attn_out_ffn_body,
        grid=(m // tm,),
        in_specs=[row(D_MODEL), row(D_MODEL), _const_spec(wo.shape), _const_spec(nf.shape),
                  _layer_spec(wgu.shape, layer), _layer_spec(wd.shape, layer),
                  _const_spec(nfin.shape)],
        out_specs=row(D_MODEL),
        out_shape=jax.ShapeDtypeStruct((m, D_MODEL), F32),
        compiler_params=_params(("parallel",)),
        name="attn_out_ffn",
    )(a, x, wo, nf, wgu, wd, nfin)


def _rope(x, cos, sin_signed, first_half):
    outs = []
    for c in range(x.shape[1] // LANES):
        xc = x[:, c * LANES:(c + 1) * LANES]
        partner = jnp.where(first_half, pltpu.roll(xc, LANES - HEAD_DIM // 2, 1),
                            pltpu.roll(xc, HEAD_DIM // 2, 1))
        outs.append(xc * cos + partner * sin_signed)
    return jnp.concatenate(outs, axis=1)


def _kvq_compute(x_ref, gkv_ref, gq_ref, wkv_ref, wq_ref, cos_ref, sin_ref):
    x = x_ref[...]
    xn = x * _inv_rms(x)
    hk = (xn * gkv_ref[...]).astype(BF16)
    hq = (xn * gq_ref[...]).astype(BF16)
    cos = cos_ref[...]
    sin = sin_ref[...]
    lane = lax.broadcasted_iota(jnp.int32, cos.shape, 1)
    first_half = (lane % HEAD_DIM) < (HEAD_DIM // 2)
    kv = jnp.dot(hk, wkv_ref[...], preferred_element_type=F32)
    k = _rope(kv[:, :KV_DIM], cos, sin, first_half)
    q = _rope(jnp.dot(hq, wq_ref[...], preferred_element_type=F32), cos, sin, first_half)
    return k, kv[:, KV_DIM:], q


def _kvq_body(x_ref, gkv_ref, gq_ref, wkv_ref, wq_ref, cos_ref, sin_ref, k_ref, v_ref, q_ref):
    k, v, q = _kvq_compute(x_ref, gkv_ref, gq_ref, wkv_ref, wq_ref, cos_ref, sin_ref)
    k_ref[...] = k
    v_ref[...] = v
    q_ref[...] = q


def _kvq_prompt_body(x_ref, gkv_ref, gq_ref, wkv_ref, wq_ref, cos_ref, sin_ref,
                     kt_ref, vt32_ref, qt_ref, kh_ref, vt_ref, km_ref):
    k, v, q = _kvq_compute(x_ref, gkv_ref, gq_ref, wkv_ref, wq_ref, cos_ref, sin_ref)
    kt_ref[0] = k.T
    v_t = v.T
    vt32_ref[0] = v_t
    qt_ref[0] = q.T
    vt = v_t.astype(BF16)
    col = lax.broadcasted_iota(jnp.int32, (MOBA_BLOCK, LANES - HEAD_DIM), 1)
    row = lax.broadcasted_iota(jnp.int32, (V_AUG - HEAD_DIM, MOBA_BLOCK), 0)
    ones_row = jnp.where(row == 0, 1.0, 0.0).astype(BF16)
    for sub in range(KVQ_BLOCKS_PER_STEP):
        blk = pl.program_id(1) * KVQ_BLOCKS_PER_STEP + sub
        toks = slice(sub * MOBA_BLOCK, (sub + 1) * MOBA_BLOCK)
        onehot = jnp.where(col == blk, 1.0, 0.0).astype(BF16)
        for kvh in range(N_KV_HEADS):
            kh_ref[0, sub, kvh] = jnp.concatenate(
                [k[toks, kvh * HEAD_DIM:(kvh + 1) * HEAD_DIM].astype(BF16), onehot], axis=1)
            vt_ref[0, sub, kvh] = jnp.concatenate(
                [vt[kvh * HEAD_DIM:(kvh + 1) * HEAD_DIM, toks], ones_row], axis=0)
        km_ref[0, sub] = jnp.sum(k[toks, :], axis=0, keepdims=True) * (1.0 / MOBA_BLOCK)


def _kvq(x, gkv, gq, wkv, wq, cos, sin, tm):
    m = x.shape[0]
    row = lambda w: pl.BlockSpec((tm, w), lambda i: (i, 0))
    tab = pl.BlockSpec((tm, LANES), lambda i: (0, 0))
    return pl.pallas_call(
        _kvq_body,
        grid=(m // tm,),
        in_specs=[row(D_MODEL), _const_spec(gkv.shape), _const_spec(gq.shape),
                  _const_spec(wkv.shape), _const_spec(wq.shape), tab, tab],
        out_specs=[row(KV_DIM), row(KV_DIM), row(D_MODEL)],
        out_shape=[jax.ShapeDtypeStruct((m, KV_DIM), F32), jax.ShapeDtypeStruct((m, KV_DIM), F32),
                   jax.ShapeDtypeStruct((m, D_MODEL), F32)],
        compiler_params=_params(("parallel",)),
        name="kvq_proj",
    )(x, gkv, gq, wkv, wq, cos, sin)


def _kvq_prompt(x, b, t, gkv, gq, wkv, wq, cos, sin):
    nb = t // MOBA_BLOCK
    per = KVQ_BLOCKS_PER_STEP
    assert nb % per == 0
    steps = nb // per
    tm = per * MOBA_BLOCK
    row = lambda w: pl.BlockSpec((tm, w), lambda bi, i: (bi * steps + i, 0))
    tab = pl.BlockSpec((tm, LANES), lambda bi, i: (i, 0))
    return pl.pallas_call(
        _kvq_prompt_body,
        grid=(b, steps),
        in_specs=[row(D_MODEL), _const_spec(gkv.shape), _const_spec(gq.shape),
                  _const_spec(wkv.shape), _const_spec(wq.shape), tab, tab],
        out_specs=[pl.BlockSpec((1, KV_DIM, tm), lambda bi, i: (bi, 0, i)),
                   pl.BlockSpec((1, KV_DIM, tm), lambda bi, i: (bi, 0, i)),
                   pl.BlockSpec((1, D_MODEL, tm), lambda bi, i: (bi, 0, i)),
                   pl.BlockSpec((1, per, N_KV_HEADS, MOBA_BLOCK, LANES),
                                lambda bi, i: (bi, i, 0, 0, 0)),
                   pl.BlockSpec((1, per, N_KV_HEADS, V_AUG, MOBA_BLOCK),
                                lambda bi, i: (bi, i, 0, 0, 0)),
                   pl.BlockSpec((1, per, 1, KV_DIM), lambda bi, i: (bi, i, 0, 0))],
        out_shape=[jax.ShapeDtypeStruct((b, KV_DIM, t), F32),
                   jax.ShapeDtypeStruct((b, KV_DIM, t), F32),
                   jax.ShapeDtypeStruct((b, D_MODEL, t), F32),
                   jax.ShapeDtypeStruct((b, nb, N_KV_HEADS, MOBA_BLOCK, LANES), BF16),
                   jax.ShapeDtypeStruct((b, nb, N_KV_HEADS, V_AUG, MOBA_BLOCK), BF16),
                   jax.ShapeDtypeStruct((b, nb, 1, KV_DIM), F32)],
        compiler_params=_params(("parallel", "parallel")),
        name="kvq_proj_prompt",
    )(x, gkv, gq, wkv, wq, cos, sin)


def _moba_prompt_body(qt_ref, kh_ref, vt_ref, km_ref, o_ref, s_a, s_b, acc_a, acc_b, *, nb):
    i = pl.program_id(1)
    g_per_unit = ATTN_UNIT_HEADS
    rows = g_per_unit * MOBA_BLOCK
    n_units = N_HEADS // g_per_unit
    scale = (HEAD_DIM ** -0.5) * LOG2_E
    blk_row = lax.broadcasted_iota(jnp.int32, (nb, rows), 0)
    key_i = lax.broadcasted_iota(jnp.int32, (MOBA_BLOCK, rows), 0)
    tok_i = lax.broadcasted_iota(jnp.int32, (MOBA_BLOCK, rows), 1) % MOBA_BLOCK
    own_ok = key_i <= tok_i
    fold = lambda s: jnp.max(s.reshape(MOBA_BLOCK // SUBLANES, SUBLANES, rows), axis=0)

    def setup(u):
        kvh = (u * g_per_unit) // Q_PER_KV
        lo = kvh * HEAD_DIM
        base = u * g_per_unit * HEAD_DIM
        qt2 = qt_ref[0, base:base + g_per_unit * HEAD_DIM, :]
        qt = jnp.concatenate([qt2[g * HEAD_DIM:(g + 1) * HEAD_DIM, :] for g in range(g_per_unit)],
                             axis=1)
        km = km_ref[0, :, lo:lo + HEAD_DIM]
        gate = jnp.dot(km, qt, preferred_element_type=F32,
                       precision=lax.Precision.HIGHEST)
        sel = _top3_mask(gate, blk_row < i, axis=0)
        pen = jnp.where(sel, 0.0, MASKED).astype(BF16)
        qb = (qt * scale).astype(BF16)
        q_own = jnp.concatenate([qb, jnp.zeros((LANES - HEAD_DIM, rows), BF16)], axis=0)
        q_aug = jnp.concatenate([qb, pen, jnp.zeros((LANES - HEAD_DIM - nb, rows), BF16)], axis=0)
        s_own = jnp.dot(kh_ref[0, i, kvh], q_own, preferred_element_type=F32)
        s_own = jnp.where(own_ok, s_own, MASKED)
        s_bufs[u % 2][nb] = s_own
        return kvh, q_aug, fold(s_own)

    def scores(u, kvh, q_aug, j0, n, m8):
        for t in range(n):
            s = jnp.dot(kh_ref[0, j0 + t, kvh], q_aug, preferred_element_type=F32)
            s_bufs[u % 2][j0 + t] = s
            m8 = jnp.maximum(m8, fold(s))
        return m8

    def weighted(u, kvh, m, j0, n):
        s_buf = s_bufs[u % 2]
        p = jnp.concatenate([jnp.exp2(s_buf[j0 + t] - m).astype(BF16) for t in range(n)], axis=0)
        v = jnp.concatenate([vt_ref[0, j0 + t, kvh] for t in range(n)], axis=1)
        acc_bufs[u % 2][...] += jnp.dot(v, p, preferred_element_type=F32)

    def over_blocks(step, carry):
        n4, n2 = i // 4, i // 2
        carry = lax.fori_loop(0, n4, lambda q, c: step(4 * q + 2, 2, step(4 * q, 2, c)), carry)
        carry = lax.fori_loop(2 * n4, n2, lambda q, c: step(2 * q, 2, c), carry)
        return lax.fori_loop(2 * n2, i, lambda j, c: step(j, 1, c), carry)

    s_bufs = (s_a, s_b)
    acc_bufs = (acc_a, acc_b)
    kvh, q_aug, m8 = setup(0)
    m8 = over_blocks(functools.partial(scores, 0, kvh, q_aug), m8)
    for u in range(n_units):
        m = jnp.max(m8, axis=0, keepdims=True)
        p_own = jnp.exp2(s_bufs[u % 2][nb] - m).astype(BF16)
        acc_bufs[u % 2][...] = jnp.dot(vt_ref[0, i, kvh], p_own, preferred_element_type=F32)
        if u + 1 < n_units:
            kvh_n, q_aug_n, m8_n = setup(u + 1)

            def both(j0, n, m8c, u=u, kvh=kvh, m=m, kvh_n=kvh_n, q_aug_n=q_aug_n):
                m8c = scores(u + 1, kvh_n, q_aug_n, j0, n, m8c)
                weighted(u, kvh, m, j0, n)
                return m8c

            m8_n = over_blocks(both, m8_n)
        else:
            def last(j0, n, carry, u=u, kvh=kvh, m=m):
                weighted(u, kvh, m, j0, n)
                return carry

            over_blocks(last, 0)
        acc = acc_bufs[u % 2][...]
        out_t = acc[:HEAD_DIM, :] / acc[HEAD_DIM:HEAD_DIM + 1, :]
        out2 = jnp.concatenate([out_t[:, g * MOBA_BLOCK:(g + 1) * MOBA_BLOCK]
                                for g in range(g_per_unit)], axis=0)
        base = u * g_per_unit * HEAD_DIM
        o_ref[0, :, base:base + g_per_unit * HEAD_DIM] = out2.T.astype(o_ref.dtype)
        if u + 1 < n_units:
            kvh, m8 = kvh_n, m8_n


def _moba_prompt(qt, kh, vt, km):
    b, t = qt.shape[0], qt.shape[2]
    nb = t // MOBA_BLOCK
    rows = ATTN_UNIT_HEADS * MOBA_BLOCK
    return pl.pallas_call(
        functools.partial(_moba_prompt_body, nb=nb),
        grid=(b, nb),
        in_specs=[pl.BlockSpec((1, D_MODEL, MOBA_BLOCK), lambda bi, i: (bi, 0, i)),
                  pl.BlockSpec((1, nb, N_KV_HEADS, MOBA_BLOCK, LANES),
                               lambda bi, i: (bi, 0, 0, 0, 0)),
                  pl.BlockSpec((1, nb, N_KV_HEADS, V_AUG, MOBA_BLOCK),
                               lambda bi, i: (bi, 0, 0, 0, 0)),
                  pl.BlockSpec((1, nb, KV_DIM), lambda bi, i: (bi, 0, 0))],
        out_specs=pl.BlockSpec((1, MOBA_BLOCK, D_MODEL), lambda bi, i: (bi, i, 0)),
        out_shape=jax.ShapeDtypeStruct((b, t, D_MODEL), BF16),
        scratch_shapes=[pltpu.VMEM((nb + 1, MOBA_BLOCK, rows), F32),
                        pltpu.VMEM((nb + 1, MOBA_BLOCK, rows), F32),
                        pltpu.VMEM((V_AUG, rows), F32), pltpu.VMEM((V_AUG, rows), F32)],
        compiler_params=_params(("parallel", "arbitrary")),
        name="moba_prompt",
    )(qt, kh, vt, km)


def _page_copy(cache_hbm, buf, sem, page, slot, p):
    return pltpu.make_async_copy(cache_hbm.at[page], buf.at[slot, p], sem)


def _moba_sample_body(pt_ref, q_ref, kn_ref, vn_ref, ck_hbm, cv_hbm, o_ref,
                      kbuf, vbuf, s_scr, km_scr, new_scr, sems, *, n_pages, n_new):
    b = pl.program_id(0)
    slot = b % 2
    rows = q_ref.shape[1]
    ppb = MOBA_BLOCK // PAGE_SIZE
    nb = n_pages // ppb
    scale = HEAD_DIM ** -0.5

    def fetch(seq, to_slot):
        def start_page(p, carry):
            page = pt_ref[seq, p]
            _page_copy(ck_hbm, kbuf, sems.at[0, to_slot], page, to_slot, p).start()
            _page_copy(cv_hbm, vbuf, sems.at[1, to_slot], page, to_slot, p).start()
            return carry
        lax.fori_loop(0, n_pages, start_page, 0, unroll=8)

    @pl.when(b == 0)
    def _():
        fetch(0, 0)

    @pl.when(b + 1 < pl.num_programs(0))
    def _():
        fetch(b + 1, 1 - slot)

    q64 = q_ref[0]
    qt = jnp.concatenate([q64] * N_KV_HEADS, axis=1)
    r_i = lax.broadcasted_iota(jnp.int32, qt.shape, 0)
    c_i = lax.broadcasted_iota(jnp.int32, qt.shape, 1)
    qpad = jnp.where(c_i // HEAD_DIM == r_i // (Q_PER_KV * n_new), qt, 0.0)
    qb = (qpad * scale).astype(BF16)

    def wait_k(p, carry):
        _page_copy(ck_hbm, kbuf, sems.at[0, slot], 0, slot, p).wait()
        return carry

    lax.fori_loop(0, n_pages, wait_k, 0, unroll=8)

    km_scr[...] = jnp.zeros(km_scr.shape, F32)
    for j in range(nb):
        kt = jnp.concatenate([kbuf[slot, ppb * j + u] for u in range(ppb)], axis=1)
        km_scr[:, j:j + 1] = jnp.sum(kt, axis=1, keepdims=True) * (1.0 / MOBA_BLOCK)
        s_scr[:, j * MOBA_BLOCK:(j + 1) * MOBA_BLOCK] = jnp.dot(
            qb, kt.astype(BF16), preferred_element_type=F32)

    gate = jnp.dot(qpad, km_scr[...], preferred_element_type=F32,
                   precision=lax.Precision.HIGHEST)
    sel = _top3_mask(gate, lax.broadcasted_iota(jnp.int32, gate.shape, 1) < nb)
    pen = jnp.where(sel, 0.0, NEG_INF)

    new_scr[...] = jnp.zeros(new_scr.shape, F32)
    new_scr[0:n_new, :] = kn_ref[0]
    s_new = lax.dot_general(qb, new_scr[...].astype(BF16), NT_DIMS, preferred_element_type=F32)
    nr = lax.broadcasted_iota(jnp.int32, s_new.shape, 0)
    ncol = lax.broadcasted_iota(jnp.int32, s_new.shape, 1)
    s_new = jnp.where(ncol <= nr % n_new, s_new, NEG_INF)

    def lane_fold(x, op):
        out = x[:, :LANES]
        for c in range(1, x.shape[1] // LANES):
            out = op(out, x[:, c * LANES:(c + 1) * LANES])
        return out

    m_part = s_new
    for j in range(nb):
        sj = s_scr[:, j * MOBA_BLOCK:(j + 1) * MOBA_BLOCK] + pen[:, j:j + 1]
        m_part = jnp.maximum(m_part, lane_fold(sj, jnp.maximum))
    m = jnp.max(m_part, axis=1, keepdims=True)

    def wait_v(p, carry):
        _page_copy(cv_hbm, vbuf, sems.at[1, slot], 0, slot, p).wait()
        return carry

    lax.fori_loop(0, n_pages, wait_v, 0, unroll=8)

    p_new = jnp.exp(s_new - m)
    l_part = p_new
    new_scr[0:n_new, :] = vn_ref[0]
    acc = jnp.dot(p_new.astype(BF16), new_scr[...].astype(BF16), preferred_element_type=F32)
    for j in range(nb):
        pj = jnp.exp(s_scr[:, j * MOBA_BLOCK:(j + 1) * MOBA_BLOCK] + pen[:, j:j + 1] - m)
        l_part = l_part + lane_fold(pj, jnp.add)
        vt = jnp.concatenate([vbuf[slot, ppb * j + u] for u in range(ppb)], axis=1).astype(BF16)
        acc = acc + lax.dot_general(pj.astype(BF16), vt, NT_DIMS, preferred_element_type=F32)
    acc = acc / jnp.sum(l_part, axis=1, keepdims=True)
    ro = lax.broadcasted_iota(jnp.int32, (rows, HEAD_DIM), 0) // (Q_PER_KV * n_new)
    out = jnp.zeros((rows, HEAD_DIM), F32)
    for kvh in range(N_KV_HEADS):
        out = jnp.where(ro == kvh, acc[:, kvh * HEAD_DIM:(kvh + 1) * HEAD_DIM], out)
    o_ref[0] = out


def _moba_sample(page_table, q_rows, k_new, v_new, cache_k, cache_v):
    nseq, n_pages = page_table.shape
    rows = q_rows.shape[1]
    n_new = k_new.shape[1]
    past = n_pages * PAGE_SIZE
    grid_spec = pltpu.PrefetchScalarGridSpec(
        num_scalar_prefetch=1,
        grid=(nseq,),
        in_specs=[pl.BlockSpec((1, rows, HEAD_DIM), lambda b, pt: (b, 0, 0)),
                  pl.BlockSpec((1, n_new, KV_DIM), lambda b, pt: (b, 0, 0)),
                  pl.BlockSpec((1, n_new, KV_DIM), lambda b, pt: (b, 0, 0)),
                  pl.BlockSpec(memory_space=pl.ANY),
                  pl.BlockSpec(memory_space=pl.ANY)],
        out_specs=pl.BlockSpec((1, rows, HEAD_DIM), lambda b, pt: (b, 0, 0)),
        scratch_shapes=[pltpu.VMEM((2, n_pages, KV_DIM, PAGE_SIZE), F32),
                        pltpu.VMEM((2, n_pages, KV_DIM, PAGE_SIZE), F32),
                        pltpu.VMEM((rows, past), F32),
                        pltpu.VMEM((KV_DIM, LANES), F32),
                        pltpu.VMEM((LANES, KV_DIM), F32),
                        pltpu.SemaphoreType.DMA((2, 2))],
    )
    return pl.pallas_call(
        functools.partial(_moba_sample_body, n_pages=n_pages, n_new=n_new),
        grid_spec=grid_spec,
        out_shape=jax.ShapeDtypeStruct((nseq, rows, HEAD_DIM), F32),
        compiler_params=_params(("arbitrary",)),
        name="moba_sample",
    )(page_table, q_rows, k_new, v_new, cache_k, cache_v)


def _rope_tables(pos):
    half = HEAD_DIM // 2
    inv = ROPE_THETA ** (-np.arange(half, dtype=np.float64) / half)
    ang = np.asarray(pos, dtype=np.float64)[:, None] * inv[None, :]
    cos, sin = np.cos(ang), np.sin(ang)
    cos_h = np.concatenate([cos, cos], axis=1)
    sin_h = np.concatenate([-sin, sin], axis=1)
    reps = LANES // HEAD_DIM
    return (jnp.asarray(np.tile(cos_h, (1, reps)), F32), jnp.asarray(np.tile(sin_h, (1, reps)), F32))


def _prep_weights(norm_mix, norm_ffn, w_in_ssm, conv_w, conv_b, dt_bias, a_log, d_skip, norm_ssm,
                  w_out_ssm, norm_kv, w_kv, w_q, w_o, w_gu, w_down, norm_final):
    pad_h = LANES - N_SSM_HEADS
    return dict(
        g_mix0=norm_mix[0][None], g_mix1=norm_mix[1][None],
        g_ffn0=norm_ffn[0][None], g_ffn1=norm_ffn[1][None],
        w_in=w_in_ssm.astype(BF16),
        cw=conv_w[0], cb=conv_b[0][None],
        dtb=jnp.pad(dt_bias[0], (0, pad_h))[None], alog=jnp.pad(a_log[0], (0, pad_h))[None],
        dsk=jnp.repeat(d_skip[0], SSM_HEAD_DIM)[None],
        gn=norm_ssm[0][None], wo_ssm=w_out_ssm[0].astype(BF16),
        g_kv=norm_kv[None], wkv=w_kv.astype(BF16), wq=w_q[0].astype(BF16),
        wo=w_o[0].astype(BF16),
        wgu=w_gu.astype(BF16), wd=w_down.astype(BF16),
        g_fin=norm_final[None],
    )


def _ssd_layer_long(x, conv_in, ssm0, w, tm):
    b, t, _ = x.shape
    keep = D_CONV - 1
    assert t >= keep
    xf = x.reshape(b * t, D_MODEL)
    z, xbc, dtr = _inproj(xf, w["g_mix0"], w["w_in"], 0, tm)
    xbc3 = xbc.reshape(b, t, CONV_DIM)
    conv0 = None if conv_in is None else jnp.pad(conv_in, ((0, 0), (SUBLANES - keep, 0), (0, 0)))
    y, ssm_new = _ssd_t(xbc3, dtr.reshape(b, t, LANES), conv0, ssm0, w["cw"], w["cb"], w["dtb"],
                        w["alog"], w["dsk"])
    x2 = _ssm_out_ffn(y.reshape(b * t, D_INNER), z, xf, w["gn"], w["wo_ssm"], w["g_ffn0"],
                      w["wgu"], w["wd"], 0, tm)
    return x2, xbc3[:, t - keep:], ssm_new


def _ssd_layer(x, conv_in, ssm0, w, tm, L, valid):
    b, t, _ = x.shape
    xf = x.reshape(b * t, D_MODEL)
    z, xbc, dtr = _inproj(xf, w["g_mix0"], w["w_in"], 0, tm)
    tp = -(-t // L) * L
    xbc3 = xbc.reshape(b, t, CONV_DIM)
    dtr3 = dtr.reshape(b, t, LANES)
    if tp != t:
        xbc3 = jnp.pad(xbc3, ((0, 0), (0, tp - t), (0, 0)))
        dtr3 = jnp.pad(dtr3, ((0, 0), (0, tp - t), (0, 0)))
    conv0 = jnp.pad(conv_in, ((0, 0), (SUBLANES - (D_CONV - 1), 0), (0, 0)))
    y, ssm_new = _ssd(xbc3, dtr3, conv0, ssm0, w["cw"], w["cb"], w["dtb"], w["alog"], w["dsk"],
                      L, valid)
    y = y[:, :t].reshape(b * t, D_INNER)
    x2 = _ssm_out_ffn(y, z, xf, w["gn"], w["wo_ssm"], w["g_ffn0"], w["wgu"], w["wd"], 0, tm)
    keep = D_CONV - 1
    conv_new = jnp.concatenate([conv_in[:, t:], xbc.reshape(b, t, CONV_DIM)[:, max(0, t - keep):]],
                               axis=1)
    return x2, conv_new, ssm_new


def kernel(x_prompt, x_sample, state_conv, state_ssm, cache_k, cache_v, page_table, norm_mix,
           norm_ffn, w_in_ssm, conv_w, conv_b, dt_bias, a_log, d_skip, norm_ssm, w_out_ssm,
           norm_kv, w_kv, w_q, w_o, w_gu, w_down, norm_final):
    w = _prep_weights(norm_mix, norm_ffn, w_in_ssm, conv_w, conv_b, dt_bias, a_log, d_skip,
                      norm_ssm, w_out_ssm, norm_kv, w_kv, w_q, w_o, w_gu, w_down, norm_final)
    bp, tp, _ = x_prompt.shape
    bs, ts, _ = x_sample.shape
    past_len = page_table.shape[1] * PAGE_SIZE

    tm_p = PROMPT_ROW_TILE
    x2_p, conv_p, ssm_p = _ssd_layer_long(x_prompt, None, None, w, tm_p)
    cos_p, sin_p = _rope_tables(np.arange(tp))
    kt_p, vt32_p, qt_p, kh_p, vt_p, km_p = _kvq_prompt(x2_p, bp, tp, w["g_kv"], w["g_mix1"],
                                                       w["wkv"], w["wq"], cos_p, sin_p)
    k_p = kt_p.reshape(bp, N_KV_HEADS, HEAD_DIM, tp).transpose(0, 3, 1, 2)
    v_p = vt32_p.reshape(bp, N_KV_HEADS, HEAD_DIM, tp).transpose(0, 3, 1, 2)
    attn_p = _moba_prompt(qt_p, kh_p, vt_p, km_p.reshape(bp, tp // MOBA_BLOCK, KV_DIM))
    y_p = _attn_out_ffn(attn_p.reshape(bp * tp, D_MODEL), x2_p, w["wo"], w["g_ffn1"], w["wgu"],
                        w["wd"], 1, w["g_fin"], tm_p)

    tm_s = bs * ts
    x2_s, conv_s, ssm_s = _ssd_layer(x_sample, state_conv[0], state_ssm[0], w, tm_s, SUBLANES, ts)
    cos_s, sin_s = _rope_tables(past_len + np.tile(np.arange(ts), bs))
    k_s, v_s, q_s = _kvq(x2_s, w["g_kv"], w["g_mix1"], w["wkv"], w["wq"], cos_s, sin_s, tm_s)
    q_rows = q_s.reshape(bs, ts, N_HEADS, HEAD_DIM).transpose(0, 2, 1, 3).reshape(
        bs, N_HEADS * ts, HEAD_DIM)
    attn_rows = _moba_sample(page_table, q_rows, k_s.reshape(bs, ts, KV_DIM),
                             v_s.reshape(bs, ts, KV_DIM),
                             cache_k.transpose(0, 2, 3, 1).reshape(-1, KV_DIM, PAGE_SIZE),
                             cache_v.transpose(0, 2, 3, 1).reshape(-1, KV_DIM, PAGE_SIZE))
    attn_s = attn_rows.reshape(bs, N_HEADS, ts, HEAD_DIM).transpose(
        0, 2, 1, 3).reshape(bs * ts, D_MODEL).astype(BF16)
    y_s = _attn_out_ffn(attn_s, x2_s, w["wo"], w["g_ffn1"], w["wgu"], w["wd"], 1,
                        w["g_fin"], tm_s)

    return (y_p.reshape(bp, tp, D_MODEL), y_s.reshape(bs, ts, D_MODEL),
            conv_p[None], ssm_p[None],
            k_p, v_p,
            conv_s[None], ssm_s[None],
            k_s.reshape(bs, ts, N_KV_HEADS, HEAD_DIM), v_s.reshape(bs, ts, N_KV_HEADS, HEAD_DIM))
```
